```python
import math
import jax
import jax.numpy as jnp
from jax import lax
import numpy as np

D_MODEL = 1024
BATCH = 8
SEQ = 2048
DEPTH = 2
DEC_BATCH = 128
DEC_SEQ = 1
PAST_LEN = 8192
PAGE_SIZE = 128

D_MIX = D_MODEL
D_RNN = D_MIX // 2
RNN_BLOCKS = 8
RNN_BW = D_RNN // RNN_BLOCKS
CONV_W = 4
LRU_C = 8.0
N_HEADS = 8
HEAD_DIM = 64
N_KV = 2
GQA_R = N_HEADS // N_KV
D_ATTN = N_HEADS * HEAD_DIM
WINDOW = 128
N_BUCKETS = 32
MAX_DIST = 128
N_EXPERTS = 32
TOP_K = 4
D_FF = D_MODEL
SWIGLU_LIMIT = 7.0
SWIGLU_ALPHA = 1.702
MOE_BLOCK = 128
D_PLE = 256
LN_EPS = 1e-5
D_IN = 2 * D_RNN + D_ATTN + 2 * N_KV * HEAD_DIM
SPLITS = [D_RNN, 2 * D_RNN, 2 * D_RNN + D_ATTN, 2 * D_RNN + D_ATTN + N_KV * HEAD_DIM]
ALPHA = (2 * DEPTH) ** 0.25
BETA = (8 * DEPTH) ** -0.25

kernel_name = "hybrid_rglru_swa_moe_decoder_step"


def layer_norm(x, g, b):
    xf = x.astype(jnp.float32)
    mu = xf.mean(-1, keepdims=True)
    var = jnp.square(xf - mu).mean(-1, keepdims=True)
    return ((xf - mu) * lax.rsqrt(var + LN_EPS) * g.astype(jnp.float32) + b.astype(jnp.float32)).astype(x.dtype)


def rms_norm(x, g, dtype):
    xf = x.astype(jnp.float32)
    return (xf * lax.rsqrt(jnp.mean(xf * xf, -1, keepdims=True) + LN_EPS) * g.astype(jnp.float32)).astype(dtype)


def rel_bucket(dist):
    n = jnp.maximum(dist, 0)
    max_exact = N_BUCKETS // 2
    nf = jnp.maximum(n, max_exact).astype(jnp.float32)
    large = max_exact + (jnp.log(nf / max_exact) / math.log(MAX_DIST / max_exact)
                         * (N_BUCKETS - max_exact)).astype(jnp.int32)
    large = jnp.minimum(large, N_BUCKETS - 1)
    return jnp.where(n < max_exact, n, large)


def sink_attention(q, k, v, dist, valid, sinks, rel_bias):
    f32 = jnp.float32
    s = jnp.einsum('bnqgrd,bnkgd->bngrqk', q.astype(f32), k.astype(f32)) * (HEAD_DIM ** -0.5)
    Q, K = dist.shape
    bias = rel_bias.astype(f32)[rel_bucket(dist)]
    bias = bias.reshape(Q, K, N_KV, GQA_R).transpose(2, 3, 0, 1)
    s = jnp.where(valid[:, None, None], s + bias, -jnp.inf)
    sink = sinks.astype(f32).reshape(N_KV, GQA_R)[None, None, :, :, None, None]
    m = jnp.maximum(s.max(-1, keepdims=True), sink)
    p = jnp.exp(s - m)
    denom = p.sum(-1, keepdims=True) + jnp.exp(sink - m)
    return jnp.einsum('bngrqk,bnkgd->bnqgrd', p / denom, v.astype(f32))


def attn_prompt(q, k, v, sinks, rel_bias):
    B, S = q.shape[:2]
    nb = S // WINDOW
    qb = q.reshape(B, nb, WINDOW, N_KV, GQA_R, HEAD_DIM)
    kb = k.reshape(B, nb, WINDOW, N_KV, HEAD_DIM)
    vb = v.reshape(B, nb, WINDOW, N_KV, HEAD_DIM)
    kk = jnp.concatenate([jnp.concatenate([jnp.zeros_like(kb[:, :1]), kb[:, :-1]], 1), kb], 2)
    vv = jnp.concatenate([jnp.concatenate([jnp.zeros_like(vb[:, :1]), vb[:, :-1]], 1), vb], 2)
    qi = jnp.arange(WINDOW)[:, None]
    kj = jnp.arange(2 * WINDOW)[None, :]
    dist = qi + WINDOW - kj
    blk = jnp.arange(nb)[:, None, None]
    valid = (dist >= 0) & (dist < WINDOW) & (blk * WINDOW + kj - WINDOW >= 0)
    o = sink_attention(qb, kk, vv, dist, valid, sinks, rel_bias)
    return o.reshape(B, S, D_ATTN)


def attn_sample(q, k_new, v_new, k_buf, v_buf, sinks, rel_bias):
    B, T = q.shape[:2]
    wb = k_buf.shape[1]
    kk = jnp.concatenate([k_buf.astype(k_new.dtype), k_new], 1)
    vv = jnp.concatenate([v_buf.astype(v_new.dtype), v_new], 1)
    dist = jnp.arange(T)[:, None] + wb - jnp.arange(wb + T)[None, :]
    valid = ((dist >= 0) & (dist < WINDOW))[None]
    o = sink_attention(q.reshape(B, 1, T, N_KV, GQA_R, HEAD_DIM), kk[:, None], vv[:, None],
                       dist, valid, sinks, rel_bias)
    return o.reshape(B, T, D_ATTN), kk[:, -wb:], vv[:, -wb:]


def rglru(x_rnn, conv_buf, h0, conv_w, conv_b, wa, ba, wx, bx, lam):
    B, T = x_rnn.shape[:2]
    f32 = jnp.float32
    xx = jnp.concatenate([conv_buf.astype(x_rnn.dtype), x_rnn], 1)
    u = conv_b + sum(xx[:, j:j + T] * conv_w[j] for j in range(CONV_W))
    new_buf = xx[:, T:]
    uf = u.astype(f32)
    ub = uf.reshape(B, T, RNN_BLOCKS, RNN_BW)
    r = jax.nn.sigmoid(jnp.einsum('btnd,nde->btne', ub, wa.astype(f32)).reshape(B, T, D_RNN) + ba)
    i = jax.nn.sigmoid(jnp.einsum('btnd,nde->btne', ub, wx.astype(f32)).reshape(B, T, D_RNN) + bx)
    log_a = -LRU_C * r * jax.nn.softplus(-lam.astype(f32))
    a = jnp.exp(log_a)
    b = jnp.sqrt(-jnp.expm1(2.0 * log_a)) * (i * uf)
    b = b.at[:, 0].add(a[:, 0] * h0.astype(f32))

    def combine(lhs, rhs):
        a1, b1 = lhs
        a2, b2 = rhs
        return a1 * a2, a2 * b1 + b2

    _, h = lax.associative_scan(combine, (a, b), axis=1)
    return h, new_buf, h[:, -1]


def moe(x, router_w, router_b, w1, b1, w2, b2):
    B, T, D = x.shape
    xt = x.reshape(-1, D)
    n = xt.shape[0]
    logits = xt.astype(jnp.float32) @ router_w.astype(jnp.float32) + router_b.astype(jnp.float32)
    top_v, top_i = lax.top_k(logits, TOP_K)
    gates = jax.nn.softmax(top_v, axis=-1)
    flat_e = top_i.reshape(-1)
    flat_tok = jnp.arange(n * TOP_K, dtype=jnp.int32) // TOP_K
    order = jnp.argsort(flat_e)
    e_sorted = flat_e[order]
    tok_sorted = flat_tok[order]
    counts = jnp.zeros((N_EXPERTS,), jnp.int32).at[flat_e].add(1)
    padded = (counts + MOE_BLOCK - 1) // MOE_BLOCK * MOE_BLOCK
    pad_end = jnp.cumsum(padded)
    pad_start = pad_end - padded
    start = jnp.cumsum(counts) - counts
    dest = pad_start[e_sorted] + (jnp.arange(n * TOP_K, dtype=jnp.int32) - start[e_sorted])
    n_blocks = -(-(n * TOP_K + N_EXPERTS * (MOE_BLOCK - 1)) // MOE_BLOCK)
    xbuf = jnp.zeros((n_blocks * MOE_BLOCK, D), x.dtype).at[dest].set(xt[tok_sorted])
    blk_e = jnp.minimum(jnp.searchsorted(pad_end, jnp.arange(n_blocks) * MOE_BLOCK, side='right'),
                        N_EXPERTS - 1)

    def run_block(args):
        xb, e = args
        hdn = xb @ w1[e] + b1[e]
        g, lin = jnp.split(hdn, 2, axis=-1)
        g = jnp.minimum(g, SWIGLU_LIMIT)
        lin = jnp.clip(lin, -SWIGLU_LIMIT, SWIGLU_LIMIT)
        return (g * jax.nn.sigmoid(SWIGLU_ALPHA * g) * (lin + 1.0)) @ w2[e] + b2[e]

    ybuf = lax.map(run_block, (xbuf.reshape(n_blocks, MOE_BLOCK, D), blk_e)).reshape(-1, D)
    w_sorted = gates.reshape(-1)[order][:, None].astype(ybuf.dtype)
    y = jnp.zeros((n, D), ybuf.dtype).at[tok_sorted].add(ybuf[dest] * w_sorted)
    return y.reshape(B, T, D).astype(x.dtype)


def decoder_layer(x, p, conv_buf, h0, k_buf, v_buf, lw, rel_bias):
    B, T = x.shape[:2]
    z = x @ lw['w_in']
    x_rnn, gate, q, k, v = jnp.split(z, SPLITS, axis=-1)
    h_rnn, new_conv, new_h = rglru(x_rnn, conv_buf, h0, lw['conv_w'], lw['conv_b'], lw['rg_wa'],
                                   lw['rg_ba'], lw['rg_wx'], lw['rg_bx'], lw['rg_lambda'])
    y_rnn = h_rnn * jax.nn.gelu(gate.astype(jnp.float32))
    q = q.reshape(B, T, N_HEADS, HEAD_DIM)
    k = k.reshape(B, T, N_KV, HEAD_DIM)
    v = v.reshape(B, T, N_KV, HEAD_DIM)
    if k_buf is None:
        o = attn_prompt(q, k, v, lw['attn_sinks'], rel_bias)
        wp = min(WINDOW, T)
        new_k, new_v = k[:, T - wp:], v[:, T - wp:]
    else:
        o, new_k, new_v = attn_sample(q, k, v, k_buf, v_buf, lw['attn_sinks'], rel_bias)
    mix = jnp.concatenate([rms_norm(y_rnn, lw['gn_rnn'], x.dtype),
                           rms_norm(o, lw['gn_attn'], x.dtype)], -1) @ lw['w_out']
    x = layer_norm(ALPHA * x + mix, lw['ln1_g'], lw['ln1_b'])
    ple = jax.nn.sigmoid(x @ lw['ple_w_gate'] + lw['ple_b_gate']) * (p.astype(x.dtype) @ lw['ple_w_proj'])
    ffn = moe(x, lw['router_w'], lw['router_b'], lw['moe_w1'], lw['moe_b1'], lw['moe_w2'], lw['moe_b2'])
    x = layer_norm(ALPHA * x + ffn + ple, lw['ln2_g'], lw['ln2_b'])
    return x, new_k, new_v, new_conv, new_h


def setup_inputs(seed: int = 0) -> dict:
    key = jax.random.key(seed)
    ks = list(jax.random.split(key, 40))
    f32 = jnp.float32

    def nrm(shape, scale):
        return jax.random.normal(ks.pop(), shape, f32) * scale

    wb = min(WINDOW, PAST_LEN)
    u = jax.random.uniform(ks.pop(), (DEPTH, D_RNN), f32, minval=0.9, maxval=0.999)
    s = u ** (1.0 / LRU_C)
    rg_lambda = jnp.log(s) - jnp.log1p(-s)
    return {
        'x_prompt': nrm((BATCH, SEQ, D_MODEL), 1.0),
        'x_sample': nrm((DEC_BATCH, DEC_SEQ, D_MODEL), 1.0),
        'cache_k': nrm((DEPTH, DEC_BATCH, wb, N_KV, HEAD_DIM), 1.0),
        'cache_v': nrm((DEPTH, DEC_BATCH, wb, N_KV, HEAD_DIM), 1.0),
        'state_conv': nrm((DEPTH, DEC_BATCH, CONV_W - 1, D_RNN), 1.0),
        'state_h': nrm((DEPTH, DEC_BATCH, D_RNN), 0.5),
        'p_prompt': nrm((DEPTH, BATCH, SEQ, D_PLE), 1.0),
        'p_sample': nrm((DEPTH, DEC_BATCH, DEC_SEQ, D_PLE), 1.0),
        'ln_in_g': 1.0 + nrm((D_MODEL,), 0.02),
        'ln_in_b': nrm((D_MODEL,), 0.02),
        'rel_bias': nrm((N_BUCKETS, N_HEADS), 0.5),
        'w_in': nrm((DEPTH, D_MODEL, D_IN), D_MODEL ** -0.5),
        'conv_w': nrm((DEPTH, CONV_W, D_RNN), CONV_W ** -0.5),
        'conv_b': nrm((DEPTH, D_RNN), 0.02),
        'rg_wa': nrm((DEPTH, RNN_BLOCKS, RNN_BW, RNN_BW), RNN_BW ** -0.5),
        'rg_ba': nrm((DEPTH, D_RNN), 0.02),
        'rg_wx': nrm((DEPTH, RNN_BLOCKS, RNN_BW, RNN_BW), RNN_BW ** -0.5),
        'rg_bx': nrm((DEPTH, D_RNN), 0.02),
        'rg_lambda': rg_lambda,
        'attn_sinks': nrm((DEPTH, N_HEADS), 0.5),
        'gn_rnn': 1.0 + nrm((DEPTH, D_RNN), 0.02),
        'gn_attn': 1.0 + nrm((DEPTH, D_ATTN), 0.02),
        'w_out': nrm((DEPTH, D_MIX, D_MODEL), D_MIX ** -0.5 * BETA),
        'ln1_g': 1.0 + nrm((DEPTH, D_MODEL), 0.02),
        'ln1_b': nrm((DEPTH, D_MODEL), 0.02),
        'router_w': nrm((DEPTH, D_MODEL, N_EXPERTS), D_MODEL ** -0.5),
        'router_b': nrm((DEPTH, N_EXPERTS), 0.01),
        'moe_w1': nrm((DEPTH, N_EXPERTS, D_MODEL, 2 * D_FF), D_MODEL ** -0.5),
        'moe_b1': nrm((DEPTH, N_EXPERTS, 2 * D_FF), 0.02),
        'moe_w2': nrm((DEPTH, N_EXPERTS, D_FF, D_MODEL), D_FF ** -0.5 * BETA),
        'moe_b2': nrm((DEPTH, N_EXPERTS, D_MODEL), 0.02),
        'ple_w_gate': nrm((DEPTH, D_MODEL, D_MODEL), D_MODEL ** -0.5),
        'ple_b_gate': nrm((DEPTH, D_MODEL), 0.02),
        'ple_w_proj': nrm((DEPTH, D_PLE, D_MODEL), D_PLE ** -0.5 * BETA),
        'ln2_g': 1.0 + nrm((DEPTH, D_MODEL), 0.02),
        'ln2_b': nrm((DEPTH, D_MODEL), 0.02),
    }


def reference(x_prompt, x_sample, cache_k, cache_v, state_conv, state_h, p_prompt, p_sample,
              ln_in_g, ln_in_b, rel_bias, w_in, conv_w, conv_b, rg_wa, rg_ba, rg_wx, rg_bx, rg_lambda,
              attn_sinks, gn_rnn, gn_attn, w_out, ln1_g, ln1_b, router_w, router_b,
              moe_w1, moe_b1, moe_w2, moe_b2, ple_w_gate, ple_b_gate, ple_w_proj, ln2_g, ln2_b):
    xp = layer_norm(x_prompt, ln_in_g, ln_in_b)
    xs = layer_norm(x_sample, ln_in_g, ln_in_b)
    B = x_prompt.shape[0]
    kp, vp, cp, hp, ksm, vsm, csm, hsm = [], [], [], [], [], [], [], []
    for l in range(DEPTH):
        lw = {'w_in': w_in[l], 'conv_w': conv_w[l], 'conv_b': conv_b[l], 'rg_wa': rg_wa[l],
              'rg_ba': rg_ba[l], 'rg_wx': rg_wx[l], 'rg_bx': rg_bx[l], 'rg_lambda': rg_lambda[l],
              'attn_sinks': attn_sinks[l], 'gn_rnn': gn_rnn[l], 'gn_attn': gn_attn[l],
              'w_out': w_out[l], 'ln1_g': ln1_g[l], 'ln1_b': ln1_b[l], 'router_w': router_w[l],
              'router_b': router_b[l], 'moe_w1': moe_w1[l], 'moe_b1': moe_b1[l], 'moe_w2': moe_w2[l],
              'moe_b2': moe_b2[l], 'ple_w_gate': ple_w_gate[l], 'ple_b_gate': ple_b_gate[l],
              'ple_w_proj': ple_w_proj[l], 'ln2_g': ln2_g[l], 'ln2_b': ln2_b[l]}
        xp, k_l, v_l, c_l, h_l = decoder_layer(
            xp, p_prompt[l], jnp.zeros((B, CONV_W - 1, D_RNN), xp.dtype),
            jnp.zeros((B, D_RNN), jnp.float32), None, None, lw, rel_bias)
        kp.append(k_l); vp.append(v_l); cp.append(c_l); hp.append(h_l)
        xs, k_l, v_l, c_l, h_l = decoder_layer(
            xs, p_sample[l], state_conv[l], state_h[l], cache_k[l], cache_v[l], lw, rel_bias)
        ksm.append(k_l); vsm.append(v_l); csm.append(c_l); hsm.append(h_l)
    return (xp, xs, jnp.stack(kp), jnp.stack(vp), jnp.stack(cp), jnp.stack(hp),
            jnp.stack(ksm), jnp.stack(vsm), jnp.stack(csm), jnp.stack(hsm))
```

```python
import functools
import math

import jax
import jax.numpy as jnp
from jax import lax
from jax.experimental import pallas as pl
from jax.experimental.pallas import tpu as pltpu

F32 = jnp.float32
BF16 = jnp.bfloat16

D_MODEL = 1024
D_RNN = 512
RNN_BLOCKS = 8
CONV_W = 4
LRU_C = 8.0
N_HEADS = 8
HEAD_DIM = 64
N_KV = 2
GQA_R = N_HEADS // N_KV
D_ATTN = N_HEADS * HEAD_DIM
WINDOW = 128
N_BUCKETS = 32
MAX_DIST = 128
N_EXPERTS = 32
TOP_K = 4
D_FF = 1024
SWIGLU_LIMIT = 7.0
SWIGLU_ALPHA = 1.702
D_PLE = 256
LN_EPS = 1e-5
D_IN = 2 * D_RNN + D_ATTN + 2 * N_KV * HEAD_DIM
NEG = -1e30

SUBLANES = 8
ROW_TILE = 512
MOE_TILE = 256
VMEM_LIMIT = 56 * 1024 * 1024


def _cparams(*sem):
    return pltpu.CompilerParams(dimension_semantics=sem, vmem_limit_bytes=VMEM_LIMIT)


def _layer_norm(x, g, b):
    mu = jnp.mean(x, axis=-1, keepdims=True)
    xc = x - mu
    var = jnp.mean(xc * xc, axis=-1, keepdims=True)
    return xc * lax.rsqrt(var + LN_EPS) * g + b


def _rms_norm(x, g):
    return x * lax.rsqrt(jnp.mean(x * x, axis=-1, keepdims=True) + LN_EPS) * g


def _const_spec(shape):
    return pl.BlockSpec(shape, lambda *_: (0,) * len(shape))


def _in_proj_kernel(x_ref, g_ref, b_ref, w_ref, *out_refs, apply_ln):
    x = x_ref[...]
    if apply_ln:
        x = _layer_norm(x, g_ref[...], b_ref[...])
        out_refs[0][...] = x
    out_refs[-1][...] = jnp.dot(x.astype(BF16), w_ref[...], preferred_element_type=F32)


def _in_proj(x, ln_g, ln_b, w_in_bf16, apply_ln):
    n = x.shape[0]
    tm = min(ROW_TILE, n)
    row = lambda w: pl.BlockSpec((tm, w), lambda i: (i, 0))
    out_shape = [jax.ShapeDtypeStruct((n, D_IN), F32)]
    out_specs = [row(D_IN)]
    if apply_ln:
        out_shape.insert(0, jax.ShapeDtypeStruct((n, D_MODEL), F32))
        out_specs.insert(0, row(D_MODEL))
    outs = pl.pallas_call(
        functools.partial(_in_proj_kernel, apply_ln=apply_ln),
        grid=(n // tm,),
        in_specs=[row(D_MODEL), _const_spec((1, D_MODEL)), _const_spec((1, D_MODEL)),
                  _const_spec((D_MODEL, D_IN))],
        out_specs=out_specs, out_shape=out_shape,
        compiler_params=_cparams("parallel"), name="in_proj",
    )(x, ln_g, ln_b, w_in_bf16)
    return (outs[0], outs[1]) if apply_ln else (x, outs[0])


def _rglru_coeffs(u, wa_ref, ba_ref, wx_ref, bx_ref, lam_ref):
    ub = u.astype(BF16)
    r = jax.nn.sigmoid(jnp.dot(ub, wa_ref[...], preferred_element_type=F32) + ba_ref[...])
    i = jax.nn.sigmoid(jnp.dot(ub, wx_ref[...], preferred_element_type=F32) + bx_ref[...])
    lam = -lam_ref[...]
    softplus = jnp.maximum(lam, 0.0) + jnp.log1p(jnp.exp(-jnp.abs(lam)))
    log_a = -LRU_C * r * softplus
    a = jnp.exp(log_a)
    b = jnp.sqrt(-jnp.tanh(log_a) * (a * a + 1.0)) * (i * u)
    return a, b


def _rnn_prompt_kernel(xr_ref, gate_ref, cw_ref, cb_ref, wa_ref, ba_ref, wx_ref, bx_ref, lam_ref,
                       y_ref, hlast_ref, xs_ref, a_ref, b_ref, carry_ref, *, tt):
    t = pl.program_id(1)

    @pl.when(t == 0)
    def _():
        xs_ref[0:SUBLANES, :] = jnp.zeros((SUBLANES, D_RNN), F32)
        carry_ref[...] = jnp.zeros((SUBLANES, D_RNN), F32)

    x = xr_ref[...]
    xs_ref[SUBLANES:SUBLANES + tt, :] = x
    cw = cw_ref[...]
    u = cb_ref[...] + x * cw[3:4]
    for j in range(CONV_W - 1):
        off = SUBLANES - (CONV_W - 1) + j
        u = u + xs_ref[off:off + tt, :] * cw[j:j + 1]
    xs_ref[0:SUBLANES, :] = x[tt - SUBLANES:tt, :]

    a, b = _rglru_coeffs(u, wa_ref, ba_ref, wx_ref, bx_ref, lam_ref)
    ng = tt // SUBLANES
    a = a.reshape(ng, SUBLANES, D_RNN)
    b = b.reshape(ng, SUBLANES, D_RNN)
    row = lax.broadcasted_iota(jnp.int32, (1, SUBLANES, 1), 1)
    for s in (1, 2, 4):
        a_sh = pltpu.roll(a, s, 1)
        b_sh = pltpu.roll(b, s, 1)
        m = row >= s
        b = jnp.where(m, a * b_sh + b, b)
        a = jnp.where(m, a * a_sh, a)
    a_ref[...] = a
    b_ref[...] = b

    def body(g, carry):
        h = a_ref[g] * carry + b_ref[g]
        b_ref[g] = h
        return jnp.broadcast_to(h[SUBLANES - 1:SUBLANES, :], (SUBLANES, D_RNN))

    carry = lax.fori_loop(0, ng, body, carry_ref[...])
    carry_ref[...] = carry
    h = b_ref[...].reshape(tt, D_RNN)
    y_ref[...] = h * jax.nn.gelu(gate_ref[...])
    hlast_ref[...] = carry[0:1, :]


def _rnn_prompt(z3, rw):
    bsz, t_len, _ = z3.shape
    tt = min(ROW_TILE, t_len)
    vec = _const_spec((1, D_RNN))
    return pl.pallas_call(
        functools.partial(_rnn_prompt_kernel, tt=tt),
        grid=(bsz, t_len // tt),
        in_specs=[pl.BlockSpec((None, tt, D_RNN), lambda b, t: (b, t, 0)),
                  pl.BlockSpec((None, tt, D_RNN), lambda b, t: (b, t, 1)),
                  _const_spec((CONV_W, D_RNN)), vec, _const_spec((D_RNN, D_RNN)), vec,
                  _const_spec((D_RNN, D_RNN)), vec, vec],
        out_specs=[pl.BlockSpec((None, tt, D_RNN), lambda b, t: (b, t, 0)),
                   pl.BlockSpec((None, 1, D_RNN), lambda b, t: (b, 0, 0))],
        out_shape=[jax.ShapeDtypeStruct((bsz, t_len, D_RNN), F32),
                   jax.ShapeDtypeStruct((bsz, 1, D_RNN), F32)],
        scratch_shapes=[pltpu.VMEM((tt + SUBLANES, D_RNN), F32),
                        pltpu.VMEM((tt // SUBLANES, SUBLANES, D_RNN), F32),
                        pltpu.VMEM((tt // SUBLANES, SUBLANES, D_RNN), F32),
                        pltpu.VMEM((SUBLANES, D_RNN), F32)],
        compiler_params=_cparams("parallel", "arbitrary"), name="rnn_prompt",
    )(z3, z3, rw["conv_w"], rw["conv_b"], rw["wa"], rw["ba"], rw["wx"], rw["bx"], rw["lam"])


def _rnn_sample_kernel(xr_ref, gate_ref, c0_ref, c1_ref, c2_ref, h0_ref, cw_ref, cb_ref, wa_ref, ba_ref,
                       wx_ref, bx_ref, lam_ref, y_ref, h_ref):
    x = xr_ref[...]
    cw = cw_ref[...]
    u = (cb_ref[...] + c0_ref[...] * cw[0:1] + c1_ref[...] * cw[1:2] + c2_ref[...] * cw[2:3]
         + x * cw[3:4])
    a, b = _rglru_coeffs(u, wa_ref, ba_ref, wx_ref, bx_ref, lam_ref)
    h = a * h0_ref[...] + b
    h_ref[...] = h
    y_ref[...] = h * jax.nn.gelu(gate_ref[...])


def _rnn_sample(z, conv_state, h0, rw):
    n = z.shape[0]
    vec = _const_spec((1, D_RNN))
    full = _const_spec((n, D_RNN))
    cs = [conv_state[:, j] for j in range(CONV_W - 1)]
    return pl.pallas_call(
        _rnn_sample_kernel,
        grid=(1,),
        in_specs=[pl.BlockSpec((n, D_RNN), lambda i: (0, 0)), pl.BlockSpec((n, D_RNN), lambda i: (0, 1)),
                  full, full, full, full,
                  _const_spec((CONV_W, D_RNN)), vec, _const_spec((D_RNN, D_RNN)), vec,
                  _const_spec((D_RNN, D_RNN)), vec, vec],
        out_specs=[full, full],
        out_shape=[jax.ShapeDtypeStruct((n, D_RNN), F32)] * 2,
        compiler_params=_cparams("arbitrary"), name="rnn_sample",
    )(z, z, *cs, h0, rw["conv_w"], rw["conv_b"], rw["wa"], rw["ba"], rw["wx"], rw["bx"], rw["lam"])


def _rel_bucket(dist):
    n = jnp.maximum(dist, 0)
    max_exact = N_BUCKETS // 2
    nf = jnp.maximum(n, max_exact).astype(F32)
    large = max_exact + (jnp.log(nf / max_exact) / math.log(MAX_DIST / max_exact)
                         * (N_BUCKETS - max_exact)).astype(jnp.int32)
    large = jnp.minimum(large, N_BUCKETS - 1)
    return jnp.where(n < max_exact, n, large)


def _half_lane_variants(x, group):
    lo = lax.broadcasted_iota(jnp.int32, (1, 2 * HEAD_DIM), 1) < HEAD_DIM
    xr = pltpu.roll(x, HEAD_DIM, 1)
    zero = jnp.zeros_like(x)
    if group == 0:
        return jnp.where(lo, x, zero), jnp.where(lo, zero, xr)
    return jnp.where(lo, xr, zero), jnp.where(lo, zero, x)


def _attn_prompt_kernel(q_ref, kc_ref, kp_ref, vc_ref, vp_ref, bias_ref, sink_ref, o_ref):
    first = pl.program_id(1) == 0
    kk = jnp.concatenate([kp_ref[...], kc_ref[...]], axis=0)
    vv = jnp.concatenate([vp_ref[...], vc_ref[...]], axis=0)
    col = lax.broadcasted_iota(jnp.int32, (1, 2 * WINDOW), 1)
    hide_prev = jnp.logical_and(first, col < WINDOW)
    for g in range(N_KV):
        k_lo, k_hi = _half_lane_variants(kk, g)
        v_lo, v_hi = _half_lane_variants(vv, g)
        for pair in range(GQA_R // 2):
            p0 = g * (GQA_R // 2) + pair
            qp = (q_ref[:, p0 * 128:(p0 + 1) * 128] * (HEAD_DIM ** -0.5)).astype(BF16)
            acc = jnp.zeros((WINDOW, 2 * HEAD_DIM), F32)
            for kx, vx, h in ((k_lo, v_lo, 2 * p0), (k_hi, v_hi, 2 * p0 + 1)):
                s = lax.dot_general(qp, kx.astype(BF16), (((1,), (1,)), ((), ())),
                                    preferred_element_type=F32)
                s = jnp.where(hide_prev, NEG, s + bias_ref[h])
                sink = sink_ref[h]
                m = jnp.maximum(jnp.max(s, axis=-1, keepdims=True), sink)
                p = jnp.exp(s - m)
                denom = jnp.sum(p, axis=-1, keepdims=True) + jnp.exp(sink - m)
                acc = acc + jnp.dot(p.astype(BF16), vx.astype(BF16), preferred_element_type=F32) / denom
            o_ref[:, p0 * 128:(p0 + 1) * 128] = acc


def _attn_prompt(z, bsz, t_len, bias_prompt, sinks):
    nb = t_len // WINDOW
    kcol = (2 * D_RNN + D_ATTN) // 128
    cur = lambda c: pl.BlockSpec((WINDOW, 128), lambda b, i: (b * nb + i, c))
    prev = lambda c: pl.BlockSpec((WINDOW, 128), lambda b, i: (b * nb + jnp.maximum(i - 1, 0), c))
    return pl.pallas_call(
        _attn_prompt_kernel,
        grid=(bsz, nb),
        in_specs=[pl.BlockSpec((WINDOW, D_ATTN), lambda b, i: (b * nb + i, 2 * D_RNN // D_ATTN)),
                  cur(kcol), prev(kcol), cur(kcol + 1), prev(kcol + 1),
                  _const_spec((N_HEADS, WINDOW, 2 * WINDOW)),
                  pl.BlockSpec(memory_space=pltpu.SMEM)],
        out_specs=pl.BlockSpec((WINDOW, D_ATTN), lambda b, i: (b * nb + i, 0)),
        out_shape=jax.ShapeDtypeStruct((bsz * t_len, D_ATTN), F32),
        compiler_params=_cparams("parallel", "arbitrary"), name="attn_prompt",
    )(z, z, z, z, z, bias_prompt, sinks)


def _attn_sample_kernel(q_ref, kn_ref, vn_ref, ck_ref, cv_ref, bias_ref, sink_ref, hmask_ref,
                        o_ref, nk_ref, nv_ref):
    pos = lax.broadcasted_iota(jnp.int32, (1, WINDOW, 1), 1)
    last = pos == WINDOW - 1
    bb = ck_ref.shape[0]
    newk = jnp.where(last, jnp.broadcast_to(kn_ref[...], (bb, WINDOW, 128)), pltpu.roll(ck_ref[...], WINDOW - 1, 1))
    newv = jnp.where(last, jnp.broadcast_to(vn_ref[...], (bb, WINDOW, 128)), pltpu.roll(cv_ref[...], WINDOW - 1, 1))
    nk_ref[...] = newk
    nv_ref[...] = newv
    s = lax.dot_general(q_ref[...].astype(BF16), newk.astype(BF16), (((2,), (2,)), ((0,), (0,))),
                        preferred_element_type=F32)
    s = s + bias_ref[...][None]
    sink = sink_ref[...][None]
    m = jnp.maximum(jnp.max(s, axis=-1, keepdims=True), sink)
    p = jnp.exp(s - m)
    denom = jnp.sum(p, axis=-1, keepdims=True) + jnp.exp(sink - m)
    o = lax.dot_general(p.astype(BF16), newv.astype(BF16), (((2,), (1,)), ((0,), (0,))),
                        preferred_element_type=F32)
    o_ref[...] = o / denom * hmask_ref[...][None]


def _attn_sample(q, k_new, v_new, k_buf, v_buf, bias_sample, sinks):
    n = q.shape[0]
    bb = 16
    qh = q.reshape(n, N_KV, GQA_R, HEAD_DIM) * (HEAD_DIM ** -0.5)
    eye = jnp.eye(N_KV, dtype=F32)
    q8 = (qh[:, :, :, None, :] * eye[None, :, None, :, None]).reshape(n, N_HEADS, N_KV * HEAD_DIM)
    hmask = jnp.repeat(jnp.repeat(eye, GQA_R, axis=0), HEAD_DIM, axis=1)
    blk3 = lambda a, b: pl.BlockSpec((bb, a, b), lambda i: (i, 0, 0))
    o8, nk, nv = pl.pallas_call(
        _attn_sample_kernel,
        grid=(n // bb,),
        in_specs=[blk3(N_HEADS, 128), blk3(1, 128), blk3(1, 128), blk3(WINDOW, 128), blk3(WINDOW, 128),
                  _const_spec((N_HEADS, WINDOW)), _const_spec((N_HEADS, 1)), _const_spec((N_HEADS, 128))],
        out_specs=[blk3(N_HEADS, 128), blk3(WINDOW, 128), blk3(WINDOW, 128)],
        out_shape=[jax.ShapeDtypeStruct((n, N_HEADS, 128), F32),
                   jax.ShapeDtypeStruct((n, WINDOW, 128), F32),
                   jax.ShapeDtypeStruct((n, WINDOW, 128), F32)],
        compiler_params=_cparams("parallel"), name="attn_sample",
    )(q8, k_new.reshape(n, 1, 128), v_new.reshape(n, 1, 128), k_buf, v_buf, bias_sample,
      sinks.reshape(N_HEADS, 1), hmask)
    o = o8.reshape(n, N_KV, GQA_R, N_KV, HEAD_DIM)
    o = jnp.stack([o[:, g, :, g, :] for g in range(N_KV)], axis=1).reshape(n, D_ATTN)
    return o, nk, nv


def _out_proj_kernel(x_ref, yr_ref, o_ref, gnr_ref, gna_ref, wr_ref, wa_ref, g1_ref, b1_ref, rw_ref, rb_ref,
                     x1_ref, ti_ref, gt_ref, rk_ref, cnt_ref, carry_ref, *, tm, alpha):
    i = pl.program_id(0)

    @pl.when(i == 0)
    def _():
        carry_ref[...] = jnp.zeros((1, N_EXPERTS), F32)

    yn = _rms_norm(yr_ref[...], gnr_ref[...]).astype(BF16)
    on = _rms_norm(o_ref[...], gna_ref[...]).astype(BF16)
    mix = (jnp.dot(yn, wr_ref[...], preferred_element_type=F32)
           + jnp.dot(on, wa_ref[...], preferred_element_type=F32))
    x1 = _layer_norm(alpha * x_ref[...] + mix, g1_ref[...], b1_ref[...])
    x1_ref[...] = x1

    logits = jnp.dot(x1, rw_ref[...], preferred_element_type=F32, precision=lax.Precision.HIGHEST) + rb_ref[...]
    lane = lax.broadcasted_iota(jnp.int32, (tm, N_EXPERTS), 1).astype(F32)
    work = logits
    vals, idxs, hots = [], [], []
    for _ in range(TOP_K):
        v = jnp.max(work, axis=-1, keepdims=True)
        idx = jnp.min(jnp.where(work == v, lane, float(N_EXPERTS)), axis=-1, keepdims=True)
        hot = lane == idx
        work = jnp.where(hot, -jnp.inf, work)
        vals.append(v)
        idxs.append(idx)
        hots.append(hot.astype(F32))
    ex = [jnp.exp(v - vals[0]) for v in vals]
    tot = ex[0] + ex[1] + ex[2] + ex[3]
    tok_hot = hots[0] + hots[1] + hots[2] + hots[3]
    r_i = lax.broadcasted_iota(jnp.int32, (tm, tm), 0)
    c_i = lax.broadcasted_iota(jnp.int32, (tm, tm), 1)
    lower = (c_i < r_i).astype(BF16)
    before = jnp.dot(lower, tok_hot.astype(BF16), preferred_element_type=F32) + carry_ref[...]
    for k in range(TOP_K):
        ti_ref[:, k:k + 1] = idxs[k].astype(jnp.int32)
        gt_ref[:, k:k + 1] = ex[k] / tot
        rk_ref[:, k:k + 1] = jnp.sum(hots[k] * before, axis=-1, keepdims=True).astype(jnp.int32)
    carry = carry_ref[...] + jnp.sum(tok_hot, axis=0, keepdims=True)
    carry_ref[...] = carry
    cnt_ref[...] = carry.astype(jnp.int32)


def _out_proj(x, y_rnn, o, lw):
    n = x.shape[0]
    tm = min(ROW_TILE, n)
    row = lambda w: pl.BlockSpec((tm, w), lambda i: (i, 0))
    vec = lambda w: _const_spec((1, w))
    return pl.pallas_call(
        functools.partial(_out_proj_kernel, tm=tm, alpha=lw["alpha"]),
        grid=(n // tm,),
        in_specs=[row(D_MODEL), row(D_RNN), row(D_ATTN), vec(D_RNN), vec(D_ATTN),
                  _const_spec((D_RNN, D_MODEL)), _const_spec((D_ATTN, D_MODEL)), vec(D_MODEL), vec(D_MODEL),
                  _const_spec((D_MODEL, N_EXPERTS)), vec(N_EXPERTS)],
        out_specs=[row(D_MODEL), row(TOP_K), row(TOP_K), row(TOP_K), vec(N_EXPERTS)],
        out_shape=[jax.ShapeDtypeStruct((n, D_MODEL), F32), jax.ShapeDtypeStruct((n, TOP_K), jnp.int32),
                   jax.ShapeDtypeStruct((n, TOP_K), F32), jax.ShapeDtypeStruct((n, TOP_K), jnp.int32),
                   jax.ShapeDtypeStruct((1, N_EXPERTS), jnp.int32)],
        scratch_shapes=[pltpu.VMEM((1, N_EXPERTS), F32)],
        compiler_params=_cparams("arbitrary"), name="out_proj_route",
    )(x, y_rnn, o, lw["gn_rnn"], lw["gn_attn"], lw["w_out_rnn"], lw["w_out_attn"], lw["ln1_g"], lw["ln1_b"],
      lw["router_w"], lw["router_b"])


def _scatter_rows(n_rows, ti_ref, rk_ref, base_ref, x_ref, xbuf_ref, sem):
    def row_copy(src_row, dst_row):
        return pltpu.make_async_copy(x_ref.at[pl.ds(src_row, 1)], xbuf_ref.at[pl.ds(dst_row, 1)], sem)

    def start(j, c):
        dst = base_ref[ti_ref[0, 0, j]] + rk_ref[0, 0, j]
        row_copy(j // TOP_K, dst).start()
        return c

    def wait(j, c):
        row_copy(0, 0).wait()
        return c

    lax.fori_loop(0, n_rows * TOP_K, start, 0)
    lax.fori_loop(0, n_rows * TOP_K, wait, 0)


def _dispatch_kernel(tip_ref, rkp_ref, tis_ref, rks_ref, basep_ref, bases_ref, xp_ref, xs_ref, xbuf_ref, sem,
                     *, tm, ns):
    _scatter_rows(tm, tip_ref, rkp_ref, basep_ref, xp_ref, xbuf_ref, sem)

    @pl.when(pl.program_id(0) == pl.num_programs(0) - 1)
    def _():
        _scatter_rows(ns, tis_ref, rks_ref, bases_ref, xs_ref, xbuf_ref, sem)


def _dispatch(x1p, x1s, tip, rkp, tis, rks, base_p, base_s, n_rows):
    n_p, n_s = x1p.shape[0], x1s.shape[0]
    tm = ROW_TILE
    smem_tile = lambda w, f: pl.BlockSpec((1, 1, w), f, memory_space=pltpu.SMEM)
    smem = pl.BlockSpec(memory_space=pltpu.SMEM)
    return pl.pallas_call(
        functools.partial(_dispatch_kernel, tm=tm, ns=n_s),
        grid=(n_p // tm,),
        in_specs=[smem_tile(tm * TOP_K, lambda i: (i, 0, 0)), smem_tile(tm * TOP_K, lambda i: (i, 0, 0)),
                  smem_tile(n_s * TOP_K, lambda i: (0, 0, 0)), smem_tile(n_s * TOP_K, lambda i: (0, 0, 0)),
                  smem, smem,
                  pl.BlockSpec((tm, D_MODEL), lambda i: (i, 0)), _const_spec((n_s, D_MODEL))],
        out_specs=pl.BlockSpec(memory_space=pl.ANY),
        out_shape=jax.ShapeDtypeStruct((n_rows, D_MODEL), F32),
        scratch_shapes=[pltpu.SemaphoreType.DMA],
        compiler_params=_cparams("arbitrary"), name="moe_dispatch",
    )(tip.reshape(n_p // tm, 1, tm * TOP_K), rkp.reshape(n_p // tm, 1, tm * TOP_K),
      tis.reshape(1, 1, n_s * TOP_K), rks.reshape(1, 1, n_s * TOP_K), base_p, base_s, x1p, x1s)


def _moe_kernel(be_ref, nv_ref, x_ref, w1_ref, b1_ref, w2_ref, b2_ref, y_ref, w1s_ref, w2s_ref):
    i = pl.program_id(0)
    e = be_ref[i]
    e_prev = be_ref[jnp.maximum(i - 1, 0)]

    @pl.when(jnp.logical_or(i == 0, e != e_prev))
    def _():
        w1s_ref[...] = w1_ref[0].astype(BF16)
        w2s_ref[...] = w2_ref[0].astype(BF16)

    nv = nv_ref[i]

    @pl.when(nv > 0)
    def _():
        rows = lax.broadcasted_iota(jnp.int32, (MOE_TILE, 1), 0)
        x = jnp.where(rows < nv, x_ref[...], 0.0).astype(BF16)
        hdn = jnp.dot(x, w1s_ref[...], preferred_element_type=F32) + b1_ref[0]
        g = jnp.minimum(hdn[:, :D_FF], SWIGLU_LIMIT)
        lin = jnp.clip(hdn[:, D_FF:], -SWIGLU_LIMIT, SWIGLU_LIMIT)
        act = g * jax.nn.sigmoid(SWIGLU_ALPHA * g) * (lin + 1.0)
        y_ref[...] = jnp.dot(act.astype(BF16), w2s_ref[...], preferred_element_type=F32) + b2_ref[0]

    @pl.when(nv == 0)
    def _():
        y_ref[...] = jnp.zeros((MOE_TILE, D_MODEL), F32)


def _moe_experts(xbuf, blk_e, blk_nv, w1, b1, w2, b2):
    n_blocks = xbuf.shape[0] // MOE_TILE
    grid_spec = pltpu.PrefetchScalarGridSpec(
        num_scalar_prefetch=2,
        grid=(n_blocks,),
        in_specs=[pl.BlockSpec((MOE_TILE, D_MODEL), lambda i, be, nv: (i, 0)),
                  pl.BlockSpec((1, D_MODEL, 2 * D_FF), lambda i, be, nv: (be[i], 0, 0)),
                  pl.BlockSpec((1, 1, 2 * D_FF), lambda i, be, nv: (be[i], 0, 0)),
                  pl.BlockSpec((1, D_FF, D_MODEL), lambda i, be, nv: (be[i], 0, 0)),
                  pl.BlockSpec((1, 1, D_MODEL), lambda i, be, nv: (be[i], 0, 0))],
        out_specs=pl.BlockSpec((MOE_TILE, D_MODEL), lambda i, be, nv: (i, 0)),
        scratch_shapes=[pltpu.VMEM((D_MODEL, 2 * D_FF), BF16), pltpu.VMEM((D_FF, D_MODEL), BF16)],
    )
    return pl.pallas_call(
        _moe_kernel, grid_spec=grid_spec,
        out_shape=jax.ShapeDtypeStruct(xbuf.shape, F32),
        compiler_params=_cparams("arbitrary"), name="moe_experts",
    )(blk_e, blk_nv, xbuf, w1, b1.reshape(N_EXPERTS, 1, 2 * D_FF), w2, b2.reshape(N_EXPERTS, 1, D_MODEL))


def _final_kernel(ti_ref, rk_ref, base_ref, x_ref, p_ref, gt_ref, wg_ref, bg_ref, wp_ref, g2_ref,
                  b2_ref, ybuf_ref, out_ref, rows_ref, sem, *, tm, alpha):
    def row_copy(src_row, k, dst_row):
        return pltpu.make_async_copy(ybuf_ref.at[pl.ds(src_row, 1)], rows_ref.at[k, pl.ds(dst_row, 1)], sem)

    def start(t, c):
        for k in range(TOP_K):
            j = t * TOP_K + k
            row_copy(base_ref[ti_ref[0, 0, j]] + rk_ref[0, 0, j], k, t).start()
        return c

    def wait(j, c):
        row_copy(0, 0, 0).wait()
        return c

    lax.fori_loop(0, tm, start, 0)
    x = x_ref[...]
    ple = (jax.nn.sigmoid(jnp.dot(x.astype(BF16), wg_ref[...], preferred_element_type=F32) + bg_ref[...])
           * jnp.dot(p_ref[...].astype(BF16), wp_ref[...], preferred_element_type=F32))
    lax.fori_loop(0, tm * TOP_K, wait, 0)
    gt = gt_ref[...]
    ffn = rows_ref[0] * gt[:, 0:1]
    for k in range(1, TOP_K):
        ffn = ffn + rows_ref[k] * gt[:, k:k + 1]
    out_ref[...] = _layer_norm(alpha * x + ffn + ple, g2_ref[...], b2_ref[...])


def _final(x1, p, gates, ti, rk, base, ybuf, lw):
    n = x1.shape[0]
    tm = min(MOE_TILE, n)
    row = lambda w: pl.BlockSpec((tm, w), lambda i: (i, 0))
    vec = lambda w: _const_spec((1, w))
    smem_tile = pl.BlockSpec((1, 1, tm * TOP_K), lambda i: (i, 0, 0), memory_space=pltpu.SMEM)
    smem = pl.BlockSpec(memory_space=pltpu.SMEM)
    return pl.pallas_call(
        functools.partial(_final_kernel, tm=tm, alpha=lw["alpha"]),
        grid=(n // tm,),
        in_specs=[smem_tile, smem_tile, smem, row(D_MODEL), row(D_PLE), row(TOP_K),
                  _const_spec((D_MODEL, D_MODEL)), vec(D_MODEL), _const_spec((D_PLE, D_MODEL)),
                  vec(D_MODEL), vec(D_MODEL), pl.BlockSpec(memory_space=pl.ANY)],
        out_specs=row(D_MODEL),
        out_shape=jax.ShapeDtypeStruct((n, D_MODEL), F32),
        scratch_shapes=[pltpu.VMEM((TOP_K, tm, D_MODEL), F32), pltpu.SemaphoreType.DMA],
        compiler_params=_cparams("arbitrary"), name="moe_combine_ln2",
    )(ti.reshape(n // tm, 1, tm * TOP_K), rk.reshape(n // tm, 1, tm * TOP_K), base, x1, p, gates,
      lw["ple_w_gate"], lw["ple_b_gate"], lw["ple_w_proj"], lw["ln2_g"], lw["ln2_b"], ybuf)


def _block_diag(w):
    nb, bw, _ = w.shape
    eye = jnp.eye(nb, dtype=w.dtype)
    return (w[:, :, None, :] * eye[:, None, :, None]).reshape(nb * bw, nb * bw)


def kernel(x_prompt, x_sample, cache_k, cache_v, state_conv, state_h, p_prompt, p_sample, ln_in_g, ln_in_b, rel_bias, w_in, conv_w, conv_b, rg_wa, rg_ba, rg_wx, rg_bx, rg_lambda, attn_sinks, gn_rnn, gn_attn, w_out, ln1_g, ln1_b, router_w, router_b, moe_w1, moe_b1, moe_w2, moe_b2, ple_w_gate, ple_b_gate, ple_w_proj, ln2_g, ln2_b):
    depth = w_in.shape[0]
    bsz, t_len, _ = x_prompt.shape
    n_p = bsz * t_len
    n_s = x_sample.shape[0]
    alpha = (2 * depth) ** 0.25

    qi = jnp.arange(WINDOW)[:, None]
    kj = jnp.arange(2 * WINDOW)[None, :]
    dist = qi + WINDOW - kj
    valid = (dist >= 0) & (dist < WINDOW)
    bias_prompt = jnp.where(valid[None], rel_bias.astype(F32)[_rel_bucket(dist)].transpose(2, 0, 1), NEG)
    dist_s = WINDOW - 1 - jnp.arange(WINDOW)
    bias_sample = rel_bias.astype(F32)[_rel_bucket(dist_s)].T

    n_assign = (n_p + n_s) * TOP_K
    n_blocks = -(-(n_assign + N_EXPERTS * (MOE_TILE - 1)) // MOE_TILE)
    blk_ids = jnp.arange(n_blocks, dtype=jnp.int32)

    xp = x_prompt.reshape(n_p, D_MODEL)
    xs = x_sample.reshape(n_s, D_MODEL)
    row2 = lambda v: v.reshape(1, -1)
    outs = {k: [] for k in ("kp", "vp", "cp", "hp", "ks", "vs", "cs", "hs")}
    for l in range(depth):
        rw = {"conv_w": conv_w[l], "conv_b": row2(conv_b[l]), "wa": _block_diag(rg_wa[l]).astype(BF16),
              "ba": row2(rg_ba[l]), "wx": _block_diag(rg_wx[l]).astype(BF16), "bx": row2(rg_bx[l]),
              "lam": row2(rg_lambda[l])}
        lw = {"gn_rnn": row2(gn_rnn[l]), "gn_attn": row2(gn_attn[l]),
              "w_out_rnn": w_out[l, :D_RNN].astype(BF16), "w_out_attn": w_out[l, D_RNN:].astype(BF16),
              "ln1_g": row2(ln1_g[l]), "ln1_b": row2(ln1_b[l]), "router_w": router_w[l],
              "router_b": row2(router_b[l]), "alpha": alpha,
              "ple_w_gate": ple_w_gate[l].astype(BF16), "ple_b_gate": row2(ple_b_gate[l]),
              "ple_w_proj": ple_w_proj[l].astype(BF16), "ln2_g": row2(ln2_g[l]), "ln2_b": row2(ln2_b[l])}
        w_in_l = w_in[l].astype(BF16)
        first = l == 0

        xp, zp = _in_proj(xp, row2(ln_in_g), row2(ln_in_b), w_in_l, first)
        yp, hp = _rnn_prompt(zp.reshape(bsz, t_len, D_IN), rw)
        op = _attn_prompt(zp, bsz, t_len, bias_prompt, attn_sinks[l])
        zp3 = zp.reshape(bsz, t_len, D_IN)
        kv0 = 2 * D_RNN + D_ATTN
        wp = min(WINDOW, t_len)
        outs["kp"].append(zp3[:, t_len - wp:, kv0:kv0 + 128].reshape(bsz, wp, N_KV, HEAD_DIM))
        outs["vp"].append(zp3[:, t_len - wp:, kv0 + 128:kv0 + 256].reshape(bsz, wp, N_KV, HEAD_DIM))
        outs["cp"].append(zp3[:, t_len - (CONV_W - 1):, :D_RNN])
        outs["hp"].append(hp.reshape(bsz, D_RNN))

        xs, zs = _in_proj(xs, row2(ln_in_g), row2(ln_in_b), w_in_l, first)
        ys, hs = _rnn_sample(zs, state_conv[l], state_h[l], rw)
        wb = cache_k.shape[2]
        os_, nk, nv = _attn_sample(zs[:, 2 * D_RNN:kv0], zs[:, kv0:kv0 + 128], zs[:, kv0 + 128:],
                                   cache_k[l].reshape(n_s, wb, 128), cache_v[l].reshape(n_s, wb, 128),
                                   bias_sample, attn_sinks[l])
        outs["ks"].append(nk.reshape(n_s, wb, N_KV, HEAD_DIM))
        outs["vs"].append(nv.reshape(n_s, wb, N_KV, HEAD_DIM))
        outs["cs"].append(jnp.concatenate([state_conv[l][:, 1:], zs[:, None, :D_RNN]], axis=1))
        outs["hs"].append(hs)

        x1p, tip, gtp, rkp, cntp = _out_proj(xp, yp.reshape(n_p, D_RNN), op, lw)
        x1s, tis, gts, rks, cnts = _out_proj(xs, ys, os_, lw)

        cnt_p, cnt_s = cntp[0], cnts[0]
        cnt = cnt_p + cnt_s
        nblk = (cnt + MOE_TILE - 1) // MOE_TILE
        blk_end = jnp.cumsum(nblk)
        blk_start = blk_end - nblk
        base_p = (blk_start * MOE_TILE).astype(jnp.int32)
        base_s = base_p + cnt_p
        n_valid = blk_end[-1]
        be = jnp.minimum(jnp.searchsorted(blk_end, blk_ids, side="right"), N_EXPERTS - 1).astype(jnp.int32)
        be_last = jnp.minimum(jnp.searchsorted(blk_end, n_valid - 1, side="right"), N_EXPERTS - 1).astype(jnp.int32)
        is_valid = blk_ids < n_valid
        blk_e = jnp.where(is_valid, be, be_last)
        blk_nv = jnp.where(is_valid, jnp.clip(cnt[be] - (blk_ids - blk_start[be]) * MOE_TILE, 0, MOE_TILE), 0)
        blk_nv = blk_nv.astype(jnp.int32)

        xbuf = _dispatch(x1p, x1s, tip, rkp, tis, rks, base_p, base_s, n_blocks * MOE_TILE)
        ybuf = _moe_experts(xbuf, blk_e, blk_nv, moe_w1[l], moe_b1[l], moe_w2[l], moe_b2[l])
        xp = _final(x1p, p_prompt[l].reshape(n_p, D_PLE), gtp, tip, rkp, base_p, ybuf, lw)
        xs = _final(x1s, p_sample[l].reshape(n_s, D_PLE), gts, tis, rks, base_s, ybuf, lw)

    st = lambda k: jnp.stack(outs[k])
    return (xp.reshape(bsz, t_len, D_MODEL), xs.reshape(n_s, 1, D_MODEL),
            st("kp"), st("vp"), st("cp"), st("hp"), st("ks"), st("vs"), st("cs"), st("hs"))
```

```python
import functools
import math

import jax
import jax.numpy as jnp
from jax import lax
from jax.experimental import pallas as pl
from jax.experimental.pallas import tpu as pltpu

F32 = jnp.float32
BF16 = jnp.bfloat16
I32 = jnp.int32
U32 = jnp.uint32

D_MODEL = 1024
D_RNN = 512
RNN_BLOCKS = 8
CONV_W = 4
LRU_C = 8.0
N_HEADS = 8
HEAD_DIM = 64
N_KV = 2
GQA_R = N_HEADS // N_KV
D_ATTN = N_HEADS * HEAD_DIM
WINDOW = 128
N_BUCKETS = 32
MAX_DIST = 128
N_EXPERTS = 32
TOP_K = 4
D_FF = 1024
SWIGLU_LIMIT = 7.0
SWIGLU_ALPHA = 1.702
D_PLE = 256
LN_EPS = 1e-5
D_IN = 2 * D_RNN + D_ATTN + 2 * N_KV * HEAD_DIM
NEG = -1e30

SUBLANES = 8
LANES = 128
ROW_TILE = 512
MOE_TILE = 256
D_PACK = D_MODEL // 2
VMEM_LIMIT = 56 * 1024 * 1024
HI_MASK = 0xFFFF0000


def _cparams(*sem):
    return pltpu.CompilerParams(dimension_semantics=sem, vmem_limit_bytes=VMEM_LIMIT)


def _layer_norm(x, g, b):
    mu = jnp.mean(x, axis=-1, keepdims=True)
    xc = x - mu
    var = jnp.mean(xc * xc, axis=-1, keepdims=True)
    return xc * lax.rsqrt(var + LN_EPS) * g + b


def _rms_norm(x, g):
    return x * lax.rsqrt(jnp.mean(x * x, axis=-1, keepdims=True) + LN_EPS) * g


def _const_spec(shape):
    return pl.BlockSpec(shape, lambda *_: (0,) * len(shape))


def _in_proj_kernel(x_ref, g_ref, b_ref, w_ref, *out_refs, apply_ln):
    x = x_ref[...]
    if apply_ln:
        x = _layer_norm(x, g_ref[...], b_ref[...])
        out_refs[0][...] = x
    out_refs[-1][...] = jnp.dot(x.astype(BF16), w_ref[...], preferred_element_type=F32)


def _in_proj(x, ln_g, ln_b, w_in_bf16, apply_ln):
    n = x.shape[0]
    tm = min(ROW_TILE, n)
    row = lambda w: pl.BlockSpec((tm, w), lambda i: (i, 0))
    out_shape = [jax.ShapeDtypeStruct((n, D_IN), F32)]
    out_specs = [row(D_IN)]
    if apply_ln:
        out_shape.insert(0, jax.ShapeDtypeStruct((n, D_MODEL), F32))
        out_specs.insert(0, row(D_MODEL))
    outs = pl.pallas_call(
        functools.partial(_in_proj_kernel, apply_ln=apply_ln),
        grid=(n // tm,),
        in_specs=[row(D_MODEL), _const_spec((1, D_MODEL)), _const_spec((1, D_MODEL)),
                  _const_spec((D_MODEL, D_IN))],
        out_specs=out_specs, out_shape=out_shape,
        compiler_params=_cparams("parallel"), name="in_proj",
    )(x, ln_g, ln_b, w_in_bf16)
    return (outs[0], outs[1]) if apply_ln else (x, outs[0])


def _rglru_coeffs(u, wa_ref, ba_ref, wx_ref, bx_ref, lam_ref):
    ub = u.astype(BF16)
    r = jax.nn.sigmoid(jnp.dot(ub, wa_ref[...], preferred_element_type=F32) + ba_ref[...])
    i = jax.nn.sigmoid(jnp.dot(ub, wx_ref[...], preferred_element_type=F32) + bx_ref[...])
    lam = -lam_ref[...]
    softplus = jnp.maximum(lam, 0.0) + jnp.log1p(jnp.exp(-jnp.abs(lam)))
    log_a = -LRU_C * r * softplus
    a = jnp.exp(log_a)
    b = jnp.sqrt(-jnp.tanh(log_a) * (a * a + 1.0)) * (i * u)
    return a, b


def _rnn_prompt_kernel(xr_ref, gate_ref, cw_ref, cb_ref, wa_ref, ba_ref, wx_ref, bx_ref, lam_ref,
                       y_ref, hlast_ref, xs_ref, a_ref, b_ref, carry_ref, *, tt):
    t = pl.program_id(1)

    @pl.when(t == 0)
    def _():
        xs_ref[0:SUBLANES, :] = jnp.zeros((SUBLANES, D_RNN), F32)
        carry_ref[...] = jnp.zeros((SUBLANES, D_RNN), F32)

    x = xr_ref[...]
    xs_ref[SUBLANES:SUBLANES + tt, :] = x
    cw = cw_ref[...]
    u = cb_ref[...] + x * cw[3:4]
    for j in range(CONV_W - 1):
        off = SUBLANES - (CONV_W - 1) + j
        u = u + xs_ref[off:off + tt, :] * cw[j:j + 1]
    xs_ref[0:SUBLANES, :] = x[tt - SUBLANES:tt, :]

    a, b = _rglru_coeffs(u, wa_ref, ba_ref, wx_ref, bx_ref, lam_ref)
    ng = tt // SUBLANES
    a = a.reshape(ng, SUBLANES, D_RNN)
    b = b.reshape(ng, SUBLANES, D_RNN)
    row = lax.broadcasted_iota(I32, (1, SUBLANES, 1), 1)
    for s in (1, 2, 4):
        a_sh = pltpu.roll(a, s, 1)
        b_sh = pltpu.roll(b, s, 1)
        m = row >= s
        b = jnp.where(m, a * b_sh + b, b)
        a = jnp.where(m, a * a_sh, a)
    a_ref[...] = a
    b_ref[...] = b

    def body(g, carry):
        h = a_ref[g] * carry + b_ref[g]
        b_ref[g] = h
        return jnp.broadcast_to(h[SUBLANES - 1:SUBLANES, :], (SUBLANES, D_RNN))

    carry = lax.fori_loop(0, ng, body, carry_ref[...])
    carry_ref[...] = carry
    h = b_ref[...].reshape(tt, D_RNN)
    y_ref[...] = h * jax.nn.gelu(gate_ref[...])
    hlast_ref[...] = carry[0:1, :]


def _rnn_prompt(z3, rw):
    bsz, t_len, _ = z3.shape
    tt = min(ROW_TILE, t_len)
    vec = _const_spec((1, D_RNN))
    return pl.pallas_call(
        functools.partial(_rnn_prompt_kernel, tt=tt),
        grid=(bsz, t_len // tt),
        in_specs=[pl.BlockSpec((None, tt, D_RNN), lambda b, t: (b, t, 0)),
                  pl.BlockSpec((None, tt, D_RNN), lambda b, t: (b, t, 1)),
                  _const_spec((CONV_W, D_RNN)), vec, _const_spec((D_RNN, D_RNN)), vec,
                  _const_spec((D_RNN, D_RNN)), vec, vec],
        out_specs=[pl.BlockSpec((None, tt, D_RNN), lambda b, t: (b, t, 0)),
                   pl.BlockSpec((None, 1, D_RNN), lambda b, t: (b, 0, 0))],
        out_shape=[jax.ShapeDtypeStruct((bsz, t_len, D_RNN), F32),
                   jax.ShapeDtypeStruct((bsz, 1, D_RNN), F32)],
        scratch_shapes=[pltpu.VMEM((tt + SUBLANES, D_RNN), F32),
                        pltpu.VMEM((tt // SUBLANES, SUBLANES, D_RNN), F32),
                        pltpu.VMEM((tt // SUBLANES, SUBLANES, D_RNN), F32),
                        pltpu.VMEM((SUBLANES, D_RNN), F32)],
        compiler_params=_cparams("parallel", "arbitrary"), name="rnn_prompt",
    )(z3, z3, rw["conv_w"], rw["conv_b"], rw["wa"], rw["ba"], rw["wx"], rw["bx"], rw["lam"])


def _rnn_sample_kernel(xr_ref, gate_ref, c0_ref, c1_ref, c2_ref, h0_ref, cw_ref, cb_ref, wa_ref, ba_ref,
                       wx_ref, bx_ref, lam_ref, y_ref, h_ref):
    x = xr_ref[...]
    cw = cw_ref[...]
    u = (cb_ref[...] + c0_ref[...] * cw[0:1] + c1_ref[...] * cw[1:2] + c2_ref[...] * cw[2:3]
         + x * cw[3:4])
    a, b = _rglru_coeffs(u, wa_ref, ba_ref, wx_ref, bx_ref, lam_ref)
    h = a * h0_ref[...] + b
    h_ref[...] = h
    y_ref[...] = h * jax.nn.gelu(gate_ref[...])


def _rnn_sample(z, conv_state, h0, rw):
    n = z.shape[0]
    vec = _const_spec((1, D_RNN))
    full = _const_spec((n, D_RNN))
    cs = [conv_state[:, j] for j in range(CONV_W - 1)]
    return pl.pallas_call(
        _rnn_sample_kernel,
        grid=(1,),
        in_specs=[pl.BlockSpec((n, D_RNN), lambda i: (0, 0)), pl.BlockSpec((n, D_RNN), lambda i: (0, 1)),
                  full, full, full, full,
                  _const_spec((CONV_W, D_RNN)), vec, _const_spec((D_RNN, D_RNN)), vec,
                  _const_spec((D_RNN, D_RNN)), vec, vec],
        out_specs=[full, full],
        out_shape=[jax.ShapeDtypeStruct((n, D_RNN), F32)] * 2,
        compiler_params=_cparams("arbitrary"), name="rnn_sample",
    )(z, z, *cs, h0, rw["conv_w"], rw["conv_b"], rw["wa"], rw["ba"], rw["wx"], rw["bx"], rw["lam"])


def _rel_bucket(dist):
    n = jnp.maximum(dist, 0)
    max_exact = N_BUCKETS // 2
    nf = jnp.maximum(n, max_exact).astype(F32)
    large = max_exact + (jnp.log(nf / max_exact) / math.log(MAX_DIST / max_exact)
                         * (N_BUCKETS - max_exact)).astype(I32)
    large = jnp.minimum(large, N_BUCKETS - 1)
    return jnp.where(n < max_exact, n, large)


def _bias_lookup(rel_bias, dist):
    hot = (_rel_bucket(dist)[..., None] == jnp.arange(N_BUCKETS)).astype(F32)
    out = jnp.tensordot(hot, rel_bias.astype(F32), axes=1, precision=lax.Precision.HIGHEST)
    return jnp.moveaxis(out, -1, 0)


def _half_lane_variants(x, group):
    lo = lax.broadcasted_iota(I32, (1, 2 * HEAD_DIM), 1) < HEAD_DIM
    xr = pltpu.roll(x, HEAD_DIM, 1)
    zero = jnp.zeros_like(x)
    if group == 0:
        return jnp.where(lo, x, zero), jnp.where(lo, zero, xr)
    return jnp.where(lo, xr, zero), jnp.where(lo, zero, x)


def _attn_prompt_kernel(q_ref, kc_ref, kp_ref, vc_ref, vp_ref, bias_ref, sink_ref, o_ref):
    first = pl.program_id(1) == 0
    kk = jnp.concatenate([kp_ref[...], kc_ref[...]], axis=0)
    vv = jnp.concatenate([vp_ref[...], vc_ref[...]], axis=0)
    col = lax.broadcasted_iota(I32, (1, 2 * WINDOW), 1)
    hide_prev = jnp.logical_and(first, col < WINDOW)
    for g in range(N_KV):
        k_lo, k_hi = _half_lane_variants(kk, g)
        v_lo, v_hi = _half_lane_variants(vv, g)
        for pair in range(GQA_R // 2):
            p0 = g * (GQA_R // 2) + pair
            qp = (q_ref[:, p0 * 128:(p0 + 1) * 128] * (HEAD_DIM ** -0.5)).astype(BF16)
            acc = jnp.zeros((WINDOW, 2 * HEAD_DIM), F32)
            for kx, vx, h in ((k_lo, v_lo, 2 * p0), (k_hi, v_hi, 2 * p0 + 1)):
                s = lax.dot_general(qp, kx.astype(BF16), (((1,), (1,)), ((), ())),
                                    preferred_element_type=F32)
                s = jnp.where(hide_prev, NEG, s + bias_ref[h])
                sink = sink_ref[h]
                m = jnp.maximum(jnp.max(s, axis=-1, keepdims=True), sink)
                p = jnp.exp(s - m)
                denom = jnp.sum(p, axis=-1, keepdims=True) + jnp.exp(sink - m)
                acc = acc + jnp.dot(p.astype(BF16), vx.astype(BF16), preferred_element_type=F32) / denom
            o_ref[:, p0 * 128:(p0 + 1) * 128] = acc


def _attn_prompt(z, bsz, t_len, bias_prompt, sinks):
    nb = t_len // WINDOW
    kcol = (2 * D_RNN + D_ATTN) // 128
    cur = lambda c: pl.BlockSpec((WINDOW, 128), lambda b, i: (b * nb + i, c))
    prev = lambda c: pl.BlockSpec((WINDOW, 128), lambda b, i: (b * nb + jnp.maximum(i - 1, 0), c))
    return pl.pallas_call(
        _attn_prompt_kernel,
        grid=(bsz, nb),
        in_specs=[pl.BlockSpec((WINDOW, D_ATTN), lambda b, i: (b * nb + i, 2 * D_RNN // D_ATTN)),
                  cur(kcol), prev(kcol), cur(kcol + 1), prev(kcol + 1),
                  _const_spec((N_HEADS, WINDOW, 2 * WINDOW)),
                  pl.BlockSpec(memory_space=pltpu.SMEM)],
        out_specs=pl.BlockSpec((WINDOW, D_ATTN), lambda b, i: (b * nb + i, 0)),
        out_shape=jax.ShapeDtypeStruct((bsz * t_len, D_ATTN), F32),
        compiler_params=_cparams("parallel", "arbitrary"), name="attn_prompt",
    )(z, z, z, z, z, bias_prompt, sinks)


def _attn_sample_kernel(q_ref, kn_ref, vn_ref, ck_ref, cv_ref, bias_ref, sink_ref, hmask_ref,
                        o_ref, nk_ref, nv_ref):
    pos = lax.broadcasted_iota(I32, (1, WINDOW, 1), 1)
    last = pos == WINDOW - 1
    bb = ck_ref.shape[0]
    newk = jnp.where(last, jnp.broadcast_to(kn_ref[...], (bb, WINDOW, 128)), pltpu.roll(ck_ref[...], WINDOW - 1, 1))
    newv = jnp.where(last, jnp.broadcast_to(vn_ref[...], (bb, WINDOW, 128)), pltpu.roll(cv_ref[...], WINDOW - 1, 1))
    nk_ref[...] = newk
    nv_ref[...] = newv
    s = lax.dot_general(q_ref[...].astype(BF16), newk.astype(BF16), (((2,), (2,)), ((0,), (0,))),
                        preferred_element_type=F32)
    s = s + bias_ref[...][None]
    sink = sink_ref[...][None]
    m = jnp.maximum(jnp.max(s, axis=-1, keepdims=True), sink)
    p = jnp.exp(s - m)
    denom = jnp.sum(p, axis=-1, keepdims=True) + jnp.exp(sink - m)
    o = lax.dot_general(p.astype(BF16), newv.astype(BF16), (((2,), (1,)), ((0,), (0,))),
                        preferred_element_type=F32)
    o_ref[...] = o / denom * hmask_ref[...][None]


def _attn_sample(q, k_new, v_new, k_cache, v_cache, layer, bias_sample, sinks):
    n = q.shape[0]
    bb = 16
    qh = q.reshape(n, N_KV, GQA_R, HEAD_DIM) * (HEAD_DIM ** -0.5)
    eye = jnp.eye(N_KV, dtype=F32)
    q8 = (qh[:, :, :, None, :] * eye[None, :, None, :, None]).reshape(n, N_HEADS, N_KV * HEAD_DIM)
    hmask = jnp.repeat(jnp.repeat(eye, GQA_R, axis=0), HEAD_DIM, axis=1)
    blk3 = lambda a, b: pl.BlockSpec((bb, a, b), lambda i: (i, 0, 0))
    cache = pl.BlockSpec((None, bb, WINDOW, 128), lambda i: (layer, i, 0, 0))
    o8, nk, nv = pl.pallas_call(
        _attn_sample_kernel,
        grid=(n // bb,),
        in_specs=[blk3(N_HEADS, 128), blk3(1, 128), blk3(1, 128), cache, cache,
                  _const_spec((N_HEADS, WINDOW)), _const_spec((N_HEADS, 1)), _const_spec((N_HEADS, 128))],
        out_specs=[blk3(N_HEADS, 128), blk3(WINDOW, 128), blk3(WINDOW, 128)],
        out_shape=[jax.ShapeDtypeStruct((n, N_HEADS, 128), F32),
                   jax.ShapeDtypeStruct((n, WINDOW, 128), F32),
                   jax.ShapeDtypeStruct((n, WINDOW, 128), F32)],
        compiler_params=_cparams("parallel"), name="attn_sample",
    )(q8, k_new.reshape(n, 1, 128), v_new.reshape(n, 1, 128), k_cache, v_cache, bias_sample,
      sinks.reshape(N_HEADS, 1), hmask)
    o = o8.reshape(n, N_KV, GQA_R, N_KV, HEAD_DIM)
    o = jnp.stack([o[:, g, :, g, :] for g in range(N_KV)], axis=1).reshape(n, D_ATTN)
    return o, nk, nv


def _sorted_rows(tm):
    return TOP_K * tm + N_EXPERTS * SUBLANES


def _out_proj_kernel(x_ref, yr_ref, o_ref, gnr_ref, gna_ref, wr_ref, wa_ref, g1_ref, b1_ref, rwt_ref, rb_ref,
                     x1_ref, rrow_ref, rcol_ref, cnt_ref, *, tm, alpha):
    yn = _rms_norm(yr_ref[...], gnr_ref[...]).astype(BF16)
    on = _rms_norm(o_ref[...], gna_ref[...]).astype(BF16)
    mix = (jnp.dot(yn, wr_ref[...], preferred_element_type=F32)
           + jnp.dot(on, wa_ref[...], preferred_element_type=F32))
    x1 = _layer_norm(alpha * x_ref[...] + mix, g1_ref[...], b1_ref[...])
    x1_ref[...] = x1

    logits = lax.dot_general(rwt_ref[...], x1, (((1,), (1,)), ((), ())), preferred_element_type=F32,
                             precision=lax.Precision.HIGHEST) + rb_ref[...]
    eidx = lax.broadcasted_iota(I32, (N_EXPERTS, tm), 0).astype(F32)
    work = logits
    vals, hots = [], []
    for _ in range(TOP_K):
        v = jnp.max(work, axis=0, keepdims=True)
        idx = jnp.min(jnp.where(work == v, eidx, float(N_EXPERTS)), axis=0, keepdims=True)
        hot = eidx == idx
        work = jnp.where(hot, -jnp.inf, work)
        vals.append(v)
        hots.append(hot.astype(F32))
    ex = [jnp.exp(v - vals[0]) for v in vals]
    tot = ex[0] + ex[1] + ex[2] + ex[3]
    tok_hot = hots[0] + hots[1] + hots[2] + hots[3]
    s_i = lax.broadcasted_iota(I32, (tm, tm), 0)
    t_i = lax.broadcasted_iota(I32, (tm, tm), 1)
    earlier = (s_i < t_i).astype(BF16)
    before = jnp.dot(tok_hot.astype(BF16), earlier, preferred_element_type=F32)
    cnt = jnp.sum(tok_hot, axis=1, keepdims=True)
    cnt8 = jnp.floor((cnt + (SUBLANES - 1)) * (1.0 / SUBLANES)) * SUBLANES
    cnt8_b = jnp.broadcast_to(cnt8, (N_EXPERTS, LANES))
    e_r = lax.broadcasted_iota(I32, (N_EXPERTS, N_EXPERTS), 0)
    e_c = lax.broadcasted_iota(I32, (N_EXPERTS, N_EXPERTS), 1)
    start = jnp.dot((e_c < e_r).astype(F32), cnt8_b, preferred_element_type=F32,
                    precision=lax.Precision.HIGHEST)[:, 0:1]
    slot = start + before
    rows = [jnp.sum(h * slot, axis=0, keepdims=True) for h in hots] + [e / tot for e in ex]
    rrow = jnp.concatenate(rows, axis=0)
    rrow_ref[...] = rrow
    rcol_ref[...] = rrow.T
    cnt_ref[...] = cnt8_b.T[0:1, 0:N_EXPERTS].astype(I32)


def _out_proj(x, y_rnn, o, lw):
    n = x.shape[0]
    tm = min(ROW_TILE, n)
    nt = n // tm
    row = lambda w: pl.BlockSpec((tm, w), lambda i: (i, 0))
    vec = lambda w: _const_spec((1, w))
    return pl.pallas_call(
        functools.partial(_out_proj_kernel, tm=tm, alpha=lw["alpha"]),
        grid=(nt,),
        in_specs=[row(D_MODEL), row(D_RNN), row(D_ATTN), vec(D_RNN), vec(D_ATTN),
                  _const_spec((D_RNN, D_MODEL)), _const_spec((D_ATTN, D_MODEL)), vec(D_MODEL), vec(D_MODEL),
                  _const_spec((N_EXPERTS, D_MODEL)), _const_spec((N_EXPERTS, 1))],
        out_specs=[row(D_MODEL), pl.BlockSpec((None, 2 * TOP_K, tm), lambda i: (i, 0, 0)), row(2 * TOP_K),
                   pl.BlockSpec((None, 1, N_EXPERTS), lambda i: (i, 0, 0))],
        out_shape=[jax.ShapeDtypeStruct((n, D_MODEL), F32), jax.ShapeDtypeStruct((nt, 2 * TOP_K, tm), F32),
                   jax.ShapeDtypeStruct((n, 2 * TOP_K), F32), jax.ShapeDtypeStruct((nt, 1, N_EXPERTS), I32)],
        compiler_params=_cparams("parallel"), name="out_proj_route",
    )(x, y_rnn, o, lw["gn_rnn"], lw["gn_attn"], lw["w_out_rnn"], lw["w_out_attn"], lw["ln1_g"], lw["ln1_b"],
      lw["router_wt"], lw["router_b"])


def _run_sizes(max_rows):
    sizes, s = [], SUBLANES
    while s <= max_rows:
        sizes.append(s)
        s *= 2
    return sizes[::-1]


def _for_each_piece(n8, max_rows, fn):
    for s in _run_sizes(max_rows):
        @pl.when((n8 & s) != 0)
        def _(s=s):
            fn(n8 & (-2 * s), s)


def _dispatch_kernel(cnt_ref, base_ref, x_ref, rrow_ref, *rest, tm):
    xbuf_ref, xs_ref, carry_ref, sem = rest[-4:]
    nrow = _sorted_rows(tm)

    @pl.when(pl.program_id(0) == 0)
    def _():
        def init(e, c):
            carry_ref[e] = base_ref[e]
            return c
        lax.fori_loop(0, N_EXPERTS, init, 0)

    r = lax.broadcasted_iota(I32, (nrow, 1), 0).astype(F32)
    hit = r == rrow_ref[0:1, :]
    for k in range(1, TOP_K):
        hit = jnp.logical_or(hit, r == rrow_ref[k:k + 1, :])
    perm = jnp.where(hit, 1.0, 0.0).astype(BF16)
    xs = jnp.dot(perm, x_ref[...].astype(BF16), preferred_element_type=F32)
    lo = lax.bitcast_convert_type(xs[:, :D_PACK], U32)
    hi = lax.bitcast_convert_type(xs[:, D_PACK:], U32)
    xs_ref[...] = (hi & U32(HI_MASK)) | (lo >> 16)

    def run_copy(src, dst, s):
        return pltpu.make_async_copy(xs_ref.at[pl.ds(pl.multiple_of(src, SUBLANES), s)],
                                     xbuf_ref.at[pl.ds(pl.multiple_of(dst, SUBLANES), s)], sem)

    def start(e, src):
        n8 = cnt_ref[0, e]
        dst = carry_ref[e]
        _for_each_piece(n8, tm, lambda off, s: run_copy(src + off, dst + off, s).start())
        carry_ref[e] = dst + n8
        return src + n8

    def wait(e, c):
        _for_each_piece(cnt_ref[0, e], tm, lambda off, s: run_copy(0, 0, s).wait())
        return c

    lax.fori_loop(0, N_EXPERTS, start, 0)
    lax.fori_loop(0, N_EXPERTS, wait, 0)


def _dispatch(x1, rrow, cnt8, base, xbuf, n_rows):
    n = x1.shape[0]
    nt, _, tm = rrow.shape
    in_specs = [pl.BlockSpec((None, 1, N_EXPERTS), lambda i: (i, 0, 0), memory_space=pltpu.SMEM),
                pl.BlockSpec(memory_space=pltpu.SMEM),
                pl.BlockSpec((tm, D_MODEL), lambda i: (i, 0)),
                pl.BlockSpec((None, 2 * TOP_K, tm), lambda i: (i, 0, 0))]
    args = [cnt8, base, x1, rrow]
    aliases = {}
    if xbuf is not None:
        in_specs.append(pl.BlockSpec(memory_space=pl.ANY))
        args.append(xbuf)
        aliases = {len(args) - 1: 0}
    return pl.pallas_call(
        functools.partial(_dispatch_kernel, tm=tm),
        grid=(nt,),
        in_specs=in_specs,
        out_specs=pl.BlockSpec(memory_space=pl.ANY),
        out_shape=jax.ShapeDtypeStruct((n_rows, D_PACK), U32),
        scratch_shapes=[pltpu.VMEM((_sorted_rows(tm), D_PACK), U32), pltpu.SMEM((N_EXPERTS,), I32),
                        pltpu.SemaphoreType.DMA],
        input_output_aliases=aliases,
        compiler_params=_cparams("arbitrary"), name="moe_dispatch",
    )(*args)


def _moe_kernel(br_ref, be_ref, nv_ref, x_ref, w1_ref, b1_ref, w2_ref, b2_ref, y_ref, w1s_ref, w2s_ref):
    i = pl.program_id(0)
    e = be_ref[i]
    e_prev = be_ref[jnp.maximum(i - 1, 0)]

    @pl.when(jnp.logical_or(i == 0, e != e_prev))
    def _():
        w1s_ref[...] = w1_ref[...].astype(BF16)
        w2s_ref[...] = w2_ref[...].astype(BF16)

    nv = nv_ref[i]

    @pl.when(nv > 0)
    def _():
        rows = lax.broadcasted_iota(I32, (MOE_TILE, 1), 0)
        xw = jnp.where(rows < nv, x_ref[...], U32(0))
        lo = lax.bitcast_convert_type(xw << 16, F32).astype(BF16)
        hi = lax.bitcast_convert_type(xw & U32(HI_MASK), F32).astype(BF16)
        hdn = (jnp.dot(lo, w1s_ref[:D_PACK, :], preferred_element_type=F32)
               + jnp.dot(hi, w1s_ref[D_PACK:, :], preferred_element_type=F32) + b1_ref[...])
        g = jnp.minimum(hdn[:, :D_FF], SWIGLU_LIMIT)
        lin = jnp.clip(hdn[:, D_FF:], -SWIGLU_LIMIT, SWIGLU_LIMIT)
        act = g * jax.nn.sigmoid(SWIGLU_ALPHA * g) * (lin + 1.0)
        y_ref[...] = jnp.dot(act.astype(BF16), w2s_ref[...], preferred_element_type=F32) + b2_ref[...]


def _moe_experts(xbuf, blk_row, blk_e, blk_nv, w1, b1, w2, b2, layer):
    n_blocks = xbuf.shape[0] // MOE_TILE
    grid_spec = pltpu.PrefetchScalarGridSpec(
        num_scalar_prefetch=3,
        grid=(n_blocks,),
        in_specs=[pl.BlockSpec((MOE_TILE, D_PACK), lambda i, br, be, nv: (br[i], 0)),
                  pl.BlockSpec((None, None, D_MODEL, 2 * D_FF), lambda i, br, be, nv: (layer, be[i], 0, 0)),
                  pl.BlockSpec((None, None, 1, 2 * D_FF), lambda i, br, be, nv: (layer, be[i], 0, 0)),
                  pl.BlockSpec((None, None, D_FF, D_MODEL), lambda i, br, be, nv: (layer, be[i], 0, 0)),
                  pl.BlockSpec((None, None, 1, D_MODEL), lambda i, br, be, nv: (layer, be[i], 0, 0))],
        out_specs=pl.BlockSpec((MOE_TILE, D_MODEL), lambda i, br, be, nv: (br[i], 0)),
        scratch_shapes=[pltpu.VMEM((D_MODEL, 2 * D_FF), BF16), pltpu.VMEM((D_FF, D_MODEL), BF16)],
    )
    depth = w1.shape[0]
    return pl.pallas_call(
        _moe_kernel, grid_spec=grid_spec,
        out_shape=jax.ShapeDtypeStruct((xbuf.shape[0], D_MODEL), F32),
        compiler_params=_cparams("arbitrary"), name="moe_experts",
    )(blk_row, blk_e, blk_nv, xbuf, w1, b1.reshape(depth, N_EXPERTS, 1, 2 * D_FF), w2,
      b2.reshape(depth, N_EXPERTS, 1, D_MODEL))


def _final_kernel(cnt_ref, base_ref, x_ref, p_ref, rcol_ref, wg_ref, bg_ref, wp_ref, g2_ref, b2_ref, ybuf_ref,
                  out_ref, ys_ref, carry_ref, sem, *, tm, alpha):
    nrow = _sorted_rows(tm)

    @pl.when(pl.program_id(0) == 0)
    def _():
        def init(e, c):
            carry_ref[e] = base_ref[e]
            return c
        lax.fori_loop(0, N_EXPERTS, init, 0)

    def run_copy(src, dst, s):
        return pltpu.make_async_copy(ybuf_ref.at[pl.ds(pl.multiple_of(src, SUBLANES), s)],
                                     ys_ref.at[pl.ds(pl.multiple_of(dst, SUBLANES), s)], sem)

    def start(e, dst):
        n8 = cnt_ref[0, e]
        src = carry_ref[e]
        _for_each_piece(n8, tm, lambda off, s: run_copy(src + off, dst + off, s).start())
        carry_ref[e] = src + n8
        return dst + n8

    def wait(e, c):
        _for_each_piece(cnt_ref[0, e], tm, lambda off, s: run_copy(0, 0, s).wait())
        return c

    n_rows = lax.fori_loop(0, N_EXPERTS, start, 0)
    x = x_ref[...]
    ple = (jax.nn.sigmoid(jnp.dot(x.astype(BF16), wg_ref[...], preferred_element_type=F32) + bg_ref[...])
           * jnp.dot(p_ref[...].astype(BF16), wp_ref[...], preferred_element_type=F32))
    lax.fori_loop(0, N_EXPERTS, wait, 0)

    r = lax.broadcasted_iota(I32, (nrow, 1), 0)
    ys = jnp.where(r < n_rows, ys_ref[...], 0.0).astype(BF16)
    slot = lax.broadcasted_iota(I32, (1, nrow), 1).astype(F32)
    rc = rcol_ref[...]
    gmat = jnp.zeros((tm, nrow), F32)
    for k in range(TOP_K):
        gmat = gmat + jnp.where(rc[:, k:k + 1] == slot, rc[:, TOP_K + k:TOP_K + k + 1], 0.0)
    g_hi = gmat.astype(BF16)
    g_lo = (gmat - g_hi.astype(F32)).astype(BF16)
    ffn = (jnp.dot(g_hi, ys, preferred_element_type=F32) + jnp.dot(g_lo, ys, preferred_element_type=F32))
    out_ref[...] = _layer_norm(alpha * x + ffn + ple, g2_ref[...], b2_ref[...])


def _final(x1, p_all, layer, rcol, cnt8, base, ybuf, lw):
    n = x1.shape[0]
    nt = cnt8.shape[0]
    tm = n // nt
    row = lambda w: pl.BlockSpec((tm, w), lambda i: (i, 0))
    vec = lambda w: _const_spec((1, w))
    return pl.pallas_call(
        functools.partial(_final_kernel, tm=tm, alpha=lw["alpha"]),
        grid=(nt,),
        in_specs=[pl.BlockSpec((None, 1, N_EXPERTS), lambda i: (i, 0, 0), memory_space=pltpu.SMEM),
                  pl.BlockSpec(memory_space=pltpu.SMEM),
                  row(D_MODEL), pl.BlockSpec((None, tm, D_PLE), lambda i: (layer, i, 0)), row(2 * TOP_K),
                  _const_spec((D_MODEL, D_MODEL)), vec(D_MODEL), _const_spec((D_PLE, D_MODEL)),
                  vec(D_MODEL), vec(D_MODEL), pl.BlockSpec(memory_space=pl.ANY)],
        out_specs=row(D_MODEL),
        out_shape=jax.ShapeDtypeStruct((n, D_MODEL), F32),
        scratch_shapes=[pltpu.VMEM((_sorted_rows(tm), D_MODEL), F32), pltpu.SMEM((N_EXPERTS,), I32),
                        pltpu.SemaphoreType.DMA],
        compiler_params=_cparams("arbitrary"), name="moe_combine_ln2",
    )(cnt8, base, x1, p_all, rcol, lw["ple_w_gate"], lw["ple_b_gate"], lw["ple_w_proj"], lw["ln2_g"], lw["ln2_b"],
      ybuf)


def _block_diag(w):
    nb, bw, _ = w.shape
    eye = jnp.eye(nb, dtype=w.dtype)
    return (w[:, :, None, :] * eye[:, None, :, None]).reshape(nb * bw, nb * bw)


def _block_geometry(tot, n_blocks):
    nblk = (tot + MOE_TILE - 1) // MOE_TILE
    blk_end = jnp.cumsum(nblk)
    blk_start = blk_end - nblk
    ids = jnp.arange(n_blocks, dtype=I32)
    n_valid = blk_end[-1]
    src = jnp.minimum(ids, n_valid - 1)
    e_of = jnp.minimum(jnp.sum(src[:, None] >= blk_end[None, :], axis=1), N_EXPERTS - 1).astype(I32)
    hot = e_of[:, None] == jnp.arange(N_EXPERTS)[None, :]
    start_of = jnp.sum(jnp.where(hot, blk_start[None, :], 0), axis=1)
    tot_of = jnp.sum(jnp.where(hot, tot[None, :], 0), axis=1)
    nv = jnp.where(ids < n_valid, jnp.clip(tot_of - (ids - start_of) * MOE_TILE, 0, MOE_TILE), 0)
    return src.astype(I32), e_of, nv.astype(I32), (blk_start * MOE_TILE).astype(I32)


def kernel(x_prompt, x_sample, cache_k, cache_v, state_conv, state_h, p_prompt, p_sample, ln_in_g, ln_in_b, rel_bias, w_in, conv_w, conv_b, rg_wa, rg_ba, rg_wx, rg_bx, rg_lambda, attn_sinks, gn_rnn, gn_attn, w_out, ln1_g, ln1_b, router_w, router_b, moe_w1, moe_b1, moe_w2, moe_b2, ple_w_gate, ple_b_gate, ple_w_proj, ln2_g, ln2_b):
    depth = w_in.shape[0]
    bsz, t_len, _ = x_prompt.shape
    n_p = bsz * t_len
    n_s = x_sample.shape[0]
    wb = cache_k.shape[2]
    alpha = (2 * depth) ** 0.25

    dist = jnp.arange(WINDOW)[:, None] + WINDOW - jnp.arange(2 * WINDOW)[None, :]
    valid = (dist >= 0) & (dist < WINDOW)
    bias_prompt = jnp.where(valid[None], _bias_lookup(rel_bias, dist), NEG)
    bias_sample = _bias_lookup(rel_bias, WINDOW - 1 - jnp.arange(WINDOW))

    tm_p = min(ROW_TILE, n_p)
    n_tiles = n_p // tm_p + 1
    max_rows = (n_p + n_s) * TOP_K + n_tiles * N_EXPERTS * (SUBLANES - 1)
    n_blocks = -(-(max_rows + N_EXPERTS * (MOE_TILE - 1)) // MOE_TILE)

    xp = x_prompt.reshape(n_p, D_MODEL)
    xs = x_sample.reshape(n_s, D_MODEL)
    pp = p_prompt.reshape(depth, n_p, D_PLE)
    ps = p_sample.reshape(depth, n_s, D_PLE)
    ck = cache_k.reshape(depth, n_s, wb, 128)
    cv = cache_v.reshape(depth, n_s, wb, 128)
    row2 = lambda v: v.reshape(1, -1)
    kv0 = 2 * D_RNN + D_ATTN
    outs = {k: [] for k in ("kp", "vp", "cp", "hp", "ks", "vs", "cs", "hs")}
    for l in range(depth):
        rw = {"conv_w": conv_w[l], "conv_b": row2(conv_b[l]), "wa": _block_diag(rg_wa[l]).astype(BF16),
              "ba": row2(rg_ba[l]), "wx": _block_diag(rg_wx[l]).astype(BF16), "bx": row2(rg_bx[l]),
              "lam": row2(rg_lambda[l])}
        lw = {"gn_rnn": row2(gn_rnn[l]), "gn_attn": row2(gn_attn[l]),
              "w_out_rnn": w_out[l, :D_RNN].astype(BF16), "w_out_attn": w_out[l, D_RNN:].astype(BF16),
              "ln1_g": row2(ln1_g[l]), "ln1_b": row2(ln1_b[l]), "router_wt": router_w[l].T,
              "router_b": router_b[l].reshape(N_EXPERTS, 1), "alpha": alpha,
              "ple_w_gate": ple_w_gate[l].astype(BF16), "ple_b_gate": row2(ple_b_gate[l]),
              "ple_w_proj": ple_w_proj[l].astype(BF16), "ln2_g": row2(ln2_g[l]), "ln2_b": row2(ln2_b[l])}
        w_in_l = w_in[l].astype(BF16)
        first = l == 0

        xp, zp = _in_proj(xp, row2(ln_in_g), row2(ln_in_b), w_in_l, first)
        zp3 = zp.reshape(bsz, t_len, D_IN)
        yp, hp = _rnn_prompt(zp3, rw)
        op = _attn_prompt(zp, bsz, t_len, bias_prompt, attn_sinks[l])
        wp = min(WINDOW, t_len)
        outs["kp"].append(zp3[:, t_len - wp:, kv0:kv0 + 128].reshape(bsz, wp, N_KV, HEAD_DIM))
        outs["vp"].append(zp3[:, t_len - wp:, kv0 + 128:kv0 + 256].reshape(bsz, wp, N_KV, HEAD_DIM))
        outs["cp"].append(zp3[:, t_len - (CONV_W - 1):, :D_RNN])
        outs["hp"].append(hp.reshape(bsz, D_RNN))

        xs, zs = _in_proj(xs, row2(ln_in_g), row2(ln_in_b), w_in_l, first)
        ys, hs = _rnn_sample(zs, state_conv[l], state_h[l], rw)
        os_, nk, nv = _attn_sample(zs[:, 2 * D_RNN:kv0], zs[:, kv0:kv0 + 128], zs[:, kv0 + 128:], ck, cv, l,
                                   bias_sample, attn_sinks[l])
        outs["ks"].append(nk.reshape(n_s, wb, N_KV, HEAD_DIM))
        outs["vs"].append(nv.reshape(n_s, wb, N_KV, HEAD_DIM))
        outs["cs"].append(jnp.concatenate([state_conv[l][:, 1:], zs[:, None, :D_RNN]], axis=1))
        outs["hs"].append(hs)

        x1p, rrow_p, rcol_p, cnt_p = _out_proj(xp, yp.reshape(n_p, D_RNN), op, lw)
        x1s, rrow_s, rcol_s, cnt_s = _out_proj(xs, ys, os_, lw)

        tot_p = jnp.sum(cnt_p, axis=(0, 1))
        tot = tot_p + jnp.sum(cnt_s, axis=(0, 1))
        blk_row, blk_e, blk_nv, base_p = _block_geometry(tot, n_blocks)
        base_s = base_p + tot_p

        xbuf = _dispatch(x1p, rrow_p, cnt_p, base_p, None, n_blocks * MOE_TILE)
        xbuf = _dispatch(x1s, rrow_s, cnt_s, base_s, xbuf, n_blocks * MOE_TILE)
        ybuf = _moe_experts(xbuf, blk_row, blk_e, blk_nv, moe_w1, moe_b1, moe_w2, moe_b2, l)
        xp = _final(x1p, pp, l, rcol_p, cnt_p, base_p, ybuf, lw)
        xs = _final(x1s, ps, l, rcol_s, cnt_s, base_s, ybuf, lw)

    st = lambda k: jnp.stack(outs[k])
    return (xp.reshape(bsz, t_len, D_MODEL), xs.reshape(n_s, 1, D_MODEL),
            st("kp"), st("vp"), st("cp"), st("hp"), st("ks"), st("vs"), st("cs"), st("hs"))
```

```python
import functools
import math

import jax
import jax.numpy as jnp
from jax import lax
from jax.experimental import pallas as pl
from jax.experimental.pallas import tpu as pltpu

F32 = jnp.float32
BF16 = jnp.bfloat16
I32 = jnp.int32
U32 = jnp.uint32

D_MODEL = 1024
D_RNN = 512
RNN_BLOCKS = 8
CONV_W = 4
LRU_C = 8.0
N_HEADS = 8
HEAD_DIM = 64
N_KV = 2
GQA_R = N_HEADS // N_KV
D_ATTN = N_HEADS * HEAD_DIM
WINDOW = 128
N_BUCKETS = 32
MAX_DIST = 128
N_EXPERTS = 32
TOP_K = 4
D_FF = 1024
SWIGLU_LIMIT = 7.0
SWIGLU_ALPHA = 1.702
D_PLE = 256
LN_EPS = 1e-5
D_IN = 2 * D_RNN + D_ATTN + 2 * N_KV * HEAD_DIM
NEG = -1e30

SUBLANES = 8
LANES = 128
ROW_TILE = 512
MOE_TILE = 512
D_PACK = D_MODEL // 2
VMEM_LIMIT = 56 * 1024 * 1024
HI_MASK = 0xFFFF0000


def _cparams(*sem):
    return pltpu.CompilerParams(dimension_semantics=sem, vmem_limit_bytes=VMEM_LIMIT)


def _layer_norm(x, g, b):
    mu = jnp.mean(x, axis=-1, keepdims=True)
    xc = x - mu
    var = jnp.mean(xc * xc, axis=-1, keepdims=True)
    return xc * lax.rsqrt(var + LN_EPS) * g + b


def _rms_norm(x, g):
    return x * lax.rsqrt(jnp.mean(x * x, axis=-1, keepdims=True) + LN_EPS) * g


def _const_spec(shape):
    return pl.BlockSpec(shape, lambda *_: (0,) * len(shape))


def _in_proj_kernel(x_ref, g_ref, b_ref, w_ref, *out_refs, apply_ln):
    x = x_ref[...]
    if apply_ln:
        x = _layer_norm(x, g_ref[...], b_ref[...])
        out_refs[0][...] = x
    out_refs[-1][...] = jnp.dot(x.astype(BF16), w_ref[...], preferred_element_type=F32)


def _in_proj(x, ln_g, ln_b, w_in_bf16, apply_ln):
    n = x.shape[0]
    tm = min(ROW_TILE, n)
    row = lambda w: pl.BlockSpec((tm, w), lambda i: (i, 0))
    out_shape = [jax.ShapeDtypeStruct((n, D_IN), F32)]
    out_specs = [row(D_IN)]
    if apply_ln:
        out_shape.insert(0, jax.ShapeDtypeStruct((n, D_MODEL), F32))
        out_specs.insert(0, row(D_MODEL))
    outs = pl.pallas_call(
        functools.partial(_in_proj_kernel, apply_ln=apply_ln),
        grid=(n // tm,),
        in_specs=[row(D_MODEL), _const_spec((1, D_MODEL)), _const_spec((1, D_MODEL)),
                  _const_spec((D_MODEL, D_IN))],
        out_specs=out_specs, out_shape=out_shape,
        compiler_params=_cparams("parallel"), name="in_proj",
    )(x, ln_g, ln_b, w_in_bf16)
    return (outs[0], outs[1]) if apply_ln else (x, outs[0])


def _rglru_coeffs(u, wa_ref, ba_ref, wx_ref, bx_ref, lam_ref):
    ub = u.astype(BF16)
    r = jax.nn.sigmoid(jnp.dot(ub, wa_ref[...], preferred_element_type=F32) + ba_ref[...])
    i = jax.nn.sigmoid(jnp.dot(ub, wx_ref[...], preferred_element_type=F32) + bx_ref[...])
    lam = -lam_ref[...]
    softplus = jnp.maximum(lam, 0.0) + jnp.log1p(jnp.exp(-jnp.abs(lam)))
    log_a = -LRU_C * r * softplus
    a = jnp.exp(log_a)
    b = jnp.sqrt(-jnp.tanh(log_a) * (a * a + 1.0)) * (i * u)
    return a, b


def _rnn_prompt_kernel(xr_ref, gate_ref, cw_ref, cb_ref, wa_ref, ba_ref, wx_ref, bx_ref, lam_ref,
                       y_ref, hlast_ref, xs_ref, a_ref, b_ref, carry_ref, *, tt):
    t = pl.program_id(1)

    @pl.when(t == 0)
    def _():
        xs_ref[0:SUBLANES, :] = jnp.zeros((SUBLANES, D_RNN), F32)
        carry_ref[...] = jnp.zeros((SUBLANES, D_RNN), F32)

    x = xr_ref[...]
    xs_ref[SUBLANES:SUBLANES + tt, :] = x
    cw = cw_ref[...]
    u = cb_ref[...] + x * cw[3:4]
    for j in range(CONV_W - 1):
        off = SUBLANES - (CONV_W - 1) + j
        u = u + xs_ref[off:off + tt, :] * cw[j:j + 1]
    xs_ref[0:SUBLANES, :] = x[tt - SUBLANES:tt, :]

    a, b = _rglru_coeffs(u, wa_ref, ba_ref, wx_ref, bx_ref, lam_ref)
    ng = tt // SUBLANES
    a = a.reshape(ng, SUBLANES, D_RNN)
    b = b.reshape(ng, SUBLANES, D_RNN)
    row = lax.broadcasted_iota(I32, (1, SUBLANES, 1), 1)
    for s in (1, 2, 4):
        a_sh = pltpu.roll(a, s, 1)
        b_sh = pltpu.roll(b, s, 1)
        m = row >= s
        b = jnp.where(m, a * b_sh + b, b)
        a = jnp.where(m, a * a_sh, a)
    a_ref[...] = a
    b_ref[...] = b

    def body(g, carry):
        h = a_ref[g] * carry + b_ref[g]
        b_ref[g] = h
        return jnp.broadcast_to(h[SUBLANES - 1:SUBLANES, :], (SUBLANES, D_RNN))

    carry = lax.fori_loop(0, ng, body, carry_ref[...])
    carry_ref[...] = carry
    h = b_ref[...].reshape(tt, D_RNN)
    y_ref[...] = h * jax.nn.gelu(gate_ref[...])
    hlast_ref[...] = carry[0:1, :]


def _rnn_prompt(z3, rw):
    bsz, t_len, _ = z3.shape
    tt = min(ROW_TILE, t_len)
    vec = _const_spec((1, D_RNN))
    return pl.pallas_call(
        functools.partial(_rnn_prompt_kernel, tt=tt),
        grid=(bsz, t_len // tt),
        in_specs=[pl.BlockSpec((None, tt, D_RNN), lambda b, t: (b, t, 0)),
                  pl.BlockSpec((None, tt, D_RNN), lambda b, t: (b, t, 1)),
                  _const_spec((CONV_W, D_RNN)), vec, _const_spec((D_RNN, D_RNN)), vec,
                  _const_spec((D_RNN, D_RNN)), vec, vec],
        out_specs=[pl.BlockSpec((None, tt, D_RNN), lambda b, t: (b, t, 0)),
                   pl.BlockSpec((None, 1, D_RNN), lambda b, t: (b, 0, 0))],
        out_shape=[jax.ShapeDtypeStruct((bsz, t_len, D_RNN), F32),
                   jax.ShapeDtypeStruct((bsz, 1, D_RNN), F32)],
        scratch_shapes=[pltpu.VMEM((tt + SUBLANES, D_RNN), F32),
                        pltpu.VMEM((tt // SUBLANES, SUBLANES, D_RNN), F32),
                        pltpu.VMEM((tt // SUBLANES, SUBLANES, D_RNN), F32),
                        pltpu.VMEM((SUBLANES, D_RNN), F32)],
        compiler_params=_cparams("parallel", "arbitrary"), name="rnn_prompt",
    )(z3, z3, rw["conv_w"], rw["conv_b"], rw["wa"], rw["ba"], rw["wx"], rw["bx"], rw["lam"])


def _rnn_sample_kernel(xr_ref, gate_ref, c0_ref, c1_ref, c2_ref, h0_ref, cw_ref, cb_ref, wa_ref, ba_ref,
                       wx_ref, bx_ref, lam_ref, y_ref, h_ref):
    x = xr_ref[...]
    cw = cw_ref[...]
    u = (cb_ref[...] + c0_ref[...] * cw[0:1] + c1_ref[...] * cw[1:2] + c2_ref[...] * cw[2:3]
         + x * cw[3:4])
    a, b = _rglru_coeffs(u, wa_ref, ba_ref, wx_ref, bx_ref, lam_ref)
    h = a * h0_ref[...] + b
    h_ref[...] = h
    y_ref[...] = h * jax.nn.gelu(gate_ref[...])


def _rnn_sample(z, conv_state, h0, rw):
    n = z.shape[0]
    vec = _const_spec((1, D_RNN))
    full = _const_spec((n, D_RNN))
    cs = [conv_state[:, j] for j in range(CONV_W - 1)]
    return pl.pallas_call(
        _rnn_sample_kernel,
        grid=(1,),
        in_specs=[pl.BlockSpec((n, D_RNN), lambda i: (0, 0)), pl.BlockSpec((n, D_RNN), lambda i: (0, 1)),
                  full, full, full, full,
                  _const_spec((CONV_W, D_RNN)), vec, _const_spec((D_RNN, D_RNN)), vec,
                  _const_spec((D_RNN, D_RNN)), vec, vec],
        out_specs=[full, full],
        out_shape=[jax.ShapeDtypeStruct((n, D_RNN), F32)] * 2,
        compiler_params=_cparams("arbitrary"), name="rnn_sample",
    )(z, z, *cs, h0, rw["conv_w"], rw["conv_b"], rw["wa"], rw["ba"], rw["wx"], rw["bx"], rw["lam"])


def _rel_bucket(dist):
    n = jnp.maximum(dist, 0)
    max_exact = N_BUCKETS // 2
    nf = jnp.maximum(n, max_exact).astype(F32)
    large = max_exact + (jnp.log(nf / max_exact) / math.log(MAX_DIST / max_exact)
                         * (N_BUCKETS - max_exact)).astype(I32)
    large = jnp.minimum(large, N_BUCKETS - 1)
    return jnp.where(n < max_exact, n, large)


def _bias_lookup(rel_bias, dist):
    hot = (_rel_bucket(dist)[..., None] == jnp.arange(N_BUCKETS)).astype(F32)
    out = jnp.tensordot(hot, rel_bias.astype(F32), axes=1, precision=lax.Precision.HIGHEST)
    return jnp.moveaxis(out, -1, 0)


def _half_lane_variants(x, group):
    lo = lax.broadcasted_iota(I32, (1, 2 * HEAD_DIM), 1) < HEAD_DIM
    xr = pltpu.roll(x, HEAD_DIM, 1)
    zero = jnp.zeros_like(x)
    if group == 0:
        return jnp.where(lo, x, zero), jnp.where(lo, zero, xr)
    return jnp.where(lo, xr, zero), jnp.where(lo, zero, x)


def _attn_prompt_kernel(q_ref, kc_ref, kp_ref, vc_ref, vp_ref, bias_ref, sink_ref, o_ref, *, nq):
    first = pl.program_id(1) == 0
    col = lax.broadcasted_iota(I32, (1, 2 * WINDOW), 1)
    hide_prev = jnp.logical_and(first, col < WINDOW)
    for j in range(nq):
        rows = slice(j * WINDOW, (j + 1) * WINDOW)
        before = slice((j - 1) * WINDOW, j * WINDOW)
        kk = jnp.concatenate([kp_ref[...] if j == 0 else kc_ref[before, :], kc_ref[rows, :]], axis=0)
        vv = jnp.concatenate([vp_ref[...] if j == 0 else vc_ref[before, :], vc_ref[rows, :]], axis=0)
        for g in range(N_KV):
            k_lo, k_hi = _half_lane_variants(kk, g)
            v_lo, v_hi = _half_lane_variants(vv, g)
            for pair in range(GQA_R // 2):
                p0 = g * (GQA_R // 2) + pair
                qp = (q_ref[rows, p0 * 128:(p0 + 1) * 128] * (HEAD_DIM ** -0.5)).astype(BF16)
                acc = jnp.zeros((WINDOW, 2 * HEAD_DIM), F32)
                for kx, vx, h in ((k_lo, v_lo, 2 * p0), (k_hi, v_hi, 2 * p0 + 1)):
                    s = lax.dot_general(qp, kx.astype(BF16), (((1,), (1,)), ((), ())),
                                        preferred_element_type=F32) + bias_ref[h]
                    if j == 0:
                        s = jnp.where(hide_prev, NEG, s)
                    sink = sink_ref[h]
                    m = jnp.maximum(jnp.max(s, axis=-1, keepdims=True), sink)
                    p = jnp.exp(s - m)
                    denom = jnp.sum(p, axis=-1, keepdims=True) + jnp.exp(sink - m)
                    acc = acc + jnp.dot(p.astype(BF16), vx.astype(BF16), preferred_element_type=F32) / denom
                o_ref[rows, p0 * 128:(p0 + 1) * 128] = acc


def _attn_prompt(z, bsz, t_len, bias_prompt, sinks):
    nb = t_len // WINDOW
    nq = min(4, nb)
    ns = nb // nq
    kcol = (2 * D_RNN + D_ATTN) // 128
    cur = lambda c: pl.BlockSpec((nq * WINDOW, 128), lambda b, i: (b * ns + i, c))
    prev = lambda c: pl.BlockSpec((WINDOW, 128), lambda b, i: (b * nb + jnp.maximum(i * nq - 1, 0), c))
    return pl.pallas_call(
        functools.partial(_attn_prompt_kernel, nq=nq),
        grid=(bsz, ns),
        in_specs=[pl.BlockSpec((nq * WINDOW, D_ATTN), lambda b, i: (b * ns + i, 2 * D_RNN // D_ATTN)),
                  cur(kcol), prev(kcol), cur(kcol + 1), prev(kcol + 1),
                  _const_spec((N_HEADS, WINDOW, 2 * WINDOW)),
                  pl.BlockSpec(memory_space=pltpu.SMEM)],
        out_specs=pl.BlockSpec((nq * WINDOW, D_ATTN), lambda b, i: (b * ns + i, 0)),
        out_shape=jax.ShapeDtypeStruct((bsz * t_len, D_ATTN), F32),
        compiler_params=_cparams("parallel", "arbitrary"), name="attn_prompt",
    )(z, z, z, z, z, bias_prompt, sinks)


def _attn_sample_kernel(q_ref, kn_ref, vn_ref, ck_ref, cv_ref, bias_ref, sink_ref, hmask_ref,
                        o_ref, nk_ref, nv_ref):
    pos = lax.broadcasted_iota(I32, (1, WINDOW, 1), 1)
    last = pos == WINDOW - 1
    bb = ck_ref.shape[0]
    newk = jnp.where(last, jnp.broadcast_to(kn_ref[...], (bb, WINDOW, 128)), pltpu.roll(ck_ref[...], WINDOW - 1, 1))
    newv = jnp.where(last, jnp.broadcast_to(vn_ref[...], (bb, WINDOW, 128)), pltpu.roll(cv_ref[...], WINDOW - 1, 1))
    nk_ref[...] = newk
    nv_ref[...] = newv
    s = lax.dot_general(q_ref[...].astype(BF16), newk.astype(BF16), (((2,), (2,)), ((0,), (0,))),
                        preferred_element_type=F32)
    s = s + bias_ref[...][None]
    sink = sink_ref[...][None]
    m = jnp.maximum(jnp.max(s, axis=-1, keepdims=True), sink)
    p = jnp.exp(s - m)
    denom = jnp.sum(p, axis=-1, keepdims=True) + jnp.exp(sink - m)
    o = lax.dot_general(p.astype(BF16), newv.astype(BF16), (((2,), (1,)), ((0,), (0,))),
                        preferred_element_type=F32)
    o_ref[...] = o / denom * hmask_ref[...][None]


def _attn_sample(q, k_new, v_new, k_cache, v_cache, layer, bias_sample, sinks):
    n = q.shape[0]
    bb = 16
    qh = q.reshape(n, N_KV, GQA_R, HEAD_DIM) * (HEAD_DIM ** -0.5)
    eye = jnp.eye(N_KV, dtype=F32)
    q8 = (qh[:, :, :, None, :] * eye[None, :, None, :, None]).reshape(n, N_HEADS, N_KV * HEAD_DIM)
    hmask = jnp.repeat(jnp.repeat(eye, GQA_R, axis=0), HEAD_DIM, axis=1)
    blk3 = lambda a, b: pl.BlockSpec((bb, a, b), lambda i: (i, 0, 0))
    cache = pl.BlockSpec((None, bb, WINDOW, 128), lambda i: (layer, i, 0, 0))
    o8, nk, nv = pl.pallas_call(
        _attn_sample_kernel,
        grid=(n // bb,),
        in_specs=[blk3(N_HEADS, 128), blk3(1, 128), blk3(1, 128), cache, cache,
                  _const_spec((N_HEADS, WINDOW)), _const_spec((N_HEADS, 1)), _const_spec((N_HEADS, 128))],
        out_specs=[blk3(N_HEADS, 128), blk3(WINDOW, 128), blk3(WINDOW, 128)],
        out_shape=[jax.ShapeDtypeStruct((n, N_HEADS, 128), F32),
                   jax.ShapeDtypeStruct((n, WINDOW, 128), F32),
                   jax.ShapeDtypeStruct((n, WINDOW, 128), F32)],
        compiler_params=_cparams("parallel"), name="attn_sample",
    )(q8, k_new.reshape(n, 1, 128), v_new.reshape(n, 1, 128), k_cache, v_cache, bias_sample,
      sinks.reshape(N_HEADS, 1), hmask)
    o = o8.reshape(n, N_KV, GQA_R, N_KV, HEAD_DIM)
    o = jnp.stack([o[:, g, :, g, :] for g in range(N_KV)], axis=1).reshape(n, D_ATTN)
    return o, nk, nv


def _sorted_rows(tm):
    return TOP_K * tm + N_EXPERTS * SUBLANES


def _out_proj_kernel(x_ref, yr_ref, o_ref, gnr_ref, gna_ref, wr_ref, wa_ref, g1_ref, b1_ref, rwt_ref, rb_ref,
                     x1_ref, rrow_ref, rcol_ref, cnt_ref, *, tm, alpha):
    yn = _rms_norm(yr_ref[...], gnr_ref[...]).astype(BF16)
    on = _rms_norm(o_ref[...], gna_ref[...]).astype(BF16)
    mix = (jnp.dot(yn, wr_ref[...], preferred_element_type=F32)
           + jnp.dot(on, wa_ref[...], preferred_element_type=F32))
    x1 = _layer_norm(alpha * x_ref[...] + mix, g1_ref[...], b1_ref[...])
    x1_ref[...] = x1

    x1h = x1.astype(BF16)
    x1l = (x1 - x1h.astype(F32)).astype(BF16)
    nt_dims = (((1,), (1,)), ((), ()))
    by_hi = lax.dot_general(rwt_ref[...], x1h, nt_dims, preferred_element_type=F32)
    by_lo = lax.dot_general(rwt_ref[0:N_EXPERTS, :], x1l, nt_dims, preferred_element_type=F32)
    logits = by_hi[0:N_EXPERTS] + by_hi[N_EXPERTS:] + by_lo + rb_ref[...]
    eidx = lax.broadcasted_iota(I32, (N_EXPERTS, tm), 0).astype(F32)
    work = logits
    vals, hots = [], []
    for _ in range(TOP_K):
        v = jnp.max(work, axis=0, keepdims=True)
        idx = jnp.min(jnp.where(work == v, eidx, float(N_EXPERTS)), axis=0, keepdims=True)
        hot = eidx == idx
        work = jnp.where(hot, -jnp.inf, work)
        vals.append(v)
        hots.append(hot.astype(F32))
    ex = [jnp.exp(v - vals[0]) for v in vals]
    tot = ex[0] + ex[1] + ex[2] + ex[3]
    tok_hot = hots[0] + hots[1] + hots[2] + hots[3]
    s_i = lax.broadcasted_iota(I32, (tm, tm), 0)
    t_i = lax.broadcasted_iota(I32, (tm, tm), 1)
    earlier = (s_i < t_i).astype(BF16)
    before = jnp.dot(tok_hot.astype(BF16), earlier, preferred_element_type=F32)
    cnt = jnp.sum(tok_hot, axis=1, keepdims=True)
    cnt8 = jnp.floor((cnt + (SUBLANES - 1)) * (1.0 / SUBLANES)) * SUBLANES
    cnt8_b = jnp.broadcast_to(cnt8, (N_EXPERTS, LANES))
    e_r = lax.broadcasted_iota(I32, (N_EXPERTS, N_EXPERTS), 0)
    e_c = lax.broadcasted_iota(I32, (N_EXPERTS, N_EXPERTS), 1)
    start = jnp.dot((e_c < e_r).astype(F32), cnt8_b, preferred_element_type=F32,
                    precision=lax.Precision.HIGHEST)[:, 0:1]
    slot = start + before
    rows = [jnp.sum(h * slot, axis=0, keepdims=True) for h in hots] + [e / tot for e in ex]
    rrow = jnp.concatenate(rows, axis=0)
    rrow_ref[...] = rrow
    rcol_ref[...] = rrow.T
    cnt_ref[...] = cnt8_b.T[0:1, 0:N_EXPERTS].astype(I32)


def _out_proj(x, y_rnn, o, lw):
    n = x.shape[0]
    tm = min(ROW_TILE, n)
    nt = n // tm
    row = lambda w: pl.BlockSpec((tm, w), lambda i: (i, 0))
    vec = lambda w: _const_spec((1, w))
    return pl.pallas_call(
        functools.partial(_out_proj_kernel, tm=tm, alpha=lw["alpha"]),
        grid=(nt,),
        in_specs=[row(D_MODEL), row(D_RNN), row(D_ATTN), vec(D_RNN), vec(D_ATTN),
                  _const_spec((D_RNN, D_MODEL)), _const_spec((D_ATTN, D_MODEL)), vec(D_MODEL), vec(D_MODEL),
                  _const_spec((2 * N_EXPERTS, D_MODEL)), _const_spec((N_EXPERTS, 1))],
        out_specs=[row(D_MODEL), pl.BlockSpec((None, 2 * TOP_K, tm), lambda i: (i, 0, 0)), row(2 * TOP_K),
                   pl.BlockSpec((None, 1, N_EXPERTS), lambda i: (i, 0, 0))],
        out_shape=[jax.ShapeDtypeStruct((n, D_MODEL), F32), jax.ShapeDtypeStruct((nt, 2 * TOP_K, tm), F32),
                   jax.ShapeDtypeStruct((n, 2 * TOP_K), F32), jax.ShapeDtypeStruct((nt, 1, N_EXPERTS), I32)],
        compiler_params=_cparams("parallel"), name="out_proj_route",
    )(x, y_rnn, o, lw["gn_rnn"], lw["gn_attn"], lw["w_out_rnn"], lw["w_out_attn"], lw["ln1_g"], lw["ln1_b"],
      lw["router_wt"], lw["router_b"])


def _run_sizes(max_rows):
    sizes, s = [], SUBLANES
    while s <= max_rows:
        sizes.append(s)
        s *= 2
    return sizes[::-1]


def _for_each_piece(n8, max_rows, fn):
    for s in _run_sizes(max_rows):
        @pl.when((n8 & s) != 0)
        def _(s=s):
            fn(n8 & (-2 * s), s)


def _dispatch_kernel(cnt_ref, cnt_prev_ref, base_ref, x_ref, rrow_ref, *rest, tm):
    xbuf_ref, xs_ref, carry_ref, sems = rest[-4:]
    nrow = _sorted_rows(tm)
    i = pl.program_id(0)
    slot = i % 2

    @pl.when(i == 0)
    def _():
        def init(e, c):
            carry_ref[e] = base_ref[e]
            return c
        lax.fori_loop(0, N_EXPERTS, init, 0)

    r = lax.broadcasted_iota(I32, (nrow, 1), 0).astype(F32)
    hit = r == rrow_ref[0:1, :]
    for k in range(1, TOP_K):
        hit = jnp.logical_or(hit, r == rrow_ref[k:k + 1, :])
    perm = jnp.where(hit, 1.0, 0.0).astype(BF16)
    xs = jnp.dot(perm, x_ref[...].astype(BF16), preferred_element_type=F32)
    lo = lax.bitcast_convert_type(xs[:, :D_PACK], U32)
    hi = lax.bitcast_convert_type(xs[:, D_PACK:], U32)
    xs_ref[slot] = (hi & U32(HI_MASK)) | (lo >> 16)

    def run_copy(buf, src, dst, s):
        return pltpu.make_async_copy(xs_ref.at[buf, pl.ds(pl.multiple_of(src, SUBLANES), s)],
                                     xbuf_ref.at[pl.ds(pl.multiple_of(dst, SUBLANES), s)], sems.at[buf])

    def start(e, src):
        n8 = cnt_ref[0, e]
        dst = carry_ref[e]
        _for_each_piece(n8, tm, lambda off, s: run_copy(slot, src + off, dst + off, s).start())
        carry_ref[e] = dst + n8
        return src + n8

    def wait_tile(counts_ref, buf):
        def wait(e, c):
            _for_each_piece(counts_ref[0, e], tm, lambda off, s: run_copy(buf, 0, 0, s).wait())
            return c
        lax.fori_loop(0, N_EXPERTS, wait, 0)

    lax.fori_loop(0, N_EXPERTS, start, 0)

    @pl.when(i > 0)
    def _():
        wait_tile(cnt_prev_ref, 1 - slot)

    @pl.when(i == pl.num_programs(0) - 1)
    def _():
        wait_tile(cnt_ref, slot)


def _dispatch(x1, rrow, cnt8, base, xbuf, n_rows):
    n = x1.shape[0]
    nt, _, tm = rrow.shape
    in_specs = [pl.BlockSpec((None, 1, N_EXPERTS), lambda i: (i, 0, 0), memory_space=pltpu.SMEM),
                pl.BlockSpec((None, 1, N_EXPERTS), lambda i: (jnp.maximum(i - 1, 0), 0, 0),
                             memory_space=pltpu.SMEM),
                pl.BlockSpec(memory_space=pltpu.SMEM),
                pl.BlockSpec((tm, D_MODEL), lambda i: (i, 0)),
                pl.BlockSpec((None, 2 * TOP_K, tm), lambda i: (i, 0, 0))]
    args = [cnt8, cnt8, base, x1, rrow]
    aliases = {}
    if xbuf is not None:
        in_specs.append(pl.BlockSpec(memory_space=pl.ANY))
        args.append(xbuf)
        aliases = {len(args) - 1: 0}
    return pl.pallas_call(
        functools.partial(_dispatch_kernel, tm=tm),
        grid=(nt,),
        in_specs=in_specs,
        out_specs=pl.BlockSpec(memory_space=pl.ANY),
        out_shape=jax.ShapeDtypeStruct((n_rows, D_PACK), U32),
        scratch_shapes=[pltpu.VMEM((2, _sorted_rows(tm), D_PACK), U32), pltpu.SMEM((N_EXPERTS,), I32),
                        pltpu.SemaphoreType.DMA((2,))],
        input_output_aliases=aliases,
        compiler_params=_cparams("arbitrary"), name="moe_dispatch",
    )(*args)


def _moe_kernel(br_ref, be_ref, nv_ref, x_ref, w1_ref, b1_ref, w2_ref, b2_ref, y_ref, w1s_ref, w2s_ref):
    i = pl.program_id(0)
    e = be_ref[i]
    e_prev = be_ref[jnp.maximum(i - 1, 0)]

    @pl.when(jnp.logical_or(i == 0, e != e_prev))
    def _():
        w1s_ref[...] = w1_ref[...].astype(BF16)
        w2s_ref[...] = w2_ref[...].astype(BF16)

    nv = nv_ref[i]

    def expert_rows(m):
        rows = lax.broadcasted_iota(I32, (m, 1), 0)
        xw = jnp.where(rows < nv, x_ref[0:m, :], U32(0))
        lo = lax.bitcast_convert_type(xw << 16, F32).astype(BF16)
        hi = lax.bitcast_convert_type(xw & U32(HI_MASK), F32).astype(BF16)
        hdn = (jnp.dot(lo, w1s_ref[:D_PACK, :], preferred_element_type=F32)
               + jnp.dot(hi, w1s_ref[D_PACK:, :], preferred_element_type=F32) + b1_ref[...])
        g = jnp.minimum(hdn[:, :D_FF], SWIGLU_LIMIT)
        lin = jnp.clip(hdn[:, D_FF:], -SWIGLU_LIMIT, SWIGLU_LIMIT)
        act = g * jax.nn.sigmoid(SWIGLU_ALPHA * g) * (lin + 1.0)
        y_ref[0:m, :] = jnp.dot(act.astype(BF16), w2s_ref[...], preferred_element_type=F32) + b2_ref[...]

    @pl.when(nv > MOE_TILE // 2)
    def _():
        expert_rows(MOE_TILE)

    @pl.when(jnp.logical_and(nv > 0, nv <= MOE_TILE // 2))
    def _():
        expert_rows(MOE_TILE // 2)


def _moe_experts(xbuf, blk_row, blk_e, blk_nv, w1, b1, w2, b2, layer):
    n_blocks = xbuf.shape[0] // MOE_TILE
    grid_spec = pltpu.PrefetchScalarGridSpec(
        num_scalar_prefetch=3,
        grid=(n_blocks,),
        in_specs=[pl.BlockSpec((MOE_TILE, D_PACK), lambda i, br, be, nv: (br[i], 0)),
                  pl.BlockSpec((None, None, D_MODEL, 2 * D_FF), lambda i, br, be, nv: (layer, be[i], 0, 0)),
                  pl.BlockSpec((None, None, 1, 2 * D_FF), lambda i, br, be, nv: (layer, be[i], 0, 0)),
                  pl.BlockSpec((None, None, D_FF, D_MODEL), lambda i, br, be, nv: (layer, be[i], 0, 0)),
                  pl.BlockSpec((None, None, 1, D_MODEL), lambda i, br, be, nv: (layer, be[i], 0, 0))],
        out_specs=pl.BlockSpec((MOE_TILE, D_MODEL), lambda i, br, be, nv: (br[i], 0)),
        scratch_shapes=[pltpu.VMEM((D_MODEL, 2 * D_FF), BF16), pltpu.VMEM((D_FF, D_MODEL), BF16)],
    )
    depth = w1.shape[0]
    return pl.pallas_call(
        _moe_kernel, grid_spec=grid_spec,
        out_shape=jax.ShapeDtypeStruct((xbuf.shape[0], D_MODEL), F32),
        compiler_params=_cparams("arbitrary"), name="moe_experts",
    )(blk_row, blk_e, blk_nv, xbuf, w1, b1.reshape(depth, N_EXPERTS, 1, 2 * D_FF), w2,
      b2.reshape(depth, N_EXPERTS, 1, D_MODEL))


def _final_kernel(cnt_ref, cnt_next_ref, base_ref, x_ref, p_ref, rcol_ref, wg_ref, bg_ref, wp_ref, g2_ref, b2_ref,
                  ybuf_ref, out_ref, ys_ref, carry_ref, nrows_ref, sems, *, tm, alpha):
    nrow = _sorted_rows(tm)
    i = pl.program_id(0)
    slot = i % 2

    def run_copy(buf, src, dst, s):
        return pltpu.make_async_copy(ybuf_ref.at[pl.ds(pl.multiple_of(src, SUBLANES), s)],
                                     ys_ref.at[buf, pl.ds(pl.multiple_of(dst, SUBLANES), s)], sems.at[buf])

    def gather_tile(counts_ref, buf):
        def start(e, dst):
            n8 = counts_ref[0, e]
            src = carry_ref[e]
            _for_each_piece(n8, tm, lambda off, s: run_copy(buf, src + off, dst + off, s).start())
            carry_ref[e] = src + n8
            return dst + n8
        nrows_ref[buf] = lax.fori_loop(0, N_EXPERTS, start, 0)

    @pl.when(i == 0)
    def _():
        def init(e, c):
            carry_ref[e] = base_ref[e]
            return c
        lax.fori_loop(0, N_EXPERTS, init, 0)
        gather_tile(cnt_ref, 0)

    @pl.when(i + 1 < pl.num_programs(0))
    def _():
        gather_tile(cnt_next_ref, 1 - slot)

    x = x_ref[...]
    ple = (jax.nn.sigmoid(jnp.dot(x.astype(BF16), wg_ref[...], preferred_element_type=F32) + bg_ref[...])
           * jnp.dot(p_ref[...].astype(BF16), wp_ref[...], preferred_element_type=F32))

    def wait(e, c):
        _for_each_piece(cnt_ref[0, e], tm, lambda off, s: run_copy(slot, 0, 0, s).wait())
        return c
    lax.fori_loop(0, N_EXPERTS, wait, 0)

    r = lax.broadcasted_iota(I32, (nrow, 1), 0)
    ys = jnp.where(r < nrows_ref[slot], ys_ref[slot], 0.0).astype(BF16)
    col = lax.broadcasted_iota(I32, (1, nrow), 1).astype(F32)
    rc = rcol_ref[...]
    gmat = jnp.zeros((tm, nrow), F32)
    for k in range(TOP_K):
        gmat = gmat + jnp.where(rc[:, k:k + 1] == col, rc[:, TOP_K + k:TOP_K + k + 1], 0.0)
    ffn = jnp.dot(gmat.astype(BF16), ys, preferred_element_type=F32)
    out_ref[...] = _layer_norm(alpha * x + ffn + ple, g2_ref[...], b2_ref[...])


def _final(x1, p_all, layer, rcol, cnt8, base, ybuf, lw):
    n = x1.shape[0]
    nt = cnt8.shape[0]
    tm = n // nt
    row = lambda w: pl.BlockSpec((tm, w), lambda i: (i, 0))
    vec = lambda w: _const_spec((1, w))
    return pl.pallas_call(
        functools.partial(_final_kernel, tm=tm, alpha=lw["alpha"]),
        grid=(nt,),
        in_specs=[pl.BlockSpec((None, 1, N_EXPERTS), lambda i: (i, 0, 0), memory_space=pltpu.SMEM),
                  pl.BlockSpec((None, 1, N_EXPERTS), lambda i: (jnp.minimum(i + 1, nt - 1), 0, 0),
                               memory_space=pltpu.SMEM),
                  pl.BlockSpec(memory_space=pltpu.SMEM),
                  row(D_MODEL), pl.BlockSpec((None, tm, D_PLE), lambda i: (layer, i, 0)), row(2 * TOP_K),
                  _const_spec((D_MODEL, D_MODEL)), vec(D_MODEL), _const_spec((D_PLE, D_MODEL)),
                  vec(D_MODEL), vec(D_MODEL), pl.BlockSpec(memory_space=pl.ANY)],
        out_specs=row(D_MODEL),
        out_shape=jax.ShapeDtypeStruct((n, D_MODEL), F32),
        scratch_shapes=[pltpu.VMEM((2, _sorted_rows(tm), D_MODEL), F32), pltpu.SMEM((N_EXPERTS,), I32),
                        pltpu.SMEM((2,), I32), pltpu.SemaphoreType.DMA((2,))],
        compiler_params=_cparams("arbitrary"), name="moe_combine_ln2",
    )(cnt8, cnt8, base, x1, p_all, rcol, lw["ple_w_gate"], lw["ple_b_gate"], lw["ple_w_proj"], lw["ln2_g"], lw["ln2_b"],
      ybuf)


def _block_diag(w):
    nb, bw, _ = w.shape
    eye = jnp.eye(nb, dtype=w.dtype)
    return (w[:, :, None, :] * eye[:, None, :, None]).reshape(nb * bw, nb * bw)


def _hi_lo_rows(w):
    hi = w.astype(BF16)
    lo = (w - hi.astype(F32)).astype(BF16)
    return jnp.concatenate([hi, lo], axis=0)


def _block_geometry(tot, n_blocks):
    nblk = (tot + MOE_TILE - 1) // MOE_TILE
    blk_end = jnp.cumsum(nblk)
    blk_start = blk_end - nblk
    ids = jnp.arange(n_blocks, dtype=I32)
    n_valid = blk_end[-1]
    src = jnp.minimum(ids, n_valid - 1)
    e_of = jnp.minimum(jnp.sum(src[:, None] >= blk_end[None, :], axis=1), N_EXPERTS - 1).astype(I32)
    hot = e_of[:, None] == jnp.arange(N_EXPERTS)[None, :]
    start_of = jnp.sum(jnp.where(hot, blk_start[None, :], 0), axis=1)
    tot_of = jnp.sum(jnp.where(hot, tot[None, :], 0), axis=1)
    nv = jnp.where(ids < n_valid, jnp.clip(tot_of - (ids - start_of) * MOE_TILE, 0, MOE_TILE), 0)
    return src.astype(I32), e_of, nv.astype(I32), (blk_start * MOE_TILE).astype(I32)


def kernel(x_prompt, x_sample, cache_k, cache_v, state_conv, state_h, p_prompt, p_sample, ln_in_g, ln_in_b, rel_bias, w_in, conv_w, conv_b, rg_wa, rg_ba, rg_wx, rg_bx, rg_lambda, attn_sinks, gn_rnn, gn_attn, w_out, ln1_g, ln1_b, router_w, router_b, moe_w1, moe_b1, moe_w2, moe_b2, ple_w_gate, ple_b_gate, ple_w_proj, ln2_g, ln2_b):
    depth = w_in.shape[0]
    bsz, t_len, _ = x_prompt.shape
    n_p = bsz * t_len
    n_s = x_sample.shape[0]
    wb = cache_k.shape[2]
    alpha = (2 * depth) ** 0.25

    dist = jnp.arange(WINDOW)[:, None] + WINDOW - jnp.arange(2 * WINDOW)[None, :]
    valid = (dist >= 0) & (dist < WINDOW)
    bias_prompt = jnp.where(valid[None], _bias_lookup(rel_bias, dist), NEG)
    bias_sample = _bias_lookup(rel_bias, WINDOW - 1 - jnp.arange(WINDOW))

    tm_p = min(ROW_TILE, n_p)
    n_tiles = n_p // tm_p + 1
    max_rows = (n_p + n_s) * TOP_K + n_tiles * N_EXPERTS * (SUBLANES - 1)
    n_blocks = -(-(max_rows + N_EXPERTS * (MOE_TILE - 1)) // MOE_TILE)

    xp = x_prompt.reshape(n_p, D_MODEL)
    xs = x_sample.reshape(n_s, D_MODEL)
    pp = p_prompt.reshape(depth, n_p, D_PLE)
    ps = p_sample.reshape(depth, n_s, D_PLE)
    ck = cache_k.reshape(depth, n_s, wb, 128)
    cv = cache_v.reshape(depth, n_s, wb, 128)
    row2 = lambda v: v.reshape(1, -1)
    kv0 = 2 * D_RNN + D_ATTN
    outs = {k: [] for k in ("kp", "vp", "cp", "hp", "ks", "vs", "cs", "hs")}
    for l in range(depth):
        rw = {"conv_w": conv_w[l], "conv_b": row2(conv_b[l]), "wa": _block_diag(rg_wa[l]).astype(BF16),
              "ba": row2(rg_ba[l]), "wx": _block_diag(rg_wx[l]).astype(BF16), "bx": row2(rg_bx[l]),
              "lam": row2(rg_lambda[l])}
        lw = {"gn_rnn": row2(gn_rnn[l]), "gn_attn": row2(gn_attn[l]),
              "w_out_rnn": w_out[l, :D_RNN].astype(BF16), "w_out_attn": w_out[l, D_RNN:].astype(BF16),
              "ln1_g": row2(ln1_g[l]), "ln1_b": row2(ln1_b[l]), "router_wt": _hi_lo_rows(router_w[l].T),
              "router_b": router_b[l].reshape(N_EXPERTS, 1), "alpha": alpha,
              "ple_w_gate": ple_w_gate[l].astype(BF16), "ple_b_gate": row2(ple_b_gate[l]),
              "ple_w_proj": ple_w_proj[l].astype(BF16), "ln2_g": row2(ln2_g[l]), "ln2_b": row2(ln2_b[l])}
        w_in_l = w_in[l].astype(BF16)
        first = l == 0

        xp, zp = _in_proj(xp, row2(ln_in_g), row2(ln_in_b), w_in_l, first)
        zp3 = zp.reshape(bsz, t_len, D_IN)
        yp, hp = _rnn_prompt(zp3, rw)
        op = _attn_prompt(zp, bsz, t_len, bias_prompt, attn_sinks[l])
        wp = min(WINDOW, t_len)
        outs["kp"].append(zp3[:, t_len - wp:, kv0:kv0 + 128].reshape(bsz, wp, N_KV, HEAD_DIM))
        outs["vp"].append(zp3[:, t_len - wp:, kv0 + 128:kv0 + 256].reshape(bsz, wp, N_KV, HEAD_DIM))
        outs["cp"].append(zp3[:, t_len - (CONV_W - 1):, :D_RNN])
        outs["hp"].append(hp.reshape(bsz, D_RNN))

        xs, zs = _in_proj(xs, row2(ln_in_g), row2(ln_in_b), w_in_l, first)
        ys, hs = _rnn_sample(zs, state_conv[l], state_h[l], rw)
        os_, nk, nv = _attn_sample(zs[:, 2 * D_RNN:kv0], zs[:, kv0:kv0 + 128], zs[:, kv0 + 128:], ck, cv, l,
                                   bias_sample, attn_sinks[l])
        outs["ks"].append(nk.reshape(n_s, wb, N_KV, HEAD_DIM))
        outs["vs"].append(nv.reshape(n_s, wb, N_KV, HEAD_DIM))
        outs["cs"].append(jnp.concatenate([state_conv[l][:, 1:], zs[:, None, :D_RNN]], axis=1))
        outs["hs"].append(hs)

        x1p, rrow_p, rcol_p, cnt_p = _out_proj(xp, yp.reshape(n_p, D_RNN), op, lw)
        x1s, rrow_s, rcol_s, cnt_s = _out_proj(xs, ys, os_, lw)

        tot_p = jnp.sum(cnt_p, axis=(0, 1))
        tot = tot_p + jnp.sum(cnt_s, axis=(0, 1))
        blk_row, blk_e, blk_nv, base_p = _block_geometry(tot, n_blocks)
        base_s = base_p + tot_p

        xbuf = _dispatch(x1p, rrow_p, cnt_p, base_p, None, n_blocks * MOE_TILE)
        xbuf = _dispatch(x1s, rrow_s, cnt_s, base_s, xbuf, n_blocks * MOE_TILE)
        ybuf = _moe_experts(xbuf, blk_row, blk_e, blk_nv, moe_w1, moe_b1, moe_w2, moe_b2, l)
        xp = _final(x1p, pp, l, rcol_p, cnt_p, base_p, ybuf, lw)
        xs = _final(x1s, ps, l, rcol_s, cnt_s, base_s, ybuf, lw)

    st = lambda k: jnp.stack(outs[k])
    return (xp.reshape(bsz, t_len, D_MODEL), xs.reshape(n_s, 1, D_MODEL),
            st("kp"), st("vp"), st("cp"), st("hp"), st("ks"), st("vs"), st("cs"), st("hs"))
```

```python
import functools
import math

import jax
import jax.numpy as jnp
from jax import lax
from jax.experimental import pallas as pl
from jax.experimental.pallas import tpu as pltpu

F32 = jnp.float32
BF16 = jnp.bfloat16
I32 = jnp.int32
U32 = jnp.uint32

D_MODEL = 1024
D_RNN = 512
RNN_BLOCKS = 8
CONV_W = 4
LRU_C = 8.0
N_HEADS = 8
HEAD_DIM = 64
N_KV = 2
GQA_R = N_HEADS // N_KV
D_ATTN = N_HEADS * HEAD_DIM
WINDOW = 128
N_BUCKETS = 32
MAX_DIST = 128
N_EXPERTS = 32
TOP_K = 4
D_FF = 1024
SWIGLU_LIMIT = 7.0
SWIGLU_ALPHA = 1.702
D_PLE = 256
LN_EPS = 1e-5
D_IN = 2 * D_RNN + D_ATTN + 2 * N_KV * HEAD_DIM
NEG = -1e30

SUBLANES = 8
LANES = 128
ROW_TILE = 512
MOE_TILE = 512
D_PACK = D_MODEL // 2
VMEM_LIMIT = 56 * 1024 * 1024
HI_MASK = 0xFFFF0000


def _cparams(*sem):
    return pltpu.CompilerParams(dimension_semantics=sem, vmem_limit_bytes=VMEM_LIMIT)


def _layer_norm(x, g, b):
    mu = jnp.mean(x, axis=-1, keepdims=True)
    xc = x - mu
    var = jnp.mean(xc * xc, axis=-1, keepdims=True)
    return xc * lax.rsqrt(var + LN_EPS) * g + b


def _rms_norm(x, g):
    return x * lax.rsqrt(jnp.mean(x * x, axis=-1, keepdims=True) + LN_EPS) * g


def _const_spec(shape):
    return pl.BlockSpec(shape, lambda *_: (0,) * len(shape))


def _in_proj_kernel(x_ref, g_ref, b_ref, w_ref, *out_refs, apply_ln):
    x = x_ref[...]
    if apply_ln:
        x = _layer_norm(x, g_ref[...], b_ref[...])
        out_refs[0][...] = x
    out_refs[-1][...] = jnp.dot(x.astype(BF16), w_ref[...], preferred_element_type=F32)


def _in_proj(x, ln_g, ln_b, w_in_bf16, apply_ln):
    n = x.shape[0]
    tm = min(ROW_TILE, n)
    row = lambda w: pl.BlockSpec((tm, w), lambda i: (i, 0))
    out_shape = [jax.ShapeDtypeStruct((n, D_IN), F32)]
    out_specs = [row(D_IN)]
    if apply_ln:
        out_shape.insert(0, jax.ShapeDtypeStruct((n, D_MODEL), F32))
        out_specs.insert(0, row(D_MODEL))
    outs = pl.pallas_call(
        functools.partial(_in_proj_kernel, apply_ln=apply_ln),
        grid=(n // tm,),
        in_specs=[row(D_MODEL), _const_spec((1, D_MODEL)), _const_spec((1, D_MODEL)),
                  _const_spec((D_MODEL, D_IN))],
        out_specs=out_specs, out_shape=out_shape,
        compiler_params=_cparams("parallel"), name="in_proj",
    )(x, ln_g, ln_b, w_in_bf16)
    return (outs[0], outs[1]) if apply_ln else (x, outs[0])


def _rglru_coeffs(u, wa_ref, ba_ref, wx_ref, bx_ref, lam_ref):
    ub = u.astype(BF16)
    r = jax.nn.sigmoid(jnp.dot(ub, wa_ref[...], preferred_element_type=F32) + ba_ref[...])
    i = jax.nn.sigmoid(jnp.dot(ub, wx_ref[...], preferred_element_type=F32) + bx_ref[...])
    lam = -lam_ref[...]
    softplus = jnp.maximum(lam, 0.0) + jnp.log1p(jnp.exp(-jnp.abs(lam)))
    log_a = -LRU_C * r * softplus
    a = jnp.exp(log_a)
    b = jnp.sqrt(-jnp.tanh(log_a) * (a * a + 1.0)) * (i * u)
    return a, b


def _rnn_prompt_kernel(xr_ref, gate_ref, cw_ref, cb_ref, wa_ref, ba_ref, wx_ref, bx_ref, lam_ref,
                       y_ref, hlast_ref, xs_ref, a_ref, b_ref, carry_ref, *, tt):
    t = pl.program_id(1)

    @pl.when(t == 0)
    def _():
        xs_ref[0:SUBLANES, :] = jnp.zeros((SUBLANES, D_RNN), F32)
        carry_ref[...] = jnp.zeros((SUBLANES, D_RNN), F32)

    x = xr_ref[...]
    xs_ref[SUBLANES:SUBLANES + tt, :] = x
    cw = cw_ref[...]
    u = cb_ref[...] + x * cw[3:4]
    for j in range(CONV_W - 1):
        off = SUBLANES - (CONV_W - 1) + j
        u = u + xs_ref[off:off + tt, :] * cw[j:j + 1]
    xs_ref[0:SUBLANES, :] = x[tt - SUBLANES:tt, :]

    a, b = _rglru_coeffs(u, wa_ref, ba_ref, wx_ref, bx_ref, lam_ref)
    ng = tt // SUBLANES
    a = a.reshape(ng, SUBLANES, D_RNN)
    b = b.reshape(ng, SUBLANES, D_RNN)
    row = lax.broadcasted_iota(I32, (1, SUBLANES, 1), 1)
    for s in (1, 2, 4):
        a_sh = pltpu.roll(a, s, 1)
        b_sh = pltpu.roll(b, s, 1)
        m = row >= s
        b = jnp.where(m, a * b_sh + b, b)
        a = jnp.where(m, a * a_sh, a)
    a_ref[...] = a
    b_ref[...] = b

    def body(g, carry):
        h = a_ref[g] * carry + b_ref[g]
        b_ref[g] = h
        return jnp.broadcast_to(h[SUBLANES - 1:SUBLANES, :], (SUBLANES, D_RNN))

    carry = lax.fori_loop(0, ng, body, carry_ref[...])
    carry_ref[...] = carry
    h = b_ref[...].reshape(tt, D_RNN)
    y_ref[...] = h * jax.nn.gelu(gate_ref[...])
    hlast_ref[...] = carry[0:1, :]


def _rnn_prompt(z3, rw):
    bsz, t_len, _ = z3.shape
    tt = min(ROW_TILE, t_len)
    vec = _const_spec((1, D_RNN))
    return pl.pallas_call(
        functools.partial(_rnn_prompt_kernel, tt=tt),
        grid=(bsz, t_len // tt),
        in_specs=[pl.BlockSpec((None, tt, D_RNN), lambda b, t: (b, t, 0)),
                  pl.BlockSpec((None, tt, D_RNN), lambda b, t: (b, t, 1)),
                  _const_spec((CONV_W, D_RNN)), vec, _const_spec((D_RNN, D_RNN)), vec,
                  _const_spec((D_RNN, D_RNN)), vec, vec],
        out_specs=[pl.BlockSpec((None, tt, D_RNN), lambda b, t: (b, t, 0)),
                   pl.BlockSpec((None, 1, D_RNN), lambda b, t: (b, 0, 0))],
        out_shape=[jax.ShapeDtypeStruct((bsz, t_len, D_RNN), F32),
                   jax.ShapeDtypeStruct((bsz, 1, D_RNN), F32)],
        scratch_shapes=[pltpu.VMEM((tt + SUBLANES, D_RNN), F32),
                        pltpu.VMEM((tt // SUBLANES, SUBLANES, D_RNN), F32),
                        pltpu.VMEM((tt // SUBLANES, SUBLANES, D_RNN), F32),
                        pltpu.VMEM((SUBLANES, D_RNN), F32)],
        compiler_params=_cparams("parallel", "arbitrary"), name="rnn_prompt",
    )(z3, z3, rw["conv_w"], rw["conv_b"], rw["wa"], rw["ba"], rw["wx"], rw["bx"], rw["lam"])


def _rnn_sample_kernel(xr_ref, gate_ref, c0_ref, c1_ref, c2_ref, h0_ref, cw_ref, cb_ref, wa_ref, ba_ref,
                       wx_ref, bx_ref, lam_ref, y_ref, h_ref):
    x = xr_ref[...]
    cw = cw_ref[...]
    u = (cb_ref[...] + c0_ref[...] * cw[0:1] + c1_ref[...] * cw[1:2] + c2_ref[...] * cw[2:3]
         + x * cw[3:4])
    a, b = _rglru_coeffs(u, wa_ref, ba_ref, wx_ref, bx_ref, lam_ref)
    h = a * h0_ref[...] + b
    h_ref[...] = h
    y_ref[...] = h * jax.nn.gelu(gate_ref[...])


def _rnn_sample(z, conv_state, h0, rw):
    n = z.shape[0]
    vec = _const_spec((1, D_RNN))
    full = _const_spec((n, D_RNN))
    cs = [conv_state[:, j] for j in range(CONV_W - 1)]
    return pl.pallas_call(
        _rnn_sample_kernel,
        grid=(1,),
        in_specs=[pl.BlockSpec((n, D_RNN), lambda i: (0, 0)), pl.BlockSpec((n, D_RNN), lambda i: (0, 1)),
                  full, full, full, full,
                  _const_spec((CONV_W, D_RNN)), vec, _const_spec((D_RNN, D_RNN)), vec,
                  _const_spec((D_RNN, D_RNN)), vec, vec],
        out_specs=[full, full],
        out_shape=[jax.ShapeDtypeStruct((n, D_RNN), F32)] * 2,
        compiler_params=_cparams("arbitrary"), name="rnn_sample",
    )(z, z, *cs, h0, rw["conv_w"], rw["conv_b"], rw["wa"], rw["ba"], rw["wx"], rw["bx"], rw["lam"])


def _rel_bucket(dist):
    n = jnp.maximum(dist, 0)
    max_exact = N_BUCKETS // 2
    nf = jnp.maximum(n, max_exact).astype(F32)
    large = max_exact + (jnp.log(nf / max_exact) / math.log(MAX_DIST / max_exact)
                         * (N_BUCKETS - max_exact)).astype(I32)
    large = jnp.minimum(large, N_BUCKETS - 1)
    return jnp.where(n < max_exact, n, large)


def _bias_lookup(rel_bias, dist):
    hot = (_rel_bucket(dist)[..., None] == jnp.arange(N_BUCKETS)).astype(F32)
    out = jnp.tensordot(hot, rel_bias.astype(F32), axes=1, precision=lax.Precision.HIGHEST)
    return jnp.moveaxis(out, -1, 0)


def _half_lane_variants(x, group):
    lo = lax.broadcasted_iota(I32, (1, 2 * HEAD_DIM), 1) < HEAD_DIM
    xr = pltpu.roll(x, HEAD_DIM, 1)
    zero = jnp.zeros_like(x)
    if group == 0:
        return jnp.where(lo, x, zero), jnp.where(lo, zero, xr)
    return jnp.where(lo, xr, zero), jnp.where(lo, zero, x)


def _attn_prompt_kernel(q_ref, kc_ref, kp_ref, vc_ref, vp_ref, bias_ref, sink_ref, o_ref, *, nq):
    first = pl.program_id(1) == 0
    col = lax.broadcasted_iota(I32, (1, 2 * WINDOW), 1)
    hide_prev = jnp.logical_and(first, col < WINDOW)
    for j in range(nq):
        rows = slice(j * WINDOW, (j + 1) * WINDOW)
        before = slice((j - 1) * WINDOW, j * WINDOW)
        kk = jnp.concatenate([kp_ref[...] if j == 0 else kc_ref[before, :], kc_ref[rows, :]], axis=0)
        vv = jnp.concatenate([vp_ref[...] if j == 0 else vc_ref[before, :], vc_ref[rows, :]], axis=0)
        for g in range(N_KV):
            k_lo, k_hi = _half_lane_variants(kk, g)
            v_lo, v_hi = _half_lane_variants(vv, g)
            for pair in range(GQA_R // 2):
                p0 = g * (GQA_R // 2) + pair
                qp = (q_ref[rows, p0 * 128:(p0 + 1) * 128] * (HEAD_DIM ** -0.5)).astype(BF16)
                acc = jnp.zeros((WINDOW, 2 * HEAD_DIM), F32)
                for kx, vx, h in ((k_lo, v_lo, 2 * p0), (k_hi, v_hi, 2 * p0 + 1)):
                    s = lax.dot_general(qp, kx.astype(BF16), (((1,), (1,)), ((), ())),
                                        preferred_element_type=F32) + bias_ref[h]
                    if j == 0:
                        s = jnp.where(hide_prev, NEG, s)
                    sink = sink_ref[h]
                    m = jnp.maximum(jnp.max(s, axis=-1, keepdims=True), sink)
                    p = jnp.exp(s - m)
                    denom = jnp.sum(p, axis=-1, keepdims=True) + jnp.exp(sink - m)
                    acc = acc + jnp.dot(p.astype(BF16), vx.astype(BF16), preferred_element_type=F32) / denom
                o_ref[rows, p0 * 128:(p0 + 1) * 128] = acc


def _attn_prompt(z, bsz, t_len, bias_prompt, sinks):
    nb = t_len // WINDOW
    nq = min(4, nb)
    ns = nb // nq
    kcol = (2 * D_RNN + D_ATTN) // 128
    cur = lambda c: pl.BlockSpec((nq * WINDOW, 128), lambda b, i: (b * ns + i, c))
    prev = lambda c: pl.BlockSpec((WINDOW, 128), lambda b, i: (b * nb + jnp.maximum(i * nq - 1, 0), c))
    return pl.pallas_call(
        functools.partial(_attn_prompt_kernel, nq=nq),
        grid=(bsz, ns),
        in_specs=[pl.BlockSpec((nq * WINDOW, D_ATTN), lambda b, i: (b * ns + i, 2 * D_RNN // D_ATTN)),
                  cur(kcol), prev(kcol), cur(kcol + 1), prev(kcol + 1),
                  _const_spec((N_HEADS, WINDOW, 2 * WINDOW)),
                  pl.BlockSpec(memory_space=pltpu.SMEM)],
        out_specs=pl.BlockSpec((nq * WINDOW, D_ATTN), lambda b, i: (b * ns + i, 0)),
        out_shape=jax.ShapeDtypeStruct((bsz * t_len, D_ATTN), F32),
        compiler_params=_cparams("parallel", "arbitrary"), name="attn_prompt",
    )(z, z, z, z, z, bias_prompt, sinks)


def _attn_sample_kernel(q_ref, kn_ref, vn_ref, ck_ref, cv_ref, bias_ref, sink_ref, hmask_ref,
                        o_ref, nk_ref, nv_ref):
    pos = lax.broadcasted_iota(I32, (1, WINDOW, 1), 1)
    last = pos == WINDOW - 1
    bb = ck_ref.shape[0]
    newk = jnp.where(last, jnp.broadcast_to(kn_ref[...], (bb, WINDOW, 128)), pltpu.roll(ck_ref[...], WINDOW - 1, 1))
    newv = jnp.where(last, jnp.broadcast_to(vn_ref[...], (bb, WINDOW, 128)), pltpu.roll(cv_ref[...], WINDOW - 1, 1))
    nk_ref[...] = newk
    nv_ref[...] = newv
    s = lax.dot_general(q_ref[...].astype(BF16), newk.astype(BF16), (((2,), (2,)), ((0,), (0,))),
                        preferred_element_type=F32)
    s = s + bias_ref[...][None]
    sink = sink_ref[...][None]
    m = jnp.maximum(jnp.max(s, axis=-1, keepdims=True), sink)
    p = jnp.exp(s - m)
    denom = jnp.sum(p, axis=-1, keepdims=True) + jnp.exp(sink - m)
    o = lax.dot_general(p.astype(BF16), newv.astype(BF16), (((2,), (1,)), ((0,), (0,))),
                        preferred_element_type=F32)
    o_ref[...] = o / denom * hmask_ref[...][None]


def _attn_sample(q, k_new, v_new, k_cache, v_cache, layer, bias_sample, sinks):
    n = q.shape[0]
    bb = 16
    qh = q.reshape(n, N_KV, GQA_R, HEAD_DIM) * (HEAD_DIM ** -0.5)
    eye = jnp.eye(N_KV, dtype=F32)
    q8 = (qh[:, :, :, None, :] * eye[None, :, None, :, None]).reshape(n, N_HEADS, N_KV * HEAD_DIM)
    hmask = jnp.repeat(jnp.repeat(eye, GQA_R, axis=0), HEAD_DIM, axis=1)
    blk3 = lambda a, b: pl.BlockSpec((bb, a, b), lambda i: (i, 0, 0))
    cache = pl.BlockSpec((None, bb, WINDOW, 128), lambda i: (layer, i, 0, 0))
    o8, nk, nv = pl.pallas_call(
        _attn_sample_kernel,
        grid=(n // bb,),
        in_specs=[blk3(N_HEADS, 128), blk3(1, 128), blk3(1, 128), cache, cache,
                  _const_spec((N_HEADS, WINDOW)), _const_spec((N_HEADS, 1)), _const_spec((N_HEADS, 128))],
        out_specs=[blk3(N_HEADS, 128), blk3(WINDOW, 128), blk3(WINDOW, 128)],
        out_shape=[jax.ShapeDtypeStruct((n, N_HEADS, 128), F32),
                   jax.ShapeDtypeStruct((n, WINDOW, 128), F32),
                   jax.ShapeDtypeStruct((n, WINDOW, 128), F32)],
        compiler_params=_cparams("parallel"), name="attn_sample",
    )(q8, k_new.reshape(n, 1, 128), v_new.reshape(n, 1, 128), k_cache, v_cache, bias_sample,
      sinks.reshape(N_HEADS, 1), hmask)
    o = o8.reshape(n, N_KV, GQA_R, N_KV, HEAD_DIM)
    o = jnp.stack([o[:, g, :, g, :] for g in range(N_KV)], axis=1).reshape(n, D_ATTN)
    return o, nk, nv


def _sorted_rows(tm):
    return TOP_K * tm + N_EXPERTS * SUBLANES


def _out_proj_kernel(x_ref, yr_ref, o_ref, gnr_ref, gna_ref, wr_ref, wa_ref, g1_ref, b1_ref, rwt_ref, rb_ref,
                     x1_ref, rrow_ref, rcol_ref, cnt_ref, *, tm, alpha):
    yn = _rms_norm(yr_ref[...], gnr_ref[...]).astype(BF16)
    on = _rms_norm(o_ref[...], gna_ref[...]).astype(BF16)
    mix = (jnp.dot(yn, wr_ref[...], preferred_element_type=F32)
           + jnp.dot(on, wa_ref[...], preferred_element_type=F32))
    x1 = _layer_norm(alpha * x_ref[...] + mix, g1_ref[...], b1_ref[...])
    x1_ref[...] = x1

    x1h = x1.astype(BF16)
    x1l = (x1 - x1h.astype(F32)).astype(BF16)
    nt_dims = (((1,), (1,)), ((), ()))
    by_hi = lax.dot_general(rwt_ref[...], x1h, nt_dims, preferred_element_type=F32)
    by_lo = lax.dot_general(rwt_ref[0:N_EXPERTS, :], x1l, nt_dims, preferred_element_type=F32)
    logits = by_hi[0:N_EXPERTS] + by_hi[N_EXPERTS:] + by_lo + rb_ref[...]
    eidx = lax.broadcasted_iota(I32, (N_EXPERTS, tm), 0).astype(F32)
    work = logits
    vals, hots = [], []
    for _ in range(TOP_K):
        v = jnp.max(work, axis=0, keepdims=True)
        idx = jnp.min(jnp.where(work == v, eidx, float(N_EXPERTS)), axis=0, keepdims=True)
        hot = eidx == idx
        work = jnp.where(hot, -jnp.inf, work)
        vals.append(v)
        hots.append(hot.astype(F32))
    ex = [jnp.exp(v - vals[0]) for v in vals]
    tot = ex[0] + ex[1] + ex[2] + ex[3]
    tok_hot = hots[0] + hots[1] + hots[2] + hots[3]
    s_i = lax.broadcasted_iota(I32, (tm, tm), 0)
    t_i = lax.broadcasted_iota(I32, (tm, tm), 1)
    earlier = (s_i < t_i).astype(BF16)
    before = jnp.dot(tok_hot.astype(BF16), earlier, preferred_element_type=F32)
    cnt = jnp.sum(tok_hot, axis=1, keepdims=True)
    cnt8 = jnp.floor((cnt + (SUBLANES - 1)) * (1.0 / SUBLANES)) * SUBLANES
    cnt8_b = jnp.broadcast_to(cnt8, (N_EXPERTS, LANES))
    e_r = lax.broadcasted_iota(I32, (N_EXPERTS, N_EXPERTS), 0)
    e_c = lax.broadcasted_iota(I32, (N_EXPERTS, N_EXPERTS), 1)
    start = jnp.dot((e_c < e_r).astype(F32), cnt8_b, preferred_element_type=F32,
                    precision=lax.Precision.HIGHEST)[:, 0:1]
    slot = start + before
    rows = [jnp.sum(h * slot, axis=0, keepdims=True) for h in hots] + [e / tot for e in ex]
    rrow = jnp.concatenate(rows, axis=0)
    rrow_ref[...] = rrow
    rcol_ref[...] = rrow.T
    cnt_ref[...] = cnt8_b.T[0:1, 0:N_EXPERTS].astype(I32)


def _out_proj(x, y_rnn, o, lw):
    n = x.shape[0]
    tm = min(ROW_TILE, n)
    nt = n // tm
    row = lambda w: pl.BlockSpec((tm, w), lambda i: (i, 0))
    vec = lambda w: _const_spec((1, w))
    return pl.pallas_call(
        functools.partial(_out_proj_kernel, tm=tm, alpha=lw["alpha"]),
        grid=(nt,),
        in_specs=[row(D_MODEL), row(D_RNN), row(D_ATTN), vec(D_RNN), vec(D_ATTN),
                  _const_spec((D_RNN, D_MODEL)), _const_spec((D_ATTN, D_MODEL)), vec(D_MODEL), vec(D_MODEL),
                  _const_spec((2 * N_EXPERTS, D_MODEL)), _const_spec((N_EXPERTS, 1))],
        out_specs=[row(D_MODEL), pl.BlockSpec((None, 2 * TOP_K, tm), lambda i: (i, 0, 0)), row(2 * TOP_K),
                   pl.BlockSpec((None, 1, N_EXPERTS), lambda i: (i, 0, 0))],
        out_shape=[jax.ShapeDtypeStruct((n, D_MODEL), F32), jax.ShapeDtypeStruct((nt, 2 * TOP_K, tm), F32),
                   jax.ShapeDtypeStruct((n, 2 * TOP_K), F32), jax.ShapeDtypeStruct((nt, 1, N_EXPERTS), I32)],
        compiler_params=_cparams("parallel"), name="out_proj_route",
    )(x, y_rnn, o, lw["gn_rnn"], lw["gn_attn"], lw["w_out_rnn"], lw["w_out_attn"], lw["ln1_g"], lw["ln1_b"],
      lw["router_wt"], lw["router_b"])


RUN_CHUNK = 64
_SMALL_PIECES = (32, 16, 8)
_MAX_PAD_ROWS = N_EXPERTS * SUBLANES
_PAD_PIECES = (128, 64, 32, 16, 8)


def _for_each_piece(n8, fn):
    def chunk(j, c):
        fn(pl.multiple_of(j * RUN_CHUNK, RUN_CHUNK), RUN_CHUNK)
        return c
    lax.fori_loop(0, n8 // RUN_CHUNK, chunk, 0)
    for s in _SMALL_PIECES:
        @pl.when((n8 & s) != 0)
        def _(s=s):
            fn(n8 & (-2 * s), s)


def _wait_tile_rows(n_rows, tm, wait_rows):
    wait_rows(TOP_K * tm)
    pad = n_rows - TOP_K * tm
    for s in _PAD_PIECES:
        @pl.when((pad & s) != 0)
        def _(s=s):
            wait_rows(s)


def _dispatch_kernel(cnt_ref, base_ref, x_ref, rrow_ref, *rest, tm):
    xbuf_ref, xs_ref, carry_ref, nrows_ref, sems = rest[-5:]
    nrow = _sorted_rows(tm)
    i = pl.program_id(0)
    slot = i % 2

    @pl.when(i == 0)
    def _():
        def init(e, c):
            carry_ref[e] = base_ref[e]
            return c
        lax.fori_loop(0, N_EXPERTS, init, 0)

    r = lax.broadcasted_iota(I32, (nrow, 1), 0).astype(F32)
    hit = r == rrow_ref[0:1, :]
    for k in range(1, TOP_K):
        hit = jnp.logical_or(hit, r == rrow_ref[k:k + 1, :])
    perm = jnp.where(hit, 1.0, 0.0).astype(BF16)
    xs = jnp.dot(perm, x_ref[...].astype(BF16), preferred_element_type=F32)
    lo = lax.bitcast_convert_type(xs[:, :D_PACK], U32)
    hi = lax.bitcast_convert_type(xs[:, D_PACK:], U32)
    xs_ref[slot] = (hi & U32(HI_MASK)) | (lo >> 16)

    def run_copy(buf, src, dst, s):
        return pltpu.make_async_copy(xs_ref.at[buf, pl.ds(pl.multiple_of(src, SUBLANES), s)],
                                     xbuf_ref.at[pl.ds(pl.multiple_of(dst, SUBLANES), s)], sems.at[buf])

    def start(e, src):
        n8 = cnt_ref[0, e]
        dst = carry_ref[e]
        _for_each_piece(n8, lambda off, s: run_copy(slot, src + off, dst + off, s).start())
        carry_ref[e] = dst + n8
        return src + n8

    def wait_tile(buf):
        _wait_tile_rows(nrows_ref[buf], tm, lambda s: run_copy(buf, 0, 0, s).wait())

    nrows_ref[slot] = lax.fori_loop(0, N_EXPERTS, start, 0)

    @pl.when(i > 0)
    def _():
        wait_tile(1 - slot)

    @pl.when(i == pl.num_programs(0) - 1)
    def _():
        wait_tile(slot)


def _dispatch(x1, rrow, cnt8, base, xbuf, n_rows):
    n = x1.shape[0]
    nt, _, tm = rrow.shape
    in_specs = [pl.BlockSpec((None, 1, N_EXPERTS), lambda i: (i, 0, 0), memory_space=pltpu.SMEM),
                pl.BlockSpec(memory_space=pltpu.SMEM),
                pl.BlockSpec((tm, D_MODEL), lambda i: (i, 0)),
                pl.BlockSpec((None, 2 * TOP_K, tm), lambda i: (i, 0, 0))]
    args = [cnt8, base, x1, rrow]
    aliases = {}
    if xbuf is not None:
        in_specs.append(pl.BlockSpec(memory_space=pl.ANY))
        args.append(xbuf)
        aliases = {len(args) - 1: 0}
    return pl.pallas_call(
        functools.partial(_dispatch_kernel, tm=tm),
        grid=(nt,),
        in_specs=in_specs,
        out_specs=pl.BlockSpec(memory_space=pl.ANY),
        out_shape=jax.ShapeDtypeStruct((n_rows, D_PACK), U32),
        scratch_shapes=[pltpu.VMEM((2, _sorted_rows(tm), D_PACK), U32), pltpu.SMEM((N_EXPERTS,), I32),
                        pltpu.SMEM((2,), I32), pltpu.SemaphoreType.DMA((2,))],
        input_output_aliases=aliases,
        compiler_params=_cparams("arbitrary"), name="moe_dispatch",
    )(*args)


def _moe_kernel(br_ref, be_ref, nv_ref, first_ref, next_ref, x_ref, w1_hbm, b1_ref, w2_hbm, b2_ref, y_ref,
                w1f_ref, w2f_ref, w1s_ref, w2s_ref, sems, *, layer):
    i = pl.program_id(0)

    def load(e):
        return (pltpu.make_async_copy(w1_hbm.at[layer, e], w1f_ref, sems.at[0]),
                pltpu.make_async_copy(w2_hbm.at[layer, e], w2f_ref, sems.at[1]))

    @pl.when(jnp.logical_and(i == 0, first_ref[0] == 1))
    def _():
        for c in load(be_ref[0]):
            c.start()

    @pl.when(first_ref[i] == 1)
    def _():
        for c in load(be_ref[i]):
            c.wait()
        w1s_ref[...] = w1f_ref[...].astype(BF16)
        w2s_ref[...] = w2f_ref[...].astype(BF16)

        @pl.when(next_ref[i] >= 0)
        def _():
            for c in load(next_ref[i]):
                c.start()

    nv = nv_ref[i]

    def expert_rows(m):
        rows = lax.broadcasted_iota(I32, (m, 1), 0)
        xw = jnp.where(rows < nv, x_ref[0:m, :], U32(0))
        lo = lax.bitcast_convert_type(xw << 16, F32).astype(BF16)
        hi = lax.bitcast_convert_type(xw & U32(HI_MASK), F32).astype(BF16)
        hdn = (jnp.dot(lo, w1s_ref[:D_PACK, :], preferred_element_type=F32)
               + jnp.dot(hi, w1s_ref[D_PACK:, :], preferred_element_type=F32) + b1_ref[...])
        g = jnp.minimum(hdn[:, :D_FF], SWIGLU_LIMIT)
        lin = jnp.clip(hdn[:, D_FF:], -SWIGLU_LIMIT, SWIGLU_LIMIT)
        act = g * jax.nn.sigmoid(SWIGLU_ALPHA * g) * (lin + 1.0)
        y_ref[0:m, :] = jnp.dot(act.astype(BF16), w2s_ref[...], preferred_element_type=F32) + b2_ref[...]

    @pl.when(nv > MOE_TILE // 2)
    def _():
        expert_rows(MOE_TILE)

    @pl.when(jnp.logical_and(nv > 0, nv <= MOE_TILE // 2))
    def _():
        expert_rows(MOE_TILE // 2)


def _moe_experts(xbuf, geom, w1, b1, w2, b2, layer):
    n_blocks = xbuf.shape[0] // MOE_TILE
    bias = lambda w: pl.BlockSpec((None, None, 1, w), lambda i, br, be, *_: (layer, be[i], 0, 0))
    grid_spec = pltpu.PrefetchScalarGridSpec(
        num_scalar_prefetch=5,
        grid=(n_blocks,),
        in_specs=[pl.BlockSpec((MOE_TILE, D_PACK), lambda i, br, *_: (br[i], 0)),
                  pl.BlockSpec(memory_space=pl.ANY), bias(2 * D_FF),
                  pl.BlockSpec(memory_space=pl.ANY), bias(D_MODEL)],
        out_specs=pl.BlockSpec((MOE_TILE, D_MODEL), lambda i, br, *_: (br[i], 0)),
        scratch_shapes=[pltpu.VMEM((D_MODEL, 2 * D_FF), F32), pltpu.VMEM((D_FF, D_MODEL), F32),
                        pltpu.VMEM((D_MODEL, 2 * D_FF), BF16), pltpu.VMEM((D_FF, D_MODEL), BF16),
                        pltpu.SemaphoreType.DMA((2,))],
    )
    depth = w1.shape[0]
    return pl.pallas_call(
        functools.partial(_moe_kernel, layer=layer), grid_spec=grid_spec,
        out_shape=jax.ShapeDtypeStruct((xbuf.shape[0], D_MODEL), F32),
        compiler_params=_cparams("arbitrary"), name="moe_experts",
    )(*geom, xbuf, w1, b1.reshape(depth, N_EXPERTS, 1, 2 * D_FF), w2, b2.reshape(depth, N_EXPERTS, 1, D_MODEL))


def _final_kernel(cnt_ref, cnt_next_ref, base_ref, x_ref, p_ref, rcol_ref, wg_ref, bg_ref, wp_ref, g2_ref, b2_ref,
                  ybuf_ref, out_ref, ys_ref, carry_ref, nrows_ref, sems, *, tm, alpha):
    nrow = _sorted_rows(tm)
    i = pl.program_id(0)
    slot = i % 2

    def run_copy(buf, src, dst, s):
        return pltpu.make_async_copy(ybuf_ref.at[pl.ds(pl.multiple_of(src, SUBLANES), s)],
                                     ys_ref.at[buf, pl.ds(pl.multiple_of(dst, SUBLANES), s)], sems.at[buf])

    def gather_tile(counts_ref, buf):
        def start(e, dst):
            n8 = counts_ref[0, e]
            src = carry_ref[e]
            _for_each_piece(n8, lambda off, s: run_copy(buf, src + off, dst + off, s).start())
            carry_ref[e] = src + n8
            return dst + n8
        nrows_ref[buf] = lax.fori_loop(0, N_EXPERTS, start, 0)

    @pl.when(i == 0)
    def _():
        def init(e, c):
            carry_ref[e] = base_ref[e]
            return c
        lax.fori_loop(0, N_EXPERTS, init, 0)
        gather_tile(cnt_ref, 0)

    @pl.when(i + 1 < pl.num_programs(0))
    def _():
        gather_tile(cnt_next_ref, 1 - slot)

    x = x_ref[...]
    ple = (jax.nn.sigmoid(jnp.dot(x.astype(BF16), wg_ref[...], preferred_element_type=F32) + bg_ref[...])
           * jnp.dot(p_ref[...].astype(BF16), wp_ref[...], preferred_element_type=F32))

    _wait_tile_rows(nrows_ref[slot], tm, lambda s: run_copy(slot, 0, 0, s).wait())

    r = lax.broadcasted_iota(I32, (nrow, 1), 0)
    ys = jnp.where(r < nrows_ref[slot], ys_ref[slot], 0.0).astype(BF16)
    col = lax.broadcasted_iota(I32, (1, nrow), 1).astype(F32)
    rc = rcol_ref[...]
    gmat = jnp.zeros((tm, nrow), F32)
    for k in range(TOP_K):
        gmat = gmat + jnp.where(rc[:, k:k + 1] == col, rc[:, TOP_K + k:TOP_K + k + 1], 0.0)
    ffn = jnp.dot(gmat.astype(BF16), ys, preferred_element_type=F32)
    out_ref[...] = _layer_norm(alpha * x + ffn + ple, g2_ref[...], b2_ref[...])


def _final(x1, p_all, layer, rcol, cnt8, base, ybuf, lw):
    n = x1.shape[0]
    nt = cnt8.shape[0]
    tm = n // nt
    row = lambda w: pl.BlockSpec((tm, w), lambda i: (i, 0))
    vec = lambda w: _const_spec((1, w))
    return pl.pallas_call(
        functools.partial(_final_kernel, tm=tm, alpha=lw["alpha"]),
        grid=(nt,),
        in_specs=[pl.BlockSpec((None, 1, N_EXPERTS), lambda i: (i, 0, 0), memory_space=pltpu.SMEM),
                  pl.BlockSpec((None, 1, N_EXPERTS), lambda i: (jnp.minimum(i + 1, nt - 1), 0, 0),
                               memory_space=pltpu.SMEM),
                  pl.BlockSpec(memory_space=pltpu.SMEM),
                  row(D_MODEL), pl.BlockSpec((None, tm, D_PLE), lambda i: (layer, i, 0)), row(2 * TOP_K),
                  _const_spec((D_MODEL, D_MODEL)), vec(D_MODEL), _const_spec((D_PLE, D_MODEL)),
                  vec(D_MODEL), vec(D_MODEL), pl.BlockSpec(memory_space=pl.ANY)],
        out_specs=row(D_MODEL),
        out_shape=jax.ShapeDtypeStruct((n, D_MODEL), F32),
        scratch_shapes=[pltpu.VMEM((2, _sorted_rows(tm), D_MODEL), F32), pltpu.SMEM((N_EXPERTS,), I32),
                        pltpu.SMEM((2,), I32), pltpu.SemaphoreType.DMA((2,))],
        compiler_params=_cparams("arbitrary"), name="moe_combine_ln2",
    )(cnt8, cnt8, base, x1, p_all, rcol, lw["ple_w_gate"], lw["ple_b_gate"], lw["ple_w_proj"], lw["ln2_g"], lw["ln2_b"],
      ybuf)


def _block_diag(w):
    nb, bw, _ = w.shape
    eye = jnp.eye(nb, dtype=w.dtype)
    return (w[:, :, None, :] * eye[:, None, :, None]).reshape(nb * bw, nb * bw)


def _hi_lo_rows(w):
    hi = w.astype(BF16)
    lo = (w - hi.astype(F32)).astype(BF16)
    return jnp.concatenate([hi, lo], axis=0)


def _block_geometry(tot, n_blocks):
    nblk = (tot + MOE_TILE - 1) // MOE_TILE
    blk_end = jnp.cumsum(nblk)
    blk_start = blk_end - nblk
    ids = jnp.arange(n_blocks, dtype=I32)
    n_valid = blk_end[-1]
    expert_at = lambda j: jnp.minimum(jnp.sum(j[:, None] >= blk_end[None, :], axis=1), N_EXPERTS - 1).astype(I32)
    src = jnp.clip(ids, 0, jnp.maximum(n_valid - 1, 0))
    e_of = expert_at(src)
    hot = e_of[:, None] == jnp.arange(N_EXPERTS)[None, :]
    pick = lambda v: jnp.sum(jnp.where(hot, v[None, :], 0), axis=1)
    start_of, end_of, tot_of = pick(blk_start), pick(blk_end), pick(tot)
    real = ids < n_valid
    nv = jnp.where(real, jnp.clip(tot_of - (ids - start_of) * MOE_TILE, 0, MOE_TILE), 0)
    first = jnp.logical_and(real, ids == start_of)
    nxt = jnp.where(end_of < n_valid, expert_at(end_of), -1)
    geom = tuple(v.astype(I32) for v in (src, e_of, nv, first, nxt))
    return geom, (blk_start * MOE_TILE).astype(I32)


def kernel(x_prompt, x_sample, cache_k, cache_v, state_conv, state_h, p_prompt, p_sample, ln_in_g, ln_in_b, rel_bias, w_in, conv_w, conv_b, rg_wa, rg_ba, rg_wx, rg_bx, rg_lambda, attn_sinks, gn_rnn, gn_attn, w_out, ln1_g, ln1_b, router_w, router_b, moe_w1, moe_b1, moe_w2, moe_b2, ple_w_gate, ple_b_gate, ple_w_proj, ln2_g, ln2_b):
    depth = w_in.shape[0]
    bsz, t_len, _ = x_prompt.shape
    n_p = bsz * t_len
    n_s = x_sample.shape[0]
    wb = cache_k.shape[2]
    alpha = (2 * depth) ** 0.25

    dist = jnp.arange(WINDOW)[:, None] + WINDOW - jnp.arange(2 * WINDOW)[None, :]
    valid = (dist >= 0) & (dist < WINDOW)
    bias_prompt = jnp.where(valid[None], _bias_lookup(rel_bias, dist), NEG)
    bias_sample = _bias_lookup(rel_bias, WINDOW - 1 - jnp.arange(WINDOW))

    tm_p = min(ROW_TILE, n_p)
    n_tiles = n_p // tm_p + 1
    max_rows = (n_p + n_s) * TOP_K + n_tiles * N_EXPERTS * (SUBLANES - 1)
    n_blocks = -(-(max_rows + N_EXPERTS * (MOE_TILE - 1)) // MOE_TILE)

    xp = x_prompt.reshape(n_p, D_MODEL)
    xs = x_sample.reshape(n_s, D_MODEL)
    pp = p_prompt.reshape(depth, n_p, D_PLE)
    ps = p_sample.reshape(depth, n_s, D_PLE)
    ck = cache_k.reshape(depth, n_s, wb, 128)
    cv = cache_v.reshape(depth, n_s, wb, 128)
    row2 = lambda v: v.reshape(1, -1)
    kv0 = 2 * D_RNN + D_ATTN
    outs = {k: [] for k in ("kp", "vp", "cp", "hp", "ks", "vs", "cs", "hs")}
    for l in range(depth):
        rw = {"conv_w": conv_w[l], "conv_b": row2(conv_b[l]), "wa": _block_diag(rg_wa[l]).astype(BF16),
              "ba": row2(rg_ba[l]), "wx": _block_diag(rg_wx[l]).astype(BF16), "bx": row2(rg_bx[l]),
              "lam": row2(rg_lambda[l])}
        lw = {"gn_rnn": row2(gn_rnn[l]), "gn_attn": row2(gn_attn[l]),
              "w_out_rnn": w_out[l, :D_RNN].astype(BF16), "w_out_attn": w_out[l, D_RNN:].astype(BF16),
              "ln1_g": row2(ln1_g[l]), "ln1_b": row2(ln1_b[l]), "router_wt": _hi_lo_rows(router_w[l].T),
              "router_b": router_b[l].reshape(N_EXPERTS, 1), "alpha": alpha,
              "ple_w_gate": ple_w_gate[l].astype(BF16), "ple_b_gate": row2(ple_b_gate[l]),
              "ple_w_proj": ple_w_proj[l].astype(BF16), "ln2_g": row2(ln2_g[l]), "ln2_b": row2(ln2_b[l])}
        w_in_l = w_in[l].astype(BF16)
        first = l == 0

        xp, zp = _in_proj(xp, row2(ln_in_g), row2(ln_in_b), w_in_l, first)
        zp3 = zp.reshape(bsz, t_len, D_IN)
        yp, hp = _rnn_prompt(zp3, rw)
        op = _attn_prompt(zp, bsz, t_len, bias_prompt, attn_sinks[l])
        wp = min(WINDOW, t_len)
        outs["kp"].append(zp3[:, t_len - wp:, kv0:kv0 + 128].reshape(bsz, wp, N_KV, HEAD_DIM))
        outs["vp"].append(zp3[:, t_len - wp:, kv0 + 128:kv0 + 256].reshape(bsz, wp, N_KV, HEAD_DIM))
        outs["cp"].append(zp3[:, t_len - (CONV_W - 1):, :D_RNN])
        outs["hp"].append(hp.reshape(bsz, D_RNN))

        xs, zs = _in_proj(xs, row2(ln_in_g), row2(ln_in_b), w_in_l, first)
        ys, hs = _rnn_sample(zs, state_conv[l], state_h[l], rw)
        os_, nk, nv = _attn_sample(zs[:, 2 * D_RNN:kv0], zs[:, kv0:kv0 + 128], zs[:, kv0 + 128:], ck, cv, l,
                                   bias_sample, attn_sinks[l])
        outs["ks"].append(nk.reshape(n_s, wb, N_KV, HEAD_DIM))
        outs["vs"].append(nv.reshape(n_s, wb, N_KV, HEAD_DIM))
        outs["cs"].append(jnp.concatenate([state_conv[l][:, 1:], zs[:, None, :D_RNN]], axis=1))
        outs["hs"].append(hs)

        x1p, rrow_p, rcol_p, cnt_p = _out_proj(xp, yp.reshape(n_p, D_RNN), op, lw)
        x1s, rrow_s, rcol_s, cnt_s = _out_proj(xs, ys, os_, lw)

        tot_p = jnp.sum(cnt_p, axis=(0, 1))
        tot = tot_p + jnp.sum(cnt_s, axis=(0, 1))
        geom, base_p = _block_geometry(tot, n_blocks)
        base_s = base_p + tot_p

        xbuf = _dispatch(x1p, rrow_p, cnt_p, base_p, None, n_blocks * MOE_TILE)
        xbuf = _dispatch(x1s, rrow_s, cnt_s, base_s, xbuf, n_blocks * MOE_TILE)
        ybuf = _moe_experts(xbuf, geom, moe_w1, moe_b1, moe_w2, moe_b2, l)
        xp = _final(x1p, pp, l, rcol_p, cnt_p, base_p, ybuf, lw)
        xs = _final(x1s, ps, l, rcol_s, cnt_s, base_s, ybuf, lw)

    st = lambda k: jnp.stack(outs[k])
    return (xp.reshape(bsz, t_len, D_MODEL), xs.reshape(n_s, 1, D_MODEL),
            st("kp"), st("vp"), st("cp"), st("hp"), st("ks"), st("vs"), st("cs"), st("hs"))
```

```python
import functools
import math

import jax
import jax.numpy as jnp
from jax import lax
from jax.experimental import pallas as pl
from jax.experimental.pallas import tpu as pltpu

F32 = jnp.float32
BF16 = jnp.bfloat16
I32 = jnp.int32
U32 = jnp.uint32

D_MODEL = 1024
D_RNN = 512
RNN_BLOCKS = 8
CONV_W = 4
LRU_C = 8.0
N_HEADS = 8
HEAD_DIM = 64
N_KV = 2
GQA_R = N_HEADS // N_KV
D_ATTN = N_HEADS * HEAD_DIM
WINDOW = 128
N_BUCKETS = 32
MAX_DIST = 128
N_EXPERTS = 32
TOP_K = 4
D_FF = 1024
SWIGLU_LIMIT = 7.0
SWIGLU_ALPHA = 1.702
D_PLE = 256
LN_EPS = 1e-5
D_IN = 2 * D_RNN + D_ATTN + 2 * N_KV * HEAD_DIM
NEG = -1e30

SUBLANES = 8
LANES = 128
ROW_TILE = 512
MOE_TILE = 512
D_PACK = D_MODEL // 2
VMEM_LIMIT = 56 * 1024 * 1024
HI_MASK = 0xFFFF0000


def _cparams(*sem):
    return pltpu.CompilerParams(dimension_semantics=sem, vmem_limit_bytes=VMEM_LIMIT)


def _layer_norm(x, g, b):
    mu = jnp.mean(x, axis=-1, keepdims=True)
    xc = x - mu
    var = jnp.mean(xc * xc, axis=-1, keepdims=True)
    return xc * lax.rsqrt(var + LN_EPS) * g + b


def _rms_norm(x, g):
    return x * lax.rsqrt(jnp.mean(x * x, axis=-1, keepdims=True) + LN_EPS) * g


def _const_spec(shape):
    return pl.BlockSpec(shape, lambda *_: (0,) * len(shape))


def _layer_spec(layer, *tail):
    return pl.BlockSpec((None,) + tail, lambda *_: (layer,) + (0,) * len(tail))


def _in_proj_kernel(x_ref, g_ref, b_ref, w_ref, *out_refs, apply_ln):
    x = x_ref[...]
    if apply_ln:
        x = _layer_norm(x, g_ref[...], b_ref[...])
        out_refs[0][...] = x
    out_refs[-1][...] = jnp.dot(x.astype(BF16), w_ref[...], preferred_element_type=F32)


def _in_proj(x, ln_g, ln_b, w_in_bf16, layer, apply_ln):
    n = x.shape[0]
    tm = min(ROW_TILE, n)
    row = lambda w: pl.BlockSpec((tm, w), lambda i: (i, 0))
    out_shape = [jax.ShapeDtypeStruct((n, D_IN), F32)]
    out_specs = [row(D_IN)]
    if apply_ln:
        out_shape.insert(0, jax.ShapeDtypeStruct((n, D_MODEL), F32))
        out_specs.insert(0, row(D_MODEL))
    outs = pl.pallas_call(
        functools.partial(_in_proj_kernel, apply_ln=apply_ln),
        grid=(n // tm,),
        in_specs=[row(D_MODEL), _const_spec((1, D_MODEL)), _const_spec((1, D_MODEL)),
                  _layer_spec(layer, D_MODEL, D_IN)],
        out_specs=out_specs, out_shape=out_shape,
        compiler_params=_cparams("parallel"), name="in_proj",
    )(x, ln_g, ln_b, w_in_bf16)
    return (outs[0], outs[1]) if apply_ln else (x, outs[0])


def _rglru_coeffs(u, wa_ref, ba_ref, wx_ref, bx_ref, lam_ref):
    ub = u.astype(BF16)
    r = jax.nn.sigmoid(jnp.dot(ub, wa_ref[...], preferred_element_type=F32) + ba_ref[...])
    i = jax.nn.sigmoid(jnp.dot(ub, wx_ref[...], preferred_element_type=F32) + bx_ref[...])
    lam = -lam_ref[...]
    softplus = jnp.maximum(lam, 0.0) + jnp.log1p(jnp.exp(-jnp.abs(lam)))
    log_a = -LRU_C * r * softplus
    a = jnp.exp(log_a)
    b = jnp.sqrt(-jnp.tanh(log_a) * (a * a + 1.0)) * (i * u)
    return a, b


def _rnn_prompt_kernel(xr_ref, gate_ref, cw_ref, cb_ref, wa_ref, ba_ref, wx_ref, bx_ref, lam_ref,
                       y_ref, hlast_ref, xs_ref, a_ref, b_ref, carry_ref, *, tt):
    t = pl.program_id(1)

    @pl.when(t == 0)
    def _():
        xs_ref[0:SUBLANES, :] = jnp.zeros((SUBLANES, D_RNN), F32)
        carry_ref[...] = jnp.zeros((SUBLANES, D_RNN), F32)

    x = xr_ref[...]
    xs_ref[SUBLANES:SUBLANES + tt, :] = x
    cw = cw_ref[...]
    u = cb_ref[...] + x * cw[3:4]
    for j in range(CONV_W - 1):
        off = SUBLANES - (CONV_W - 1) + j
        u = u + xs_ref[off:off + tt, :] * cw[j:j + 1]
    xs_ref[0:SUBLANES, :] = x[tt - SUBLANES:tt, :]

    a, b = _rglru_coeffs(u, wa_ref, ba_ref, wx_ref, bx_ref, lam_ref)
    ng = tt // SUBLANES
    a = a.reshape(ng, SUBLANES, D_RNN)
    b = b.reshape(ng, SUBLANES, D_RNN)
    row = lax.broadcasted_iota(I32, (1, SUBLANES, 1), 1)
    for s in (1, 2, 4):
        a_sh = pltpu.roll(a, s, 1)
        b_sh = pltpu.roll(b, s, 1)
        m = row >= s
        b = jnp.where(m, a * b_sh + b, b)
        a = jnp.where(m, a * a_sh, a)
    a_ref[...] = a
    b_ref[...] = b

    def body(g, carry):
        h = a_ref[g] * carry + b_ref[g]
        b_ref[g] = h
        return jnp.broadcast_to(h[SUBLANES - 1:SUBLANES, :], (SUBLANES, D_RNN))

    carry = lax.fori_loop(0, ng, body, carry_ref[...])
    carry_ref[...] = carry
    h = b_ref[...].reshape(tt, D_RNN)
    y_ref[...] = h * jax.nn.gelu(gate_ref[...])
    hlast_ref[...] = carry[0:1, :]


def _rnn_prompt(z3, rw):
    bsz, t_len, _ = z3.shape
    tt = min(ROW_TILE, t_len)
    vec = _layer_spec(rw["layer"], 1, D_RNN)
    mat = _layer_spec(rw["layer"], D_RNN, D_RNN)
    return pl.pallas_call(
        functools.partial(_rnn_prompt_kernel, tt=tt),
        grid=(bsz, t_len // tt),
        in_specs=[pl.BlockSpec((None, tt, D_RNN), lambda b, t: (b, t, 0)),
                  pl.BlockSpec((None, tt, D_RNN), lambda b, t: (b, t, 1)),
                  _layer_spec(rw["layer"], CONV_W, D_RNN), vec, mat, vec, mat, vec, vec],
        out_specs=[pl.BlockSpec((None, tt, D_RNN), lambda b, t: (b, t, 0)),
                   pl.BlockSpec((None, 1, D_RNN), lambda b, t: (b, 0, 0))],
        out_shape=[jax.ShapeDtypeStruct((bsz, t_len, D_RNN), F32),
                   jax.ShapeDtypeStruct((bsz, 1, D_RNN), F32)],
        scratch_shapes=[pltpu.VMEM((tt + SUBLANES, D_RNN), F32),
                        pltpu.VMEM((tt // SUBLANES, SUBLANES, D_RNN), F32),
                        pltpu.VMEM((tt // SUBLANES, SUBLANES, D_RNN), F32),
                        pltpu.VMEM((SUBLANES, D_RNN), F32)],
        compiler_params=_cparams("parallel", "arbitrary"), name="rnn_prompt",
    )(z3, z3, rw["conv_w"], rw["conv_b"], rw["wa"], rw["ba"], rw["wx"], rw["bx"], rw["lam"])


def _rnn_sample_kernel(xr_ref, gate_ref, c0_ref, c1_ref, c2_ref, h0_ref, cw_ref, cb_ref, wa_ref, ba_ref,
                       wx_ref, bx_ref, lam_ref, y_ref, h_ref):
    x = xr_ref[...]
    cw = cw_ref[...]
    u = (cb_ref[...] + c0_ref[...] * cw[0:1] + c1_ref[...] * cw[1:2] + c2_ref[...] * cw[2:3]
         + x * cw[3:4])
    a, b = _rglru_coeffs(u, wa_ref, ba_ref, wx_ref, bx_ref, lam_ref)
    h = a * h0_ref[...] + b
    h_ref[...] = h
    y_ref[...] = h * jax.nn.gelu(gate_ref[...])


def _rnn_sample(z, conv_state, h0, rw):
    n = z.shape[0]
    layer = rw["layer"]
    vec = _layer_spec(layer, 1, D_RNN)
    mat = _layer_spec(layer, D_RNN, D_RNN)
    full = _const_spec((n, D_RNN))
    depth = conv_state.shape[0]
    conv_flat = conv_state.reshape(depth, n, (CONV_W - 1) * D_RNN)
    tap = lambda j: pl.BlockSpec((None, n, D_RNN), lambda i: (layer, 0, j))
    return pl.pallas_call(
        _rnn_sample_kernel,
        grid=(1,),
        in_specs=[pl.BlockSpec((n, D_RNN), lambda i: (0, 0)), pl.BlockSpec((n, D_RNN), lambda i: (0, 1)),
                  tap(0), tap(1), tap(2), _layer_spec(layer, n, D_RNN),
                  _layer_spec(layer, CONV_W, D_RNN), vec, mat, vec, mat, vec, vec],
        out_specs=[full, full],
        out_shape=[jax.ShapeDtypeStruct((n, D_RNN), F32)] * 2,
        compiler_params=_cparams("arbitrary"), name="rnn_sample",
    )(z, z, conv_flat, conv_flat, conv_flat, h0, rw["conv_w"], rw["conv_b"], rw["wa"], rw["ba"], rw["wx"], rw["bx"], rw["lam"])


def _rel_bucket(dist):
    n = jnp.maximum(dist, 0)
    max_exact = N_BUCKETS // 2
    nf = jnp.maximum(n, max_exact).astype(F32)
    large = max_exact + (jnp.log(nf / max_exact) / math.log(MAX_DIST / max_exact)
                         * (N_BUCKETS - max_exact)).astype(I32)
    large = jnp.minimum(large, N_BUCKETS - 1)
    return jnp.where(n < max_exact, n, large)


def _bias_lookup(rel_bias, dist):
    hot = (_rel_bucket(dist)[..., None] == jnp.arange(N_BUCKETS)).astype(F32)
    out = jnp.tensordot(hot, rel_bias.astype(F32), axes=1, precision=lax.Precision.HIGHEST)
    return jnp.moveaxis(out, -1, 0)


def _half_lane_variants(x, group):
    lo = lax.broadcasted_iota(I32, (1, 2 * HEAD_DIM), 1) < HEAD_DIM
    xr = pltpu.roll(x, HEAD_DIM, 1)
    zero = jnp.zeros_like(x)
    if group == 0:
        return jnp.where(lo, x, zero), jnp.where(lo, zero, xr)
    return jnp.where(lo, xr, zero), jnp.where(lo, zero, x)


def _attn_prompt_kernel(q_ref, kc_ref, kp_ref, vc_ref, vp_ref, bias_ref, sink_ref, o_ref, *, nq, layer):
    first = pl.program_id(1) == 0
    col = lax.broadcasted_iota(I32, (1, 2 * WINDOW), 1)
    hide_prev = jnp.logical_and(first, col < WINDOW)
    for j in range(nq):
        rows = slice(j * WINDOW, (j + 1) * WINDOW)
        before = slice((j - 1) * WINDOW, j * WINDOW)
        kk = jnp.concatenate([kp_ref[...] if j == 0 else kc_ref[before, :], kc_ref[rows, :]], axis=0)
        vv = jnp.concatenate([vp_ref[...] if j == 0 else vc_ref[before, :], vc_ref[rows, :]], axis=0)
        for g in range(N_KV):
            k_lo, k_hi = _half_lane_variants(kk, g)
            v_lo, v_hi = _half_lane_variants(vv, g)
            for pair in range(GQA_R // 2):
                p0 = g * (GQA_R // 2) + pair
                qp = (q_ref[rows, p0 * 128:(p0 + 1) * 128] * (HEAD_DIM ** -0.5)).astype(BF16)
                acc = jnp.zeros((WINDOW, 2 * HEAD_DIM), F32)
                for kx, vx, h in ((k_lo, v_lo, 2 * p0), (k_hi, v_hi, 2 * p0 + 1)):
                    s = lax.dot_general(qp, kx.astype(BF16), (((1,), (1,)), ((), ())),
                                        preferred_element_type=F32) + bias_ref[h]
                    if j == 0:
                        s = jnp.where(hide_prev, NEG, s)
                    sink = sink_ref[layer, h]
                    m = jnp.maximum(jnp.max(s, axis=-1, keepdims=True), sink)
                    p = jnp.exp(s - m)
                    denom = jnp.sum(p, axis=-1, keepdims=True) + jnp.exp(sink - m)
                    acc = acc + jnp.dot(p.astype(BF16), vx.astype(BF16), preferred_element_type=F32) / denom
                o_ref[rows, p0 * 128:(p0 + 1) * 128] = acc


def _attn_prompt(z, bsz, t_len, bias_prompt, sinks, layer):
    nb = t_len // WINDOW
    nq = min(4, nb)
    ns = nb // nq
    kcol = (2 * D_RNN + D_ATTN) // 128
    cur = lambda c: pl.BlockSpec((nq * WINDOW, 128), lambda b, i: (b * ns + i, c))
    prev = lambda c: pl.BlockSpec((WINDOW, 128), lambda b, i: (b * nb + jnp.maximum(i * nq - 1, 0), c))
    return pl.pallas_call(
        functools.partial(_attn_prompt_kernel, nq=nq, layer=layer),
        grid=(bsz, ns),
        in_specs=[pl.BlockSpec((nq * WINDOW, D_ATTN), lambda b, i: (b * ns + i, 2 * D_RNN // D_ATTN)),
                  cur(kcol), prev(kcol), cur(kcol + 1), prev(kcol + 1),
                  _const_spec((N_HEADS, WINDOW, 2 * WINDOW)),
                  pl.BlockSpec(memory_space=pltpu.SMEM)],
        out_specs=pl.BlockSpec((nq * WINDOW, D_ATTN), lambda b, i: (b * ns + i, 0)),
        out_shape=jax.ShapeDtypeStruct((bsz * t_len, D_ATTN), F32),
        compiler_params=_cparams("parallel", "arbitrary"), name="attn_prompt",
    )(z, z, z, z, z, bias_prompt, sinks)


def _attn_sample_kernel(q_ref, kn_ref, vn_ref, ck_ref, cv_ref, bias_ref, sink_ref, hmask_ref,
                        o_ref, nk_ref, nv_ref):
    pos = lax.broadcasted_iota(I32, (1, WINDOW, 1), 1)
    last = pos == WINDOW - 1
    bb = ck_ref.shape[0]
    newk = jnp.where(last, jnp.broadcast_to(kn_ref[...], (bb, WINDOW, 128)), pltpu.roll(ck_ref[...], WINDOW - 1, 1))
    newv = jnp.where(last, jnp.broadcast_to(vn_ref[...], (bb, WINDOW, 128)), pltpu.roll(cv_ref[...], WINDOW - 1, 1))
    nk_ref[...] = newk
    nv_ref[...] = newv
    s = lax.dot_general(q_ref[...].astype(BF16), newk.astype(BF16), (((2,), (2,)), ((0,), (0,))),
                        preferred_element_type=F32)
    s = s + bias_ref[...][None]
    sink = sink_ref[...][None]
    m = jnp.maximum(jnp.max(s, axis=-1, keepdims=True), sink)
    p = jnp.exp(s - m)
    denom = jnp.sum(p, axis=-1, keepdims=True) + jnp.exp(sink - m)
    o = lax.dot_general(p.astype(BF16), newv.astype(BF16), (((2,), (1,)), ((0,), (0,))),
                        preferred_element_type=F32)
    o_ref[...] = o / denom * hmask_ref[...][None]


def _attn_sample(q, k_new, v_new, k_cache, v_cache, layer, bias_sample, sinks):
    n = q.shape[0]
    bb = 16
    qh = q.reshape(n, N_KV, GQA_R, HEAD_DIM) * (HEAD_DIM ** -0.5)
    eye = jnp.eye(N_KV, dtype=F32)
    q8 = (qh[:, :, :, None, :] * eye[None, :, None, :, None]).reshape(n, N_HEADS, N_KV * HEAD_DIM)
    hmask = jnp.repeat(jnp.repeat(eye, GQA_R, axis=0), HEAD_DIM, axis=1)
    blk3 = lambda a, b: pl.BlockSpec((bb, a, b), lambda i: (i, 0, 0))
    cache = pl.BlockSpec((None, bb, WINDOW, 128), lambda i: (layer, i, 0, 0))
    o8, nk, nv = pl.pallas_call(
        _attn_sample_kernel,
        grid=(n // bb,),
        in_specs=[blk3(N_HEADS, 128), blk3(1, 128), blk3(1, 128), cache, cache,
                  _const_spec((N_HEADS, WINDOW)), _layer_spec(layer, N_HEADS, 1), _const_spec((N_HEADS, 128))],
        out_specs=[blk3(N_HEADS, 128), blk3(WINDOW, 128), blk3(WINDOW, 128)],
        out_shape=[jax.ShapeDtypeStruct((n, N_HEADS, 128), F32),
                   jax.ShapeDtypeStruct((n, WINDOW, 128), F32),
                   jax.ShapeDtypeStruct((n, WINDOW, 128), F32)],
        compiler_params=_cparams("parallel"), name="attn_sample",
    )(q8, k_new.reshape(n, 1, 128), v_new.reshape(n, 1, 128), k_cache, v_cache, bias_sample,
      sinks.reshape(-1, N_HEADS, 1), hmask)
    o = o8.reshape(n, N_KV, GQA_R, N_KV, HEAD_DIM)
    o = jnp.stack([o[:, g, :, g, :] for g in range(N_KV)], axis=1).reshape(n, D_ATTN)
    return o, nk, nv


ONE_HOT_BLOCK = 256


def _sorted_rows(tm):
    return TOP_K * tm + N_EXPERTS * SUBLANES


def _out_proj_kernel(x_ref, yr_ref, o_ref, gnr_ref, gna_ref, wr_ref, wa_ref, g1_ref, b1_ref, rwt_ref, rb_ref,
                     x1_ref, rrow_ref, rcol_ref, cnt_ref, *, tm, alpha):
    yn = _rms_norm(yr_ref[...], gnr_ref[...]).astype(BF16)
    on = _rms_norm(o_ref[...], gna_ref[...]).astype(BF16)
    mix = (jnp.dot(yn, wr_ref[...], preferred_element_type=F32)
           + jnp.dot(on, wa_ref[...], preferred_element_type=F32))
    x1 = _layer_norm(alpha * x_ref[...] + mix, g1_ref[...], b1_ref[...])
    x1_ref[...] = x1

    x1h = x1.astype(BF16)
    x1l = (x1 - x1h.astype(F32)).astype(BF16)
    nt_dims = (((1,), (1,)), ((), ()))
    by_hi = lax.dot_general(rwt_ref[...], x1h, nt_dims, preferred_element_type=F32)
    by_lo = lax.dot_general(rwt_ref[0:N_EXPERTS, :], x1l, nt_dims, preferred_element_type=F32)
    logits = by_hi[0:N_EXPERTS] + by_hi[N_EXPERTS:] + by_lo + rb_ref[...]
    eidx = lax.broadcasted_iota(I32, (N_EXPERTS, tm), 0).astype(F32)
    work = logits
    vals, hots = [], []
    for _ in range(TOP_K):
        v = jnp.max(work, axis=0, keepdims=True)
        idx = jnp.min(jnp.where(work == v, eidx, float(N_EXPERTS)), axis=0, keepdims=True)
        hot = eidx == idx
        work = jnp.where(hot, -jnp.inf, work)
        vals.append(v)
        hots.append(hot.astype(F32))
    ex = [jnp.exp(v - vals[0]) for v in vals]
    tot = ex[0] + ex[1] + ex[2] + ex[3]
    tok_hot = hots[0] + hots[1] + hots[2] + hots[3]
    s_i = lax.broadcasted_iota(I32, (tm, tm), 0)
    t_i = lax.broadcasted_iota(I32, (tm, tm), 1)
    earlier = (s_i < t_i).astype(BF16)
    before = jnp.dot(tok_hot.astype(BF16), earlier, preferred_element_type=F32)
    cnt = jnp.sum(tok_hot, axis=1, keepdims=True)
    cnt8 = jnp.floor((cnt + (SUBLANES - 1)) * (1.0 / SUBLANES)) * SUBLANES
    cnt8_b = jnp.broadcast_to(cnt8, (N_EXPERTS, LANES))
    e_r = lax.broadcasted_iota(I32, (N_EXPERTS, N_EXPERTS), 0)
    e_c = lax.broadcasted_iota(I32, (N_EXPERTS, N_EXPERTS), 1)
    start = jnp.dot((e_c < e_r).astype(F32), cnt8_b, preferred_element_type=F32,
                    precision=lax.Precision.HIGHEST)[:, 0:1]
    slot = start + before
    rows = [jnp.sum(h * slot, axis=0, keepdims=True) for h in hots] + [e / tot for e in ex]
    rrow = jnp.concatenate(rows, axis=0)
    rrow_ref[...] = rrow
    rcol_ref[...] = rrow.T
    cnt_ref[...] = cnt8_b.T[0:1, 0:N_EXPERTS].astype(I32)


def _out_proj(x, y_rnn, o, lw):
    n = x.shape[0]
    tm = min(ROW_TILE, n)
    nt = n // tm
    row = lambda w: pl.BlockSpec((tm, w), lambda i: (i, 0))
    layer = lw["layer"]
    vec = lambda w: _layer_spec(layer, 1, w)
    w_out_half = lambda j: pl.BlockSpec((None, D_MODEL // 2, D_MODEL), lambda i: (layer, j, 0))
    return pl.pallas_call(
        functools.partial(_out_proj_kernel, tm=tm, alpha=lw["alpha"]),
        grid=(nt,),
        in_specs=[row(D_MODEL), row(D_RNN), row(D_ATTN), vec(D_RNN), vec(D_ATTN),
                  w_out_half(0), w_out_half(1), vec(D_MODEL), vec(D_MODEL),
                  _layer_spec(layer, 2 * N_EXPERTS, D_MODEL), _layer_spec(layer, N_EXPERTS, 1)],
        out_specs=[row(D_MODEL), pl.BlockSpec((None, 2 * TOP_K, tm), lambda i: (i, 0, 0)), row(2 * TOP_K),
                   pl.BlockSpec((None, 1, N_EXPERTS), lambda i: (i, 0, 0))],
        out_shape=[jax.ShapeDtypeStruct((n, D_MODEL), F32), jax.ShapeDtypeStruct((nt, 2 * TOP_K, tm), F32),
                   jax.ShapeDtypeStruct((n, 2 * TOP_K), F32), jax.ShapeDtypeStruct((nt, 1, N_EXPERTS), I32)],
        compiler_params=_cparams("parallel"), name="out_proj_route",
    )(x, y_rnn, o, lw["gn_rnn"], lw["gn_attn"], lw["w_out"], lw["w_out"], lw["ln1_g"], lw["ln1_b"],
      lw["router_wt"], lw["router_b"])


RUN_CHUNK = 64
_SMALL_PIECES = (32, 16, 8)
_MAX_PAD_ROWS = N_EXPERTS * SUBLANES
_PAD_PIECES = (128, 64, 32, 16, 8)


def _for_each_piece(n8, fn):
    def chunk(j, c):
        fn(pl.multiple_of(j * RUN_CHUNK, RUN_CHUNK), RUN_CHUNK)
        return c
    lax.fori_loop(0, n8 // RUN_CHUNK, chunk, 0)
    for s in _SMALL_PIECES:
        @pl.when((n8 & s) != 0)
        def _(s=s):
            fn(n8 & (-2 * s), s)


def _wait_tile_rows(n_rows, tm, wait_rows):
    wait_rows(TOP_K * tm)
    pad = n_rows - TOP_K * tm
    for s in _PAD_PIECES:
        @pl.when((pad & s) != 0)
        def _(s=s):
            wait_rows(s)


def _dispatch_kernel(cnt_ref, base_ref, x_ref, rrow_ref, *rest, tm):
    xbuf_ref, xs_ref, carry_ref, nrows_ref, sems = rest[-5:]
    nrow = _sorted_rows(tm)
    i = pl.program_id(0)
    slot = i % 2

    @pl.when(i == 0)
    def _():
        def init(e, c):
            carry_ref[e] = base_ref[e]
            return c
        lax.fori_loop(0, N_EXPERTS, init, 0)

    r = lax.broadcasted_iota(I32, (ONE_HOT_BLOCK, 1), 0).astype(F32).astype(BF16)
    blocks = []
    for c in range(nrow // ONE_HOT_BLOCK):
        hit = None
        for k in range(TOP_K):
            h = r == (rrow_ref[k:k + 1, :] - float(c * ONE_HOT_BLOCK)).astype(BF16)
            hit = h if hit is None else jnp.logical_or(hit, h)
        blocks.append(jnp.where(hit, jnp.ones((), BF16), jnp.zeros((), BF16)))
    perm = jnp.concatenate(blocks, axis=0)
    xs = jnp.dot(perm, x_ref[...].astype(BF16), preferred_element_type=F32)
    lo = lax.bitcast_convert_type(xs[:, :D_PACK], U32)
    hi = lax.bitcast_convert_type(xs[:, D_PACK:], U32)
    xs_ref[slot] = (hi & U32(HI_MASK)) | (lo >> 16)

    def run_copy(buf, src, dst, s):
        return pltpu.make_async_copy(xs_ref.at[buf, pl.ds(pl.multiple_of(src, SUBLANES), s)],
                                     xbuf_ref.at[pl.ds(pl.multiple_of(dst, SUBLANES), s)], sems.at[buf])

    def start(e, src):
        n8 = cnt_ref[0, e]
        dst = carry_ref[e]
        _for_each_piece(n8, lambda off, s: run_copy(slot, src + off, dst + off, s).start())
        carry_ref[e] = dst + n8
        return src + n8

    def wait_tile(buf):
        _wait_tile_rows(nrows_ref[buf], tm, lambda s: run_copy(buf, 0, 0, s).wait())

    nrows_ref[slot] = lax.fori_loop(0, N_EXPERTS, start, 0)

    @pl.when(i > 0)
    def _():
        wait_tile(1 - slot)

    @pl.when(i == pl.num_programs(0) - 1)
    def _():
        wait_tile(slot)


def _dispatch(x1, rrow, cnt8, base, xbuf, n_rows):
    n = x1.shape[0]
    nt, _, tm = rrow.shape
    in_specs = [pl.BlockSpec((None, 1, N_EXPERTS), lambda i: (i, 0, 0), memory_space=pltpu.SMEM),
                pl.BlockSpec(memory_space=pltpu.SMEM),
                pl.BlockSpec((tm, D_MODEL), lambda i: (i, 0)),
                pl.BlockSpec((None, 2 * TOP_K, tm), lambda i: (i, 0, 0))]
    args = [cnt8, base, x1, rrow]
    aliases = {}
    if xbuf is not None:
        in_specs.append(pl.BlockSpec(memory_space=pl.ANY))
        args.append(xbuf)
        aliases = {len(args) - 1: 0}
    return pl.pallas_call(
        functools.partial(_dispatch_kernel, tm=tm),
        grid=(nt,),
        in_specs=in_specs,
        out_specs=pl.BlockSpec(memory_space=pl.ANY),
        out_shape=jax.ShapeDtypeStruct((n_rows, D_PACK), U32),
        scratch_shapes=[pltpu.VMEM((2, _sorted_rows(tm), D_PACK), U32), pltpu.SMEM((N_EXPERTS,), I32),
                        pltpu.SMEM((2,), I32), pltpu.SemaphoreType.DMA((2,))],
        input_output_aliases=aliases,
        compiler_params=_cparams("arbitrary"), name="moe_dispatch",
    )(*args)


def _moe_kernel(br_ref, be_ref, nv_ref, first_ref, next_ref, x_ref, w1_hbm, b1_ref, w2_hbm, b2_ref, y_ref,
                w1f_ref, w2f_ref, w1s_ref, w2s_ref, sems, *, layer):
    i = pl.program_id(0)

    def load(e):
        return (pltpu.make_async_copy(w1_hbm.at[layer, e], w1f_ref, sems.at[0]),
                pltpu.make_async_copy(w2_hbm.at[layer, e], w2f_ref, sems.at[1]))

    @pl.when(jnp.logical_and(i == 0, first_ref[0] == 1))
    def _():
        for c in load(be_ref[0]):
            c.start()

    @pl.when(first_ref[i] == 1)
    def _():
        for c in load(be_ref[i]):
            c.wait()
        w1s_ref[...] = w1f_ref[...].astype(BF16)
        w2s_ref[...] = w2f_ref[...].astype(BF16)

        @pl.when(next_ref[i] >= 0)
        def _():
            for c in load(next_ref[i]):
                c.start()

    nv = nv_ref[i]

    def expert_rows(m):
        rows = lax.broadcasted_iota(I32, (m, 1), 0)
        xw = jnp.where(rows < nv, x_ref[0:m, :], U32(0))
        lo = lax.bitcast_convert_type(xw << 16, F32).astype(BF16)
        hi = lax.bitcast_convert_type(xw & U32(HI_MASK), F32).astype(BF16)
        hdn = (jnp.dot(lo, w1s_ref[:D_PACK, :], preferred_element_type=F32)
               + jnp.dot(hi, w1s_ref[D_PACK:, :], preferred_element_type=F32) + b1_ref[...])
        g = jnp.minimum(hdn[:, :D_FF], SWIGLU_LIMIT)
        lin = jnp.clip(hdn[:, D_FF:], -SWIGLU_LIMIT, SWIGLU_LIMIT)
        act = g * jax.nn.sigmoid(SWIGLU_ALPHA * g) * (lin + 1.0)
        y_ref[0:m, :] = jnp.dot(act.astype(BF16), w2s_ref[...], preferred_element_type=F32) + b2_ref[...]

    @pl.when(nv > MOE_TILE // 2)
    def _():
        expert_rows(MOE_TILE)

    @pl.when(jnp.logical_and(nv > 0, nv <= MOE_TILE // 2))
    def _():
        expert_rows(MOE_TILE // 2)


def _moe_experts(xbuf, geom, w1, b1, w2, b2, layer):
    n_blocks = xbuf.shape[0] // MOE_TILE
    bias = lambda w: pl.BlockSpec((None, None, 1, w), lambda i, br, be, *_: (layer, be[i], 0, 0))
    grid_spec = pltpu.PrefetchScalarGridSpec(
        num_scalar_prefetch=5,
        grid=(n_blocks,),
        in_specs=[pl.BlockSpec((MOE_TILE, D_PACK), lambda i, br, *_: (br[i], 0)),
                  pl.BlockSpec(memory_space=pl.ANY), bias(2 * D_FF),
                  pl.BlockSpec(memory_space=pl.ANY), bias(D_MODEL)],
        out_specs=pl.BlockSpec((MOE_TILE, D_MODEL), lambda i, br, *_: (br[i], 0)),
        scratch_shapes=[pltpu.VMEM((D_MODEL, 2 * D_FF), F32), pltpu.VMEM((D_FF, D_MODEL), F32),
                        pltpu.VMEM((D_MODEL, 2 * D_FF), BF16), pltpu.VMEM((D_FF, D_MODEL), BF16),
                        pltpu.SemaphoreType.DMA((2,))],
    )
    depth = w1.shape[0]
    return pl.pallas_call(
        functools.partial(_moe_kernel, layer=layer), grid_spec=grid_spec,
        out_shape=jax.ShapeDtypeStruct((xbuf.shape[0], D_MODEL), F32),
        compiler_params=_cparams("arbitrary"), name="moe_experts",
    )(*geom, xbuf, w1, b1.reshape(depth, N_EXPERTS, 1, 2 * D_FF), w2, b2.reshape(depth, N_EXPERTS, 1, D_MODEL))


def _final_kernel(cnt_ref, cnt_next_ref, base_ref, x_ref, p_ref, rcol_ref, wg_ref, bg_ref, wp_ref, g2_ref, b2_ref,
                  ybuf_ref, out_ref, ys_ref, carry_ref, nrows_ref, sems, *, tm, alpha):
    nrow = _sorted_rows(tm)
    i = pl.program_id(0)
    slot = i % 2

    def run_copy(buf, src, dst, s):
        return pltpu.make_async_copy(ybuf_ref.at[pl.ds(pl.multiple_of(src, SUBLANES), s)],
                                     ys_ref.at[buf, pl.ds(pl.multiple_of(dst, SUBLANES), s)], sems.at[buf])

    def gather_tile(counts_ref, buf):
        def start(e, dst):
            n8 = counts_ref[0, e]
            src = carry_ref[e]
            _for_each_piece(n8, lambda off, s: run_copy(buf, src + off, dst + off, s).start())
            carry_ref[e] = src + n8
            return dst + n8
        nrows_ref[buf] = lax.fori_loop(0, N_EXPERTS, start, 0)

    @pl.when(i == 0)
    def _():
        def init(e, c):
            carry_ref[e] = base_ref[e]
            return c
        lax.fori_loop(0, N_EXPERTS, init, 0)
        gather_tile(cnt_ref, 0)

    @pl.when(i + 1 < pl.num_programs(0))
    def _():
        gather_tile(cnt_next_ref, 1 - slot)

    x = x_ref[...]
    ple = (jax.nn.sigmoid(jnp.dot(x.astype(BF16), wg_ref[...], preferred_element_type=F32) + bg_ref[...])
           * jnp.dot(p_ref[...].astype(BF16), wp_ref[...], preferred_element_type=F32))

    _wait_tile_rows(nrows_ref[slot], tm, lambda s: run_copy(slot, 0, 0, s).wait())

    r = lax.broadcasted_iota(I32, (nrow, 1), 0)
    ys = jnp.where(r < nrows_ref[slot], ys_ref[slot], 0.0).astype(BF16)
    col = lax.broadcasted_iota(I32, (1, ONE_HOT_BLOCK), 1).astype(F32).astype(BF16)
    rc = rcol_ref[...]
    gates = rc[:, TOP_K:].astype(BF16)
    blocks = []
    for c in range(nrow // ONE_HOT_BLOCK):
        blk = None
        for k in range(TOP_K):
            hit = col == (rc[:, k:k + 1] - float(c * ONE_HOT_BLOCK)).astype(BF16)
            term = jnp.where(hit, gates[:, k:k + 1], jnp.zeros((), BF16))
            blk = term if blk is None else blk + term
        blocks.append(blk)
    ffn = jnp.dot(jnp.concatenate(blocks, axis=1), ys, preferred_element_type=F32)
    out_ref[...] = _layer_norm(alpha * x + ffn + ple, g2_ref[...], b2_ref[...])


def _final(x1, p_all, layer, rcol, cnt8, base, ybuf, lw):
    n = x1.shape[0]
    nt = cnt8.shape[0]
    tm = n // nt
    row = lambda w: pl.BlockSpec((tm, w), lambda i: (i, 0))
    vec = lambda w: _layer_spec(layer, 1, w)
    return pl.pallas_call(
        functools.partial(_final_kernel, tm=tm, alpha=lw["alpha"]),
        grid=(nt,),
        in_specs=[pl.BlockSpec((None, 1, N_EXPERTS), lambda i: (i, 0, 0), memory_space=pltpu.SMEM),
                  pl.BlockSpec((None, 1, N_EXPERTS), lambda i: (jnp.minimum(i + 1, nt - 1), 0, 0),
                               memory_space=pltpu.SMEM),
                  pl.BlockSpec(memory_space=pltpu.SMEM),
                  row(D_MODEL), pl.BlockSpec((None, tm, D_PLE), lambda i: (layer, i, 0)), row(2 * TOP_K),
                  _layer_spec(layer, D_MODEL, D_MODEL), vec(D_MODEL), _layer_spec(layer, D_PLE, D_MODEL),
                  vec(D_MODEL), vec(D_MODEL), pl.BlockSpec(memory_space=pl.ANY)],
        out_specs=row(D_MODEL),
        out_shape=jax.ShapeDtypeStruct((n, D_MODEL), F32),
        scratch_shapes=[pltpu.VMEM((2, _sorted_rows(tm), D_MODEL), F32), pltpu.SMEM((N_EXPERTS,), I32),
                        pltpu.SMEM((2,), I32), pltpu.SemaphoreType.DMA((2,))],
        compiler_params=_cparams("arbitrary"), name="moe_combine_ln2",
    )(cnt8, cnt8, base, x1, p_all, rcol, lw["ple_w_gate"], lw["ple_b_gate"], lw["ple_w_proj"], lw["ln2_g"], lw["ln2_b"],
      ybuf)


def _block_diag(w):
    depth, nb, bw, _ = w.shape
    eye = jnp.eye(nb, dtype=w.dtype)
    return (w[:, :, :, None, :] * eye[None, :, None, :, None]).reshape(depth, nb * bw, nb * bw)


def _hi_lo_rows(w):
    hi = w.astype(BF16)
    lo = (w - hi.astype(F32)).astype(BF16)
    return jnp.concatenate([hi, lo], axis=-2)


def _block_geometry(tot, n_blocks):
    nblk = (tot + MOE_TILE - 1) // MOE_TILE
    blk_end = jnp.cumsum(nblk)
    blk_start = blk_end - nblk
    ids = jnp.arange(n_blocks, dtype=I32)
    n_valid = blk_end[-1]
    expert_at = lambda j: jnp.minimum(jnp.sum(j[:, None] >= blk_end[None, :], axis=1), N_EXPERTS - 1).astype(I32)
    src = jnp.clip(ids, 0, jnp.maximum(n_valid - 1, 0))
    e_of = expert_at(src)
    hot = e_of[:, None] == jnp.arange(N_EXPERTS)[None, :]
    pick = lambda v: jnp.sum(jnp.where(hot, v[None, :], 0), axis=1)
    start_of, end_of, tot_of = pick(blk_start), pick(blk_end), pick(tot)
    real = ids < n_valid
    nv = jnp.where(real, jnp.clip(tot_of - (ids - start_of) * MOE_TILE, 0, MOE_TILE), 0)
    first = jnp.logical_and(real, ids == start_of)
    nxt = jnp.where(end_of < n_valid, expert_at(end_of), -1)
    geom = tuple(v.astype(I32) for v in (src, e_of, nv, first, nxt))
    return geom, (blk_start * MOE_TILE).astype(I32)


def kernel(x_prompt, x_sample, cache_k, cache_v, state_conv, state_h, p_prompt, p_sample, ln_in_g, ln_in_b, rel_bias, w_in, conv_w, conv_b, rg_wa, rg_ba, rg_wx, rg_bx, rg_lambda, attn_sinks, gn_rnn, gn_attn, w_out, ln1_g, ln1_b, router_w, router_b, moe_w1, moe_b1, moe_w2, moe_b2, ple_w_gate, ple_b_gate, ple_w_proj, ln2_g, ln2_b):
    depth = w_in.shape[0]
    bsz, t_len, _ = x_prompt.shape
    n_p = bsz * t_len
    n_s = x_sample.shape[0]
    wb = cache_k.shape[2]
    alpha = (2 * depth) ** 0.25

    dist = jnp.arange(WINDOW)[:, None] + WINDOW - jnp.arange(2 * WINDOW)[None, :]
    valid = (dist >= 0) & (dist < WINDOW)
    bias_prompt = jnp.where(valid[None], _bias_lookup(rel_bias, dist), NEG)
    bias_sample = _bias_lookup(rel_bias, WINDOW - 1 - jnp.arange(WINDOW))

    tm_p = min(ROW_TILE, n_p)
    n_tiles = n_p // tm_p + 1
    max_rows = (n_p + n_s) * TOP_K + n_tiles * N_EXPERTS * (SUBLANES - 1)
    n_blocks = -(-(max_rows + N_EXPERTS * (MOE_TILE - 1)) // MOE_TILE)

    xp = x_prompt.reshape(n_p, D_MODEL)
    xs = x_sample.reshape(n_s, D_MODEL)
    pp = p_prompt.reshape(depth, n_p, D_PLE)
    ps = p_sample.reshape(depth, n_s, D_PLE)
    ck = cache_k.reshape(depth, n_s, wb, 128)
    cv = cache_v.reshape(depth, n_s, wb, 128)
    row2 = lambda v: v.reshape(1, -1)
    kv0 = 2 * D_RNN + D_ATTN
    outs = {k: [] for k in ("kp", "vp", "cp", "hp", "ks", "vs", "cs", "hs")}

    vecs = lambda v: v.reshape(depth, 1, -1)
    rw_all = {"conv_w": conv_w, "conv_b": vecs(conv_b), "wa": _block_diag(rg_wa).astype(BF16),
              "ba": vecs(rg_ba), "wx": _block_diag(rg_wx).astype(BF16), "bx": vecs(rg_bx),
              "lam": vecs(rg_lambda)}
    lw_all = {"gn_rnn": vecs(gn_rnn), "gn_attn": vecs(gn_attn), "w_out": w_out.astype(BF16),
              "ln1_g": vecs(ln1_g), "ln1_b": vecs(ln1_b),
              "router_wt": _hi_lo_rows(jnp.swapaxes(router_w, 1, 2)),
              "router_b": router_b.reshape(depth, N_EXPERTS, 1), "alpha": alpha,
              "ple_w_gate": ple_w_gate.astype(BF16), "ple_b_gate": vecs(ple_b_gate),
              "ple_w_proj": ple_w_proj.astype(BF16), "ln2_g": vecs(ln2_g), "ln2_b": vecs(ln2_b)}
    w_in_b = w_in.astype(BF16)

    for l in range(depth):
        rw = dict(rw_all, layer=l)
        lw = dict(lw_all, layer=l)
        first = l == 0

        xp, zp = _in_proj(xp, row2(ln_in_g), row2(ln_in_b), w_in_b, l, first)
        zp3 = zp.reshape(bsz, t_len, D_IN)
        yp, hp = _rnn_prompt(zp3, rw)
        op = _attn_prompt(zp, bsz, t_len, bias_prompt, attn_sinks, l)
        wp = min(WINDOW, t_len)
        outs["kp"].append(zp3[:, t_len - wp:, kv0:kv0 + 128].reshape(bsz, wp, N_KV, HEAD_DIM))
        outs["vp"].append(zp3[:, t_len - wp:, kv0 + 128:kv0 + 256].reshape(bsz, wp, N_KV, HEAD_DIM))
        outs["cp"].append(zp3[:, t_len - (CONV_W - 1):, :D_RNN])
        outs["hp"].append(hp.reshape(bsz, D_RNN))

        xs, zs = _in_proj(xs, row2(ln_in_g), row2(ln_in_b), w_in_b, l, first)
        ys, hs = _rnn_sample(zs, state_conv, state_h, rw)
        os_, nk, nv = _attn_sample(zs[:, 2 * D_RNN:kv0], zs[:, kv0:kv0 + 128], zs[:, kv0 + 128:], ck, cv, l,
                                   bias_sample, attn_sinks)
        outs["ks"].append(nk.reshape(n_s, wb, N_KV, HEAD_DIM))
        outs["vs"].append(nv.reshape(n_s, wb, N_KV, HEAD_DIM))
        outs["cs"].append(jnp.concatenate([state_conv[l][:, 1:], zs[:, None, :D_RNN]], axis=1))
        outs["hs"].append(hs)

        x1p, rrow_p, rcol_p, cnt_p = _out_proj(xp, yp.reshape(n_p, D_RNN), op, lw)
        x1s, rrow_s, rcol_s, cnt_s = _out_proj(xs, ys, os_, lw)

        tot_p = jnp.sum(cnt_p, axis=(0, 1))
        tot = tot_p + jnp.sum(cnt_s, axis=(0, 1))
        geom, base_p = _block_geometry(tot, n_blocks)
        base_s = base_p + tot_p

        xbuf = _dispatch(x1p, rrow_p, cnt_p, base_p, None, n_blocks * MOE_TILE)
        xbuf = _dispatch(x1s, rrow_s, cnt_s, base_s, xbuf, n_blocks * MOE_TILE)
        ybuf = _moe_experts(xbuf, geom, moe_w1, moe_b1, moe_w2, moe_b2, l)
        xp = _final(x1p, pp, l, rcol_p, cnt_p, base_p, ybuf, lw)
        xs = _final(x1s, ps, l, rcol_s, cnt_s, base_s, ybuf, lw)

    st = lambda k: jnp.stack(outs[k])
    return (xp.reshape(bsz, t_len, D_MODEL), xs.reshape(n_s, 1, D_MODEL),
            st("kp"), st("vp"), st("cp"), st("hp"), st("ks"), st("vs"), st("cs"), st("hs"))
```

```python
import functools
import math

import jax
import jax.numpy as jnp
from jax import lax
from jax.experimental import pallas as pl
from jax.experimental.pallas import tpu as pltpu

F32 = jnp.float32
BF16 = jnp.bfloat16
I32 = jnp.int32
U32 = jnp.uint32

D_MODEL = 1024
D_RNN = 512
RNN_BLOCKS = 8
CONV_W = 4
LRU_C = 8.0
N_HEADS = 8
HEAD_DIM = 64
N_KV = 2
GQA_R = N_HEADS // N_KV
D_ATTN = N_HEADS * HEAD_DIM
WINDOW = 128
N_BUCKETS = 32
MAX_DIST = 128
N_EXPERTS = 32
TOP_K = 4
D_FF = 1024
SWIGLU_LIMIT = 7.0
SWIGLU_ALPHA = 1.702
D_PLE = 256
LN_EPS = 1e-5
D_IN = 2 * D_RNN + D_ATTN + 2 * N_KV * HEAD_DIM
NEG = -1e30

SUBLANES = 8
LANES = 128
ROW_TILE = 512
MOE_TILE = 512
D_PACK = D_MODEL // 2
VMEM_LIMIT = 56 * 1024 * 1024
HI_MASK = 0xFFFF0000


def _cparams(*sem):
    return pltpu.CompilerParams(dimension_semantics=sem, vmem_limit_bytes=VMEM_LIMIT)


def _layer_norm(x, g, b):
    mu = jnp.mean(x, axis=-1, keepdims=True)
    xc = x - mu
    var = jnp.mean(xc * xc, axis=-1, keepdims=True)
    return xc * lax.rsqrt(var + LN_EPS) * g + b


def _rms_norm(x, g):
    return x * lax.rsqrt(jnp.mean(x * x, axis=-1, keepdims=True) + LN_EPS) * g


def _const_spec(shape):
    return pl.BlockSpec(shape, lambda *_: (0,) * len(shape))


def _layer_spec(layer, *tail):
    return pl.BlockSpec((None,) + tail, lambda *_: (layer,) + (0,) * len(tail))


def _in_proj_kernel(x_ref, g_ref, b_ref, w_ref, *out_refs, apply_ln):
    x = x_ref[...]
    if apply_ln:
        x = _layer_norm(x, g_ref[...], b_ref[...])
        out_refs[0][...] = x
    out_refs[-1][...] = jnp.dot(x.astype(BF16), w_ref[...], preferred_element_type=F32)


def _in_proj(x, ln_g, ln_b, w_in_bf16, layer, apply_ln):
    n = x.shape[0]
    tm = min(ROW_TILE, n)
    row = lambda w: pl.BlockSpec((tm, w), lambda i: (i, 0))
    out_shape = [jax.ShapeDtypeStruct((n, D_IN), F32)]
    out_specs = [row(D_IN)]
    if apply_ln:
        out_shape.insert(0, jax.ShapeDtypeStruct((n, D_MODEL), F32))
        out_specs.insert(0, row(D_MODEL))
    outs = pl.pallas_call(
        functools.partial(_in_proj_kernel, apply_ln=apply_ln),
        grid=(n // tm,),
        in_specs=[row(D_MODEL), _const_spec((1, D_MODEL)), _const_spec((1, D_MODEL)),
                  _layer_spec(layer, D_MODEL, D_IN)],
        out_specs=out_specs, out_shape=out_shape,
        compiler_params=_cparams("parallel"), name="in_proj",
    )(x, ln_g, ln_b, w_in_bf16)
    return (outs[0], outs[1]) if apply_ln else (x, outs[0])


def _rglru_coeffs(u, wa_ref, ba_ref, wx_ref, bx_ref, lam_ref):
    ub = u.astype(BF16)
    r = jax.nn.sigmoid(jnp.dot(ub, wa_ref[...], preferred_element_type=F32) + ba_ref[...])
    i = jax.nn.sigmoid(jnp.dot(ub, wx_ref[...], preferred_element_type=F32) + bx_ref[...])
    lam = -lam_ref[...]
    softplus = jnp.maximum(lam, 0.0) + jnp.log1p(jnp.exp(-jnp.abs(lam)))
    log_a = -LRU_C * r * softplus
    a = jnp.exp(log_a)
    b = jnp.sqrt(-jnp.tanh(log_a) * (a * a + 1.0)) * (i * u)
    return a, b


def _rnn_prompt_kernel(xr_ref, gate_ref, cw_ref, cb_ref, wa_ref, ba_ref, wx_ref, bx_ref, lam_ref,
                       y_ref, hlast_ref, xs_ref, a_ref, b_ref, carry_ref, *, tt):
    t = pl.program_id(1)

    @pl.when(t == 0)
    def _():
        xs_ref[0:SUBLANES, :] = jnp.zeros((SUBLANES, D_RNN), F32)
        carry_ref[...] = jnp.zeros((SUBLANES, D_RNN), F32)

    x = xr_ref[...]
    xs_ref[SUBLANES:SUBLANES + tt, :] = x
    cw = cw_ref[...]
    u = cb_ref[...] + x * cw[3:4]
    for j in range(CONV_W - 1):
        off = SUBLANES - (CONV_W - 1) + j
        u = u + xs_ref[off:off + tt, :] * cw[j:j + 1]
    xs_ref[0:SUBLANES, :] = x[tt - SUBLANES:tt, :]

    a, b = _rglru_coeffs(u, wa_ref, ba_ref, wx_ref, bx_ref, lam_ref)
    ng = tt // SUBLANES
    a = a.reshape(ng, SUBLANES, D_RNN)
    b = b.reshape(ng, SUBLANES, D_RNN)
    row = lax.broadcasted_iota(I32, (1, SUBLANES, 1), 1)
    for s in (1, 2, 4):
        a_sh = pltpu.roll(a, s, 1)
        b_sh = pltpu.roll(b, s, 1)
        m = row >= s
        b = jnp.where(m, a * b_sh + b, b)
        a = jnp.where(m, a * a_sh, a)
    a_ref[...] = a
    b_ref[...] = b

    def body(g, carry):
        h = a_ref[g] * carry + b_ref[g]
        b_ref[g] = h
        return jnp.broadcast_to(h[SUBLANES - 1:SUBLANES, :], (SUBLANES, D_RNN))

    carry = lax.fori_loop(0, ng, body, carry_ref[...])
    carry_ref[...] = carry
    h = b_ref[...].reshape(tt, D_RNN)
    y_ref[...] = h * jax.nn.gelu(gate_ref[...])
    hlast_ref[...] = carry[0:1, :]


def _rnn_prompt(z3, rw):
    bsz, t_len, _ = z3.shape
    tt = min(ROW_TILE, t_len)
    vec = _layer_spec(rw["layer"], 1, D_RNN)
    mat = _layer_spec(rw["layer"], D_RNN, D_RNN)
    return pl.pallas_call(
        functools.partial(_rnn_prompt_kernel, tt=tt),
        grid=(bsz, t_len // tt),
        in_specs=[pl.BlockSpec((None, tt, D_RNN), lambda b, t: (b, t, 0)),
                  pl.BlockSpec((None, tt, D_RNN), lambda b, t: (b, t, 1)),
                  _layer_spec(rw["layer"], CONV_W, D_RNN), vec, mat, vec, mat, vec, vec],
        out_specs=[pl.BlockSpec((None, tt, D_RNN), lambda b, t: (b, t, 0)),
                   pl.BlockSpec((None, 1, D_RNN), lambda b, t: (b, 0, 0))],
        out_shape=[jax.ShapeDtypeStruct((bsz, t_len, D_RNN), F32),
                   jax.ShapeDtypeStruct((bsz, 1, D_RNN), F32)],
        scratch_shapes=[pltpu.VMEM((tt + SUBLANES, D_RNN), F32),
                        pltpu.VMEM((tt // SUBLANES, SUBLANES, D_RNN), F32),
                        pltpu.VMEM((tt // SUBLANES, SUBLANES, D_RNN), F32),
                        pltpu.VMEM((SUBLANES, D_RNN), F32)],
        compiler_params=_cparams("parallel", "arbitrary"), name="rnn_prompt",
    )(z3, z3, rw["conv_w"], rw["conv_b"], rw["wa"], rw["ba"], rw["wx"], rw["bx"], rw["lam"])


def _rnn_sample_kernel(xr_ref, gate_ref, c0_ref, c1_ref, c2_ref, h0_ref, cw_ref, cb_ref, wa_ref, ba_ref,
                       wx_ref, bx_ref, lam_ref, y_ref, h_ref):
    x = xr_ref[...]
    cw = cw_ref[...]
    u = (cb_ref[...] + c0_ref[...] * cw[0:1] + c1_ref[...] * cw[1:2] + c2_ref[...] * cw[2:3]
         + x * cw[3:4])
    a, b = _rglru_coeffs(u, wa_ref, ba_ref, wx_ref, bx_ref, lam_ref)
    h = a * h0_ref[...] + b
    h_ref[...] = h
    y_ref[...] = h * jax.nn.gelu(gate_ref[...])


def _rnn_sample(z, conv_state, h0, rw):
    n = z.shape[0]
    layer = rw["layer"]
    vec = _layer_spec(layer, 1, D_RNN)
    mat = _layer_spec(layer, D_RNN, D_RNN)
    full = _const_spec((n, D_RNN))
    depth = conv_state.shape[0]
    conv_flat = conv_state.reshape(depth, n, (CONV_W - 1) * D_RNN)
    tap = lambda j: pl.BlockSpec((None, n, D_RNN), lambda i: (layer, 0, j))
    return pl.pallas_call(
        _rnn_sample_kernel,
        grid=(1,),
        in_specs=[pl.BlockSpec((n, D_RNN), lambda i: (0, 0)), pl.BlockSpec((n, D_RNN), lambda i: (0, 1)),
                  tap(0), tap(1), tap(2), _layer_spec(layer, n, D_RNN),
                  _layer_spec(layer, CONV_W, D_RNN), vec, mat, vec, mat, vec, vec],
        out_specs=[full, full],
        out_shape=[jax.ShapeDtypeStruct((n, D_RNN), F32)] * 2,
        compiler_params=_cparams("arbitrary"), name="rnn_sample",
    )(z, z, conv_flat, conv_flat, conv_flat, h0, rw["conv_w"], rw["conv_b"], rw["wa"], rw["ba"], rw["wx"], rw["bx"], rw["lam"])


def _rel_bucket(dist):
    n = jnp.maximum(dist, 0)
    max_exact = N_BUCKETS // 2
    nf = jnp.maximum(n, max_exact).astype(F32)
    large = max_exact + (jnp.log(nf / max_exact) / math.log(MAX_DIST / max_exact)
                         * (N_BUCKETS - max_exact)).astype(I32)
    large = jnp.minimum(large, N_BUCKETS - 1)
    return jnp.where(n < max_exact, n, large)


def _bias_lookup(rel_bias, dist):
    hot = (_rel_bucket(dist)[..., None] == jnp.arange(N_BUCKETS)).astype(F32)
    out = jnp.tensordot(hot, rel_bias.astype(F32), axes=1, precision=lax.Precision.HIGHEST)
    return jnp.moveaxis(out, -1, 0)


def _half_lane_variants(x, group):
    lo = lax.broadcasted_iota(I32, (1, 2 * HEAD_DIM), 1) < HEAD_DIM
    xr = pltpu.roll(x, HEAD_DIM, 1)
    zero = jnp.zeros_like(x)
    if group == 0:
        return jnp.where(lo, x, zero), jnp.where(lo, zero, xr)
    return jnp.where(lo, xr, zero), jnp.where(lo, zero, x)


def _attn_prompt_kernel(q_ref, kc_ref, kp_ref, vc_ref, vp_ref, bias_ref, sink_ref, o_ref, *, nq, layer):
    first = pl.program_id(1) == 0
    col = lax.broadcasted_iota(I32, (1, 2 * WINDOW), 1)
    hide_prev = jnp.logical_and(first, col < WINDOW)
    for j in range(nq):
        rows = slice(j * WINDOW, (j + 1) * WINDOW)
        before = slice((j - 1) * WINDOW, j * WINDOW)
        kk = jnp.concatenate([kp_ref[...] if j == 0 else kc_ref[before, :], kc_ref[rows, :]], axis=0)
        vv = jnp.concatenate([vp_ref[...] if j == 0 else vc_ref[before, :], vc_ref[rows, :]], axis=0)
        for g in range(N_KV):
            k_lo, k_hi = _half_lane_variants(kk, g)
            v_lo, v_hi = _half_lane_variants(vv, g)
            for pair in range(GQA_R // 2):
                p0 = g * (GQA_R // 2) + pair
                qp = (q_ref[rows, p0 * 128:(p0 + 1) * 128] * (HEAD_DIM ** -0.5)).astype(BF16)
                acc = jnp.zeros((WINDOW, 2 * HEAD_DIM), F32)
                for kx, vx, h in ((k_lo, v_lo, 2 * p0), (k_hi, v_hi, 2 * p0 + 1)):
                    s = lax.dot_general(qp, kx.astype(BF16), (((1,), (1,)), ((), ())),
                                        preferred_element_type=F32) + bias_ref[h]
                    if j == 0:
                        s = jnp.where(hide_prev, NEG, s)
                    sink = sink_ref[layer, h]
                    m = jnp.maximum(jnp.max(s, axis=-1, keepdims=True), sink)
                    p = jnp.exp(s - m)
                    denom = jnp.sum(p, axis=-1, keepdims=True) + jnp.exp(sink - m)
                    acc = acc + jnp.dot(p.astype(BF16), vx.astype(BF16), preferred_element_type=F32) / denom
                o_ref[rows, p0 * 128:(p0 + 1) * 128] = acc


def _attn_prompt(z, bsz, t_len, bias_prompt, sinks, layer):
    nb = t_len // WINDOW
    nq = min(4, nb)
    ns = nb // nq
    kcol = (2 * D_RNN + D_ATTN) // 128
    cur = lambda c: pl.BlockSpec((nq * WINDOW, 128), lambda b, i: (b * ns + i, c))
    prev = lambda c: pl.BlockSpec((WINDOW, 128), lambda b, i: (b * nb + jnp.maximum(i * nq - 1, 0), c))
    return pl.pallas_call(
        functools.partial(_attn_prompt_kernel, nq=nq, layer=layer),
        grid=(bsz, ns),
        in_specs=[pl.BlockSpec((nq * WINDOW, D_ATTN), lambda b, i: (b * ns + i, 2 * D_RNN // D_ATTN)),
                  cur(kcol), prev(kcol), cur(kcol + 1), prev(kcol + 1),
                  _const_spec((N_HEADS, WINDOW, 2 * WINDOW)),
                  pl.BlockSpec(memory_space=pltpu.SMEM)],
        out_specs=pl.BlockSpec((nq * WINDOW, D_ATTN), lambda b, i: (b * ns + i, 0)),
        out_shape=jax.ShapeDtypeStruct((bsz * t_len, D_ATTN), F32),
        compiler_params=_cparams("parallel", "arbitrary"), name="attn_prompt",
    )(z, z, z, z, z, bias_prompt, sinks)


def _attn_sample_kernel(q_ref, kn_ref, vn_ref, ck_ref, cv_ref, bias_ref, sink_ref, hmask_ref, *rest):
    o_ref, nk_ref, nv_ref = rest[-3:]
    pos = lax.broadcasted_iota(I32, (1, 1, WINDOW), 2)
    last = pos == WINDOW - 1
    bb = ck_ref.shape[0]
    newk = jnp.where(last, jnp.broadcast_to(kn_ref[...], (bb, 128, WINDOW)), pltpu.roll(ck_ref[...], WINDOW - 1, 2))
    newv = jnp.where(last, jnp.broadcast_to(vn_ref[...], (bb, 128, WINDOW)), pltpu.roll(cv_ref[...], WINDOW - 1, 2))
    nk_ref[...] = newk
    nv_ref[...] = newv
    s = lax.dot_general(q_ref[...].astype(BF16), newk.astype(BF16), (((2,), (1,)), ((0,), (0,))),
                        preferred_element_type=F32)
    s = s + bias_ref[...][None]
    sink = sink_ref[...][None]
    m = jnp.maximum(jnp.max(s, axis=-1, keepdims=True), sink)
    p = jnp.exp(s - m)
    denom = jnp.sum(p, axis=-1, keepdims=True) + jnp.exp(sink - m)
    o = lax.dot_general(p.astype(BF16), newv.astype(BF16), (((2,), (2,)), ((0,), (0,))),
                        preferred_element_type=F32)
    o_ref[...] = o / denom * hmask_ref[...][None]


def _attn_sample(q, k_new, v_new, k_cache, v_cache, new_caches, layer, bias_sample, sinks):
    n = q.shape[0]
    depth = k_cache.shape[0]
    bb = 16
    qh = q.reshape(n, N_KV, GQA_R, HEAD_DIM) * (HEAD_DIM ** -0.5)
    eye = jnp.eye(N_KV, dtype=F32)
    q8 = (qh[:, :, :, None, :] * eye[None, :, None, :, None]).reshape(n, N_HEADS, N_KV * HEAD_DIM)
    hmask = jnp.repeat(jnp.repeat(eye, GQA_R, axis=0), HEAD_DIM, axis=1)
    blk3 = lambda a, b: pl.BlockSpec((bb, a, b), lambda i: (i, 0, 0))
    cache = pl.BlockSpec((None, bb, 128, WINDOW), lambda i: (layer, i, 0, 0))
    in_specs = [blk3(N_HEADS, 128), blk3(128, 1), blk3(128, 1), cache, cache,
                _const_spec((N_HEADS, WINDOW)), _layer_spec(layer, N_HEADS, 1), _const_spec((N_HEADS, 128))]
    args = [q8, k_new.reshape(n, 128, 1), v_new.reshape(n, 128, 1), k_cache, v_cache, bias_sample,
            sinks.reshape(-1, N_HEADS, 1), hmask]
    aliases = {}
    if new_caches is not None:
        in_specs += [pl.BlockSpec(memory_space=pl.ANY)] * 2
        args += list(new_caches)
        aliases = {len(args) - 2: 1, len(args) - 1: 2}
    o8, nk, nv = pl.pallas_call(
        _attn_sample_kernel,
        grid=(n // bb,),
        in_specs=in_specs,
        out_specs=[blk3(N_HEADS, 128), cache, cache],
        out_shape=[jax.ShapeDtypeStruct((n, N_HEADS, 128), F32),
                   jax.ShapeDtypeStruct((depth, n, 128, WINDOW), F32),
                   jax.ShapeDtypeStruct((depth, n, 128, WINDOW), F32)],
        input_output_aliases=aliases,
        compiler_params=_cparams("arbitrary"), name="attn_sample",
    )(*args)
    o = o8.reshape(n, N_KV, GQA_R, N_KV, HEAD_DIM)
    o = jnp.stack([o[:, g, :, g, :] for g in range(N_KV)], axis=1).reshape(n, D_ATTN)
    return o, (nk, nv)


ONE_HOT_BLOCK = 256


def _sorted_rows(tm):
    return TOP_K * tm + N_EXPERTS * SUBLANES


def _out_proj_kernel(x_ref, yr_ref, o_ref, gnr_ref, gna_ref, wr_ref, wa_ref, g1_ref, b1_ref, rwt_ref, rb_ref,
                     x1_ref, rrow_ref, rcol_ref, cnt_ref, *, tm, alpha):
    yn = _rms_norm(yr_ref[...], gnr_ref[...]).astype(BF16)
    on = _rms_norm(o_ref[...], gna_ref[...]).astype(BF16)
    mix = (jnp.dot(yn, wr_ref[...], preferred_element_type=F32)
           + jnp.dot(on, wa_ref[...], preferred_element_type=F32))
    x1 = _layer_norm(alpha * x_ref[...] + mix, g1_ref[...], b1_ref[...])
    x1_ref[...] = x1

    x1h = x1.astype(BF16)
    x1l = (x1 - x1h.astype(F32)).astype(BF16)
    nt_dims = (((1,), (1,)), ((), ()))
    by_hi = lax.dot_general(rwt_ref[...], x1h, nt_dims, preferred_element_type=F32)
    by_lo = lax.dot_general(rwt_ref[0:N_EXPERTS, :], x1l, nt_dims, preferred_element_type=F32)
    logits = by_hi[0:N_EXPERTS] + by_hi[N_EXPERTS:] + by_lo + rb_ref[...]
    eidx = lax.broadcasted_iota(I32, (N_EXPERTS, tm), 0).astype(F32)
    work = logits
    vals, hots = [], []
    for _ in range(TOP_K):
        v = jnp.max(work, axis=0, keepdims=True)
        idx = jnp.min(jnp.where(work == v, eidx, float(N_EXPERTS)), axis=0, keepdims=True)
        hot = eidx == idx
        work = jnp.where(hot, -jnp.inf, work)
        vals.append(v)
        hots.append(hot.astype(F32))
    ex = [jnp.exp(v - vals[0]) for v in vals]
    tot = ex[0] + ex[1] + ex[2] + ex[3]
    tok_hot = hots[0] + hots[1] + hots[2] + hots[3]
    s_i = lax.broadcasted_iota(I32, (tm, tm), 0)
    t_i = lax.broadcasted_iota(I32, (tm, tm), 1)
    earlier = (s_i < t_i).astype(BF16)
    before = jnp.dot(tok_hot.astype(BF16), earlier, preferred_element_type=F32)
    cnt = jnp.sum(tok_hot, axis=1, keepdims=True)
    cnt8 = jnp.floor((cnt + (SUBLANES - 1)) * (1.0 / SUBLANES)) * SUBLANES
    cnt8_b = jnp.broadcast_to(cnt8, (N_EXPERTS, LANES))
    e_r = lax.broadcasted_iota(I32, (N_EXPERTS, N_EXPERTS), 0)
    e_c = lax.broadcasted_iota(I32, (N_EXPERTS, N_EXPERTS), 1)
    start = jnp.dot((e_c < e_r).astype(F32), cnt8_b, preferred_element_type=F32,
                    precision=lax.Precision.HIGHEST)[:, 0:1]
    slot = start + before
    rows = [jnp.sum(h * slot, axis=0, keepdims=True) for h in hots] + [e / tot for e in ex]
    rrow = jnp.concatenate(rows, axis=0)
    rrow_ref[...] = rrow
    rcol_ref[...] = rrow.T
    cnt_ref[...] = cnt8_b.T[0:1, 0:N_EXPERTS].astype(I32)


def _out_proj(x, y_rnn, o, lw):
    n = x.shape[0]
    tm = min(ROW_TILE, n)
    nt = n // tm
    row = lambda w: pl.BlockSpec((tm, w), lambda i: (i, 0))
    layer = lw["layer"]
    vec = lambda w: _layer_spec(layer, 1, w)
    w_out_half = lambda j: pl.BlockSpec((None, D_MODEL // 2, D_MODEL), lambda i: (layer, j, 0))
    return pl.pallas_call(
        functools.partial(_out_proj_kernel, tm=tm, alpha=lw["alpha"]),
        grid=(nt,),
        in_specs=[row(D_MODEL), row(D_RNN), row(D_ATTN), vec(D_RNN), vec(D_ATTN),
                  w_out_half(0), w_out_half(1), vec(D_MODEL), vec(D_MODEL),
                  _layer_spec(layer, 2 * N_EXPERTS, D_MODEL), _layer_spec(layer, N_EXPERTS, 1)],
        out_specs=[row(D_MODEL), pl.BlockSpec((None, 2 * TOP_K, tm), lambda i: (i, 0, 0)), row(2 * TOP_K),
                   pl.BlockSpec((None, 1, N_EXPERTS), lambda i: (i, 0, 0))],
        out_shape=[jax.ShapeDtypeStruct((n, D_MODEL), F32), jax.ShapeDtypeStruct((nt, 2 * TOP_K, tm), F32),
                   jax.ShapeDtypeStruct((n, 2 * TOP_K), F32), jax.ShapeDtypeStruct((nt, 1, N_EXPERTS), I32)],
        compiler_params=_cparams("parallel"), name="out_proj_route",
    )(x, y_rnn, o, lw["gn_rnn"], lw["gn_attn"], lw["w_out"], lw["w_out"], lw["ln1_g"], lw["ln1_b"],
      lw["router_wt"], lw["router_b"])


RUN_CHUNK = 64
_SMALL_PIECES = (32, 16, 8)
_MAX_PAD_ROWS = N_EXPERTS * SUBLANES
_PAD_PIECES = (128, 64, 32, 16, 8)


def _for_each_piece(n8, fn):
    def chunk(j, c):
        fn(pl.multiple_of(j * RUN_CHUNK, RUN_CHUNK), RUN_CHUNK)
        return c
    lax.fori_loop(0, n8 // RUN_CHUNK, chunk, 0)
    for s in _SMALL_PIECES:
        @pl.when((n8 & s) != 0)
        def _(s=s):
            fn(n8 & (-2 * s), s)


def _wait_tile_rows(n_rows, tm, wait_rows):
    wait_rows(TOP_K * tm)
    pad = n_rows - TOP_K * tm
    for s in _PAD_PIECES:
        @pl.when((pad & s) != 0)
        def _(s=s):
            wait_rows(s)


def _dispatch_kernel(cnt_ref, base_ref, x_ref, rrow_ref, *rest, tm):
    xbuf_ref, xs_ref, carry_ref, nrows_ref, sems = rest[-5:]
    nrow = _sorted_rows(tm)
    i = pl.program_id(0)
    slot = i % 2

    @pl.when(i == 0)
    def _():
        def init(e, c):
            carry_ref[e] = base_ref[e]
            return c
        lax.fori_loop(0, N_EXPERTS, init, 0)

    r = lax.broadcasted_iota(I32, (ONE_HOT_BLOCK, 1), 0).astype(F32).astype(BF16)
    blocks = []
    for c in range(nrow // ONE_HOT_BLOCK):
        hit = None
        for k in range(TOP_K):
            h = r == (rrow_ref[k:k + 1, :] - float(c * ONE_HOT_BLOCK)).astype(BF16)
            hit = h if hit is None else jnp.logical_or(hit, h)
        blocks.append(jnp.where(hit, jnp.ones((), BF16), jnp.zeros((), BF16)))
    perm = jnp.concatenate(blocks, axis=0)
    xs = jnp.dot(perm, x_ref[...].astype(BF16), preferred_element_type=F32)
    lo = lax.bitcast_convert_type(xs[:, :D_PACK], U32)
    hi = lax.bitcast_convert_type(xs[:, D_PACK:], U32)
    xs_ref[slot] = (hi & U32(HI_MASK)) | (lo >> 16)

    def run_copy(buf, src, dst, s):
        return pltpu.make_async_copy(xs_ref.at[buf, pl.ds(pl.multiple_of(src, SUBLANES), s)],
                                     xbuf_ref.at[pl.ds(pl.multiple_of(dst, SUBLANES), s)], sems.at[buf])

    def start(e, src):
        n8 = cnt_ref[0, e]
        dst = carry_ref[e]
        _for_each_piece(n8, lambda off, s: run_copy(slot, src + off, dst + off, s).start())
        carry_ref[e] = dst + n8
        return src + n8

    def wait_tile(buf):
        _wait_tile_rows(nrows_ref[buf], tm, lambda s: run_copy(buf, 0, 0, s).wait())

    nrows_ref[slot] = lax.fori_loop(0, N_EXPERTS, start, 0)

    @pl.when(i > 0)
    def _():
        wait_tile(1 - slot)

    @pl.when(i == pl.num_programs(0) - 1)
    def _():
        wait_tile(slot)


def _dispatch(x1, rrow, cnt8, base, xbuf, n_rows):
    n = x1.shape[0]
    nt, _, tm = rrow.shape
    in_specs = [pl.BlockSpec((None, 1, N_EXPERTS), lambda i: (i, 0, 0), memory_space=pltpu.SMEM),
                pl.BlockSpec(memory_space=pltpu.SMEM),
                pl.BlockSpec((tm, D_MODEL), lambda i: (i, 0)),
                pl.BlockSpec((None, 2 * TOP_K, tm), lambda i: (i, 0, 0))]
    args = [cnt8, base, x1, rrow]
    aliases = {}
    if xbuf is not None:
        in_specs.append(pl.BlockSpec(memory_space=pl.ANY))
        args.append(xbuf)
        aliases = {len(args) - 1: 0}
    return pl.pallas_call(
        functools.partial(_dispatch_kernel, tm=tm),
        grid=(nt,),
        in_specs=in_specs,
        out_specs=pl.BlockSpec(memory_space=pl.ANY),
        out_shape=jax.ShapeDtypeStruct((n_rows, D_PACK), U32),
        scratch_shapes=[pltpu.VMEM((2, _sorted_rows(tm), D_PACK), U32), pltpu.SMEM((N_EXPERTS,), I32),
                        pltpu.SMEM((2,), I32), pltpu.SemaphoreType.DMA((2,))],
        input_output_aliases=aliases,
        compiler_params=_cparams("arbitrary"), name="moe_dispatch",
    )(*args)


def _moe_kernel(br_ref, be_ref, nv_ref, first_ref, next_ref, x_ref, w1_hbm, b1_ref, w2_hbm, b2_ref, y_ref,
                w1f_ref, w2f_ref, w1s_ref, w2s_ref, sems, *, layer):
    i = pl.program_id(0)

    def load(e):
        return (pltpu.make_async_copy(w1_hbm.at[layer, e], w1f_ref, sems.at[0]),
                pltpu.make_async_copy(w2_hbm.at[layer, e], w2f_ref, sems.at[1]))

    @pl.when(jnp.logical_and(i == 0, first_ref[0] == 1))
    def _():
        for c in load(be_ref[0]):
            c.start()

    @pl.when(first_ref[i] == 1)
    def _():
        for c in load(be_ref[i]):
            c.wait()
        w1s_ref[...] = w1f_ref[...].astype(BF16)
        w2s_ref[...] = w2f_ref[...].astype(BF16)

        @pl.when(next_ref[i] >= 0)
        def _():
            for c in load(next_ref[i]):
                c.start()

    nv = nv_ref[i]

    def expert_rows(m):
        rows = lax.broadcasted_iota(I32, (m, 1), 0)
        xw = jnp.where(rows < nv, x_ref[0:m, :], U32(0))
        lo = lax.bitcast_convert_type(xw << 16, F32).astype(BF16)
        hi = lax.bitcast_convert_type(xw & U32(HI_MASK), F32).astype(BF16)
        hdn = (jnp.dot(lo, w1s_ref[:D_PACK, :], preferred_element_type=F32)
               + jnp.dot(hi, w1s_ref[D_PACK:, :], preferred_element_type=F32) + b1_ref[...])
        g = jnp.minimum(hdn[:, :D_FF], SWIGLU_LIMIT)
        lin = jnp.clip(hdn[:, D_FF:], -SWIGLU_LIMIT, SWIGLU_LIMIT)
        act = g * jax.nn.sigmoid(SWIGLU_ALPHA * g) * (lin + 1.0)
        y_ref[0:m, :] = jnp.dot(act.astype(BF16), w2s_ref[...], preferred_element_type=F32) + b2_ref[...]

    quarter = MOE_TILE // 4
    for q in range(1, 5):
        @pl.when(jnp.logical_and(nv > (q - 1) * quarter, nv <= q * quarter))
        def _(q=q):
            expert_rows(q * quarter)


def _moe_experts(xbuf, geom, w1, b1, w2, b2, layer):
    n_blocks = xbuf.shape[0] // MOE_TILE
    bias = lambda w: pl.BlockSpec((None, None, 1, w), lambda i, br, be, *_: (layer, be[i], 0, 0))
    grid_spec = pltpu.PrefetchScalarGridSpec(
        num_scalar_prefetch=5,
        grid=(n_blocks,),
        in_specs=[pl.BlockSpec((MOE_TILE, D_PACK), lambda i, br, *_: (br[i], 0)),
                  pl.BlockSpec(memory_space=pl.ANY), bias(2 * D_FF),
                  pl.BlockSpec(memory_space=pl.ANY), bias(D_MODEL)],
        out_specs=pl.BlockSpec((MOE_TILE, D_MODEL), lambda i, br, *_: (br[i], 0)),
        scratch_shapes=[pltpu.VMEM((D_MODEL, 2 * D_FF), F32), pltpu.VMEM((D_FF, D_MODEL), F32),
                        pltpu.VMEM((D_MODEL, 2 * D_FF), BF16), pltpu.VMEM((D_FF, D_MODEL), BF16),
                        pltpu.SemaphoreType.DMA((2,))],
    )
    depth = w1.shape[0]
    return pl.pallas_call(
        functools.partial(_moe_kernel, layer=layer), grid_spec=grid_spec,
        out_shape=jax.ShapeDtypeStruct((xbuf.shape[0], D_MODEL), F32),
        compiler_params=_cparams("arbitrary"), name="moe_experts",
    )(*geom, xbuf, w1, b1.reshape(depth, N_EXPERTS, 1, 2 * D_FF), w2, b2.reshape(depth, N_EXPERTS, 1, D_MODEL))


def _final_kernel(cnt_ref, cnt_next_ref, base_ref, x_ref, p_ref, rcol_ref, wg_ref, bg_ref, wp_ref, g2_ref, b2_ref,
                  ybuf_ref, out_ref, ys_ref, carry_ref, nrows_ref, sems, *, tm, alpha):
    nrow = _sorted_rows(tm)
    i = pl.program_id(0)
    slot = i % 2

    def run_copy(buf, src, dst, s):
        return pltpu.make_async_copy(ybuf_ref.at[pl.ds(pl.multiple_of(src, SUBLANES), s)],
                                     ys_ref.at[buf, pl.ds(pl.multiple_of(dst, SUBLANES), s)], sems.at[buf])

    def gather_tile(counts_ref, buf):
        def start(e, dst):
            n8 = counts_ref[0, e]
            src = carry_ref[e]
            _for_each_piece(n8, lambda off, s: run_copy(buf, src + off, dst + off, s).start())
            carry_ref[e] = src + n8
            return dst + n8
        nrows_ref[buf] = lax.fori_loop(0, N_EXPERTS, start, 0)

    @pl.when(i == 0)
    def _():
        def init(e, c):
            carry_ref[e] = base_ref[e]
            return c
        lax.fori_loop(0, N_EXPERTS, init, 0)
        ys_ref[:, TOP_K * tm:, :] = jnp.zeros((2, nrow - TOP_K * tm, D_MODEL), F32)
        gather_tile(cnt_ref, 0)

    @pl.when(i + 1 < pl.num_programs(0))
    def _():
        gather_tile(cnt_next_ref, 1 - slot)

    x = x_ref[...]
    ple = (jax.nn.sigmoid(jnp.dot(x.astype(BF16), wg_ref[...], preferred_element_type=F32) + bg_ref[...])
           * jnp.dot(p_ref[...].astype(BF16), wp_ref[...], preferred_element_type=F32))

    _wait_tile_rows(nrows_ref[slot], tm, lambda s: run_copy(slot, 0, 0, s).wait())

    ys = ys_ref[slot].astype(BF16)
    col = lax.broadcasted_iota(I32, (1, ONE_HOT_BLOCK), 1).astype(F32).astype(BF16)
    rc = rcol_ref[...]
    gates = rc[:, TOP_K:].astype(BF16)
    blocks = []
    for c in range(nrow // ONE_HOT_BLOCK):
        blk = None
        for k in range(TOP_K):
            hit = col == (rc[:, k:k + 1] - float(c * ONE_HOT_BLOCK)).astype(BF16)
            term = jnp.where(hit, gates[:, k:k + 1], jnp.zeros((), BF16))
            blk = term if blk is None else blk + term
        blocks.append(blk)
    ffn = jnp.dot(jnp.concatenate(blocks, axis=1), ys, preferred_element_type=F32)
    out_ref[...] = _layer_norm(alpha * x + ffn + ple, g2_ref[...], b2_ref[...])


def _final(x1, p_all, layer, rcol, cnt8, base, ybuf, lw):
    n = x1.shape[0]
    nt = cnt8.shape[0]
    tm = n // nt
    row = lambda w: pl.BlockSpec((tm, w), lambda i: (i, 0))
    vec = lambda w: _layer_spec(layer, 1, w)
    return pl.pallas_call(
        functools.partial(_final_kernel, tm=tm, alpha=lw["alpha"]),
        grid=(nt,),
        in_specs=[pl.BlockSpec((None, 1, N_EXPERTS), lambda i: (i, 0, 0), memory_space=pltpu.SMEM),
                  pl.BlockSpec((None, 1, N_EXPERTS), lambda i: (jnp.minimum(i + 1, nt - 1), 0, 0),
                               memory_space=pltpu.SMEM),
                  pl.BlockSpec(memory_space=pltpu.SMEM),
                  row(D_MODEL), pl.BlockSpec((None, tm, D_PLE), lambda i: (layer, i, 0)), row(2 * TOP_K),
                  _layer_spec(layer, D_MODEL, D_MODEL), vec(D_MODEL), _layer_spec(layer, D_PLE, D_MODEL),
                  vec(D_MODEL), vec(D_MODEL), pl.BlockSpec(memory_space=pl.ANY)],
        out_specs=row(D_MODEL),
        out_shape=jax.ShapeDtypeStruct((n, D_MODEL), F32),
        scratch_shapes=[pltpu.VMEM((2, _sorted_rows(tm), D_MODEL), F32), pltpu.SMEM((N_EXPERTS,), I32),
                        pltpu.SMEM((2,), I32), pltpu.SemaphoreType.DMA((2,))],
        compiler_params=_cparams("arbitrary"), name="moe_combine_ln2",
    )(cnt8, cnt8, base, x1, p_all, rcol, lw["ple_w_gate"], lw["ple_b_gate"], lw["ple_w_proj"], lw["ln2_g"], lw["ln2_b"],
      ybuf)


def _block_diag(w):
    depth, nb, bw, _ = w.shape
    eye = jnp.eye(nb, dtype=w.dtype)
    return (w[:, :, :, None, :] * eye[None, :, None, :, None]).reshape(depth, nb * bw, nb * bw)


def _hi_lo_rows(w):
    hi = w.astype(BF16)
    lo = (w - hi.astype(F32)).astype(BF16)
    return jnp.concatenate([hi, lo], axis=-2)


def _block_geometry(tot, n_blocks):
    nblk = (tot + MOE_TILE - 1) // MOE_TILE
    blk_end = jnp.cumsum(nblk)
    blk_start = blk_end - nblk
    ids = jnp.arange(n_blocks, dtype=I32)
    n_valid = blk_end[-1]
    expert_at = lambda j: jnp.minimum(jnp.sum(j[:, None] >= blk_end[None, :], axis=1), N_EXPERTS - 1).astype(I32)
    src = jnp.clip(ids, 0, jnp.maximum(n_valid - 1, 0))
    e_of = expert_at(src)
    hot = e_of[:, None] == jnp.arange(N_EXPERTS)[None, :]
    pick = lambda v: jnp.sum(jnp.where(hot, v[None, :], 0), axis=1)
    start_of, end_of, tot_of = pick(blk_start), pick(blk_end), pick(tot)
    real = ids < n_valid
    nv = jnp.where(real, jnp.clip(tot_of - (ids - start_of) * MOE_TILE, 0, MOE_TILE), 0)
    first = jnp.logical_and(real, ids == start_of)
    nxt = jnp.where(end_of < n_valid, expert_at(end_of), -1)
    geom = tuple(v.astype(I32) for v in (src, e_of, nv, first, nxt))
    return geom, (blk_start * MOE_TILE).astype(I32)


def kernel(x_prompt, x_sample, cache_k, cache_v, state_conv, state_h, p_prompt, p_sample, ln_in_g, ln_in_b, rel_bias, w_in, conv_w, conv_b, rg_wa, rg_ba, rg_wx, rg_bx, rg_lambda, attn_sinks, gn_rnn, gn_attn, w_out, ln1_g, ln1_b, router_w, router_b, moe_w1, moe_b1, moe_w2, moe_b2, ple_w_gate, ple_b_gate, ple_w_proj, ln2_g, ln2_b):
    depth = w_in.shape[0]
    bsz, t_len, _ = x_prompt.shape
    n_p = bsz * t_len
    n_s = x_sample.shape[0]
    wb = cache_k.shape[2]
    alpha = (2 * depth) ** 0.25

    dist = jnp.arange(WINDOW)[:, None] + WINDOW - jnp.arange(2 * WINDOW)[None, :]
    valid = (dist >= 0) & (dist < WINDOW)
    bias_prompt = jnp.where(valid[None], _bias_lookup(rel_bias, dist), NEG)
    bias_sample = _bias_lookup(rel_bias, WINDOW - 1 - jnp.arange(WINDOW))

    tm_p = min(ROW_TILE, n_p)
    n_tiles = n_p // tm_p + 1
    max_rows = (n_p + n_s) * TOP_K + n_tiles * N_EXPERTS * (SUBLANES - 1)
    n_blocks = -(-(max_rows + N_EXPERTS * (MOE_TILE - 1)) // MOE_TILE)

    xp = x_prompt.reshape(n_p, D_MODEL)
    xs = x_sample.reshape(n_s, D_MODEL)
    pp = p_prompt.reshape(depth, n_p, D_PLE)
    ps = p_sample.reshape(depth, n_s, D_PLE)
    to_feature_major = lambda c: jnp.transpose(c, (0, 1, 3, 4, 2)).reshape(depth, n_s, N_KV * HEAD_DIM, wb)
    to_position_major = lambda c: jnp.transpose(c.reshape(depth, n_s, N_KV, HEAD_DIM, wb), (0, 1, 4, 2, 3))
    ck = to_feature_major(cache_k)
    cv = to_feature_major(cache_v)
    new_caches = None
    row2 = lambda v: v.reshape(1, -1)
    kv0 = 2 * D_RNN + D_ATTN
    outs = {k: [] for k in ("kp", "vp", "cp", "hp", "cs", "hs")}

    vecs = lambda v: v.reshape(depth, 1, -1)
    rw_all = {"conv_w": conv_w, "conv_b": vecs(conv_b), "wa": _block_diag(rg_wa).astype(BF16),
              "ba": vecs(rg_ba), "wx": _block_diag(rg_wx).astype(BF16), "bx": vecs(rg_bx),
              "lam": vecs(rg_lambda)}
    lw_all = {"gn_rnn": vecs(gn_rnn), "gn_attn": vecs(gn_attn), "w_out": w_out.astype(BF16),
              "ln1_g": vecs(ln1_g), "ln1_b": vecs(ln1_b),
              "router_wt": _hi_lo_rows(jnp.swapaxes(router_w, 1, 2)),
              "router_b": router_b.reshape(depth, N_EXPERTS, 1), "alpha": alpha,
              "ple_w_gate": ple_w_gate.astype(BF16), "ple_b_gate": vecs(ple_b_gate),
              "ple_w_proj": ple_w_proj.astype(BF16), "ln2_g": vecs(ln2_g), "ln2_b": vecs(ln2_b)}
    w_in_b = w_in.astype(BF16)

    for l in range(depth):
        rw = dict(rw_all, layer=l)
        lw = dict(lw_all, layer=l)
        first = l == 0

        xp, zp = _in_proj(xp, row2(ln_in_g), row2(ln_in_b), w_in_b, l, first)
        zp3 = zp.reshape(bsz, t_len, D_IN)
        yp, hp = _rnn_prompt(zp3, rw)
        op = _attn_prompt(zp, bsz, t_len, bias_prompt, attn_sinks, l)
        wp = min(WINDOW, t_len)
        outs["kp"].append(zp3[:, t_len - wp:, kv0:kv0 + 128].reshape(bsz, wp, N_KV, HEAD_DIM))
        outs["vp"].append(zp3[:, t_len - wp:, kv0 + 128:kv0 + 256].reshape(bsz, wp, N_KV, HEAD_DIM))
        outs["cp"].append(zp3[:, t_len - (CONV_W - 1):, :D_RNN])
        outs["hp"].append(hp.reshape(bsz, D_RNN))

        xs, zs = _in_proj(xs, row2(ln_in_g), row2(ln_in_b), w_in_b, l, first)
        ys, hs = _rnn_sample(zs, state_conv, state_h, rw)
        os_, new_caches = _attn_sample(zs[:, 2 * D_RNN:kv0], zs[:, kv0:kv0 + 128], zs[:, kv0 + 128:], ck, cv,
                                       new_caches, l, bias_sample, attn_sinks)
        outs["cs"].append(jnp.concatenate([state_conv[l][:, 1:], zs[:, None, :D_RNN]], axis=1))
        outs["hs"].append(hs)

        x1p, rrow_p, rcol_p, cnt_p = _out_proj(xp, yp.reshape(n_p, D_RNN), op, lw)
        x1s, rrow_s, rcol_s, cnt_s = _out_proj(xs, ys, os_, lw)

        tot_p = jnp.sum(cnt_p, axis=(0, 1))
        tot = tot_p + jnp.sum(cnt_s, axis=(0, 1))
        geom, base_p = _block_geometry(tot, n_blocks)
        base_s = base_p + tot_p

        xbuf = _dispatch(x1p, rrow_p, cnt_p, base_p, None, n_blocks * MOE_TILE)
        xbuf = _dispatch(x1s, rrow_s, cnt_s, base_s, xbuf, n_blocks * MOE_TILE)
        ybuf = _moe_experts(xbuf, geom, moe_w1, moe_b1, moe_w2, moe_b2, l)
        xp = _final(x1p, pp, l, rcol_p, cnt_p, base_p, ybuf, lw)
        xs = _final(x1s, ps, l, rcol_s, cnt_s, base_s, ybuf, lw)

    st = lambda k: jnp.stack(outs[k])
    return (xp.reshape(bsz, t_len, D_MODEL), xs.reshape(n_s, 1, D_MODEL),
            st("kp"), st("vp"), st("cp"), st("hp"), to_position_major(new_caches[0]),
            to_position_major(new_caches[1]), st("cs"), st("hs"))
```

```python
import functools
import math

import jax
import jax.numpy as jnp
from jax import lax
from jax.experimental import pallas as pl
from jax.experimental.pallas import tpu as pltpu

F32 = jnp.float32
BF16 = jnp.bfloat16
I32 = jnp.int32
U32 = jnp.uint32

D_MODEL = 1024
D_RNN = 512
RNN_BLOCKS = 8
CONV_W = 4
LRU_C = 8.0
N_HEADS = 8
HEAD_DIM = 64
N_KV = 2
GQA_R = N_HEADS // N_KV
D_ATTN = N_HEADS * HEAD_DIM
WINDOW = 128
N_BUCKETS = 32
MAX_DIST = 128
N_EXPERTS = 32
TOP_K = 4
D_FF = 1024
SWIGLU_LIMIT = 7.0
SWIGLU_ALPHA = 1.702
D_PLE = 256
LN_EPS = 1e-5
D_IN = 2 * D_RNN + D_ATTN + 2 * N_KV * HEAD_DIM
NEG = -1e30

SUBLANES = 8
LANES = 128
ROW_TILE = 512
MOE_TILE = 1024
D_PACK = D_MODEL // 2
VMEM_LIMIT = 56 * 1024 * 1024
HI_MASK = 0xFFFF0000


def _cparams(*sem):
    return pltpu.CompilerParams(dimension_semantics=sem, vmem_limit_bytes=VMEM_LIMIT)


def _layer_norm(x, g, b):
    mu = jnp.mean(x, axis=-1, keepdims=True)
    xc = x - mu
    var = jnp.mean(xc * xc, axis=-1, keepdims=True)
    return xc * lax.rsqrt(var + LN_EPS) * g + b


def _rms_norm(x, g):
    return x * lax.rsqrt(jnp.mean(x * x, axis=-1, keepdims=True) + LN_EPS) * g


def _const_spec(shape):
    return pl.BlockSpec(shape, lambda *_: (0,) * len(shape))


def _layer_spec(layer, *tail):
    return pl.BlockSpec((None,) + tail, lambda *_: (layer,) + (0,) * len(tail))


def _in_proj_kernel(x_ref, g_ref, b_ref, w_ref, *out_refs, apply_ln):
    x = x_ref[...]
    if apply_ln:
        x = _layer_norm(x, g_ref[...], b_ref[...])
        out_refs[0][...] = x
    out_refs[-1][...] = jnp.dot(x.astype(BF16), w_ref[...], preferred_element_type=F32)


def _in_proj(x, ln_g, ln_b, w_in_bf16, layer, apply_ln):
    n = x.shape[0]
    tm = min(ROW_TILE, n)
    row = lambda w: pl.BlockSpec((tm, w), lambda i: (i, 0))
    out_shape = [jax.ShapeDtypeStruct((n, D_IN), F32)]
    out_specs = [row(D_IN)]
    if apply_ln:
        out_shape.insert(0, jax.ShapeDtypeStruct((n, D_MODEL), F32))
        out_specs.insert(0, row(D_MODEL))
    outs = pl.pallas_call(
        functools.partial(_in_proj_kernel, apply_ln=apply_ln),
        grid=(n // tm,),
        in_specs=[row(D_MODEL), _const_spec((1, D_MODEL)), _const_spec((1, D_MODEL)),
                  _layer_spec(layer, D_MODEL, D_IN)],
        out_specs=out_specs, out_shape=out_shape,
        compiler_params=_cparams("parallel"), name="in_proj",
    )(x, ln_g, ln_b, w_in_bf16)
    return (outs[0], outs[1]) if apply_ln else (x, outs[0])


def _rglru_coeffs(u, wa_ref, ba_ref, wx_ref, bx_ref, lam_ref):
    ub = u.astype(BF16)
    r = jax.nn.sigmoid(jnp.dot(ub, wa_ref[...], preferred_element_type=F32) + ba_ref[...])
    i = jax.nn.sigmoid(jnp.dot(ub, wx_ref[...], preferred_element_type=F32) + bx_ref[...])
    lam = -lam_ref[...]
    softplus = jnp.maximum(lam, 0.0) + jnp.log1p(jnp.exp(-jnp.abs(lam)))
    log_a = -LRU_C * r * softplus
    a = jnp.exp(log_a)
    b = jnp.sqrt(-jnp.tanh(log_a) * (a * a + 1.0)) * (i * u)
    return a, b


def _rnn_prompt_kernel(xr_ref, gate_ref, cw_ref, cb_ref, wa_ref, ba_ref, wx_ref, bx_ref, lam_ref,
                       y_ref, hlast_ref, xs_ref, a_ref, b_ref, carry_ref, *, tt):
    t = pl.program_id(1)

    @pl.when(t == 0)
    def _():
        xs_ref[0:SUBLANES, :] = jnp.zeros((SUBLANES, D_RNN), F32)
        carry_ref[...] = jnp.zeros((SUBLANES, D_RNN), F32)

    x = xr_ref[...]
    xs_ref[SUBLANES:SUBLANES + tt, :] = x
    cw = cw_ref[...]
    u = cb_ref[...] + x * cw[3:4]
    for j in range(CONV_W - 1):
        off = SUBLANES - (CONV_W - 1) + j
        u = u + xs_ref[off:off + tt, :] * cw[j:j + 1]
    xs_ref[0:SUBLANES, :] = x[tt - SUBLANES:tt, :]

    a, b = _rglru_coeffs(u, wa_ref, ba_ref, wx_ref, bx_ref, lam_ref)
    ng = tt // SUBLANES
    a = a.reshape(ng, SUBLANES, D_RNN)
    b = b.reshape(ng, SUBLANES, D_RNN)
    row = lax.broadcasted_iota(I32, (1, SUBLANES, 1), 1)
    for s in (1, 2, 4):
        a_sh = pltpu.roll(a, s, 1)
        b_sh = pltpu.roll(b, s, 1)
        m = row >= s
        b = jnp.where(m, a * b_sh + b, b)
        a = jnp.where(m, a * a_sh, a)
    a_ref[...] = a
    b_ref[...] = b

    def body(g, carry):
        h = a_ref[g] * carry + b_ref[g]
        b_ref[g] = h
        return jnp.broadcast_to(h[SUBLANES - 1:SUBLANES, :], (SUBLANES, D_RNN))

    carry = lax.fori_loop(0, ng, body, carry_ref[...])
    carry_ref[...] = carry
    h = b_ref[...].reshape(tt, D_RNN)
    y_ref[...] = h * jax.nn.gelu(gate_ref[...])
    hlast_ref[...] = carry[0:1, :]


def _rnn_prompt(z3, rw):
    bsz, t_len, _ = z3.shape
    tt = min(ROW_TILE, t_len)
    vec = _layer_spec(rw["layer"], 1, D_RNN)
    mat = _layer_spec(rw["layer"], D_RNN, D_RNN)
    return pl.pallas_call(
        functools.partial(_rnn_prompt_kernel, tt=tt),
        grid=(bsz, t_len // tt),
        in_specs=[pl.BlockSpec((None, tt, D_RNN), lambda b, t: (b, t, 0)),
                  pl.BlockSpec((None, tt, D_RNN), lambda b, t: (b, t, 1)),
                  _layer_spec(rw["layer"], CONV_W, D_RNN), vec, mat, vec, mat, vec, vec],
        out_specs=[pl.BlockSpec((None, tt, D_RNN), lambda b, t: (b, t, 0)),
                   pl.BlockSpec((None, 1, D_RNN), lambda b, t: (b, 0, 0))],
        out_shape=[jax.ShapeDtypeStruct((bsz, t_len, D_RNN), F32),
                   jax.ShapeDtypeStruct((bsz, 1, D_RNN), F32)],
        scratch_shapes=[pltpu.VMEM((tt + SUBLANES, D_RNN), F32),
                        pltpu.VMEM((tt // SUBLANES, SUBLANES, D_RNN), F32),
                        pltpu.VMEM((tt // SUBLANES, SUBLANES, D_RNN), F32),
                        pltpu.VMEM((SUBLANES, D_RNN), F32)],
        compiler_params=_cparams("parallel", "arbitrary"), name="rnn_prompt",
    )(z3, z3, rw["conv_w"], rw["conv_b"], rw["wa"], rw["ba"], rw["wx"], rw["bx"], rw["lam"])


def _rnn_sample_kernel(xr_ref, gate_ref, c0_ref, c1_ref, c2_ref, h0_ref, cw_ref, cb_ref, wa_ref, ba_ref,
                       wx_ref, bx_ref, lam_ref, y_ref, h_ref):
    x = xr_ref[...]
    cw = cw_ref[...]
    u = (cb_ref[...] + c0_ref[...] * cw[0:1] + c1_ref[...] * cw[1:2] + c2_ref[...] * cw[2:3]
         + x * cw[3:4])
    a, b = _rglru_coeffs(u, wa_ref, ba_ref, wx_ref, bx_ref, lam_ref)
    h = a * h0_ref[...] + b
    h_ref[...] = h
    y_ref[...] = h * jax.nn.gelu(gate_ref[...])


def _rnn_sample(z, conv_state, h0, rw):
    n = z.shape[0]
    layer = rw["layer"]
    vec = _layer_spec(layer, 1, D_RNN)
    mat = _layer_spec(layer, D_RNN, D_RNN)
    full = _const_spec((n, D_RNN))
    depth = conv_state.shape[0]
    conv_flat = conv_state.reshape(depth, n, (CONV_W - 1) * D_RNN)
    tap = lambda j: pl.BlockSpec((None, n, D_RNN), lambda i: (layer, 0, j))
    return pl.pallas_call(
        _rnn_sample_kernel,
        grid=(1,),
        in_specs=[pl.BlockSpec((n, D_RNN), lambda i: (0, 0)), pl.BlockSpec((n, D_RNN), lambda i: (0, 1)),
                  tap(0), tap(1), tap(2), _layer_spec(layer, n, D_RNN),
                  _layer_spec(layer, CONV_W, D_RNN), vec, mat, vec, mat, vec, vec],
        out_specs=[full, full],
        out_shape=[jax.ShapeDtypeStruct((n, D_RNN), F32)] * 2,
        compiler_params=_cparams("arbitrary"), name="rnn_sample",
    )(z, z, conv_flat, conv_flat, conv_flat, h0, rw["conv_w"], rw["conv_b"], rw["wa"], rw["ba"], rw["wx"], rw["bx"], rw["lam"])


def _rel_bucket(dist):
    n = jnp.maximum(dist, 0)
    max_exact = N_BUCKETS // 2
    nf = jnp.maximum(n, max_exact).astype(F32)
    large = max_exact + (jnp.log(nf / max_exact) / math.log(MAX_DIST / max_exact)
                         * (N_BUCKETS - max_exact)).astype(I32)
    large = jnp.minimum(large, N_BUCKETS - 1)
    return jnp.where(n < max_exact, n, large)


def _bias_lookup(rel_bias, dist):
    hot = (_rel_bucket(dist)[..., None] == jnp.arange(N_BUCKETS)).astype(F32)
    out = jnp.tensordot(hot, rel_bias.astype(F32), axes=1, precision=lax.Precision.HIGHEST)
    return jnp.moveaxis(out, -1, 0)


def _half_lane_variants(x, group):
    lo = lax.broadcasted_iota(I32, (1, 2 * HEAD_DIM), 1) < HEAD_DIM
    xr = pltpu.roll(x, HEAD_DIM, 1)
    zero = jnp.zeros_like(x)
    if group == 0:
        return jnp.where(lo, x, zero), jnp.where(lo, zero, xr)
    return jnp.where(lo, xr, zero), jnp.where(lo, zero, x)


def _attn_prompt_kernel(q_ref, kc_ref, kp_ref, vc_ref, vp_ref, bias_ref, sink_ref, o_ref, *, nq, layer):
    first = pl.program_id(1) == 0
    col = lax.broadcasted_iota(I32, (1, 2 * WINDOW), 1)
    hide_prev = jnp.logical_and(first, col < WINDOW)
    for j in range(nq):
        rows = slice(j * WINDOW, (j + 1) * WINDOW)
        before = slice((j - 1) * WINDOW, j * WINDOW)
        kk = jnp.concatenate([kp_ref[...] if j == 0 else kc_ref[before, :], kc_ref[rows, :]], axis=0)
        vv = jnp.concatenate([vp_ref[...] if j == 0 else vc_ref[before, :], vc_ref[rows, :]], axis=0)
        for g in range(N_KV):
            k_lo, k_hi = _half_lane_variants(kk, g)
            v_lo, v_hi = _half_lane_variants(vv, g)
            for pair in range(GQA_R // 2):
                p0 = g * (GQA_R // 2) + pair
                qp = (q_ref[rows, p0 * 128:(p0 + 1) * 128] * (HEAD_DIM ** -0.5)).astype(BF16)
                acc = jnp.zeros((WINDOW, 2 * HEAD_DIM), F32)
                for kx, vx, h in ((k_lo, v_lo, 2 * p0), (k_hi, v_hi, 2 * p0 + 1)):
                    s = lax.dot_general(qp, kx.astype(BF16), (((1,), (1,)), ((), ())),
                                        preferred_element_type=F32) + bias_ref[h]
                    if j == 0:
                        s = jnp.where(hide_prev, NEG, s)
                    sink = sink_ref[layer, h]
                    m = jnp.maximum(jnp.max(s, axis=-1, keepdims=True), sink)
                    p = jnp.exp(s - m)
                    denom = jnp.sum(p, axis=-1, keepdims=True) + jnp.exp(sink - m)
                    acc = acc + jnp.dot(p.astype(BF16), vx.astype(BF16), preferred_element_type=F32) / denom
                o_ref[rows, p0 * 128:(p0 + 1) * 128] = acc


def _attn_prompt(z, bsz, t_len, bias_prompt, sinks, layer):
    nb = t_len // WINDOW
    nq = min(4, nb)
    ns = nb // nq
    kcol = (2 * D_RNN + D_ATTN) // 128
    cur = lambda c: pl.BlockSpec((nq * WINDOW, 128), lambda b, i: (b * ns + i, c))
    prev = lambda c: pl.BlockSpec((WINDOW, 128), lambda b, i: (b * nb + jnp.maximum(i * nq - 1, 0), c))
    return pl.pallas_call(
        functools.partial(_attn_prompt_kernel, nq=nq, layer=layer),
        grid=(bsz, ns),
        in_specs=[pl.BlockSpec((nq * WINDOW, D_ATTN), lambda b, i: (b * ns + i, 2 * D_RNN // D_ATTN)),
                  cur(kcol), prev(kcol), cur(kcol + 1), prev(kcol + 1),
                  _const_spec((N_HEADS, WINDOW, 2 * WINDOW)),
                  pl.BlockSpec(memory_space=pltpu.SMEM)],
        out_specs=pl.BlockSpec((nq * WINDOW, D_ATTN), lambda b, i: (b * ns + i, 0)),
        out_shape=jax.ShapeDtypeStruct((bsz * t_len, D_ATTN), F32),
        compiler_params=_cparams("parallel", "arbitrary"), name="attn_prompt",
    )(z, z, z, z, z, bias_prompt, sinks)


def _attn_sample_kernel(q_ref, kn_ref, vn_ref, ck_ref, cv_ref, bias_ref, sink_ref, hmask_ref, *rest):
    o_ref, nk_ref, nv_ref = rest[-3:]
    pos = lax.broadcasted_iota(I32, (1, 1, WINDOW), 2)
    last = pos == WINDOW - 1
    bb = ck_ref.shape[0]
    newk = jnp.where(last, jnp.broadcast_to(kn_ref[...], (bb, 128, WINDOW)), pltpu.roll(ck_ref[...], WINDOW - 1, 2))
    newv = jnp.where(last, jnp.broadcast_to(vn_ref[...], (bb, 128, WINDOW)), pltpu.roll(cv_ref[...], WINDOW - 1, 2))
    nk_ref[...] = newk
    nv_ref[...] = newv
    s = lax.dot_general(q_ref[...].astype(BF16), newk.astype(BF16), (((2,), (1,)), ((0,), (0,))),
                        preferred_element_type=F32)
    s = s + bias_ref[...][None]
    sink = sink_ref[...][None]
    m = jnp.maximum(jnp.max(s, axis=-1, keepdims=True), sink)
    p = jnp.exp(s - m)
    denom = jnp.sum(p, axis=-1, keepdims=True) + jnp.exp(sink - m)
    o = lax.dot_general(p.astype(BF16), newv.astype(BF16), (((2,), (2,)), ((0,), (0,))),
                        preferred_element_type=F32)
    o_ref[...] = o / denom * hmask_ref[...][None]


def _attn_sample(q, k_new, v_new, k_cache, v_cache, new_caches, layer, bias_sample, sinks):
    n = q.shape[0]
    depth = k_cache.shape[0]
    bb = 16
    qh = q.reshape(n, N_KV, GQA_R, HEAD_DIM) * (HEAD_DIM ** -0.5)
    eye = jnp.eye(N_KV, dtype=F32)
    q8 = (qh[:, :, :, None, :] * eye[None, :, None, :, None]).reshape(n, N_HEADS, N_KV * HEAD_DIM)
    hmask = jnp.repeat(jnp.repeat(eye, GQA_R, axis=0), HEAD_DIM, axis=1)
    blk3 = lambda a, b: pl.BlockSpec((bb, a, b), lambda i: (i, 0, 0))
    cache = pl.BlockSpec((None, bb, 128, WINDOW), lambda i: (layer, i, 0, 0))
    in_specs = [blk3(N_HEADS, 128), blk3(128, 1), blk3(128, 1), cache, cache,
                _const_spec((N_HEADS, WINDOW)), _layer_spec(layer, N_HEADS, 1), _const_spec((N_HEADS, 128))]
    args = [q8, k_new.reshape(n, 128, 1), v_new.reshape(n, 128, 1), k_cache, v_cache, bias_sample,
            sinks.reshape(-1, N_HEADS, 1), hmask]
    aliases = {}
    if new_caches is not None:
        in_specs += [pl.BlockSpec(memory_space=pl.ANY)] * 2
        args += list(new_caches)
        aliases = {len(args) - 2: 1, len(args) - 1: 2}
    o8, nk, nv = pl.pallas_call(
        _attn_sample_kernel,
        grid=(n // bb,),
        in_specs=in_specs,
        out_specs=[blk3(N_HEADS, 128), cache, cache],
        out_shape=[jax.ShapeDtypeStruct((n, N_HEADS, 128), F32),
                   jax.ShapeDtypeStruct((depth, n, 128, WINDOW), F32),
                   jax.ShapeDtypeStruct((depth, n, 128, WINDOW), F32)],
        input_output_aliases=aliases,
        compiler_params=_cparams("arbitrary"), name="attn_sample",
    )(*args)
    o = o8.reshape(n, N_KV, GQA_R, N_KV, HEAD_DIM)
    o = jnp.stack([o[:, g, :, g, :] for g in range(N_KV)], axis=1).reshape(n, D_ATTN)
    return o, (nk, nv)


ONE_HOT_BLOCK = 256


def _sorted_rows(tm):
    return TOP_K * tm + N_EXPERTS * SUBLANES


def _out_proj_kernel(x_ref, yr_ref, o_ref, gnr_ref, gna_ref, wr_ref, wa_ref, g1_ref, b1_ref, rwt_ref, rb_ref,
                     x1_ref, rrow_ref, rcol_ref, cnt_ref, *, tm, alpha):
    yn = _rms_norm(yr_ref[...], gnr_ref[...]).astype(BF16)
    on = _rms_norm(o_ref[...], gna_ref[...]).astype(BF16)
    mix = (jnp.dot(yn, wr_ref[...], preferred_element_type=F32)
           + jnp.dot(on, wa_ref[...], preferred_element_type=F32))
    x1 = _layer_norm(alpha * x_ref[...] + mix, g1_ref[...], b1_ref[...])
    x1_ref[...] = x1

    x1h = x1.astype(BF16)
    x1l = (x1 - x1h.astype(F32)).astype(BF16)
    nt_dims = (((1,), (1,)), ((), ()))
    by_hi = lax.dot_general(rwt_ref[...], x1h, nt_dims, preferred_element_type=F32)
    by_lo = lax.dot_general(rwt_ref[0:N_EXPERTS, :], x1l, nt_dims, preferred_element_type=F32)
    logits = by_hi[0:N_EXPERTS] + by_hi[N_EXPERTS:] + by_lo + rb_ref[...]
    eidx = lax.broadcasted_iota(I32, (N_EXPERTS, tm), 0).astype(F32)
    work = logits
    vals, hots = [], []
    for _ in range(TOP_K):
        v = jnp.max(work, axis=0, keepdims=True)
        idx = jnp.min(jnp.where(work == v, eidx, float(N_EXPERTS)), axis=0, keepdims=True)
        hot = eidx == idx
        work = jnp.where(hot, -jnp.inf, work)
        vals.append(v)
        hots.append(hot.astype(F32))
    ex = [jnp.exp(v - vals[0]) for v in vals]
    tot = ex[0] + ex[1] + ex[2] + ex[3]
    tok_hot = hots[0] + hots[1] + hots[2] + hots[3]
    s_i = lax.broadcasted_iota(I32, (tm, tm), 0)
    t_i = lax.broadcasted_iota(I32, (tm, tm), 1)
    earlier = (s_i < t_i).astype(BF16)
    before = jnp.dot(tok_hot.astype(BF16), earlier, preferred_element_type=F32)
    cnt = jnp.sum(tok_hot, axis=1, keepdims=True)
    cnt8 = jnp.floor((cnt + (SUBLANES - 1)) * (1.0 / SUBLANES)) * SUBLANES
    cnt8_b = jnp.broadcast_to(cnt8, (N_EXPERTS, LANES))
    e_r = lax.broadcasted_iota(I32, (N_EXPERTS, N_EXPERTS), 0)
    e_c = lax.broadcasted_iota(I32, (N_EXPERTS, N_EXPERTS), 1)
    start = jnp.dot((e_c < e_r).astype(F32), cnt8_b, preferred_element_type=F32,
                    precision=lax.Precision.HIGHEST)[:, 0:1]
    slot = start + before
    rows = [jnp.sum(h * slot, axis=0, keepdims=True) for h in hots] + [e / tot for e in ex]
    rrow = jnp.concatenate(rows, axis=0)
    rrow_ref[...] = rrow
    rcol_ref[...] = rrow.T
    cnt_ref[...] = cnt8_b.T[0:1, 0:N_EXPERTS].astype(I32)


def _out_proj(x, y_rnn, o, lw):
    n = x.shape[0]
    tm = min(ROW_TILE, n)
    nt = n // tm
    row = lambda w: pl.BlockSpec((tm, w), lambda i: (i, 0))
    layer = lw["layer"]
    vec = lambda w: _layer_spec(layer, 1, w)
    w_out_half = lambda j: pl.BlockSpec((None, D_MODEL // 2, D_MODEL), lambda i: (layer, j, 0))
    return pl.pallas_call(
        functools.partial(_out_proj_kernel, tm=tm, alpha=lw["alpha"]),
        grid=(nt,),
        in_specs=[row(D_MODEL), row(D_RNN), row(D_ATTN), vec(D_RNN), vec(D_ATTN),
                  w_out_half(0), w_out_half(1), vec(D_MODEL), vec(D_MODEL),
                  _layer_spec(layer, 2 * N_EXPERTS, D_MODEL), _layer_spec(layer, N_EXPERTS, 1)],
        out_specs=[row(D_MODEL), pl.BlockSpec((None, 2 * TOP_K, tm), lambda i: (i, 0, 0)), row(2 * TOP_K),
                   pl.BlockSpec((None, 1, N_EXPERTS), lambda i: (i, 0, 0))],
        out_shape=[jax.ShapeDtypeStruct((n, D_MODEL), F32), jax.ShapeDtypeStruct((nt, 2 * TOP_K, tm), F32),
                   jax.ShapeDtypeStruct((n, 2 * TOP_K), F32), jax.ShapeDtypeStruct((nt, 1, N_EXPERTS), I32)],
        compiler_params=_cparams("parallel"), name="out_proj_route",
    )(x, y_rnn, o, lw["gn_rnn"], lw["gn_attn"], lw["w_out"], lw["w_out"], lw["ln1_g"], lw["ln1_b"],
      lw["router_wt"], lw["router_b"])


RUN_CHUNK = 64
_SMALL_PIECES = (32, 16, 8)
_MAX_PAD_ROWS = N_EXPERTS * SUBLANES
_PAD_PIECES = (128, 64, 32, 16, 8)


def _for_each_piece(n8, fn):
    def chunk(j, c):
        fn(pl.multiple_of(j * RUN_CHUNK, RUN_CHUNK), RUN_CHUNK)
        return c
    lax.fori_loop(0, n8 // RUN_CHUNK, chunk, 0)
    for s in _SMALL_PIECES:
        @pl.when((n8 & s) != 0)
        def _(s=s):
            fn(n8 & (-2 * s), s)


def _wait_tile_rows(n_rows, tm, wait_rows):
    wait_rows(TOP_K * tm)
    pad = n_rows - TOP_K * tm
    for s in _PAD_PIECES:
        @pl.when((pad & s) != 0)
        def _(s=s):
            wait_rows(s)


def _dispatch_kernel(cnt_ref, base_ref, x_ref, rrow_ref, *rest, tm):
    xbuf_ref, xs_ref, carry_ref, nrows_ref, sems = rest[-5:]
    nrow = _sorted_rows(tm)
    i = pl.program_id(0)
    slot = i % 2

    @pl.when(i == 0)
    def _():
        def init(e, c):
            carry_ref[e] = base_ref[e]
            return c
        lax.fori_loop(0, N_EXPERTS, init, 0)

    r = lax.broadcasted_iota(I32, (ONE_HOT_BLOCK, 1), 0).astype(F32).astype(BF16)
    blocks = []
    for c in range(nrow // ONE_HOT_BLOCK):
        hit = None
        for k in range(TOP_K):
            h = r == (rrow_ref[k:k + 1, :] - float(c * ONE_HOT_BLOCK)).astype(BF16)
            hit = h if hit is None else jnp.logical_or(hit, h)
        blocks.append(jnp.where(hit, jnp.ones((), BF16), jnp.zeros((), BF16)))
    perm = jnp.concatenate(blocks, axis=0)
    xs = jnp.dot(perm, x_ref[...].astype(BF16), preferred_element_type=F32)
    lo = lax.bitcast_convert_type(xs[:, :D_PACK], U32)
    hi = lax.bitcast_convert_type(xs[:, D_PACK:], U32)
    xs_ref[slot] = (hi & U32(HI_MASK)) | (lo >> 16)

    def run_copy(buf, src, dst, s):
        return pltpu.make_async_copy(xs_ref.at[buf, pl.ds(pl.multiple_of(src, SUBLANES), s)],
                                     xbuf_ref.at[pl.ds(pl.multiple_of(dst, SUBLANES), s)], sems.at[buf])

    def start(e, src):
        n8 = cnt_ref[0, e]
        dst = carry_ref[e]
        _for_each_piece(n8, lambda off, s: run_copy(slot, src + off, dst + off, s).start())
        carry_ref[e] = dst + n8
        return src + n8

    def wait_tile(buf):
        _wait_tile_rows(nrows_ref[buf], tm, lambda s: run_copy(buf, 0, 0, s).wait())

    nrows_ref[slot] = lax.fori_loop(0, N_EXPERTS, start, 0)

    @pl.when(i > 0)
    def _():
        wait_tile(1 - slot)

    @pl.when(i == pl.num_programs(0) - 1)
    def _():
        wait_tile(slot)


def _dispatch(x1, rrow, cnt8, base, xbuf, n_rows):
    n = x1.shape[0]
    nt, _, tm = rrow.shape
    in_specs = [pl.BlockSpec((None, 1, N_EXPERTS), lambda i: (i, 0, 0), memory_space=pltpu.SMEM),
                pl.BlockSpec(memory_space=pltpu.SMEM),
                pl.BlockSpec((tm, D_MODEL), lambda i: (i, 0)),
                pl.BlockSpec((None, 2 * TOP_K, tm), lambda i: (i, 0, 0))]
    args = [cnt8, base, x1, rrow]
    aliases = {}
    if xbuf is not None:
        in_specs.append(pl.BlockSpec(memory_space=pl.ANY))
        args.append(xbuf)
        aliases = {len(args) - 1: 0}
    return pl.pallas_call(
        functools.partial(_dispatch_kernel, tm=tm),
        grid=(nt,),
        in_specs=in_specs,
        out_specs=pl.BlockSpec(memory_space=pl.ANY),
        out_shape=jax.ShapeDtypeStruct((n_rows, D_PACK), U32),
        scratch_shapes=[pltpu.VMEM((2, _sorted_rows(tm), D_PACK), U32), pltpu.SMEM((N_EXPERTS,), I32),
                        pltpu.SMEM((2,), I32), pltpu.SemaphoreType.DMA((2,))],
        input_output_aliases=aliases,
        compiler_params=_cparams("arbitrary"), name="moe_dispatch",
    )(*args)


def _moe_kernel(br_ref, be_ref, nv_ref, first_ref, next_ref, x_ref, w1_hbm, b1_ref, w2_hbm, b2_ref, y_ref,
                w1f_ref, w2f_ref, w1s_ref, w2s_ref, sems, *, layer):
    i = pl.program_id(0)

    def load(e):
        return (pltpu.make_async_copy(w1_hbm.at[layer, e], w1f_ref, sems.at[0]),
                pltpu.make_async_copy(w2_hbm.at[layer, e], w2f_ref, sems.at[1]))

    @pl.when(jnp.logical_and(i == 0, first_ref[0] == 1))
    def _():
        for c in load(be_ref[0]):
            c.start()

    @pl.when(first_ref[i] == 1)
    def _():
        for c in load(be_ref[i]):
            c.wait()
        w1s_ref[...] = w1f_ref[...].astype(BF16)
        w2s_ref[...] = w2f_ref[...].astype(BF16)

        @pl.when(next_ref[i] >= 0)
        def _():
            for c in load(next_ref[i]):
                c.start()

    nv = nv_ref[i]

    def expert_rows(m):
        rows = lax.broadcasted_iota(I32, (m, 1), 0)
        xw = jnp.where(rows < nv, x_ref[0:m, :], U32(0))
        lo = lax.bitcast_convert_type(xw << 16, F32).astype(BF16)
        hi = lax.bitcast_convert_type(xw & U32(HI_MASK), F32).astype(BF16)
        hdn = (jnp.dot(lo, w1s_ref[:D_PACK, :], preferred_element_type=F32)
               + jnp.dot(hi, w1s_ref[D_PACK:, :], preferred_element_type=F32) + b1_ref[...])
        g = jnp.minimum(hdn[:, :D_FF], SWIGLU_LIMIT)
        lin = jnp.clip(hdn[:, D_FF:], -SWIGLU_LIMIT, SWIGLU_LIMIT)
        act = g * jax.nn.sigmoid(SWIGLU_ALPHA * g) * (lin + 1.0)
        y_ref[0:m, :] = jnp.dot(act.astype(BF16), w2s_ref[...], preferred_element_type=F32) + b2_ref[...]

    quarter = MOE_TILE // 4
    for q in range(1, 5):
        @pl.when(jnp.logical_and(nv > (q - 1) * quarter, nv <= q * quarter))
        def _(q=q):
            expert_rows(q * quarter)


def _moe_experts(xbuf, geom, w1, b1, w2, b2, layer):
    n_blocks = xbuf.shape[0] // MOE_TILE
    bias = lambda w: pl.BlockSpec((None, None, 1, w), lambda i, br, be, *_: (layer, be[i], 0, 0))
    grid_spec = pltpu.PrefetchScalarGridSpec(
        num_scalar_prefetch=5,
        grid=(n_blocks,),
        in_specs=[pl.BlockSpec((MOE_TILE, D_PACK), lambda i, br, *_: (br[i], 0)),
                  pl.BlockSpec(memory_space=pl.ANY), bias(2 * D_FF),
                  pl.BlockSpec(memory_space=pl.ANY), bias(D_MODEL)],
        out_specs=pl.BlockSpec((MOE_TILE, D_MODEL), lambda i, br, *_: (br[i], 0)),
        scratch_shapes=[pltpu.VMEM((D_MODEL, 2 * D_FF), F32), pltpu.VMEM((D_FF, D_MODEL), F32),
                        pltpu.VMEM((D_MODEL, 2 * D_FF), BF16), pltpu.VMEM((D_FF, D_MODEL), BF16),
                        pltpu.SemaphoreType.DMA((2,))],
    )
    depth = w1.shape[0]
    return pl.pallas_call(
        functools.partial(_moe_kernel, layer=layer), grid_spec=grid_spec,
        out_shape=jax.ShapeDtypeStruct((xbuf.shape[0], D_MODEL), F32),
        compiler_params=_cparams("arbitrary"), name="moe_experts",
    )(*geom, xbuf, w1, b1.reshape(depth, N_EXPERTS, 1, 2 * D_FF), w2, b2.reshape(depth, N_EXPERTS, 1, D_MODEL))


def _final_kernel(cnt_ref, cnt_next_ref, base_ref, x_ref, p_ref, rcol_ref, wg_ref, bg_ref, wp_ref, g2_ref, b2_ref,
                  ybuf_ref, out_ref, ys_ref, carry_ref, nrows_ref, sems, *, tm, alpha):
    nrow = _sorted_rows(tm)
    i = pl.program_id(0)
    slot = i % 2

    def run_copy(buf, src, dst, s):
        return pltpu.make_async_copy(ybuf_ref.at[pl.ds(pl.multiple_of(src, SUBLANES), s)],
                                     ys_ref.at[buf, pl.ds(pl.multiple_of(dst, SUBLANES), s)], sems.at[buf])

    def gather_tile(counts_ref, buf):
        def start(e, dst):
            n8 = counts_ref[0, e]
            src = carry_ref[e]
            _for_each_piece(n8, lambda off, s: run_copy(buf, src + off, dst + off, s).start())
            carry_ref[e] = src + n8
            return dst + n8
        nrows_ref[buf] = lax.fori_loop(0, N_EXPERTS, start, 0)

    @pl.when(i == 0)
    def _():
        def init(e, c):
            carry_ref[e] = base_ref[e]
            return c
        lax.fori_loop(0, N_EXPERTS, init, 0)
        ys_ref[:, TOP_K * tm:, :] = jnp.zeros((2, nrow - TOP_K * tm, D_MODEL), F32)
        gather_tile(cnt_ref, 0)

    @pl.when(i + 1 < pl.num_programs(0))
    def _():
        gather_tile(cnt_next_ref, 1 - slot)

    x = x_ref[...]
    ple = (jax.nn.sigmoid(jnp.dot(x.astype(BF16), wg_ref[...], preferred_element_type=F32) + bg_ref[...])
           * jnp.dot(p_ref[...].astype(BF16), wp_ref[...], preferred_element_type=F32))

    _wait_tile_rows(nrows_ref[slot], tm, lambda s: run_copy(slot, 0, 0, s).wait())

    ys = ys_ref[slot].astype(BF16)
    col = lax.broadcasted_iota(I32, (1, ONE_HOT_BLOCK), 1).astype(F32).astype(BF16)
    rc = rcol_ref[...]
    gates = rc[:, TOP_K:].astype(BF16)
    blocks = []
    for c in range(nrow // ONE_HOT_BLOCK):
        blk = None
        for k in range(TOP_K):
            hit = col == (rc[:, k:k + 1] - float(c * ONE_HOT_BLOCK)).astype(BF16)
            term = jnp.where(hit, gates[:, k:k + 1], jnp.zeros((), BF16))
            blk = term if blk is None else blk + term
        blocks.append(blk)
    ffn = jnp.dot(jnp.concatenate(blocks, axis=1), ys, preferred_element_type=F32)
    out_ref[...] = _layer_norm(alpha * x + ffn + ple, g2_ref[...], b2_ref[...])


def _final(x1, p_all, layer, rcol, cnt8, base, ybuf, lw):
    n = x1.shape[0]
    nt = cnt8.shape[0]
    tm = n // nt
    row = lambda w: pl.BlockSpec((tm, w), lambda i: (i, 0))
    vec = lambda w: _layer_spec(layer, 1, w)
    return pl.pallas_call(
        functools.partial(_final_kernel, tm=tm, alpha=lw["alpha"]),
        grid=(nt,),
        in_specs=[pl.BlockSpec((None, 1, N_EXPERTS), lambda i: (i, 0, 0), memory_space=pltpu.SMEM),
                  pl.BlockSpec((None, 1, N_EXPERTS), lambda i: (jnp.minimum(i + 1, nt - 1), 0, 0),
                               memory_space=pltpu.SMEM),
                  pl.BlockSpec(memory_space=pltpu.SMEM),
                  row(D_MODEL), pl.BlockSpec((None, tm, D_PLE), lambda i: (layer, i, 0)), row(2 * TOP_K),
                  _layer_spec(layer, D_MODEL, D_MODEL), vec(D_MODEL), _layer_spec(layer, D_PLE, D_MODEL),
                  vec(D_MODEL), vec(D_MODEL), pl.BlockSpec(memory_space=pl.ANY)],
        out_specs=row(D_MODEL),
        out_shape=jax.ShapeDtypeStruct((n, D_MODEL), F32),
        scratch_shapes=[pltpu.VMEM((2, _sorted_rows(tm), D_MODEL), F32), pltpu.SMEM((N_EXPERTS,), I32),
                        pltpu.SMEM((2,), I32), pltpu.SemaphoreType.DMA((2,))],
        compiler_params=_cparams("arbitrary"), name="moe_combine_ln2",
    )(cnt8, cnt8, base, x1, p_all, rcol, lw["ple_w_gate"], lw["ple_b_gate"], lw["ple_w_proj"], lw["ln2_g"], lw["ln2_b"],
      ybuf)


def _block_diag(w):
    depth, nb, bw, _ = w.shape
    eye = jnp.eye(nb, dtype=w.dtype)
    return (w[:, :, :, None, :] * eye[None, :, None, :, None]).reshape(depth, nb * bw, nb * bw)


def _hi_lo_rows(w):
    hi = w.astype(BF16)
    lo = (w - hi.astype(F32)).astype(BF16)
    return jnp.concatenate([hi, lo], axis=-2)


def _block_geometry(tot, n_blocks):
    nblk = (tot + MOE_TILE - 1) // MOE_TILE
    blk_end = jnp.cumsum(nblk)
    blk_start = blk_end - nblk
    ids = jnp.arange(n_blocks, dtype=I32)
    n_valid = blk_end[-1]
    expert_at = lambda j: jnp.minimum(jnp.sum(j[:, None] >= blk_end[None, :], axis=1), N_EXPERTS - 1).astype(I32)
    src = jnp.clip(ids, 0, jnp.maximum(n_valid - 1, 0))
    e_of = expert_at(src)
    hot = e_of[:, None] == jnp.arange(N_EXPERTS)[None, :]
    pick = lambda v: jnp.sum(jnp.where(hot, v[None, :], 0), axis=1)
    start_of, end_of, tot_of = pick(blk_start), pick(blk_end), pick(tot)
    real = ids < n_valid
    nv = jnp.where(real, jnp.clip(tot_of - (ids - start_of) * MOE_TILE, 0, MOE_TILE), 0)
    first = jnp.logical_and(real, ids == start_of)
    nxt = jnp.where(end_of < n_valid, expert_at(end_of), -1)
    geom = tuple(v.astype(I32) for v in (src, e_of, nv, first, nxt))
    return geom, (blk_start * MOE_TILE).astype(I32)


def kernel(x_prompt, x_sample, cache_k, cache_v, state_conv, state_h, p_prompt, p_sample, ln_in_g, ln_in_b, rel_bias, w_in, conv_w, conv_b, rg_wa, rg_ba, rg_wx, rg_bx, rg_lambda, attn_sinks, gn_rnn, gn_attn, w_out, ln1_g, ln1_b, router_w, router_b, moe_w1, moe_b1, moe_w2, moe_b2, ple_w_gate, ple_b_gate, ple_w_proj, ln2_g, ln2_b):
    depth = w_in.shape[0]
    bsz, t_len, _ = x_prompt.shape
    n_p = bsz * t_len
    n_s = x_sample.shape[0]
    wb = cache_k.shape[2]
    alpha = (2 * depth) ** 0.25

    dist = jnp.arange(WINDOW)[:, None] + WINDOW - jnp.arange(2 * WINDOW)[None, :]
    valid = (dist >= 0) & (dist < WINDOW)
    bias_prompt = jnp.where(valid[None], _bias_lookup(rel_bias, dist), NEG)
    bias_sample = _bias_lookup(rel_bias, WINDOW - 1 - jnp.arange(WINDOW))

    tm_p = min(ROW_TILE, n_p)
    n_tiles = n_p // tm_p + 1
    max_rows = (n_p + n_s) * TOP_K + n_tiles * N_EXPERTS * (SUBLANES - 1)
    n_blocks = -(-(max_rows + N_EXPERTS * (MOE_TILE - 1)) // MOE_TILE)

    xp = x_prompt.reshape(n_p, D_MODEL)
    xs = x_sample.reshape(n_s, D_MODEL)
    pp = p_prompt.reshape(depth, n_p, D_PLE)
    ps = p_sample.reshape(depth, n_s, D_PLE)
    to_feature_major = lambda c: jnp.transpose(c, (0, 1, 3, 4, 2)).reshape(depth, n_s, N_KV * HEAD_DIM, wb)
    to_position_major = lambda c: jnp.transpose(c.reshape(depth, n_s, N_KV, HEAD_DIM, wb), (0, 1, 4, 2, 3))
    ck = to_feature_major(cache_k)
    cv = to_feature_major(cache_v)
    new_caches = None
    row2 = lambda v: v.reshape(1, -1)
    kv0 = 2 * D_RNN + D_ATTN
    outs = {k: [] for k in ("kp", "vp", "cp", "hp", "cs", "hs")}

    vecs = lambda v: v.reshape(depth, 1, -1)
    rw_all = {"conv_w": conv_w, "conv_b": vecs(conv_b), "wa": _block_diag(rg_wa).astype(BF16),
              "ba": vecs(rg_ba), "wx": _block_diag(rg_wx).astype(BF16), "bx": vecs(rg_bx),
              "lam": vecs(rg_lambda)}
    lw_all = {"gn_rnn": vecs(gn_rnn), "gn_attn": vecs(gn_attn), "w_out": w_out.astype(BF16),
              "ln1_g": vecs(ln1_g), "ln1_b": vecs(ln1_b),
              "router_wt": _hi_lo_rows(jnp.swapaxes(router_w, 1, 2)),
              "router_b": router_b.reshape(depth, N_EXPERTS, 1), "alpha": alpha,
              "ple_w_gate": ple_w_gate.astype(BF16), "ple_b_gate": vecs(ple_b_gate),
              "ple_w_proj": ple_w_proj.astype(BF16), "ln2_g": vecs(ln2_g), "ln2_b": vecs(ln2_b)}
    w_in_b = w_in.astype(BF16)

    for l in range(depth):
        rw = dict(rw_all, layer=l)
        lw = dict(lw_all, layer=l)
        first = l == 0

        xp, zp = _in_proj(xp, row2(ln_in_g), row2(ln_in_b), w_in_b, l, first)
        zp3 = zp.reshape(bsz, t_len, D_IN)
        yp, hp = _rnn_prompt(zp3, rw)
        op = _attn_prompt(zp, bsz, t_len, bias_prompt, attn_sinks, l)
        wp = min(WINDOW, t_len)
        outs["kp"].append(zp3[:, t_len - wp:, kv0:kv0 + 128].reshape(bsz, wp, N_KV, HEAD_DIM))
        outs["vp"].append(zp3[:, t_len - wp:, kv0 + 128:kv0 + 256].reshape(bsz, wp, N_KV, HEAD_DIM))
        outs["cp"].append(zp3[:, t_len - (CONV_W - 1):, :D_RNN])
        outs["hp"].append(hp.reshape(bsz, D_RNN))

        xs, zs = _in_proj(xs, row2(ln_in_g), row2(ln_in_b), w_in_b, l, first)
        ys, hs = _rnn_sample(zs, state_conv, state_h, rw)
        os_, new_caches = _attn_sample(zs[:, 2 * D_RNN:kv0], zs[:, kv0:kv0 + 128], zs[:, kv0 + 128:], ck, cv,
                                       new_caches, l, bias_sample, attn_sinks)
        outs["cs"].append(jnp.concatenate([state_conv[l][:, 1:], zs[:, None, :D_RNN]], axis=1))
        outs["hs"].append(hs)

        x1p, rrow_p, rcol_p, cnt_p = _out_proj(xp, yp.reshape(n_p, D_RNN), op, lw)
        x1s, rrow_s, rcol_s, cnt_s = _out_proj(xs, ys, os_, lw)

        tot_p = jnp.sum(cnt_p, axis=(0, 1))
        tot = tot_p + jnp.sum(cnt_s, axis=(0, 1))
        geom, base_p = _block_geometry(tot, n_blocks)
        base_s = base_p + tot_p

        xbuf = _dispatch(x1p, rrow_p, cnt_p, base_p, None, n_blocks * MOE_TILE)
        xbuf = _dispatch(x1s, rrow_s, cnt_s, base_s, xbuf, n_blocks * MOE_TILE)
        ybuf = _moe_experts(xbuf, geom, moe_w1, moe_b1, moe_w2, moe_b2, l)
        xp = _final(x1p, pp, l, rcol_p, cnt_p, base_p, ybuf, lw)
        xs = _final(x1s, ps, l, rcol_s, cnt_s, base_s, ybuf, lw)

    st = lambda k: jnp.stack(outs[k])
    return (xp.reshape(bsz, t_len, D_MODEL), xs.reshape(n_s, 1, D_MODEL),
            st("kp"), st("vp"), st("cp"), st("hp"), to_position_major(new_caches[0]),
            to_position_major(new_caches[1]), st("cs"), st("hs"))
```

```python
import functools
import math

import jax
import jax.numpy as jnp
from jax import lax
from jax.experimental import pallas as pl
from jax.experimental.pallas import tpu as pltpu

F32 = jnp.float32
BF16 = jnp.bfloat16
I32 = jnp.int32
U32 = jnp.uint32

D_MODEL = 1024
D_RNN = 512
RNN_BLOCKS = 8
CONV_W = 4
LRU_C = 8.0
N_HEADS = 8
HEAD_DIM = 64
N_KV = 2
GQA_R = N_HEADS // N_KV
D_ATTN = N_HEADS * HEAD_DIM
WINDOW = 128
N_BUCKETS = 32
MAX_DIST = 128
N_EXPERTS = 32
TOP_K = 4
D_FF = 1024
SWIGLU_LIMIT = 7.0
SWIGLU_ALPHA = 1.702
D_PLE = 256
LN_EPS = 1e-5
D_IN = 2 * D_RNN + D_ATTN + 2 * N_KV * HEAD_DIM
NEG = -1e30

SUBLANES = 8
LANES = 128
ROW_TILE = 512
MOE_TILE = 1024
D_PACK = D_MODEL // 2
VMEM_LIMIT = 56 * 1024 * 1024
HI_MASK = 0xFFFF0000


def _cparams(*sem):
    return pltpu.CompilerParams(dimension_semantics=sem, vmem_limit_bytes=VMEM_LIMIT)


def _layer_norm(x, g, b):
    mu = jnp.mean(x, axis=-1, keepdims=True)
    xc = x - mu
    var = jnp.mean(xc * xc, axis=-1, keepdims=True)
    return xc * lax.rsqrt(var + LN_EPS) * g + b


def _rms_norm(x, g):
    return x * lax.rsqrt(jnp.mean(x * x, axis=-1, keepdims=True) + LN_EPS) * g


def _const_spec(shape):
    return pl.BlockSpec(shape, lambda *_: (0,) * len(shape))


def _layer_spec(layer, *tail):
    return pl.BlockSpec((None,) + tail, lambda *_: (layer,) + (0,) * len(tail))


def _in_proj_kernel(x_ref, g_ref, b_ref, w_ref, *out_refs, apply_ln):
    x = x_ref[...]
    if apply_ln:
        x = _layer_norm(x, g_ref[...], b_ref[...])
        out_refs[0][...] = x
    out_refs[-1][...] = jnp.dot(x.astype(BF16), w_ref[...], preferred_element_type=F32)


def _in_proj(x, ln_g, ln_b, w_in_bf16, layer, apply_ln):
    n = x.shape[0]
    tm = min(ROW_TILE, n)
    row = lambda w: pl.BlockSpec((tm, w), lambda i: (i, 0))
    out_shape = [jax.ShapeDtypeStruct((n, D_IN), F32)]
    out_specs = [row(D_IN)]
    if apply_ln:
        out_shape.insert(0, jax.ShapeDtypeStruct((n, D_MODEL), F32))
        out_specs.insert(0, row(D_MODEL))
    outs = pl.pallas_call(
        functools.partial(_in_proj_kernel, apply_ln=apply_ln),
        grid=(n // tm,),
        in_specs=[row(D_MODEL), _const_spec((1, D_MODEL)), _const_spec((1, D_MODEL)),
                  _layer_spec(layer, D_MODEL, D_IN)],
        out_specs=out_specs, out_shape=out_shape,
        compiler_params=_cparams("parallel"), name="in_proj",
    )(x, ln_g, ln_b, w_in_bf16)
    return (outs[0], outs[1]) if apply_ln else (x, outs[0])


def _rglru_coeffs(u, wa_ref, ba_ref, wx_ref, bx_ref, lam_ref):
    ub = u.astype(BF16)
    r = jax.nn.sigmoid(jnp.dot(ub, wa_ref[...], preferred_element_type=F32) + ba_ref[...])
    i = jax.nn.sigmoid(jnp.dot(ub, wx_ref[...], preferred_element_type=F32) + bx_ref[...])
    lam = -lam_ref[...]
    softplus = jnp.maximum(lam, 0.0) + jnp.log1p(jnp.exp(-jnp.abs(lam)))
    log_a = -LRU_C * r * softplus
    a = jnp.exp(log_a)
    b = jnp.sqrt(-jnp.tanh(log_a) * (a * a + 1.0)) * (i * u)
    return a, b


def _rnn_prompt_kernel(xr_ref, gate_ref, cw_ref, cb_ref, wa_ref, ba_ref, wx_ref, bx_ref, lam_ref,
                       y_ref, hlast_ref, xs_ref, a_ref, b_ref, carry_ref, *, tt):
    t = pl.program_id(1)

    @pl.when(t == 0)
    def _():
        xs_ref[0:SUBLANES, :] = jnp.zeros((SUBLANES, D_RNN), F32)
        carry_ref[...] = jnp.zeros((SUBLANES, D_RNN), F32)

    x = xr_ref[...]
    xs_ref[SUBLANES:SUBLANES + tt, :] = x
    cw = cw_ref[...]
    u = cb_ref[...] + x * cw[3:4]
    for j in range(CONV_W - 1):
        off = SUBLANES - (CONV_W - 1) + j
        u = u + xs_ref[off:off + tt, :] * cw[j:j + 1]
    xs_ref[0:SUBLANES, :] = x[tt - SUBLANES:tt, :]

    a, b = _rglru_coeffs(u, wa_ref, ba_ref, wx_ref, bx_ref, lam_ref)
    ng = tt // SUBLANES
    a = a.reshape(ng, SUBLANES, D_RNN)
    b = b.reshape(ng, SUBLANES, D_RNN)
    row = lax.broadcasted_iota(I32, (1, SUBLANES, 1), 1)
    for s in (1, 2, 4):
        a_sh = pltpu.roll(a, s, 1)
        b_sh = pltpu.roll(b, s, 1)
        m = row >= s
        b = jnp.where(m, a * b_sh + b, b)
        a = jnp.where(m, a * a_sh, a)
    a_ref[...] = a
    b_ref[...] = b

    def body(g, carry):
        h = a_ref[g] * carry + b_ref[g]
        b_ref[g] = h
        return jnp.broadcast_to(h[SUBLANES - 1:SUBLANES, :], (SUBLANES, D_RNN))

    carry = lax.fori_loop(0, ng, body, carry_ref[...])
    carry_ref[...] = carry
    h = b_ref[...].reshape(tt, D_RNN)
    y_ref[...] = h * jax.nn.gelu(gate_ref[...])
    hlast_ref[...] = carry[0:1, :]


def _rnn_prompt(z3, rw):
    bsz, t_len, _ = z3.shape
    tt = min(ROW_TILE, t_len)
    vec = _layer_spec(rw["layer"], 1, D_RNN)
    mat = _layer_spec(rw["layer"], D_RNN, D_RNN)
    return pl.pallas_call(
        functools.partial(_rnn_prompt_kernel, tt=tt),
        grid=(bsz, t_len // tt),
        in_specs=[pl.BlockSpec((None, tt, D_RNN), lambda b, t: (b, t, 0)),
                  pl.BlockSpec((None, tt, D_RNN), lambda b, t: (b, t, 1)),
                  _layer_spec(rw["layer"], CONV_W, D_RNN), vec, mat, vec, mat, vec, vec],
        out_specs=[pl.BlockSpec((None, tt, D_RNN), lambda b, t: (b, t, 0)),
                   pl.BlockSpec((None, 1, D_RNN), lambda b, t: (b, 0, 0))],
        out_shape=[jax.ShapeDtypeStruct((bsz, t_len, D_RNN), F32),
                   jax.ShapeDtypeStruct((bsz, 1, D_RNN), F32)],
        scratch_shapes=[pltpu.VMEM((tt + SUBLANES, D_RNN), F32),
                        pltpu.VMEM((tt // SUBLANES, SUBLANES, D_RNN), F32),
                        pltpu.VMEM((tt // SUBLANES, SUBLANES, D_RNN), F32),
                        pltpu.VMEM((SUBLANES, D_RNN), F32)],
        compiler_params=_cparams("parallel", "arbitrary"), name="rnn_prompt",
    )(z3, z3, rw["conv_w"], rw["conv_b"], rw["wa"], rw["ba"], rw["wx"], rw["bx"], rw["lam"])


def _rnn_sample_kernel(xr_ref, gate_ref, c0_ref, c1_ref, c2_ref, h0_ref, cw_ref, cb_ref, wa_ref, ba_ref,
                       wx_ref, bx_ref, lam_ref, y_ref, h_ref):
    x = xr_ref[...]
    cw = cw_ref[...]
    u = (cb_ref[...] + c0_ref[...] * cw[0:1] + c1_ref[...] * cw[1:2] + c2_ref[...] * cw[2:3]
         + x * cw[3:4])
    a, b = _rglru_coeffs(u, wa_ref, ba_ref, wx_ref, bx_ref, lam_ref)
    h = a * h0_ref[...] + b
    h_ref[...] = h
    y_ref[...] = h * jax.nn.gelu(gate_ref[...])


def _rnn_sample(z, conv_state, h0, rw):
    n = z.shape[0]
    layer = rw["layer"]
    vec = _layer_spec(layer, 1, D_RNN)
    mat = _layer_spec(layer, D_RNN, D_RNN)
    full = _const_spec((n, D_RNN))
    depth = conv_state.shape[0]
    conv_flat = conv_state.reshape(depth, n, (CONV_W - 1) * D_RNN)
    tap = lambda j: pl.BlockSpec((None, n, D_RNN), lambda i: (layer, 0, j))
    return pl.pallas_call(
        _rnn_sample_kernel,
        grid=(1,),
        in_specs=[pl.BlockSpec((n, D_RNN), lambda i: (0, 0)), pl.BlockSpec((n, D_RNN), lambda i: (0, 1)),
                  tap(0), tap(1), tap(2), _layer_spec(layer, n, D_RNN),
                  _layer_spec(layer, CONV_W, D_RNN), vec, mat, vec, mat, vec, vec],
        out_specs=[full, full],
        out_shape=[jax.ShapeDtypeStruct((n, D_RNN), F32)] * 2,
        compiler_params=_cparams("arbitrary"), name="rnn_sample",
    )(z, z, conv_flat, conv_flat, conv_flat, h0, rw["conv_w"], rw["conv_b"], rw["wa"], rw["ba"], rw["wx"], rw["bx"], rw["lam"])


def _rel_bucket(dist):
    n = jnp.maximum(dist, 0)
    max_exact = N_BUCKETS // 2
    nf = jnp.maximum(n, max_exact).astype(F32)
    large = max_exact + (jnp.log(nf / max_exact) / math.log(MAX_DIST / max_exact)
                         * (N_BUCKETS - max_exact)).astype(I32)
    large = jnp.minimum(large, N_BUCKETS - 1)
    return jnp.where(n < max_exact, n, large)


def _bias_lookup(rel_bias, dist):
    hot = (_rel_bucket(dist)[..., None] == jnp.arange(N_BUCKETS)).astype(F32)
    out = jnp.tensordot(hot, rel_bias.astype(F32), axes=1, precision=lax.Precision.HIGHEST)
    return jnp.moveaxis(out, -1, 0)


def _half_lane_variants(x, group):
    lo = lax.broadcasted_iota(I32, (1, 2 * HEAD_DIM), 1) < HEAD_DIM
    xr = pltpu.roll(x, HEAD_DIM, 1)
    zero = jnp.zeros_like(x)
    if group == 0:
        return jnp.where(lo, x, zero), jnp.where(lo, zero, xr)
    return jnp.where(lo, xr, zero), jnp.where(lo, zero, x)


def _attn_prompt_kernel(q_ref, kc_ref, kp_ref, vc_ref, vp_ref, bias_ref, sink_ref, o_ref, *, nq, layer):
    first = pl.program_id(1) == 0
    col = lax.broadcasted_iota(I32, (1, 2 * WINDOW), 1)
    hide_prev = jnp.logical_and(first, col < WINDOW)
    for j in range(nq):
        rows = slice(j * WINDOW, (j + 1) * WINDOW)
        before = slice((j - 1) * WINDOW, j * WINDOW)
        kk = jnp.concatenate([kp_ref[...] if j == 0 else kc_ref[before, :], kc_ref[rows, :]], axis=0)
        vv = jnp.concatenate([vp_ref[...] if j == 0 else vc_ref[before, :], vc_ref[rows, :]], axis=0)
        for g in range(N_KV):
            k_lo, k_hi = _half_lane_variants(kk, g)
            v_lo, v_hi = _half_lane_variants(vv, g)
            for pair in range(GQA_R // 2):
                p0 = g * (GQA_R // 2) + pair
                qp = (q_ref[rows, p0 * 128:(p0 + 1) * 128] * (HEAD_DIM ** -0.5)).astype(BF16)
                acc = jnp.zeros((WINDOW, 2 * HEAD_DIM), F32)
                for kx, vx, h in ((k_lo, v_lo, 2 * p0), (k_hi, v_hi, 2 * p0 + 1)):
                    s = lax.dot_general(qp, kx.astype(BF16), (((1,), (1,)), ((), ())),
                                        preferred_element_type=F32) + bias_ref[h]
                    if j == 0:
                        s = jnp.where(hide_prev, NEG, s)
                    sink = sink_ref[layer, h]
                    m = jnp.maximum(jnp.max(s, axis=-1, keepdims=True), sink)
                    p = jnp.exp(s - m)
                    denom = jnp.sum(p, axis=-1, keepdims=True) + jnp.exp(sink - m)
                    acc = acc + jnp.dot(p.astype(BF16), vx.astype(BF16), preferred_element_type=F32) / denom
                o_ref[rows, p0 * 128:(p0 + 1) * 128] = acc


def _attn_prompt(z, bsz, t_len, bias_prompt, sinks, layer):
    nb = t_len // WINDOW
    nq = min(4, nb)
    ns = nb // nq
    kcol = (2 * D_RNN + D_ATTN) // 128
    cur = lambda c: pl.BlockSpec((nq * WINDOW, 128), lambda b, i: (b * ns + i, c))
    prev = lambda c: pl.BlockSpec((WINDOW, 128), lambda b, i: (b * nb + jnp.maximum(i * nq - 1, 0), c))
    return pl.pallas_call(
        functools.partial(_attn_prompt_kernel, nq=nq, layer=layer),
        grid=(bsz, ns),
        in_specs=[pl.BlockSpec((nq * WINDOW, D_ATTN), lambda b, i: (b * ns + i, 2 * D_RNN // D_ATTN)),
                  cur(kcol), prev(kcol), cur(kcol + 1), prev(kcol + 1),
                  _const_spec((N_HEADS, WINDOW, 2 * WINDOW)),
                  pl.BlockSpec(memory_space=pltpu.SMEM)],
        out_specs=pl.BlockSpec((nq * WINDOW, D_ATTN), lambda b, i: (b * ns + i, 0)),
        out_shape=jax.ShapeDtypeStruct((bsz * t_len, D_ATTN), F32),
        compiler_params=_cparams("parallel", "arbitrary"), name="attn_prompt",
    )(z, z, z, z, z, bias_prompt, sinks)


def _attn_sample_kernel(q_ref, kn_ref, vn_ref, ck_ref, cv_ref, bias_ref, sink_ref, hmask_ref, *rest):
    o_ref, nk_ref, nv_ref = rest[-3:]
    pos = lax.broadcasted_iota(I32, (1, 1, WINDOW), 2)
    last = pos == WINDOW - 1
    bb = ck_ref.shape[0]
    newk = jnp.where(last, jnp.broadcast_to(kn_ref[...], (bb, 128, WINDOW)), pltpu.roll(ck_ref[...], WINDOW - 1, 2))
    newv = jnp.where(last, jnp.broadcast_to(vn_ref[...], (bb, 128, WINDOW)), pltpu.roll(cv_ref[...], WINDOW - 1, 2))
    nk_ref[...] = newk
    nv_ref[...] = newv
    s = lax.dot_general(q_ref[...].astype(BF16), newk.astype(BF16), (((2,), (1,)), ((0,), (0,))),
                        preferred_element_type=F32)
    s = s + bias_ref[...][None]
    sink = sink_ref[...][None]
    m = jnp.maximum(jnp.max(s, axis=-1, keepdims=True), sink)
    p = jnp.exp(s - m)
    denom = jnp.sum(p, axis=-1, keepdims=True) + jnp.exp(sink - m)
    o = lax.dot_general(p.astype(BF16), newv.astype(BF16), (((2,), (2,)), ((0,), (0,))),
                        preferred_element_type=F32)
    o_ref[...] = o / denom * hmask_ref[...][None]


def _attn_sample(q, k_new, v_new, k_cache, v_cache, new_caches, layer, bias_sample, sinks):
    n = q.shape[0]
    depth = k_cache.shape[0]
    bb = 16
    qh = q.reshape(n, N_KV, GQA_R, HEAD_DIM) * (HEAD_DIM ** -0.5)
    eye = jnp.eye(N_KV, dtype=F32)
    q8 = (qh[:, :, :, None, :] * eye[None, :, None, :, None]).reshape(n, N_HEADS, N_KV * HEAD_DIM)
    hmask = jnp.repeat(jnp.repeat(eye, GQA_R, axis=0), HEAD_DIM, axis=1)
    blk3 = lambda a, b: pl.BlockSpec((bb, a, b), lambda i: (i, 0, 0))
    cache = pl.BlockSpec((None, bb, 128, WINDOW), lambda i: (layer, i, 0, 0))
    in_specs = [blk3(N_HEADS, 128), blk3(128, 1), blk3(128, 1), cache, cache,
                _const_spec((N_HEADS, WINDOW)), _layer_spec(layer, N_HEADS, 1), _const_spec((N_HEADS, 128))]
    args = [q8, k_new.reshape(n, 128, 1), v_new.reshape(n, 128, 1), k_cache, v_cache, bias_sample,
            sinks.reshape(-1, N_HEADS, 1), hmask]
    aliases = {}
    if new_caches is not None:
        in_specs += [pl.BlockSpec(memory_space=pl.ANY)] * 2
        args += list(new_caches)
        aliases = {len(args) - 2: 1, len(args) - 1: 2}
    o8, nk, nv = pl.pallas_call(
        _attn_sample_kernel,
        grid=(n // bb,),
        in_specs=in_specs,
        out_specs=[blk3(N_HEADS, 128), cache, cache],
        out_shape=[jax.ShapeDtypeStruct((n, N_HEADS, 128), F32),
                   jax.ShapeDtypeStruct((depth, n, 128, WINDOW), F32),
                   jax.ShapeDtypeStruct((depth, n, 128, WINDOW), F32)],
        input_output_aliases=aliases,
        compiler_params=_cparams("arbitrary"), name="attn_sample",
    )(*args)
    o = o8.reshape(n, N_KV, GQA_R, N_KV, HEAD_DIM)
    o = jnp.stack([o[:, g, :, g, :] for g in range(N_KV)], axis=1).reshape(n, D_ATTN)
    return o, (nk, nv)


ONE_HOT_BLOCK = 256


def _sorted_rows(tm):
    return TOP_K * tm + N_EXPERTS * SUBLANES


def _out_proj_kernel(x_ref, yr_ref, o_ref, gnr_ref, gna_ref, wr_ref, wa_ref, g1_ref, b1_ref, rwt_ref, rb_ref,
                     x1_ref, rrow_ref, rcol_ref, cnt_ref, *, tm, alpha):
    yn = _rms_norm(yr_ref[...], gnr_ref[...]).astype(BF16)
    on = _rms_norm(o_ref[...], gna_ref[...]).astype(BF16)
    mix = (jnp.dot(yn, wr_ref[...], preferred_element_type=F32)
           + jnp.dot(on, wa_ref[...], preferred_element_type=F32))
    x1 = _layer_norm(alpha * x_ref[...] + mix, g1_ref[...], b1_ref[...])
    x1_ref[...] = x1

    x1h = x1.astype(BF16)
    x1l = (x1 - x1h.astype(F32)).astype(BF16)
    nt_dims = (((1,), (1,)), ((), ()))
    by_hi = lax.dot_general(rwt_ref[...], x1h, nt_dims, preferred_element_type=F32)
    by_lo = lax.dot_general(rwt_ref[0:N_EXPERTS, :], x1l, nt_dims, preferred_element_type=F32)
    logits = by_hi[0:N_EXPERTS] + by_hi[N_EXPERTS:] + by_lo + rb_ref[...]
    eidx = lax.broadcasted_iota(I32, (N_EXPERTS, tm), 0).astype(F32)
    work = logits
    vals, hots = [], []
    for _ in range(TOP_K):
        v = jnp.max(work, axis=0, keepdims=True)
        idx = jnp.min(jnp.where(work == v, eidx, float(N_EXPERTS)), axis=0, keepdims=True)
        hot = eidx == idx
        work = jnp.where(hot, -jnp.inf, work)
        vals.append(v)
        hots.append(hot.astype(F32))
    ex = [jnp.exp(v - vals[0]) for v in vals]
    tot = ex[0] + ex[1] + ex[2] + ex[3]
    tok_hot = hots[0] + hots[1] + hots[2] + hots[3]
    s_i = lax.broadcasted_iota(I32, (tm, tm), 0)
    t_i = lax.broadcasted_iota(I32, (tm, tm), 1)
    earlier = (s_i < t_i).astype(BF16)
    before = jnp.dot(tok_hot.astype(BF16), earlier, preferred_element_type=F32)
    cnt = jnp.sum(tok_hot, axis=1, keepdims=True)
    cnt8 = jnp.floor((cnt + (SUBLANES - 1)) * (1.0 / SUBLANES)) * SUBLANES
    cnt8_b = jnp.broadcast_to(cnt8, (N_EXPERTS, LANES))
    e_r = lax.broadcasted_iota(I32, (N_EXPERTS, N_EXPERTS), 0)
    e_c = lax.broadcasted_iota(I32, (N_EXPERTS, N_EXPERTS), 1)
    start = jnp.dot((e_c < e_r).astype(F32), cnt8_b, preferred_element_type=F32,
                    precision=lax.Precision.HIGHEST)[:, 0:1]
    slot = start + before
    rows = [jnp.sum(h * slot, axis=0, keepdims=True) for h in hots] + [e / tot for e in ex]
    rrow = jnp.concatenate(rows, axis=0)
    rrow_ref[...] = rrow
    rcol_ref[...] = rrow.T
    cnt_ref[...] = cnt8_b.T[0:1, 0:N_EXPERTS].astype(I32)


def _out_proj(x, y_rnn, o, lw):
    n = x.shape[0]
    tm = min(ROW_TILE, n)
    nt = n // tm
    row = lambda w: pl.BlockSpec((tm, w), lambda i: (i, 0))
    layer = lw["layer"]
    vec = lambda w: _layer_spec(layer, 1, w)
    w_out_half = lambda j: pl.BlockSpec((None, D_MODEL // 2, D_MODEL), lambda i: (layer, j, 0))
    return pl.pallas_call(
        functools.partial(_out_proj_kernel, tm=tm, alpha=lw["alpha"]),
        grid=(nt,),
        in_specs=[row(D_MODEL), row(D_RNN), row(D_ATTN), vec(D_RNN), vec(D_ATTN),
                  w_out_half(0), w_out_half(1), vec(D_MODEL), vec(D_MODEL),
                  _layer_spec(layer, 2 * N_EXPERTS, D_MODEL), _layer_spec(layer, N_EXPERTS, 1)],
        out_specs=[row(D_MODEL), pl.BlockSpec((None, 2 * TOP_K, tm), lambda i: (i, 0, 0)), row(2 * TOP_K),
                   pl.BlockSpec((None, 1, N_EXPERTS), lambda i: (i, 0, 0))],
        out_shape=[jax.ShapeDtypeStruct((n, D_MODEL), F32), jax.ShapeDtypeStruct((nt, 2 * TOP_K, tm), F32),
                   jax.ShapeDtypeStruct((n, 2 * TOP_K), F32), jax.ShapeDtypeStruct((nt, 1, N_EXPERTS), I32)],
        compiler_params=_cparams("parallel"), name="out_proj_route",
    )(x, y_rnn, o, lw["gn_rnn"], lw["gn_attn"], lw["w_out"], lw["w_out"], lw["ln1_g"], lw["ln1_b"],
      lw["router_wt"], lw["router_b"])


RUN_CHUNK = 64
_SMALL_PIECES = (32, 16, 8)
_MAX_PAD_ROWS = N_EXPERTS * SUBLANES
_PAD_PIECES = (128, 64, 32, 16, 8)


def _for_each_piece(n8, fn):
    def chunk(j, c):
        fn(pl.multiple_of(j * RUN_CHUNK, RUN_CHUNK), RUN_CHUNK)
        return c
    lax.fori_loop(0, n8 // RUN_CHUNK, chunk, 0)
    for s in _SMALL_PIECES:
        @pl.when((n8 & s) != 0)
        def _(s=s):
            fn(n8 & (-2 * s), s)


def _wait_tile_rows(n_rows, tm, wait_rows):
    wait_rows(TOP_K * tm)
    pad = n_rows - TOP_K * tm
    for s in _PAD_PIECES:
        @pl.when((pad & s) != 0)
        def _(s=s):
            wait_rows(s)


def _dispatch_kernel(cnt_ref, base_ref, x_ref, rrow_ref, *rest, tm):
    xbuf_ref, xs_ref, carry_ref, nrows_ref, sems = rest[-5:]
    nrow = _sorted_rows(tm)
    i = pl.program_id(0)
    slot = i % 2

    @pl.when(i == 0)
    def _():
        def init(e, c):
            carry_ref[e] = base_ref[e]
            return c
        lax.fori_loop(0, N_EXPERTS, init, 0)

    r = lax.broadcasted_iota(I32, (ONE_HOT_BLOCK, 1), 0).astype(F32).astype(BF16)
    blocks = []
    for c in range(nrow // ONE_HOT_BLOCK):
        hit = None
        for k in range(TOP_K):
            h = r == (rrow_ref[k:k + 1, :] - float(c * ONE_HOT_BLOCK)).astype(BF16)
            hit = h if hit is None else jnp.logical_or(hit, h)
        blocks.append(jnp.where(hit, jnp.ones((), BF16), jnp.zeros((), BF16)))
    perm = jnp.concatenate(blocks, axis=0)
    xs = jnp.dot(perm, x_ref[...].astype(BF16), preferred_element_type=F32)
    lo = lax.bitcast_convert_type(xs[:, :D_PACK], U32)
    hi = lax.bitcast_convert_type(xs[:, D_PACK:], U32)
    xs_ref[slot] = (hi & U32(HI_MASK)) | (lo >> 16)

    def run_copy(buf, src, dst, s):
        return pltpu.make_async_copy(xs_ref.at[buf, pl.ds(pl.multiple_of(src, SUBLANES), s)],
                                     xbuf_ref.at[pl.ds(pl.multiple_of(dst, SUBLANES), s)], sems.at[buf])

    def start(e, src):
        n8 = cnt_ref[0, e]
        dst = carry_ref[e]
        _for_each_piece(n8, lambda off, s: run_copy(slot, src + off, dst + off, s).start())
        carry_ref[e] = dst + n8
        return src + n8

    def wait_tile(buf):
        _wait_tile_rows(nrows_ref[buf], tm, lambda s: run_copy(buf, 0, 0, s).wait())

    nrows_ref[slot] = lax.fori_loop(0, N_EXPERTS, start, 0)

    @pl.when(i > 0)
    def _():
        wait_tile(1 - slot)

    @pl.when(i == pl.num_programs(0) - 1)
    def _():
        wait_tile(slot)


def _dispatch(x1, rrow, cnt8, base, xbuf, n_rows):
    n = x1.shape[0]
    nt, _, tm = rrow.shape
    in_specs = [pl.BlockSpec((None, 1, N_EXPERTS), lambda i: (i, 0, 0), memory_space=pltpu.SMEM),
                pl.BlockSpec(memory_space=pltpu.SMEM),
                pl.BlockSpec((tm, D_MODEL), lambda i: (i, 0)),
                pl.BlockSpec((None, 2 * TOP_K, tm), lambda i: (i, 0, 0))]
    args = [cnt8, base, x1, rrow]
    aliases = {}
    if xbuf is not None:
        in_specs.append(pl.BlockSpec(memory_space=pl.ANY))
        args.append(xbuf)
        aliases = {len(args) - 1: 0}
    return pl.pallas_call(
        functools.partial(_dispatch_kernel, tm=tm),
        grid=(nt,),
        in_specs=in_specs,
        out_specs=pl.BlockSpec(memory_space=pl.ANY),
        out_shape=jax.ShapeDtypeStruct((n_rows, D_PACK), U32),
        scratch_shapes=[pltpu.VMEM((2, _sorted_rows(tm), D_PACK), U32), pltpu.SMEM((N_EXPERTS,), I32),
                        pltpu.SMEM((2,), I32), pltpu.SemaphoreType.DMA((2,))],
        input_output_aliases=aliases,
        compiler_params=_cparams("arbitrary"), name="moe_dispatch",
    )(*args)


def _moe_kernel(br_ref, be_ref, nv_ref, first_ref, next_ref, x_ref, w1_hbm, b1_ref, w2_hbm, b2_ref, y_ref,
                w1f_ref, w2f_ref, w1s_ref, w2s_ref, sems, *, layer):
    i = pl.program_id(0)

    def load(e):
        return (pltpu.make_async_copy(w1_hbm.at[layer, e], w1f_ref, sems.at[0]),
                pltpu.make_async_copy(w2_hbm.at[layer, e], w2f_ref, sems.at[1]))

    @pl.when(jnp.logical_and(i == 0, first_ref[0] == 1))
    def _():
        for c in load(be_ref[0]):
            c.start()

    @pl.when(first_ref[i] == 1)
    def _():
        for c in load(be_ref[i]):
            c.wait()
        w1s_ref[...] = w1f_ref[...].astype(BF16)
        w2s_ref[...] = w2f_ref[...].astype(BF16)

        @pl.when(next_ref[i] >= 0)
        def _():
            for c in load(next_ref[i]):
                c.start()

    nv = nv_ref[i]

    def expert_rows(m):
        rows = lax.broadcasted_iota(I32, (m, 1), 0)
        xw = jnp.where(rows < nv, x_ref[0:m, :], U32(0))
        lo = lax.bitcast_convert_type(xw << 16, F32).astype(BF16)
        hi = lax.bitcast_convert_type(xw & U32(HI_MASK), F32).astype(BF16)
        xb = jnp.concatenate([lo, hi], axis=1)
        hdn = jnp.dot(xb, w1s_ref[...], preferred_element_type=F32) + b1_ref[...]
        g = jnp.minimum(hdn[:, :D_FF], SWIGLU_LIMIT)
        lin = jnp.clip(hdn[:, D_FF:], -SWIGLU_LIMIT, SWIGLU_LIMIT)
        act = g * jax.nn.sigmoid(SWIGLU_ALPHA * g) * (lin + 1.0)
        y_ref[0:m, :] = jnp.dot(act.astype(BF16), w2s_ref[...], preferred_element_type=F32) + b2_ref[...]

    quarter = MOE_TILE // 4
    for q in range(1, 5):
        @pl.when(jnp.logical_and(nv > (q - 1) * quarter, nv <= q * quarter))
        def _(q=q):
            expert_rows(q * quarter)


def _moe_experts(xbuf, geom, w1, b1, w2, b2, layer):
    n_blocks = xbuf.shape[0] // MOE_TILE
    bias = lambda w: pl.BlockSpec((None, None, 1, w), lambda i, br, be, *_: (layer, be[i], 0, 0))
    grid_spec = pltpu.PrefetchScalarGridSpec(
        num_scalar_prefetch=5,
        grid=(n_blocks,),
        in_specs=[pl.BlockSpec((MOE_TILE, D_PACK), lambda i, br, *_: (br[i], 0)),
                  pl.BlockSpec(memory_space=pl.ANY), bias(2 * D_FF),
                  pl.BlockSpec(memory_space=pl.ANY), bias(D_MODEL)],
        out_specs=pl.BlockSpec((MOE_TILE, D_MODEL), lambda i, br, *_: (br[i], 0)),
        scratch_shapes=[pltpu.VMEM((D_MODEL, 2 * D_FF), F32), pltpu.VMEM((D_FF, D_MODEL), F32),
                        pltpu.VMEM((D_MODEL, 2 * D_FF), BF16), pltpu.VMEM((D_FF, D_MODEL), BF16),
                        pltpu.SemaphoreType.DMA((2,))],
    )
    depth = w1.shape[0]
    return pl.pallas_call(
        functools.partial(_moe_kernel, layer=layer), grid_spec=grid_spec,
        out_shape=jax.ShapeDtypeStruct((xbuf.shape[0], D_MODEL), F32),
        compiler_params=_cparams("arbitrary"), name="moe_experts",
    )(*geom, xbuf, w1, b1.reshape(depth, N_EXPERTS, 1, 2 * D_FF), w2, b2.reshape(depth, N_EXPERTS, 1, D_MODEL))


def _final_kernel(cnt_ref, cnt_next_ref, base_ref, x_ref, p_ref, rcol_ref, wg_ref, bg_ref, wp_ref, g2_ref, b2_ref,
                  ybuf_ref, out_ref, ys_ref, carry_ref, nrows_ref, sems, *, tm, alpha):
    nrow = _sorted_rows(tm)
    i = pl.program_id(0)
    slot = i % 2

    def run_copy(buf, src, dst, s):
        return pltpu.make_async_copy(ybuf_ref.at[pl.ds(pl.multiple_of(src, SUBLANES), s)],
                                     ys_ref.at[buf, pl.ds(pl.multiple_of(dst, SUBLANES), s)], sems.at[buf])

    def gather_tile(counts_ref, buf):
        def start(e, dst):
            n8 = counts_ref[0, e]
            src = carry_ref[e]
            _for_each_piece(n8, lambda off, s: run_copy(buf, src + off, dst + off, s).start())
            carry_ref[e] = src + n8
            return dst + n8
        nrows_ref[buf] = lax.fori_loop(0, N_EXPERTS, start, 0)

    @pl.when(i == 0)
    def _():
        def init(e, c):
            carry_ref[e] = base_ref[e]
            return c
        lax.fori_loop(0, N_EXPERTS, init, 0)
        ys_ref[:, TOP_K * tm:, :] = jnp.zeros((2, nrow - TOP_K * tm, D_MODEL), F32)
        gather_tile(cnt_ref, 0)

    @pl.when(i + 1 < pl.num_programs(0))
    def _():
        gather_tile(cnt_next_ref, 1 - slot)

    x = x_ref[...]
    ple = (jax.nn.sigmoid(jnp.dot(x.astype(BF16), wg_ref[...], preferred_element_type=F32) + bg_ref[...])
           * jnp.dot(p_ref[...].astype(BF16), wp_ref[...], preferred_element_type=F32))

    _wait_tile_rows(nrows_ref[slot], tm, lambda s: run_copy(slot, 0, 0, s).wait())

    ys = ys_ref[slot].astype(BF16)
    col = lax.broadcasted_iota(I32, (1, ONE_HOT_BLOCK), 1).astype(F32).astype(BF16)
    rc = rcol_ref[...]
    gates = rc[:, TOP_K:].astype(BF16)
    blocks = []
    for c in range(nrow // ONE_HOT_BLOCK):
        blk = None
        for k in range(TOP_K):
            hit = col == (rc[:, k:k + 1] - float(c * ONE_HOT_BLOCK)).astype(BF16)
            term = jnp.where(hit, gates[:, k:k + 1], jnp.zeros((), BF16))
            blk = term if blk is None else blk + term
        blocks.append(blk)
    ffn = jnp.dot(jnp.concatenate(blocks, axis=1), ys, preferred_element_type=F32)
    out_ref[...] = _layer_norm(alpha * x + ffn + ple, g2_ref[...], b2_ref[...])


def _final(x1, p_all, layer, rcol, cnt8, base, ybuf, lw):
    n = x1.shape[0]
    nt = cnt8.shape[0]
    tm = n // nt
    row = lambda w: pl.BlockSpec((tm, w), lambda i: (i, 0))
    vec = lambda w: _layer_spec(layer, 1, w)
    return pl.pallas_call(
        functools.partial(_final_kernel, tm=tm, alpha=lw["alpha"]),
        grid=(nt,),
        in_specs=[pl.BlockSpec((None, 1, N_EXPERTS), lambda i: (i, 0, 0), memory_space=pltpu.SMEM),
                  pl.BlockSpec((None, 1, N_EXPERTS), lambda i: (jnp.minimum(i + 1, nt - 1), 0, 0),
                               memory_space=pltpu.SMEM),
                  pl.BlockSpec(memory_space=pltpu.SMEM),
                  row(D_MODEL), pl.BlockSpec((None, tm, D_PLE), lambda i: (layer, i, 0)), row(2 * TOP_K),
                  _layer_spec(layer, D_MODEL, D_MODEL), vec(D_MODEL), _layer_spec(layer, D_PLE, D_MODEL),
                  vec(D_MODEL), vec(D_MODEL), pl.BlockSpec(memory_space=pl.ANY)],
        out_specs=row(D_MODEL),
        out_shape=jax.ShapeDtypeStruct((n, D_MODEL), F32),
        scratch_shapes=[pltpu.VMEM((2, _sorted_rows(tm), D_MODEL), F32), pltpu.SMEM((N_EXPERTS,), I32),
                        pltpu.SMEM((2,), I32), pltpu.SemaphoreType.DMA((2,))],
        compiler_params=_cparams("arbitrary"), name="moe_combine_ln2",
    )(cnt8, cnt8, base, x1, p_all, rcol, lw["ple_w_gate"], lw["ple_b_gate"], lw["ple_w_proj"], lw["ln2_g"], lw["ln2_b"],
      ybuf)


def _block_diag(w):
    depth, nb, bw, _ = w.shape
    eye = jnp.eye(nb, dtype=w.dtype)
    return (w[:, :, :, None, :] * eye[None, :, None, :, None]).reshape(depth, nb * bw, nb * bw)


def _hi_lo_rows(w):
    hi = w.astype(BF16)
    lo = (w - hi.astype(F32)).astype(BF16)
    return jnp.concatenate([hi, lo], axis=-2)


def _block_geometry(tot, n_blocks):
    nblk = (tot + MOE_TILE - 1) // MOE_TILE
    blk_end = jnp.cumsum(nblk)
    blk_start = blk_end - nblk
    ids = jnp.arange(n_blocks, dtype=I32)
    n_valid = blk_end[-1]
    expert_at = lambda j: jnp.minimum(jnp.sum(j[:, None] >= blk_end[None, :], axis=1), N_EXPERTS - 1).astype(I32)
    src = jnp.clip(ids, 0, jnp.maximum(n_valid - 1, 0))
    e_of = expert_at(src)
    hot = e_of[:, None] == jnp.arange(N_EXPERTS)[None, :]
    pick = lambda v: jnp.sum(jnp.where(hot, v[None, :], 0), axis=1)
    start_of, end_of, tot_of = pick(blk_start), pick(blk_end), pick(tot)
    real = ids < n_valid
    nv = jnp.where(real, jnp.clip(tot_of - (ids - start_of) * MOE_TILE, 0, MOE_TILE), 0)
    first = jnp.logical_and(real, ids == start_of)
    nxt = jnp.where(end_of < n_valid, expert_at(end_of), -1)
    geom = tuple(v.astype(I32) for v in (src, e_of, nv, first, nxt))
    return geom, (blk_start * MOE_TILE).astype(I32)


def kernel(x_prompt, x_sample, cache_k, cache_v, state_conv, state_h, p_prompt, p_sample, ln_in_g, ln_in_b, rel_bias, w_in, conv_w, conv_b, rg_wa, rg_ba, rg_wx, rg_bx, rg_lambda, attn_sinks, gn_rnn, gn_attn, w_out, ln1_g, ln1_b, router_w, router_b, moe_w1, moe_b1, moe_w2, moe_b2, ple_w_gate, ple_b_gate, ple_w_proj, ln2_g, ln2_b):
    depth = w_in.shape[0]
    bsz, t_len, _ = x_prompt.shape
    n_p = bsz * t_len
    n_s = x_sample.shape[0]
    wb = cache_k.shape[2]
    alpha = (2 * depth) ** 0.25

    dist = jnp.arange(WINDOW)[:, None] + WINDOW - jnp.arange(2 * WINDOW)[None, :]
    valid = (dist >= 0) & (dist < WINDOW)
    bias_prompt = jnp.where(valid[None], _bias_lookup(rel_bias, dist), NEG)
    bias_sample = _bias_lookup(rel_bias, WINDOW - 1 - jnp.arange(WINDOW))

    tm_p = min(ROW_TILE, n_p)
    n_tiles = n_p // tm_p + 1
    max_rows = (n_p + n_s) * TOP_K + n_tiles * N_EXPERTS * (SUBLANES - 1)
    n_blocks = -(-(max_rows + N_EXPERTS * (MOE_TILE - 1)) // MOE_TILE)

    xp = x_prompt.reshape(n_p, D_MODEL)
    xs = x_sample.reshape(n_s, D_MODEL)
    pp = p_prompt.reshape(depth, n_p, D_PLE)
    ps = p_sample.reshape(depth, n_s, D_PLE)
    to_feature_major = lambda c: jnp.transpose(c, (0, 1, 3, 4, 2)).reshape(depth, n_s, N_KV * HEAD_DIM, wb)
    to_position_major = lambda c: jnp.transpose(c.reshape(depth, n_s, N_KV, HEAD_DIM, wb), (0, 1, 4, 2, 3))
    ck = to_feature_major(cache_k)
    cv = to_feature_major(cache_v)
    new_caches = None
    row2 = lambda v: v.reshape(1, -1)
    kv0 = 2 * D_RNN + D_ATTN
    outs = {k: [] for k in ("kp", "vp", "cp", "hp", "cs", "hs")}

    vecs = lambda v: v.reshape(depth, 1, -1)
    rw_all = {"conv_w": conv_w, "conv_b": vecs(conv_b), "wa": _block_diag(rg_wa).astype(BF16),
              "ba": vecs(rg_ba), "wx": _block_diag(rg_wx).astype(BF16), "bx": vecs(rg_bx),
              "lam": vecs(rg_lambda)}
    lw_all = {"gn_rnn": vecs(gn_rnn), "gn_attn": vecs(gn_attn), "w_out": w_out.astype(BF16),
              "ln1_g": vecs(ln1_g), "ln1_b": vecs(ln1_b),
              "router_wt": _hi_lo_rows(jnp.swapaxes(router_w, 1, 2)),
              "router_b": router_b.reshape(depth, N_EXPERTS, 1), "alpha": alpha,
              "ple_w_gate": ple_w_gate.astype(BF16), "ple_b_gate": vecs(ple_b_gate),
              "ple_w_proj": ple_w_proj.astype(BF16), "ln2_g": vecs(ln2_g), "ln2_b": vecs(ln2_b)}
    w_in_b = w_in.astype(BF16)

    for l in range(depth):
        rw = dict(rw_all, layer=l)
        lw = dict(lw_all, layer=l)
        first = l == 0

        xp, zp = _in_proj(xp, row2(ln_in_g), row2(ln_in_b), w_in_b, l, first)
        zp3 = zp.reshape(bsz, t_len, D_IN)
        yp, hp = _rnn_prompt(zp3, rw)
        op = _attn_prompt(zp, bsz, t_len, bias_prompt, attn_sinks, l)
        wp = min(WINDOW, t_len)
        outs["kp"].append(zp3[:, t_len - wp:, kv0:kv0 + 128].reshape(bsz, wp, N_KV, HEAD_DIM))
        outs["vp"].append(zp3[:, t_len - wp:, kv0 + 128:kv0 + 256].reshape(bsz, wp, N_KV, HEAD_DIM))
        outs["cp"].append(zp3[:, t_len - (CONV_W - 1):, :D_RNN])
        outs["hp"].append(hp.reshape(bsz, D_RNN))

        xs, zs = _in_proj(xs, row2(ln_in_g), row2(ln_in_b), w_in_b, l, first)
        ys, hs = _rnn_sample(zs, state_conv, state_h, rw)
        os_, new_caches = _attn_sample(zs[:, 2 * D_RNN:kv0], zs[:, kv0:kv0 + 128], zs[:, kv0 + 128:], ck, cv,
                                       new_caches, l, bias_sample, attn_sinks)
        outs["cs"].append(jnp.concatenate([state_conv[l][:, 1:], zs[:, None, :D_RNN]], axis=1))
        outs["hs"].append(hs)

        x1p, rrow_p, rcol_p, cnt_p = _out_proj(xp, yp.reshape(n_p, D_RNN), op, lw)
        x1s, rrow_s, rcol_s, cnt_s = _out_proj(xs, ys, os_, lw)

        tot_p = jnp.sum(cnt_p, axis=(0, 1))
        tot = tot_p + jnp.sum(cnt_s, axis=(0, 1))
        geom, base_p = _block_geometry(tot, n_blocks)
        base_s = base_p + tot_p

        xbuf = _dispatch(x1p, rrow_p, cnt_p, base_p, None, n_blocks * MOE_TILE)
        xbuf = _dispatch(x1s, rrow_s, cnt_s, base_s, xbuf, n_blocks * MOE_TILE)
        ybuf = _moe_experts(xbuf, geom, moe_w1, moe_b1, moe_w2, moe_b2, l)
        xp = _final(x1p, pp, l, rcol_p, cnt_p, base_p, ybuf, lw)
        xs = _final(x1s, ps, l, rcol_s, cnt_s, base_s, ybuf, lw)

    st = lambda k: jnp.stack(outs[k])
    return (xp.reshape(bsz, t_len, D_MODEL), xs.reshape(n_s, 1, D_MODEL),
            st("kp"), st("vp"), st("cp"), st("hp"), to_position_major(new_caches[0]),
            to_position_major(new_caches[1]), st("cs"), st("hs"))
```

```python
import functools
import math

import jax
import jax.numpy as jnp
from jax import lax
from jax.experimental import pallas as pl
from jax.experimental.pallas import tpu as pltpu

F32 = jnp.float32
BF16 = jnp.bfloat16
I32 = jnp.int32
U32 = jnp.uint32

D_MODEL = 1024
D_RNN = 512
RNN_BLOCKS = 8
CONV_W = 4
LRU_C = 8.0
N_HEADS = 8
HEAD_DIM = 64
N_KV = 2
GQA_R = N_HEADS // N_KV
D_ATTN = N_HEADS * HEAD_DIM
WINDOW = 128
N_BUCKETS = 32
MAX_DIST = 128
N_EXPERTS = 32
TOP_K = 4
D_FF = 1024
SWIGLU_LIMIT = 7.0
SWIGLU_ALPHA = 1.702
D_PLE = 256
LN_EPS = 1e-5
D_IN = 2 * D_RNN + D_ATTN + 2 * N_KV * HEAD_DIM
NEG = -1e30

SUBLANES = 8
LANES = 128
ROW_TILE = 512
MOE_TILE = 1024
D_PACK = D_MODEL // 2
VMEM_LIMIT = 56 * 1024 * 1024
HI_MASK = 0xFFFF0000


def _cparams(*sem):
    return pltpu.CompilerParams(dimension_semantics=sem, vmem_limit_bytes=VMEM_LIMIT)


def _layer_norm(x, g, b):
    mu = jnp.mean(x, axis=-1, keepdims=True)
    xc = x - mu
    var = jnp.mean(xc * xc, axis=-1, keepdims=True)
    return xc * lax.rsqrt(var + LN_EPS) * g + b


def _rms_norm(x, g):
    return x * lax.rsqrt(jnp.mean(x * x, axis=-1, keepdims=True) + LN_EPS) * g


def _const_spec(shape):
    return pl.BlockSpec(shape, lambda *_: (0,) * len(shape))


def _layer_spec(layer, *tail):
    return pl.BlockSpec((None,) + tail, lambda *_: (layer,) + (0,) * len(tail))


def _in_proj_kernel(x_ref, g_ref, b_ref, w_ref, *out_refs, apply_ln):
    x = x_ref[...]
    if apply_ln:
        x = _layer_norm(x, g_ref[...], b_ref[...])
        out_refs[0][...] = x
    out_refs[-1][...] = jnp.dot(x.astype(BF16), w_ref[...], preferred_element_type=F32)


def _in_proj(x, ln_g, ln_b, w_in_bf16, layer, apply_ln):
    n = x.shape[0]
    tm = min(ROW_TILE, n)
    row = lambda w: pl.BlockSpec((tm, w), lambda i: (i, 0))
    out_shape = [jax.ShapeDtypeStruct((n, D_IN), F32)]
    out_specs = [row(D_IN)]
    if apply_ln:
        out_shape.insert(0, jax.ShapeDtypeStruct((n, D_MODEL), F32))
        out_specs.insert(0, row(D_MODEL))
    outs = pl.pallas_call(
        functools.partial(_in_proj_kernel, apply_ln=apply_ln),
        grid=(n // tm,),
        in_specs=[row(D_MODEL), _const_spec((1, D_MODEL)), _const_spec((1, D_MODEL)),
                  _layer_spec(layer, D_MODEL, D_IN)],
        out_specs=out_specs, out_shape=out_shape,
        compiler_params=_cparams("parallel"), name="in_proj",
    )(x, ln_g, ln_b, w_in_bf16)
    return (outs[0], outs[1]) if apply_ln else (x, outs[0])


def _rglru_coeffs(u, wa_ref, ba_ref, wx_ref, bx_ref, lam_ref):
    ub = u.astype(BF16)
    r = jax.nn.sigmoid(jnp.dot(ub, wa_ref[...], preferred_element_type=F32) + ba_ref[...])
    i = jax.nn.sigmoid(jnp.dot(ub, wx_ref[...], preferred_element_type=F32) + bx_ref[...])
    lam = -lam_ref[...]
    softplus = jnp.maximum(lam, 0.0) + jnp.log1p(jnp.exp(-jnp.abs(lam)))
    log_a = -LRU_C * r * softplus
    a = jnp.exp(log_a)
    b = jnp.sqrt(-jnp.tanh(log_a) * (a * a + 1.0)) * (i * u)
    return a, b


def _rnn_prompt_kernel(xr_ref, gate_ref, cw_ref, cb_ref, wa_ref, ba_ref, wx_ref, bx_ref, lam_ref,
                       y_ref, hlast_ref, xs_ref, a_ref, b_ref, carry_ref, *, tt):
    t = pl.program_id(1)

    @pl.when(t == 0)
    def _():
        xs_ref[0:SUBLANES, :] = jnp.zeros((SUBLANES, D_RNN), F32)
        carry_ref[...] = jnp.zeros((SUBLANES, D_RNN), F32)

    x = xr_ref[...]
    xs_ref[SUBLANES:SUBLANES + tt, :] = x
    cw = cw_ref[...]
    u = cb_ref[...] + x * cw[3:4]
    for j in range(CONV_W - 1):
        off = SUBLANES - (CONV_W - 1) + j
        u = u + xs_ref[off:off + tt, :] * cw[j:j + 1]
    xs_ref[0:SUBLANES, :] = x[tt - SUBLANES:tt, :]

    a, b = _rglru_coeffs(u, wa_ref, ba_ref, wx_ref, bx_ref, lam_ref)
    ng = tt // SUBLANES
    a = a.reshape(ng, SUBLANES, D_RNN)
    b = b.reshape(ng, SUBLANES, D_RNN)
    row = lax.broadcasted_iota(I32, (1, SUBLANES, 1), 1)
    for s in (1, 2, 4):
        a_sh = pltpu.roll(a, s, 1)
        b_sh = pltpu.roll(b, s, 1)
        m = row >= s
        b = jnp.where(m, a * b_sh + b, b)
        a = jnp.where(m, a * a_sh, a)
    a_ref[...] = a
    b_ref[...] = b

    def body(g, carry):
        h = a_ref[g] * carry + b_ref[g]
        b_ref[g] = h
        return jnp.broadcast_to(h[SUBLANES - 1:SUBLANES, :], (SUBLANES, D_RNN))

    carry = lax.fori_loop(0, ng, body, carry_ref[...])
    carry_ref[...] = carry
    h = b_ref[...].reshape(tt, D_RNN)
    y_ref[...] = h * jax.nn.gelu(gate_ref[...])
    hlast_ref[...] = carry[0:1, :]


def _rnn_prompt(z3, rw):
    bsz, t_len, _ = z3.shape
    tt = min(ROW_TILE, t_len)
    vec = _layer_spec(rw["layer"], 1, D_RNN)
    mat = _layer_spec(rw["layer"], D_RNN, D_RNN)
    return pl.pallas_call(
        functools.partial(_rnn_prompt_kernel, tt=tt),
        grid=(bsz, t_len // tt),
        in_specs=[pl.BlockSpec((None, tt, D_RNN), lambda b, t: (b, t, 0)),
                  pl.BlockSpec((None, tt, D_RNN), lambda b, t: (b, t, 1)),
                  _layer_spec(rw["layer"], CONV_W, D_RNN), vec, mat, vec, mat, vec, vec],
        out_specs=[pl.BlockSpec((None, tt, D_RNN), lambda b, t: (b, t, 0)),
                   pl.BlockSpec((None, 1, D_RNN), lambda b, t: (b, 0, 0))],
        out_shape=[jax.ShapeDtypeStruct((bsz, t_len, D_RNN), F32),
                   jax.ShapeDtypeStruct((bsz, 1, D_RNN), F32)],
        scratch_shapes=[pltpu.VMEM((tt + SUBLANES, D_RNN), F32),
                        pltpu.VMEM((tt // SUBLANES, SUBLANES, D_RNN), F32),
                        pltpu.VMEM((tt // SUBLANES, SUBLANES, D_RNN), F32),
                        pltpu.VMEM((SUBLANES, D_RNN), F32)],
        compiler_params=_cparams("parallel", "arbitrary"), name="rnn_prompt",
    )(z3, z3, rw["conv_w"], rw["conv_b"], rw["wa"], rw["ba"], rw["wx"], rw["bx"], rw["lam"])


def _rnn_sample_kernel(xr_ref, gate_ref, c0_ref, c1_ref, c2_ref, h0_ref, cw_ref, cb_ref, wa_ref, ba_ref,
                       wx_ref, bx_ref, lam_ref, y_ref, h_ref):
    x = xr_ref[...]
    cw = cw_ref[...]
    u = (cb_ref[...] + c0_ref[...] * cw[0:1] + c1_ref[...] * cw[1:2] + c2_ref[...] * cw[2:3]
         + x * cw[3:4])
    a, b = _rglru_coeffs(u, wa_ref, ba_ref, wx_ref, bx_ref, lam_ref)
    h = a * h0_ref[...] + b
    h_ref[...] = h
    y_ref[...] = h * jax.nn.gelu(gate_ref[...])


def _rnn_sample(z, conv_state, h0, rw):
    n = z.shape[0]
    layer = rw["layer"]
    vec = _layer_spec(layer, 1, D_RNN)
    mat = _layer_spec(layer, D_RNN, D_RNN)
    full = _const_spec((n, D_RNN))
    depth = conv_state.shape[0]
    conv_flat = conv_state.reshape(depth, n, (CONV_W - 1) * D_RNN)
    tap = lambda j: pl.BlockSpec((None, n, D_RNN), lambda i: (layer, 0, j))
    return pl.pallas_call(
        _rnn_sample_kernel,
        grid=(1,),
        in_specs=[pl.BlockSpec((n, D_RNN), lambda i: (0, 0)), pl.BlockSpec((n, D_RNN), lambda i: (0, 1)),
                  tap(0), tap(1), tap(2), _layer_spec(layer, n, D_RNN),
                  _layer_spec(layer, CONV_W, D_RNN), vec, mat, vec, mat, vec, vec],
        out_specs=[full, full],
        out_shape=[jax.ShapeDtypeStruct((n, D_RNN), F32)] * 2,
        compiler_params=_cparams("arbitrary"), name="rnn_sample",
    )(z, z, conv_flat, conv_flat, conv_flat, h0, rw["conv_w"], rw["conv_b"], rw["wa"], rw["ba"], rw["wx"], rw["bx"], rw["lam"])


def _rel_bucket(dist):
    n = jnp.maximum(dist, 0)
    max_exact = N_BUCKETS // 2
    nf = jnp.maximum(n, max_exact).astype(F32)
    large = max_exact + (jnp.log(nf / max_exact) / math.log(MAX_DIST / max_exact)
                         * (N_BUCKETS - max_exact)).astype(I32)
    large = jnp.minimum(large, N_BUCKETS - 1)
    return jnp.where(n < max_exact, n, large)


def _bias_lookup(rel_bias, dist):
    hot = (_rel_bucket(dist)[..., None] == jnp.arange(N_BUCKETS)).astype(F32)
    out = jnp.tensordot(hot, rel_bias.astype(F32), axes=1, precision=lax.Precision.HIGHEST)
    return jnp.moveaxis(out, -1, 0)


def _half_lane_variants(x, group):
    lo = lax.broadcasted_iota(I32, (1, 2 * HEAD_DIM), 1) < HEAD_DIM
    xr = pltpu.roll(x, HEAD_DIM, 1)
    zero = jnp.zeros_like(x)
    if group == 0:
        return jnp.where(lo, x, zero), jnp.where(lo, zero, xr)
    return jnp.where(lo, xr, zero), jnp.where(lo, zero, x)


def _attn_prompt_kernel(q_ref, kc_ref, kp_ref, vc_ref, vp_ref, bias_ref, sink_ref, o_ref, *, nq, layer):
    first = pl.program_id(1) == 0
    col = lax.broadcasted_iota(I32, (1, 2 * WINDOW), 1)
    hide_prev = jnp.logical_and(first, col < WINDOW)
    for j in range(nq):
        rows = slice(j * WINDOW, (j + 1) * WINDOW)
        before = slice((j - 1) * WINDOW, j * WINDOW)
        kk = jnp.concatenate([kp_ref[...] if j == 0 else kc_ref[before, :], kc_ref[rows, :]], axis=0)
        vv = jnp.concatenate([vp_ref[...] if j == 0 else vc_ref[before, :], vc_ref[rows, :]], axis=0)
        for g in range(N_KV):
            k_lo, k_hi = _half_lane_variants(kk, g)
            v_lo, v_hi = _half_lane_variants(vv, g)
            for pair in range(GQA_R // 2):
                p0 = g * (GQA_R // 2) + pair
                qp = (q_ref[rows, p0 * 128:(p0 + 1) * 128] * (HEAD_DIM ** -0.5)).astype(BF16)
                acc = jnp.zeros((WINDOW, 2 * HEAD_DIM), F32)
                for kx, vx, h in ((k_lo, v_lo, 2 * p0), (k_hi, v_hi, 2 * p0 + 1)):
                    s = lax.dot_general(qp, kx.astype(BF16), (((1,), (1,)), ((), ())),
                                        preferred_element_type=F32) + bias_ref[h]
                    if j == 0:
                        s = jnp.where(hide_prev, NEG, s)
                    sink = sink_ref[layer, h]
                    m = jnp.maximum(jnp.max(s, axis=-1, keepdims=True), sink)
                    p = jnp.exp(s - m)
                    denom = jnp.sum(p, axis=-1, keepdims=True) + jnp.exp(sink - m)
                    acc = acc + jnp.dot(p.astype(BF16), vx.astype(BF16), preferred_element_type=F32) / denom
                o_ref[rows, p0 * 128:(p0 + 1) * 128] = acc


def _attn_prompt(z, bsz, t_len, bias_prompt, sinks, layer):
    nb = t_len // WINDOW
    nq = min(4, nb)
    ns = nb // nq
    kcol = (2 * D_RNN + D_ATTN) // 128
    cur = lambda c: pl.BlockSpec((nq * WINDOW, 128), lambda b, i: (b * ns + i, c))
    prev = lambda c: pl.BlockSpec((WINDOW, 128), lambda b, i: (b * nb + jnp.maximum(i * nq - 1, 0), c))
    return pl.pallas_call(
        functools.partial(_attn_prompt_kernel, nq=nq, layer=layer),
        grid=(bsz, ns),
        in_specs=[pl.BlockSpec((nq * WINDOW, D_ATTN), lambda b, i: (b * ns + i, 2 * D_RNN // D_ATTN)),
                  cur(kcol), prev(kcol), cur(kcol + 1), prev(kcol + 1),
                  _const_spec((N_HEADS, WINDOW, 2 * WINDOW)),
                  pl.BlockSpec(memory_space=pltpu.SMEM)],
        out_specs=pl.BlockSpec((nq * WINDOW, D_ATTN), lambda b, i: (b * ns + i, 0)),
        out_shape=jax.ShapeDtypeStruct((bsz * t_len, D_ATTN), F32),
        compiler_params=_cparams("parallel", "arbitrary"), name="attn_prompt",
    )(z, z, z, z, z, bias_prompt, sinks)


def _attn_sample_kernel(q_ref, kn_ref, vn_ref, ck_ref, cv_ref, bias_ref, sink_ref, hmask_ref, *rest):
    o_ref, nk_ref, nv_ref = rest[-3:]
    pos = lax.broadcasted_iota(I32, (1, 1, WINDOW), 2)
    last = pos == WINDOW - 1
    bb = ck_ref.shape[0]
    newk = jnp.where(last, jnp.broadcast_to(kn_ref[...], (bb, 128, WINDOW)), pltpu.roll(ck_ref[...], WINDOW - 1, 2))
    newv = jnp.where(last, jnp.broadcast_to(vn_ref[...], (bb, 128, WINDOW)), pltpu.roll(cv_ref[...], WINDOW - 1, 2))
    nk_ref[...] = newk
    nv_ref[...] = newv
    s = lax.dot_general(q_ref[...].astype(BF16), newk.astype(BF16), (((2,), (1,)), ((0,), (0,))),
                        preferred_element_type=F32)
    s = s + bias_ref[...][None]
    sink = sink_ref[...][None]
    m = jnp.maximum(jnp.max(s, axis=-1, keepdims=True), sink)
    p = jnp.exp(s - m)
    denom = jnp.sum(p, axis=-1, keepdims=True) + jnp.exp(sink - m)
    o = lax.dot_general(p.astype(BF16), newv.astype(BF16), (((2,), (2,)), ((0,), (0,))),
                        preferred_element_type=F32)
    o_ref[...] = o / denom * hmask_ref[...][None]


def _attn_sample(q, k_new, v_new, k_cache, v_cache, new_caches, layer, bias_sample, sinks):
    n = q.shape[0]
    depth = k_cache.shape[0]
    bb = 16
    qh = q.reshape(n, N_KV, GQA_R, HEAD_DIM) * (HEAD_DIM ** -0.5)
    eye = jnp.eye(N_KV, dtype=F32)
    q8 = (qh[:, :, :, None, :] * eye[None, :, None, :, None]).reshape(n, N_HEADS, N_KV * HEAD_DIM)
    hmask = jnp.repeat(jnp.repeat(eye, GQA_R, axis=0), HEAD_DIM, axis=1)
    blk3 = lambda a, b: pl.BlockSpec((bb, a, b), lambda i: (i, 0, 0))
    cache = pl.BlockSpec((None, bb, 128, WINDOW), lambda i: (layer, i, 0, 0))
    in_specs = [blk3(N_HEADS, 128), blk3(128, 1), blk3(128, 1), cache, cache,
                _const_spec((N_HEADS, WINDOW)), _layer_spec(layer, N_HEADS, 1), _const_spec((N_HEADS, 128))]
    args = [q8, k_new.reshape(n, 128, 1), v_new.reshape(n, 128, 1), k_cache, v_cache, bias_sample,
            sinks.reshape(-1, N_HEADS, 1), hmask]
    aliases = {}
    if new_caches is not None:
        in_specs += [pl.BlockSpec(memory_space=pl.ANY)] * 2
        args += list(new_caches)
        aliases = {len(args) - 2: 1, len(args) - 1: 2}
    o8, nk, nv = pl.pallas_call(
        _attn_sample_kernel,
        grid=(n // bb,),
        in_specs=in_specs,
        out_specs=[blk3(N_HEADS, 128), cache, cache],
        out_shape=[jax.ShapeDtypeStruct((n, N_HEADS, 128), F32),
                   jax.ShapeDtypeStruct((depth, n, 128, WINDOW), F32),
                   jax.ShapeDtypeStruct((depth, n, 128, WINDOW), F32)],
        input_output_aliases=aliases,
        compiler_params=_cparams("arbitrary"), name="attn_sample",
    )(*args)
    o = o8.reshape(n, N_KV, GQA_R, N_KV, HEAD_DIM)
    o = jnp.stack([o[:, g, :, g, :] for g in range(N_KV)], axis=1).reshape(n, D_ATTN)
    return o, (nk, nv)


ONE_HOT_BLOCK = 256


def _sorted_rows(tm):
    return TOP_K * tm + N_EXPERTS * SUBLANES


def _out_proj_kernel(x_ref, yr_ref, o_ref, gnr_ref, gna_ref, w_ref, g1_ref, b1_ref, rwt_ref, rb_ref,
                     x1_ref, rrow_ref, rcol_ref, cnt_ref, *, tm, alpha):
    yn = _rms_norm(yr_ref[...], gnr_ref[...]).astype(BF16)
    on = _rms_norm(o_ref[...], gna_ref[...]).astype(BF16)
    mix = jnp.dot(jnp.concatenate([yn, on], axis=1), w_ref[...], preferred_element_type=F32)
    x1 = _layer_norm(alpha * x_ref[...] + mix, g1_ref[...], b1_ref[...])
    x1_ref[...] = x1

    x1h = x1.astype(BF16)
    x1l = (x1 - x1h.astype(F32)).astype(BF16)
    nt_dims = (((1,), (1,)), ((), ()))
    by_hi = lax.dot_general(rwt_ref[...], x1h, nt_dims, preferred_element_type=F32)
    by_lo = lax.dot_general(rwt_ref[0:N_EXPERTS, :], x1l, nt_dims, preferred_element_type=F32)
    logits = by_hi[0:N_EXPERTS] + by_hi[N_EXPERTS:] + by_lo + rb_ref[...]
    eidx = lax.broadcasted_iota(I32, (N_EXPERTS, tm), 0).astype(F32)
    work = logits
    vals, hots = [], []
    for _ in range(TOP_K):
        v = jnp.max(work, axis=0, keepdims=True)
        idx = jnp.min(jnp.where(work == v, eidx, float(N_EXPERTS)), axis=0, keepdims=True)
        hot = eidx == idx
        work = jnp.where(hot, -jnp.inf, work)
        vals.append(v)
        hots.append(hot.astype(F32))
    ex = [jnp.exp(v - vals[0]) for v in vals]
    tot = ex[0] + ex[1] + ex[2] + ex[3]
    tok_hot = hots[0] + hots[1] + hots[2] + hots[3]
    s_i = lax.broadcasted_iota(I32, (tm, tm), 0)
    t_i = lax.broadcasted_iota(I32, (tm, tm), 1)
    earlier = (s_i < t_i).astype(BF16)
    before = jnp.dot(tok_hot.astype(BF16), earlier, preferred_element_type=F32)
    cnt = jnp.sum(tok_hot, axis=1, keepdims=True)
    cnt8 = jnp.floor((cnt + (SUBLANES - 1)) * (1.0 / SUBLANES)) * SUBLANES
    cnt8_b = jnp.broadcast_to(cnt8, (N_EXPERTS, LANES))
    e_r = lax.broadcasted_iota(I32, (N_EXPERTS, N_EXPERTS), 0)
    e_c = lax.broadcasted_iota(I32, (N_EXPERTS, N_EXPERTS), 1)
    start = jnp.dot((e_c < e_r).astype(F32), cnt8_b, preferred_element_type=F32,
                    precision=lax.Precision.HIGHEST)[:, 0:1]
    slot = start + before
    rows = [jnp.sum(h * slot, axis=0, keepdims=True) for h in hots] + [e / tot for e in ex]
    rrow = jnp.concatenate(rows, axis=0)
    rrow_ref[...] = rrow
    rcol_ref[...] = rrow.T
    cnt_ref[...] = cnt8_b.T[0:1, 0:N_EXPERTS].astype(I32)


def _out_proj(x, y_rnn, o, lw):
    n = x.shape[0]
    tm = min(ROW_TILE, n)
    nt = n // tm
    row = lambda w: pl.BlockSpec((tm, w), lambda i: (i, 0))
    layer = lw["layer"]
    vec = lambda w: _layer_spec(layer, 1, w)
    return pl.pallas_call(
        functools.partial(_out_proj_kernel, tm=tm, alpha=lw["alpha"]),
        grid=(nt,),
        in_specs=[row(D_MODEL), row(D_RNN), row(D_ATTN), vec(D_RNN), vec(D_ATTN),
                  _layer_spec(layer, D_MODEL, D_MODEL), vec(D_MODEL), vec(D_MODEL),
                  _layer_spec(layer, 2 * N_EXPERTS, D_MODEL), _layer_spec(layer, N_EXPERTS, 1)],
        out_specs=[row(D_MODEL), pl.BlockSpec((None, 2 * TOP_K, tm), lambda i: (i, 0, 0)), row(2 * TOP_K),
                   pl.BlockSpec((None, 1, N_EXPERTS), lambda i: (i, 0, 0))],
        out_shape=[jax.ShapeDtypeStruct((n, D_MODEL), F32), jax.ShapeDtypeStruct((nt, 2 * TOP_K, tm), F32),
                   jax.ShapeDtypeStruct((n, 2 * TOP_K), F32), jax.ShapeDtypeStruct((nt, 1, N_EXPERTS), I32)],
        compiler_params=_cparams("parallel"), name="out_proj_route",
    )(x, y_rnn, o, lw["gn_rnn"], lw["gn_attn"], lw["w_out"], lw["ln1_g"], lw["ln1_b"],
      lw["router_wt"], lw["router_b"])


RUN_CHUNK = 64
RUN_WINDOW = 128
WAIT_ROWS = 512
_SMALL_PIECES = (32, 16, 8)
_MAX_PAD_ROWS = N_EXPERTS * SUBLANES
_PAD_PIECES = (128, 64, 32, 16, 8)


def _for_each_piece(n8, fn):
    def chunk(j, c):
        fn(pl.multiple_of(j * RUN_CHUNK, RUN_CHUNK), RUN_CHUNK)
        return c
    lax.fori_loop(0, n8 // RUN_CHUNK, chunk, 0)
    for s in _SMALL_PIECES:
        @pl.when((n8 & s) != 0)
        def _(s=s):
            fn(n8 & (-2 * s), s)


def _wait_tile_rows(n_rows, tm, wait_rows):
    wait_rows(TOP_K * tm)
    pad = n_rows - TOP_K * tm
    for s in _PAD_PIECES:
        @pl.when((pad & s) != 0)
        def _(s=s):
            wait_rows(s)


def _dispatch_kernel(cnt_ref, base_ref, x_ref, rrow_ref, *rest, tm):
    xbuf_ref, xs_ref, carry_ref, nrows_ref, sems = rest[-5:]
    nrow = _sorted_rows(tm)
    i = pl.program_id(0)
    slot = i % 2

    @pl.when(i == 0)
    def _():
        def init(e, c):
            carry_ref[e] = base_ref[e]
            return c
        lax.fori_loop(0, N_EXPERTS, init, 0)
        xs_ref[:, nrow:, :] = jnp.zeros((2, RUN_WINDOW, D_PACK), U32)

    r = lax.broadcasted_iota(I32, (ONE_HOT_BLOCK, 1), 0).astype(F32).astype(BF16)
    blocks = []
    for c in range(nrow // ONE_HOT_BLOCK):
        hit = None
        for k in range(TOP_K):
            h = r == (rrow_ref[k:k + 1, :] - float(c * ONE_HOT_BLOCK)).astype(BF16)
            hit = h if hit is None else jnp.logical_or(hit, h)
        blocks.append(jnp.where(hit, jnp.ones((), BF16), jnp.zeros((), BF16)))
    perm = jnp.concatenate(blocks, axis=0)
    xs = jnp.dot(perm, x_ref[...].astype(BF16), preferred_element_type=F32)
    lo = lax.bitcast_convert_type(xs[:, :D_PACK], U32)
    hi = lax.bitcast_convert_type(xs[:, D_PACK:], U32)
    xs_ref[slot, 0:nrow, :] = (hi & U32(HI_MASK)) | (lo >> 16)

    def run_copy(buf, src, dst, s):
        return pltpu.make_async_copy(xs_ref.at[buf, pl.ds(pl.multiple_of(src, SUBLANES), s)],
                                     xbuf_ref.at[pl.ds(pl.multiple_of(dst, SUBLANES), s)], sems.at[buf])

    def wait_tile(buf):
        n_rows = nrows_ref[buf]

        @pl.when(n_rows < 0)
        def _():
            for _ in range(N_EXPERTS * RUN_WINDOW // WAIT_ROWS):
                run_copy(buf, 0, 0, WAIT_ROWS).wait()

        @pl.when(n_rows >= 0)
        def _():
            _wait_tile_rows(n_rows, tm, lambda s: run_copy(buf, 0, 0, s).wait())

    @pl.when(i > 0)
    def _():
        wait_tile(1 - slot)

    longest = lax.fori_loop(0, N_EXPERTS, lambda e, m: jnp.maximum(m, cnt_ref[0, e]), 0)

    @pl.when(longest <= RUN_WINDOW)
    def _():
        def start(e, src):
            n8 = cnt_ref[0, e]
            dst = carry_ref[e]
            run_copy(slot, src, dst, RUN_WINDOW).start()
            carry_ref[e] = dst + n8
            return src + n8
        lax.fori_loop(0, N_EXPERTS, start, 0)
        nrows_ref[slot] = -1

    @pl.when(longest > RUN_WINDOW)
    def _():
        def start(e, src):
            n8 = cnt_ref[0, e]
            dst = carry_ref[e]
            _for_each_piece(n8, lambda off, s: run_copy(slot, src + off, dst + off, s).start())
            carry_ref[e] = dst + n8
            return src + n8
        nrows_ref[slot] = lax.fori_loop(0, N_EXPERTS, start, 0)

    @pl.when(i == pl.num_programs(0) - 1)
    def _():
        wait_tile(slot)


def _dispatch(x1, rrow, cnt8, base, xbuf, n_rows):
    n = x1.shape[0]
    nt, _, tm = rrow.shape
    in_specs = [pl.BlockSpec((None, 1, N_EXPERTS), lambda i: (i, 0, 0), memory_space=pltpu.SMEM),
                pl.BlockSpec(memory_space=pltpu.SMEM),
                pl.BlockSpec((tm, D_MODEL), lambda i: (i, 0)),
                pl.BlockSpec((None, 2 * TOP_K, tm), lambda i: (i, 0, 0))]
    args = [cnt8, base, x1, rrow]
    aliases = {}
    if xbuf is not None:
        in_specs.append(pl.BlockSpec(memory_space=pl.ANY))
        args.append(xbuf)
        aliases = {len(args) - 1: 0}
    return pl.pallas_call(
        functools.partial(_dispatch_kernel, tm=tm),
        grid=(nt,),
        in_specs=in_specs,
        out_specs=pl.BlockSpec(memory_space=pl.ANY),
        out_shape=jax.ShapeDtypeStruct((n_rows, D_PACK), U32),
        scratch_shapes=[pltpu.VMEM((2, _sorted_rows(tm) + RUN_WINDOW, D_PACK), U32),
                        pltpu.SMEM((N_EXPERTS,), I32), pltpu.SMEM((2,), I32), pltpu.SemaphoreType.DMA((2,))],
        input_output_aliases=aliases,
        compiler_params=_cparams("arbitrary"), name="moe_dispatch",
    )(*args)


def _moe_kernel(br_ref, be_ref, nv_ref, first_ref, next_ref, x_ref, w1_hbm, b1_ref, w2_hbm, b2_ref, y_ref,
                w1f_ref, w2f_ref, w1s_ref, w2s_ref, sems, *, layer):
    i = pl.program_id(0)

    def load(e):
        return (pltpu.make_async_copy(w1_hbm.at[layer, e], w1f_ref, sems.at[0]),
                pltpu.make_async_copy(w2_hbm.at[layer, e], w2f_ref, sems.at[1]))

    @pl.when(jnp.logical_and(i == 0, first_ref[0] == 1))
    def _():
        for c in load(be_ref[0]):
            c.start()

    @pl.when(first_ref[i] == 1)
    def _():
        for c in load(be_ref[i]):
            c.wait()
        w1s_ref[...] = w1f_ref[...].astype(BF16)
        w2s_ref[...] = w2f_ref[...].astype(BF16)

        @pl.when(next_ref[i] >= 0)
        def _():
            for c in load(next_ref[i]):
                c.start()

    nv = nv_ref[i]

    def expert_rows(m):
        rows = lax.broadcasted_iota(I32, (m, 1), 0)
        xw = jnp.where(rows < nv, x_ref[0:m, :], U32(0))
        lo = lax.bitcast_convert_type(xw << 16, F32).astype(BF16)
        hi = lax.bitcast_convert_type(xw & U32(HI_MASK), F32).astype(BF16)
        xb = jnp.concatenate([lo, hi], axis=1)
        hdn = jnp.dot(xb, w1s_ref[...], preferred_element_type=F32) + b1_ref[...]
        g = jnp.minimum(hdn[:, :D_FF], SWIGLU_LIMIT)
        lin = jnp.clip(hdn[:, D_FF:], -SWIGLU_LIMIT, SWIGLU_LIMIT)
        act = g * jax.nn.sigmoid(SWIGLU_ALPHA * g) * (lin + 1.0)
        y_ref[0:m, :] = jnp.dot(act.astype(BF16), w2s_ref[...], preferred_element_type=F32) + b2_ref[...]

    quarter = MOE_TILE // 4
    for q in range(1, 5):
        @pl.when(jnp.logical_and(nv > (q - 1) * quarter, nv <= q * quarter))
        def _(q=q):
            expert_rows(q * quarter)


def _moe_experts(xbuf, geom, w1, b1, w2, b2, layer):
    n_blocks = xbuf.shape[0] // MOE_TILE
    bias = lambda w: pl.BlockSpec((None, None, 1, w), lambda i, br, be, *_: (layer, be[i], 0, 0))
    grid_spec = pltpu.PrefetchScalarGridSpec(
        num_scalar_prefetch=5,
        grid=(n_blocks,),
        in_specs=[pl.BlockSpec((MOE_TILE, D_PACK), lambda i, br, *_: (br[i], 0)),
                  pl.BlockSpec(memory_space=pl.ANY), bias(2 * D_FF),
                  pl.BlockSpec(memory_space=pl.ANY), bias(D_MODEL)],
        out_specs=pl.BlockSpec((MOE_TILE, D_MODEL), lambda i, br, *_: (br[i], 0)),
        scratch_shapes=[pltpu.VMEM((D_MODEL, 2 * D_FF), F32), pltpu.VMEM((D_FF, D_MODEL), F32),
                        pltpu.VMEM((D_MODEL, 2 * D_FF), BF16), pltpu.VMEM((D_FF, D_MODEL), BF16),
                        pltpu.SemaphoreType.DMA((2,))],
    )
    depth = w1.shape[0]
    return pl.pallas_call(
        functools.partial(_moe_kernel, layer=layer), grid_spec=grid_spec,
        out_shape=jax.ShapeDtypeStruct((xbuf.shape[0], D_MODEL), F32),
        compiler_params=_cparams("arbitrary"), name="moe_experts",
    )(*geom, xbuf, w1, b1.reshape(depth, N_EXPERTS, 1, 2 * D_FF), w2, b2.reshape(depth, N_EXPERTS, 1, D_MODEL))


def _final_kernel(cnt_ref, cnt_next_ref, base_ref, x_ref, p_ref, rcol_ref, wg_ref, bg_ref, wp_ref, g2_ref, b2_ref,
                  ybuf_ref, out_ref, ys_ref, carry_ref, nrows_ref, sems, *, tm, alpha):
    nrow = _sorted_rows(tm)
    i = pl.program_id(0)
    slot = i % 2

    def run_copy(buf, src, dst, s):
        return pltpu.make_async_copy(ybuf_ref.at[pl.ds(pl.multiple_of(src, SUBLANES), s)],
                                     ys_ref.at[buf, pl.ds(pl.multiple_of(dst, SUBLANES), s)], sems.at[buf])

    def gather_tile(counts_ref, buf):
        def start(e, dst):
            n8 = counts_ref[0, e]
            src = carry_ref[e]
            _for_each_piece(n8, lambda off, s: run_copy(buf, src + off, dst + off, s).start())
            carry_ref[e] = src + n8
            return dst + n8
        nrows_ref[buf] = lax.fori_loop(0, N_EXPERTS, start, 0)

    @pl.when(i == 0)
    def _():
        def init(e, c):
            carry_ref[e] = base_ref[e]
            return c
        lax.fori_loop(0, N_EXPERTS, init, 0)
        ys_ref[:, TOP_K * tm:, :] = jnp.zeros((2, nrow - TOP_K * tm, D_MODEL), F32)
        gather_tile(cnt_ref, 0)

    @pl.when(i + 1 < pl.num_programs(0))
    def _():
        gather_tile(cnt_next_ref, 1 - slot)

    x = x_ref[...]
    ple = (jax.nn.sigmoid(jnp.dot(x.astype(BF16), wg_ref[...], preferred_element_type=F32) + bg_ref[...])
           * jnp.dot(p_ref[...].astype(BF16), wp_ref[...], preferred_element_type=F32))

    _wait_tile_rows(nrows_ref[slot], tm, lambda s: run_copy(slot, 0, 0, s).wait())

    ys = ys_ref[slot].astype(BF16)
    col = lax.broadcasted_iota(I32, (1, ONE_HOT_BLOCK), 1).astype(F32).astype(BF16)
    rc = rcol_ref[...]
    gates = rc[:, TOP_K:].astype(BF16)
    blocks = []
    for c in range(nrow // ONE_HOT_BLOCK):
        blk = None
        for k in range(TOP_K):
            hit = col == (rc[:, k:k + 1] - float(c * ONE_HOT_BLOCK)).astype(BF16)
            term = jnp.where(hit, gates[:, k:k + 1], jnp.zeros((), BF16))
            blk = term if blk is None else blk + term
        blocks.append(blk)
    ffn = jnp.dot(jnp.concatenate(blocks, axis=1), ys, preferred_element_type=F32)
    out_ref[...] = _layer_norm(alpha * x + ffn + ple, g2_ref[...], b2_ref[...])


def _final(x1, p_all, layer, rcol, cnt8, base, ybuf, lw):
    n = x1.shape[0]
    nt = cnt8.shape[0]
    tm = n // nt
    row = lambda w: pl.BlockSpec((tm, w), lambda i: (i, 0))
    vec = lambda w: _layer_spec(layer, 1, w)
    return pl.pallas_call(
        functools.partial(_final_kernel, tm=tm, alpha=lw["alpha"]),
        grid=(nt,),
        in_specs=[pl.BlockSpec((None, 1, N_EXPERTS), lambda i: (i, 0, 0), memory_space=pltpu.SMEM),
                  pl.BlockSpec((None, 1, N_EXPERTS), lambda i: (jnp.minimum(i + 1, nt - 1), 0, 0),
                               memory_space=pltpu.SMEM),
                  pl.BlockSpec(memory_space=pltpu.SMEM),
                  row(D_MODEL), pl.BlockSpec((None, tm, D_PLE), lambda i: (layer, i, 0)), row(2 * TOP_K),
                  _layer_spec(layer, D_MODEL, D_MODEL), vec(D_MODEL), _layer_spec(layer, D_PLE, D_MODEL),
                  vec(D_MODEL), vec(D_MODEL), pl.BlockSpec(memory_space=pl.ANY)],
        out_specs=row(D_MODEL),
        out_shape=jax.ShapeDtypeStruct((n, D_MODEL), F32),
        scratch_shapes=[pltpu.VMEM((2, _sorted_rows(tm), D_MODEL), F32), pltpu.SMEM((N_EXPERTS,), I32),
                        pltpu.SMEM((2,), I32), pltpu.SemaphoreType.DMA((2,))],
        compiler_params=_cparams("arbitrary"), name="moe_combine_ln2",
    )(cnt8, cnt8, base, x1, p_all, rcol, lw["ple_w_gate"], lw["ple_b_gate"], lw["ple_w_proj"], lw["ln2_g"], lw["ln2_b"],
      ybuf)


def _block_diag(w):
    depth, nb, bw, _ = w.shape
    eye = jnp.eye(nb, dtype=w.dtype)
    return (w[:, :, :, None, :] * eye[None, :, None, :, None]).reshape(depth, nb * bw, nb * bw)


def _hi_lo_rows(w):
    hi = w.astype(BF16)
    lo = (w - hi.astype(F32)).astype(BF16)
    return jnp.concatenate([hi, lo], axis=-2)


def _block_geometry(tot, n_blocks):
    nblk = (tot + RUN_WINDOW + MOE_TILE - 1) // MOE_TILE
    blk_end = jnp.cumsum(nblk)
    blk_start = blk_end - nblk
    ids = jnp.arange(n_blocks, dtype=I32)
    n_valid = blk_end[-1]
    expert_at = lambda j: jnp.minimum(jnp.sum(j[:, None] >= blk_end[None, :], axis=1), N_EXPERTS - 1).astype(I32)
    src = jnp.clip(ids, 0, jnp.maximum(n_valid - 1, 0))
    e_of = expert_at(src)
    hot = e_of[:, None] == jnp.arange(N_EXPERTS)[None, :]
    pick = lambda v: jnp.sum(jnp.where(hot, v[None, :], 0), axis=1)
    start_of, end_of, tot_of = pick(blk_start), pick(blk_end), pick(tot)
    real = ids < n_valid
    nv = jnp.where(real, jnp.clip(tot_of - (ids - start_of) * MOE_TILE, 0, MOE_TILE), 0)
    first = jnp.logical_and(real, ids == start_of)
    nxt = jnp.where(end_of < n_valid, expert_at(end_of), -1)
    geom = tuple(v.astype(I32) for v in (src, e_of, nv, first, nxt))
    return geom, (blk_start * MOE_TILE).astype(I32)


def kernel(x_prompt, x_sample, cache_k, cache_v, state_conv, state_h, p_prompt, p_sample, ln_in_g, ln_in_b, rel_bias, w_in, conv_w, conv_b, rg_wa, rg_ba, rg_wx, rg_bx, rg_lambda, attn_sinks, gn_rnn, gn_attn, w_out, ln1_g, ln1_b, router_w, router_b, moe_w1, moe_b1, moe_w2, moe_b2, ple_w_gate, ple_b_gate, ple_w_proj, ln2_g, ln2_b):
    depth = w_in.shape[0]
    bsz, t_len, _ = x_prompt.shape
    n_p = bsz * t_len
    n_s = x_sample.shape[0]
    wb = cache_k.shape[2]
    alpha = (2 * depth) ** 0.25

    dist = jnp.arange(WINDOW)[:, None] + WINDOW - jnp.arange(2 * WINDOW)[None, :]
    valid = (dist >= 0) & (dist < WINDOW)
    bias_prompt = jnp.where(valid[None], _bias_lookup(rel_bias, dist), NEG)
    bias_sample = _bias_lookup(rel_bias, WINDOW - 1 - jnp.arange(WINDOW))

    tm_p = min(ROW_TILE, n_p)
    n_tiles = n_p // tm_p + 1
    max_rows = (n_p + n_s) * TOP_K + n_tiles * N_EXPERTS * (SUBLANES - 1)
    n_blocks = -(-(max_rows + N_EXPERTS * (RUN_WINDOW + MOE_TILE - 1)) // MOE_TILE)

    xp = x_prompt.reshape(n_p, D_MODEL)
    xs = x_sample.reshape(n_s, D_MODEL)
    pp = p_prompt.reshape(depth, n_p, D_PLE)
    ps = p_sample.reshape(depth, n_s, D_PLE)
    to_feature_major = lambda c: jnp.transpose(c, (0, 1, 3, 4, 2)).reshape(depth, n_s, N_KV * HEAD_DIM, wb)
    to_position_major = lambda c: jnp.transpose(c.reshape(depth, n_s, N_KV, HEAD_DIM, wb), (0, 1, 4, 2, 3))
    ck = to_feature_major(cache_k)
    cv = to_feature_major(cache_v)
    new_caches = None
    row2 = lambda v: v.reshape(1, -1)
    kv0 = 2 * D_RNN + D_ATTN
    outs = {k: [] for k in ("kp", "vp", "cp", "hp", "cs", "hs")}

    vecs = lambda v: v.reshape(depth, 1, -1)
    rw_all = {"conv_w": conv_w, "conv_b": vecs(conv_b), "wa": _block_diag(rg_wa).astype(BF16),
              "ba": vecs(rg_ba), "wx": _block_diag(rg_wx).astype(BF16), "bx": vecs(rg_bx),
              "lam": vecs(rg_lambda)}
    lw_all = {"gn_rnn": vecs(gn_rnn), "gn_attn": vecs(gn_attn), "w_out": w_out.astype(BF16),
              "ln1_g": vecs(ln1_g), "ln1_b": vecs(ln1_b),
              "router_wt": _hi_lo_rows(jnp.swapaxes(router_w, 1, 2)),
              "router_b": router_b.reshape(depth, N_EXPERTS, 1), "alpha": alpha,
              "ple_w_gate": ple_w_gate.astype(BF16), "ple_b_gate": vecs(ple_b_gate),
              "ple_w_proj": ple_w_proj.astype(BF16), "ln2_g": vecs(ln2_g), "ln2_b": vecs(ln2_b)}
    w_in_b = w_in.astype(BF16)

    for l in range(depth):
        rw = dict(rw_all, layer=l)
        lw = dict(lw_all, layer=l)
        first = l == 0

        xp, zp = _in_proj(xp, row2(ln_in_g), row2(ln_in_b), w_in_b, l, first)
        zp3 = zp.reshape(bsz, t_len, D_IN)
        yp, hp = _rnn_prompt(zp3, rw)
        op = _attn_prompt(zp, bsz, t_len, bias_prompt, attn_sinks, l)
        wp = min(WINDOW, t_len)
        outs["kp"].append(zp3[:, t_len - wp:, kv0:kv0 + 128].reshape(bsz, wp, N_KV, HEAD_DIM))
        outs["vp"].append(zp3[:, t_len - wp:, kv0 + 128:kv0 + 256].reshape(bsz, wp, N_KV, HEAD_DIM))
        outs["cp"].append(zp3[:, t_len - (CONV_W - 1):, :D_RNN])
        outs["hp"].append(hp.reshape(bsz, D_RNN))

        xs, zs = _in_proj(xs, row2(ln_in_g), row2(ln_in_b), w_in_b, l, first)
        ys, hs = _rnn_sample(zs, state_conv, state_h, rw)
        os_, new_caches = _attn_sample(zs[:, 2 * D_RNN:kv0], zs[:, kv0:kv0 + 128], zs[:, kv0 + 128:], ck, cv,
                                       new_caches, l, bias_sample, attn_sinks)
        outs["cs"].append(jnp.concatenate([state_conv[l][:, 1:], zs[:, None, :D_RNN]], axis=1))
        outs["hs"].append(hs)

        x1p, rrow_p, rcol_p, cnt_p = _out_proj(xp, yp.reshape(n_p, D_RNN), op, lw)
        x1s, rrow_s, rcol_s, cnt_s = _out_proj(xs, ys, os_, lw)

        tot_p = jnp.sum(cnt_p, axis=(0, 1))
        tot = tot_p + jnp.sum(cnt_s, axis=(0, 1))
        geom, base_p = _block_geometry(tot, n_blocks)
        base_s = base_p + tot_p

        xbuf = _dispatch(x1p, rrow_p, cnt_p, base_p, None, n_blocks * MOE_TILE)
        xbuf = _dispatch(x1s, rrow_s, cnt_s, base_s, xbuf, n_blocks * MOE_TILE)
        ybuf = _moe_experts(xbuf, geom, moe_w1, moe_b1, moe_w2, moe_b2, l)
        xp = _final(x1p, pp, l, rcol_p, cnt_p, base_p, ybuf, lw)
        xs = _final(x1s, ps, l, rcol_s, cnt_s, base_s, ybuf, lw)

    st = lambda k: jnp.stack(outs[k])
    return (xp.reshape(bsz, t_len, D_MODEL), xs.reshape(n_s, 1, D_MODEL),
            st("kp"), st("vp"), st("cp"), st("hp"), to_position_major(new_caches[0]),
            to_position_major(new_caches[1]), st("cs"), st("hs"))
```

```python
import functools
import math

import jax
import jax.numpy as jnp
from jax import lax
from jax.experimental import pallas as pl
from jax.experimental.pallas import tpu as pltpu

F32 = jnp.float32
BF16 = jnp.bfloat16
I32 = jnp.int32
U32 = jnp.uint32

D_MODEL = 1024
D_RNN = 512
RNN_BLOCKS = 8
CONV_W = 4
LRU_C = 8.0
N_HEADS = 8
HEAD_DIM = 64
N_KV = 2
GQA_R = N_HEADS // N_KV
D_ATTN = N_HEADS * HEAD_DIM
WINDOW = 128
N_BUCKETS = 32
MAX_DIST = 128
N_EXPERTS = 32
TOP_K = 4
D_FF = 1024
SWIGLU_LIMIT = 7.0
SWIGLU_ALPHA = 1.702
D_PLE = 256
LN_EPS = 1e-5
D_IN = 2 * D_RNN + D_ATTN + 2 * N_KV * HEAD_DIM
NEG = -1e30

SUBLANES = 8
LANES = 128
ROW_TILE = 512
MOE_TILE = 1024
D_PACK = D_MODEL // 2
VMEM_LIMIT = 56 * 1024 * 1024
HI_MASK = 0xFFFF0000


def _cparams(*sem):
    return pltpu.CompilerParams(dimension_semantics=sem, vmem_limit_bytes=VMEM_LIMIT)


def _layer_norm(x, g, b):
    mu = jnp.mean(x, axis=-1, keepdims=True)
    xc = x - mu
    var = jnp.mean(xc * xc, axis=-1, keepdims=True)
    return xc * lax.rsqrt(var + LN_EPS) * g + b


def _rms_norm(x, g):
    return x * lax.rsqrt(jnp.mean(x * x, axis=-1, keepdims=True) + LN_EPS) * g


def _const_spec(shape):
    return pl.BlockSpec(shape, lambda *_: (0,) * len(shape))


def _layer_spec(layer, *tail):
    return pl.BlockSpec((None,) + tail, lambda *_: (layer,) + (0,) * len(tail))


def _in_proj_kernel(x_ref, g_ref, b_ref, w_ref, *out_refs, apply_ln):
    x = x_ref[...]
    if apply_ln:
        x = _layer_norm(x, g_ref[...], b_ref[...])
        out_refs[0][...] = x
    out_refs[-1][...] = jnp.dot(x.astype(BF16), w_ref[...], preferred_element_type=F32)


def _in_proj(x, ln_g, ln_b, w_in_bf16, layer, apply_ln):
    n = x.shape[0]
    tm = min(ROW_TILE, n)
    row = lambda w: pl.BlockSpec((tm, w), lambda i: (i, 0))
    out_shape = [jax.ShapeDtypeStruct((n, D_IN), F32)]
    out_specs = [row(D_IN)]
    if apply_ln:
        out_shape.insert(0, jax.ShapeDtypeStruct((n, D_MODEL), F32))
        out_specs.insert(0, row(D_MODEL))
    outs = pl.pallas_call(
        functools.partial(_in_proj_kernel, apply_ln=apply_ln),
        grid=(n // tm,),
        in_specs=[row(D_MODEL), _const_spec((1, D_MODEL)), _const_spec((1, D_MODEL)),
                  _layer_spec(layer, D_MODEL, D_IN)],
        out_specs=out_specs, out_shape=out_shape,
        compiler_params=_cparams("parallel"), name="in_proj",
    )(x, ln_g, ln_b, w_in_bf16)
    return (outs[0], outs[1]) if apply_ln else (x, outs[0])


def _rglru_coeffs(u, wa_ref, ba_ref, wx_ref, bx_ref, lam_ref):
    ub = u.astype(BF16)
    r = jax.nn.sigmoid(jnp.dot(ub, wa_ref[...], preferred_element_type=F32) + ba_ref[...])
    i = jax.nn.sigmoid(jnp.dot(ub, wx_ref[...], preferred_element_type=F32) + bx_ref[...])
    lam = -lam_ref[...]
    softplus = jnp.maximum(lam, 0.0) + jnp.log1p(jnp.exp(-jnp.abs(lam)))
    log_a = -LRU_C * r * softplus
    a = jnp.exp(log_a)
    b = jnp.sqrt(-jnp.tanh(log_a) * (a * a + 1.0)) * (i * u)
    return a, b


def _rnn_prompt_kernel(xr_ref, gate_ref, cw_ref, cb_ref, wa_ref, ba_ref, wx_ref, bx_ref, lam_ref,
                       y_ref, hlast_ref, xs_ref, a_ref, b_ref, carry_ref, *, tt):
    t = pl.program_id(1)

    @pl.when(t == 0)
    def _():
        xs_ref[0:SUBLANES, :] = jnp.zeros((SUBLANES, D_RNN), F32)
        carry_ref[...] = jnp.zeros((SUBLANES, D_RNN), F32)

    x = xr_ref[...]
    xs_ref[SUBLANES:SUBLANES + tt, :] = x
    cw = cw_ref[...]
    u = cb_ref[...] + x * cw[3:4]
    for j in range(CONV_W - 1):
        off = SUBLANES - (CONV_W - 1) + j
        u = u + xs_ref[off:off + tt, :] * cw[j:j + 1]
    xs_ref[0:SUBLANES, :] = x[tt - SUBLANES:tt, :]

    a, b = _rglru_coeffs(u, wa_ref, ba_ref, wx_ref, bx_ref, lam_ref)
    ng = tt // SUBLANES
    a = a.reshape(ng, SUBLANES, D_RNN)
    b = b.reshape(ng, SUBLANES, D_RNN)
    row = lax.broadcasted_iota(I32, (1, SUBLANES, 1), 1)
    for s in (1, 2, 4):
        a_sh = pltpu.roll(a, s, 1)
        b_sh = pltpu.roll(b, s, 1)
        m = row >= s
        b = jnp.where(m, a * b_sh + b, b)
        a = jnp.where(m, a * a_sh, a)
    a_ref[...] = a
    b_ref[...] = b

    def body(g, carry):
        h = a_ref[g] * carry + b_ref[g]
        b_ref[g] = h
        return jnp.broadcast_to(h[SUBLANES - 1:SUBLANES, :], (SUBLANES, D_RNN))

    carry = lax.fori_loop(0, ng, body, carry_ref[...])
    carry_ref[...] = carry
    h = b_ref[...].reshape(tt, D_RNN)
    y_ref[...] = h * jax.nn.gelu(gate_ref[...])
    hlast_ref[...] = carry[0:1, :]


def _rnn_prompt(z3, rw):
    bsz, t_len, _ = z3.shape
    tt = min(ROW_TILE, t_len)
    vec = _layer_spec(rw["layer"], 1, D_RNN)
    mat = _layer_spec(rw["layer"], D_RNN, D_RNN)
    return pl.pallas_call(
        functools.partial(_rnn_prompt_kernel, tt=tt),
        grid=(bsz, t_len // tt),
        in_specs=[pl.BlockSpec((None, tt, D_RNN), lambda b, t: (b, t, 0)),
                  pl.BlockSpec((None, tt, D_RNN), lambda b, t: (b, t, 1)),
                  _layer_spec(rw["layer"], CONV_W, D_RNN), vec, mat, vec, mat, vec, vec],
        out_specs=[pl.BlockSpec((None, tt, D_RNN), lambda b, t: (b, t, 0)),
                   pl.BlockSpec((None, 1, D_RNN), lambda b, t: (b, 0, 0))],
        out_shape=[jax.ShapeDtypeStruct((bsz, t_len, D_RNN), F32),
                   jax.ShapeDtypeStruct((bsz, 1, D_RNN), F32)],
        scratch_shapes=[pltpu.VMEM((tt + SUBLANES, D_RNN), F32),
                        pltpu.VMEM((tt // SUBLANES, SUBLANES, D_RNN), F32),
                        pltpu.VMEM((tt // SUBLANES, SUBLANES, D_RNN), F32),
                        pltpu.VMEM((SUBLANES, D_RNN), F32)],
        compiler_params=_cparams("parallel", "arbitrary"), name="rnn_prompt",
    )(z3, z3, rw["conv_w"], rw["conv_b"], rw["wa"], rw["ba"], rw["wx"], rw["bx"], rw["lam"])


def _rnn_sample_kernel(xr_ref, gate_ref, c0_ref, c1_ref, c2_ref, h0_ref, cw_ref, cb_ref, wa_ref, ba_ref,
                       wx_ref, bx_ref, lam_ref, y_ref, h_ref):
    x = xr_ref[...]
    cw = cw_ref[...]
    u = (cb_ref[...] + c0_ref[...] * cw[0:1] + c1_ref[...] * cw[1:2] + c2_ref[...] * cw[2:3]
         + x * cw[3:4])
    a, b = _rglru_coeffs(u, wa_ref, ba_ref, wx_ref, bx_ref, lam_ref)
    h = a * h0_ref[...] + b
    h_ref[...] = h
    y_ref[...] = h * jax.nn.gelu(gate_ref[...])


def _rnn_sample(z, conv_state, h0, rw):
    n = z.shape[0]
    layer = rw["layer"]
    vec = _layer_spec(layer, 1, D_RNN)
    mat = _layer_spec(layer, D_RNN, D_RNN)
    full = _const_spec((n, D_RNN))
    depth = conv_state.shape[0]
    conv_flat = conv_state.reshape(depth, n, (CONV_W - 1) * D_RNN)
    tap = lambda j: pl.BlockSpec((None, n, D_RNN), lambda i: (layer, 0, j))
    return pl.pallas_call(
        _rnn_sample_kernel,
        grid=(1,),
        in_specs=[pl.BlockSpec((n, D_RNN), lambda i: (0, 0)), pl.BlockSpec((n, D_RNN), lambda i: (0, 1)),
                  tap(0), tap(1), tap(2), _layer_spec(layer, n, D_RNN),
                  _layer_spec(layer, CONV_W, D_RNN), vec, mat, vec, mat, vec, vec],
        out_specs=[full, full],
        out_shape=[jax.ShapeDtypeStruct((n, D_RNN), F32)] * 2,
        compiler_params=_cparams("arbitrary"), name="rnn_sample",
    )(z, z, conv_flat, conv_flat, conv_flat, h0, rw["conv_w"], rw["conv_b"], rw["wa"], rw["ba"], rw["wx"], rw["bx"], rw["lam"])


def _rel_bucket(dist):
    n = jnp.maximum(dist, 0)
    max_exact = N_BUCKETS // 2
    nf = jnp.maximum(n, max_exact).astype(F32)
    large = max_exact + (jnp.log(nf / max_exact) / math.log(MAX_DIST / max_exact)
                         * (N_BUCKETS - max_exact)).astype(I32)
    large = jnp.minimum(large, N_BUCKETS - 1)
    return jnp.where(n < max_exact, n, large)


def _bias_lookup(rel_bias, dist):
    hot = (_rel_bucket(dist)[..., None] == jnp.arange(N_BUCKETS)).astype(F32)
    out = jnp.tensordot(hot, rel_bias.astype(F32), axes=1, precision=lax.Precision.HIGHEST)
    return jnp.moveaxis(out, -1, 0)


def _half_lane_variants(x, group):
    lo = lax.broadcasted_iota(I32, (1, 2 * HEAD_DIM), 1) < HEAD_DIM
    xr = pltpu.roll(x, HEAD_DIM, 1)
    zero = jnp.zeros_like(x)
    if group == 0:
        return jnp.where(lo, x, zero), jnp.where(lo, zero, xr)
    return jnp.where(lo, xr, zero), jnp.where(lo, zero, x)


def _attn_prompt_kernel(q_ref, kc_ref, kp_ref, vc_ref, vp_ref, bias_ref, sink_ref, o_ref, *, nq, layer):
    first = pl.program_id(1) == 0
    col = lax.broadcasted_iota(I32, (1, 2 * WINDOW), 1)
    hide_prev = jnp.logical_and(first, col < WINDOW)
    for j in range(nq):
        rows = slice(j * WINDOW, (j + 1) * WINDOW)
        before = slice((j - 1) * WINDOW, j * WINDOW)
        kk = jnp.concatenate([kp_ref[...] if j == 0 else kc_ref[before, :], kc_ref[rows, :]], axis=0)
        vv = jnp.concatenate([vp_ref[...] if j == 0 else vc_ref[before, :], vc_ref[rows, :]], axis=0)
        for g in range(N_KV):
            k_lo, k_hi = _half_lane_variants(kk, g)
            v_lo, v_hi = _half_lane_variants(vv, g)
            for pair in range(GQA_R // 2):
                p0 = g * (GQA_R // 2) + pair
                qp = (q_ref[rows, p0 * 128:(p0 + 1) * 128] * (HEAD_DIM ** -0.5)).astype(BF16)
                acc = jnp.zeros((WINDOW, 2 * HEAD_DIM), F32)
                for kx, vx, h in ((k_lo, v_lo, 2 * p0), (k_hi, v_hi, 2 * p0 + 1)):
                    s = lax.dot_general(qp, kx.astype(BF16), (((1,), (1,)), ((), ())),
                                        preferred_element_type=F32) + bias_ref[h]
                    if j == 0:
                        s = jnp.where(hide_prev, NEG, s)
                    sink = sink_ref[layer, h]
                    m = jnp.maximum(jnp.max(s, axis=-1, keepdims=True), sink)
                    p = jnp.exp(s - m)
                    denom = jnp.sum(p, axis=-1, keepdims=True) + jnp.exp(sink - m)
                    acc = acc + jnp.dot(p.astype(BF16), vx.astype(BF16), preferred_element_type=F32) / denom
                o_ref[rows, p0 * 128:(p0 + 1) * 128] = acc


def _attn_prompt(z, bsz, t_len, bias_prompt, sinks, layer):
    nb = t_len // WINDOW
    nq = min(4, nb)
    ns = nb // nq
    kcol = (2 * D_RNN + D_ATTN) // 128
    cur = lambda c: pl.BlockSpec((nq * WINDOW, 128), lambda b, i: (b * ns + i, c))
    prev = lambda c: pl.BlockSpec((WINDOW, 128), lambda b, i: (b * nb + jnp.maximum(i * nq - 1, 0), c))
    return pl.pallas_call(
        functools.partial(_attn_prompt_kernel, nq=nq, layer=layer),
        grid=(bsz, ns),
        in_specs=[pl.BlockSpec((nq * WINDOW, D_ATTN), lambda b, i: (b * ns + i, 2 * D_RNN // D_ATTN)),
                  cur(kcol), prev(kcol), cur(kcol + 1), prev(kcol + 1),
                  _const_spec((N_HEADS, WINDOW, 2 * WINDOW)),
                  pl.BlockSpec(memory_space=pltpu.SMEM)],
        out_specs=pl.BlockSpec((nq * WINDOW, D_ATTN), lambda b, i: (b * ns + i, 0)),
        out_shape=jax.ShapeDtypeStruct((bsz * t_len, D_ATTN), F32),
        compiler_params=_cparams("parallel", "arbitrary"), name="attn_prompt",
    )(z, z, z, z, z, bias_prompt, sinks)


def _attn_sample_kernel(q_ref, kn_ref, vn_ref, ck_ref, cv_ref, bias_ref, sink_ref, hmask_ref, *rest):
    o_ref, nk_ref, nv_ref = rest[-3:]
    pos = lax.broadcasted_iota(I32, (1, 1, WINDOW), 2)
    last = pos == WINDOW - 1
    bb = ck_ref.shape[0]
    newk = jnp.where(last, jnp.broadcast_to(kn_ref[...], (bb, 128, WINDOW)), pltpu.roll(ck_ref[...], WINDOW - 1, 2))
    newv = jnp.where(last, jnp.broadcast_to(vn_ref[...], (bb, 128, WINDOW)), pltpu.roll(cv_ref[...], WINDOW - 1, 2))
    nk_ref[...] = newk
    nv_ref[...] = newv
    s = lax.dot_general(q_ref[...].astype(BF16), newk.astype(BF16), (((2,), (1,)), ((0,), (0,))),
                        preferred_element_type=F32)
    s = s + bias_ref[...][None]
    sink = sink_ref[...][None]
    m = jnp.maximum(jnp.max(s, axis=-1, keepdims=True), sink)
    p = jnp.exp(s - m)
    denom = jnp.sum(p, axis=-1, keepdims=True) + jnp.exp(sink - m)
    o = lax.dot_general(p.astype(BF16), newv.astype(BF16), (((2,), (2,)), ((0,), (0,))),
                        preferred_element_type=F32)
    o_ref[...] = o / denom * hmask_ref[...][None]


def _attn_sample(q, k_new, v_new, k_cache, v_cache, new_caches, layer, bias_sample, sinks):
    n = q.shape[0]
    depth = k_cache.shape[0]
    bb = 16
    qh = q.reshape(n, N_KV, GQA_R, HEAD_DIM) * (HEAD_DIM ** -0.5)
    eye = jnp.eye(N_KV, dtype=F32)
    q8 = (qh[:, :, :, None, :] * eye[None, :, None, :, None]).reshape(n, N_HEADS, N_KV * HEAD_DIM)
    hmask = jnp.repeat(jnp.repeat(eye, GQA_R, axis=0), HEAD_DIM, axis=1)
    blk3 = lambda a, b: pl.BlockSpec((bb, a, b), lambda i: (i, 0, 0))
    cache = pl.BlockSpec((None, bb, 128, WINDOW), lambda i: (layer, i, 0, 0))
    in_specs = [blk3(N_HEADS, 128), blk3(128, 1), blk3(128, 1), cache, cache,
                _const_spec((N_HEADS, WINDOW)), _layer_spec(layer, N_HEADS, 1), _const_spec((N_HEADS, 128))]
    args = [q8, k_new.reshape(n, 128, 1), v_new.reshape(n, 128, 1), k_cache, v_cache, bias_sample,
            sinks.reshape(-1, N_HEADS, 1), hmask]
    aliases = {}
    if new_caches is not None:
        in_specs += [pl.BlockSpec(memory_space=pl.ANY)] * 2
        args += list(new_caches)
        aliases = {len(args) - 2: 1, len(args) - 1: 2}
    o8, nk, nv = pl.pallas_call(
        _attn_sample_kernel,
        grid=(n // bb,),
        in_specs=in_specs,
        out_specs=[blk3(N_HEADS, 128), cache, cache],
        out_shape=[jax.ShapeDtypeStruct((n, N_HEADS, 128), F32),
                   jax.ShapeDtypeStruct((depth, n, 128, WINDOW), F32),
                   jax.ShapeDtypeStruct((depth, n, 128, WINDOW), F32)],
        input_output_aliases=aliases,
        compiler_params=_cparams("arbitrary"), name="attn_sample",
    )(*args)
    o = o8.reshape(n, N_KV, GQA_R, N_KV, HEAD_DIM)
    o = jnp.stack([o[:, g, :, g, :] for g in range(N_KV)], axis=1).reshape(n, D_ATTN)
    return o, (nk, nv)


ONE_HOT_BLOCK = 256


def _sorted_rows(tm):
    return TOP_K * tm + N_EXPERTS * SUBLANES


def _out_proj_kernel(x_ref, yr_ref, o_ref, gnr_ref, gna_ref, w_ref, g1_ref, b1_ref, rwt_ref, rb_ref,
                     x1_ref, rrow_ref, rcol_ref, cnt_ref, *, tm, alpha):
    yn = _rms_norm(yr_ref[...], gnr_ref[...]).astype(BF16)
    on = _rms_norm(o_ref[...], gna_ref[...]).astype(BF16)
    mix = jnp.dot(jnp.concatenate([yn, on], axis=1), w_ref[...], preferred_element_type=F32)
    x1 = _layer_norm(alpha * x_ref[...] + mix, g1_ref[...], b1_ref[...])
    x1_ref[...] = x1

    x1h = x1.astype(BF16)
    x1l = (x1 - x1h.astype(F32)).astype(BF16)
    nt_dims = (((1,), (1,)), ((), ()))
    by_hi = lax.dot_general(rwt_ref[...], x1h, nt_dims, preferred_element_type=F32)
    by_lo = lax.dot_general(rwt_ref[0:N_EXPERTS, :], x1l, nt_dims, preferred_element_type=F32)
    logits = by_hi[0:N_EXPERTS] + by_hi[N_EXPERTS:] + by_lo + rb_ref[...]
    eidx = lax.broadcasted_iota(I32, (N_EXPERTS, tm), 0).astype(F32)
    work = logits
    vals, hots = [], []
    for _ in range(TOP_K):
        v = jnp.max(work, axis=0, keepdims=True)
        idx = jnp.min(jnp.where(work == v, eidx, float(N_EXPERTS)), axis=0, keepdims=True)
        hot = eidx == idx
        work = jnp.where(hot, -jnp.inf, work)
        vals.append(v)
        hots.append(hot.astype(F32))
    ex = [jnp.exp(v - vals[0]) for v in vals]
    tot = ex[0] + ex[1] + ex[2] + ex[3]
    tok_hot = hots[0] + hots[1] + hots[2] + hots[3]
    s_i = lax.broadcasted_iota(I32, (tm, tm), 0)
    t_i = lax.broadcasted_iota(I32, (tm, tm), 1)
    earlier = (s_i < t_i).astype(BF16)
    before = jnp.dot(tok_hot.astype(BF16), earlier, preferred_element_type=F32)
    cnt = jnp.sum(tok_hot, axis=1, keepdims=True)
    cnt8 = jnp.floor((cnt + (SUBLANES - 1)) * (1.0 / SUBLANES)) * SUBLANES
    cnt8_b = jnp.broadcast_to(cnt8, (N_EXPERTS, LANES))
    e_r = lax.broadcasted_iota(I32, (N_EXPERTS, N_EXPERTS), 0)
    e_c = lax.broadcasted_iota(I32, (N_EXPERTS, N_EXPERTS), 1)
    start = jnp.dot((e_c < e_r).astype(F32), cnt8_b, preferred_element_type=F32,
                    precision=lax.Precision.HIGHEST)[:, 0:1]
    slot = start + before
    rows = [jnp.sum(h * slot, axis=0, keepdims=True) for h in hots] + [e / tot for e in ex]
    rrow = jnp.concatenate(rows, axis=0)
    rrow_ref[...] = rrow
    rcol_ref[...] = rrow.T
    cnt_ref[...] = cnt8_b.T[0:1, 0:N_EXPERTS].astype(I32)


def _out_proj(x, y_rnn, o, lw):
    n = x.shape[0]
    tm = min(ROW_TILE, n)
    nt = n // tm
    row = lambda w: pl.BlockSpec((tm, w), lambda i: (i, 0))
    layer = lw["layer"]
    vec = lambda w: _layer_spec(layer, 1, w)
    return pl.pallas_call(
        functools.partial(_out_proj_kernel, tm=tm, alpha=lw["alpha"]),
        grid=(nt,),
        in_specs=[row(D_MODEL), row(D_RNN), row(D_ATTN), vec(D_RNN), vec(D_ATTN),
                  _layer_spec(layer, D_MODEL, D_MODEL), vec(D_MODEL), vec(D_MODEL),
                  _layer_spec(layer, 2 * N_EXPERTS, D_MODEL), _layer_spec(layer, N_EXPERTS, 1)],
        out_specs=[row(D_MODEL), pl.BlockSpec((None, 2 * TOP_K, tm), lambda i: (i, 0, 0)), row(2 * TOP_K),
                   pl.BlockSpec((None, 1, N_EXPERTS), lambda i: (i, 0, 0))],
        out_shape=[jax.ShapeDtypeStruct((n, D_MODEL), F32), jax.ShapeDtypeStruct((nt, 2 * TOP_K, tm), F32),
                   jax.ShapeDtypeStruct((n, 2 * TOP_K), F32), jax.ShapeDtypeStruct((nt, 1, N_EXPERTS), I32)],
        compiler_params=_cparams("parallel"), name="out_proj_route",
    )(x, y_rnn, o, lw["gn_rnn"], lw["gn_attn"], lw["w_out"], lw["ln1_g"], lw["ln1_b"],
      lw["router_wt"], lw["router_b"])


RUN_CHUNK = 64
_SMALL_PIECES = (32, 16, 8)
_MAX_PAD_ROWS = N_EXPERTS * SUBLANES
_PAD_PIECES = (128, 64, 32, 16, 8)


def _for_each_piece(n8, fn):
    def chunk(j, c):
        fn(pl.multiple_of(j * RUN_CHUNK, RUN_CHUNK), RUN_CHUNK)
        return c
    lax.fori_loop(0, n8 // RUN_CHUNK, chunk, 0)
    for s in _SMALL_PIECES:
        @pl.when((n8 & s) != 0)
        def _(s=s):
            fn(n8 & (-2 * s), s)


def _wait_tile_rows(n_rows, tm, wait_rows):
    wait_rows(TOP_K * tm)
    pad = n_rows - TOP_K * tm
    for s in _PAD_PIECES:
        @pl.when((pad & s) != 0)
        def _(s=s):
            wait_rows(s)


def _dispatch_kernel(cnt_ref, base_ref, x_ref, rrow_ref, *rest, tm):
    xbuf_ref, xs_ref, carry_ref, nrows_ref, sems = rest[-5:]
    nrow = _sorted_rows(tm)
    i = pl.program_id(0)
    slot = i % 2

    @pl.when(i == 0)
    def _():
        def init(e, c):
            carry_ref[e] = base_ref[e]
            return c
        lax.fori_loop(0, N_EXPERTS, init, 0)

    r = lax.broadcasted_iota(I32, (ONE_HOT_BLOCK, 1), 0).astype(F32).astype(BF16)
    blocks = []
    for c in range(nrow // ONE_HOT_BLOCK):
        hit = None
        for k in range(TOP_K):
            h = r == (rrow_ref[k:k + 1, :] - float(c * ONE_HOT_BLOCK)).astype(BF16)
            hit = h if hit is None else jnp.logical_or(hit, h)
        blocks.append(jnp.where(hit, jnp.ones((), BF16), jnp.zeros((), BF16)))
    perm = jnp.concatenate(blocks, axis=0)
    xs = jnp.dot(perm, x_ref[...].astype(BF16), preferred_element_type=F32)
    lo = lax.bitcast_convert_type(xs[:, :D_PACK], U32)
    hi = lax.bitcast_convert_type(xs[:, D_PACK:], U32)
    xs_ref[slot] = (hi & U32(HI_MASK)) | (lo >> 16)

    def run_copy(buf, src, dst, s):
        return pltpu.make_async_copy(xs_ref.at[buf, pl.ds(pl.multiple_of(src, SUBLANES), s)],
                                     xbuf_ref.at[pl.ds(pl.multiple_of(dst, SUBLANES), s)], sems.at[buf])

    def start(e, src):
        n8 = cnt_ref[0, e]
        dst = carry_ref[e]
        _for_each_piece(n8, lambda off, s: run_copy(slot, src + off, dst + off, s).start())
        carry_ref[e] = dst + n8
        return src + n8

    def wait_tile(buf):
        _wait_tile_rows(nrows_ref[buf], tm, lambda s: run_copy(buf, 0, 0, s).wait())

    nrows_ref[slot] = lax.fori_loop(0, N_EXPERTS, start, 0)

    @pl.when(i > 0)
    def _():
        wait_tile(1 - slot)

    @pl.when(i == pl.num_programs(0) - 1)
    def _():
        wait_tile(slot)


def _dispatch(x1, rrow, cnt8, base, xbuf, n_rows):
    n = x1.shape[0]
    nt, _, tm = rrow.shape
    in_specs = [pl.BlockSpec((None, 1, N_EXPERTS), lambda i: (i, 0, 0), memory_space=pltpu.SMEM),
                pl.BlockSpec(memory_space=pltpu.SMEM),
                pl.BlockSpec((tm, D_MODEL), lambda i: (i, 0)),
                pl.BlockSpec((None, 2 * TOP_K, tm), lambda i: (i, 0, 0))]
    args = [cnt8, base, x1, rrow]
    aliases = {}
    if xbuf is not None:
        in_specs.append(pl.BlockSpec(memory_space=pl.ANY))
        args.append(xbuf)
        aliases = {len(args) - 1: 0}
    return pl.pallas_call(
        functools.partial(_dispatch_kernel, tm=tm),
        grid=(nt,),
        in_specs=in_specs,
        out_specs=pl.BlockSpec(memory_space=pl.ANY),
        out_shape=jax.ShapeDtypeStruct((n_rows, D_PACK), U32),
        scratch_shapes=[pltpu.VMEM((2, _sorted_rows(tm), D_PACK), U32), pltpu.SMEM((N_EXPERTS,), I32),
                        pltpu.SMEM((2,), I32), pltpu.SemaphoreType.DMA((2,))],
        input_output_aliases=aliases,
        compiler_params=_cparams("arbitrary"), name="moe_dispatch",
    )(*args)


def _moe_kernel(br_ref, be_ref, nv_ref, first_ref, next_ref, x_ref, w1_hbm, b1_ref, w2_hbm, b2_ref, y_ref,
                w1f_ref, w2f_ref, w1s_ref, w2s_ref, sems, *, layer):
    i = pl.program_id(0)

    def load(e):
        return (pltpu.make_async_copy(w1_hbm.at[layer, e], w1f_ref, sems.at[0]),
                pltpu.make_async_copy(w2_hbm.at[layer, e], w2f_ref, sems.at[1]))

    @pl.when(jnp.logical_and(i == 0, first_ref[0] == 1))
    def _():
        for c in load(be_ref[0]):
            c.start()

    @pl.when(first_ref[i] == 1)
    def _():
        for c in load(be_ref[i]):
            c.wait()
        w1s_ref[...] = w1f_ref[...].astype(BF16)
        w2s_ref[...] = w2f_ref[...].astype(BF16)

        @pl.when(next_ref[i] >= 0)
        def _():
            for c in load(next_ref[i]):
                c.start()

    nv = nv_ref[i]

    def expert_rows(m):
        rows = lax.broadcasted_iota(I32, (m, 1), 0)
        xw = jnp.where(rows < nv, x_ref[0:m, :], U32(0))
        lo = lax.bitcast_convert_type(xw << 16, F32).astype(BF16)
        hi = lax.bitcast_convert_type(xw & U32(HI_MASK), F32).astype(BF16)
        xb = jnp.concatenate([lo, hi], axis=1)
        hdn = jnp.dot(xb, w1s_ref[...], preferred_element_type=F32) + b1_ref[...]
        g = jnp.minimum(hdn[:, :D_FF], SWIGLU_LIMIT)
        lin = jnp.clip(hdn[:, D_FF:], -SWIGLU_LIMIT, SWIGLU_LIMIT)
        act = g * jax.nn.sigmoid(SWIGLU_ALPHA * g) * (lin + 1.0)
        y_ref[0:m, :] = jnp.dot(act.astype(BF16), w2s_ref[...], preferred_element_type=F32) + b2_ref[...]

    quarter = MOE_TILE // 4
    for q in range(1, 5):
        @pl.when(jnp.logical_and(nv > (q - 1) * quarter, nv <= q * quarter))
        def _(q=q):
            expert_rows(q * quarter)


def _moe_experts(xbuf, geom, w1, b1, w2, b2, layer):
    n_blocks = xbuf.shape[0] // MOE_TILE
    bias = lambda w: pl.BlockSpec((None, None, 1, w), lambda i, br, be, *_: (layer, be[i], 0, 0))
    grid_spec = pltpu.PrefetchScalarGridSpec(
        num_scalar_prefetch=5,
        grid=(n_blocks,),
        in_specs=[pl.BlockSpec((MOE_TILE, D_PACK), lambda i, br, *_: (br[i], 0)),
                  pl.BlockSpec(memory_space=pl.ANY), bias(2 * D_FF),
                  pl.BlockSpec(memory_space=pl.ANY), bias(D_MODEL)],
        out_specs=pl.BlockSpec((MOE_TILE, D_MODEL), lambda i, br, *_: (br[i], 0)),
        scratch_shapes=[pltpu.VMEM((D_MODEL, 2 * D_FF), F32), pltpu.VMEM((D_FF, D_MODEL), F32),
                        pltpu.VMEM((D_MODEL, 2 * D_FF), BF16), pltpu.VMEM((D_FF, D_MODEL), BF16),
                        pltpu.SemaphoreType.DMA((2,))],
    )
    depth = w1.shape[0]
    return pl.pallas_call(
        functools.partial(_moe_kernel, layer=layer), grid_spec=grid_spec,
        out_shape=jax.ShapeDtypeStruct((xbuf.shape[0], D_MODEL), F32),
        compiler_params=_cparams("arbitrary"), name="moe_experts",
    )(*geom, xbuf, w1, b1.reshape(depth, N_EXPERTS, 1, 2 * D_FF), w2, b2.reshape(depth, N_EXPERTS, 1, D_MODEL))


def _final_kernel(cnt_ref, cnt_next_ref, base_ref, x_ref, p_ref, rcol_ref, wg_ref, bg_ref, wp_ref, g2_ref, b2_ref,
                  ybuf_ref, out_ref, ys_ref, carry_ref, nrows_ref, sems, *, tm, alpha):
    nrow = _sorted_rows(tm)
    i = pl.program_id(0)
    slot = i % 2

    def run_copy(buf, src, dst, s):
        return pltpu.make_async_copy(ybuf_ref.at[pl.ds(pl.multiple_of(src, SUBLANES), s)],
                                     ys_ref.at[buf, pl.ds(pl.multiple_of(dst, SUBLANES), s)], sems.at[buf])

    def gather_tile(counts_ref, buf):
        def start(e, dst):
            n8 = counts_ref[0, e]
            src = carry_ref[e]
            _for_each_piece(n8, lambda off, s: run_copy(buf, src + off, dst + off, s).start())
            carry_ref[e] = src + n8
            return dst + n8
        nrows_ref[buf] = lax.fori_loop(0, N_EXPERTS, start, 0)

    @pl.when(i == 0)
    def _():
        def init(e, c):
            carry_ref[e] = base_ref[e]
            return c
        lax.fori_loop(0, N_EXPERTS, init, 0)
        ys_ref[:, TOP_K * tm:, :] = jnp.zeros((2, nrow - TOP_K * tm, D_MODEL), F32)
        gather_tile(cnt_ref, 0)

    @pl.when(i + 1 < pl.num_programs(0))
    def _():
        gather_tile(cnt_next_ref, 1 - slot)

    x = x_ref[...]
    ple = (jax.nn.sigmoid(jnp.dot(x.astype(BF16), wg_ref[...], preferred_element_type=F32) + bg_ref[...])
           * jnp.dot(p_ref[...].astype(BF16), wp_ref[...], preferred_element_type=F32))

    _wait_tile_rows(nrows_ref[slot], tm, lambda s: run_copy(slot, 0, 0, s).wait())

    ys = ys_ref[slot].astype(BF16)
    col = lax.broadcasted_iota(I32, (1, ONE_HOT_BLOCK), 1).astype(F32).astype(BF16)
    rc = rcol_ref[...]
    gates = rc[:, TOP_K:].astype(BF16)
    blocks = []
    for c in range(nrow // ONE_HOT_BLOCK):
        blk = jnp.zeros((tm, ONE_HOT_BLOCK), BF16)
        for k in range(TOP_K):
            hit = col == (rc[:, k:k + 1] - float(c * ONE_HOT_BLOCK)).astype(BF16)
            blk = jnp.where(hit, gates[:, k:k + 1], blk)
        blocks.append(blk)
    ffn = jnp.dot(jnp.concatenate(blocks, axis=1), ys, preferred_element_type=F32)
    out_ref[...] = _layer_norm(alpha * x + ffn + ple, g2_ref[...], b2_ref[...])


def _final(x1, p_all, layer, rcol, cnt8, base, ybuf, lw):
    n = x1.shape[0]
    nt = cnt8.shape[0]
    tm = n // nt
    row = lambda w: pl.BlockSpec((tm, w), lambda i: (i, 0))
    vec = lambda w: _layer_spec(layer, 1, w)
    return pl.pallas_call(
        functools.partial(_final_kernel, tm=tm, alpha=lw["alpha"]),
        grid=(nt,),
        in_specs=[pl.BlockSpec((None, 1, N_EXPERTS), lambda i: (i, 0, 0), memory_space=pltpu.SMEM),
                  pl.BlockSpec((None, 1, N_EXPERTS), lambda i: (jnp.minimum(i + 1, nt - 1), 0, 0),
                               memory_space=pltpu.SMEM),
                  pl.BlockSpec(memory_space=pltpu.SMEM),
                  row(D_MODEL), pl.BlockSpec((None, tm, D_PLE), lambda i: (layer, i, 0)), row(2 * TOP_K),
                  _layer_spec(layer, D_MODEL, D_MODEL), vec(D_MODEL), _layer_spec(layer, D_PLE, D_MODEL),
                  vec(D_MODEL), vec(D_MODEL), pl.BlockSpec(memory_space=pl.ANY)],
        out_specs=row(D_MODEL),
        out_shape=jax.ShapeDtypeStruct((n, D_MODEL), F32),
        scratch_shapes=[pltpu.VMEM((2, _sorted_rows(tm), D_MODEL), F32), pltpu.SMEM((N_EXPERTS,), I32),
                        pltpu.SMEM((2,), I32), pltpu.SemaphoreType.DMA((2,))],
        compiler_params=_cparams("arbitrary"), name="moe_combine_ln2",
    )(cnt8, cnt8, base, x1, p_all, rcol, lw["ple_w_gate"], lw["ple_b_gate"], lw["ple_w_proj"], lw["ln2_g"], lw["ln2_b"],
      ybuf)


def _block_diag(w):
    depth, nb, bw, _ = w.shape
    eye = jnp.eye(nb, dtype=w.dtype)
    return (w[:, :, :, None, :] * eye[None, :, None, :, None]).reshape(depth, nb * bw, nb * bw)


def _hi_lo_rows(w):
    hi = w.astype(BF16)
    lo = (w - hi.astype(F32)).astype(BF16)
    return jnp.concatenate([hi, lo], axis=-2)


def _block_geometry(tot, n_blocks):
    nblk = (tot + MOE_TILE - 1) // MOE_TILE
    blk_end = jnp.cumsum(nblk)
    blk_start = blk_end - nblk
    ids = jnp.arange(n_blocks, dtype=I32)
    n_valid = blk_end[-1]
    expert_at = lambda j: jnp.minimum(jnp.sum(j[:, None] >= blk_end[None, :], axis=1), N_EXPERTS - 1).astype(I32)
    src = jnp.clip(ids, 0, jnp.maximum(n_valid - 1, 0))
    e_of = expert_at(src)
    hot = e_of[:, None] == jnp.arange(N_EXPERTS)[None, :]
    pick = lambda v: jnp.sum(jnp.where(hot, v[None, :], 0), axis=1)
    start_of, end_of, tot_of = pick(blk_start), pick(blk_end), pick(tot)
    real = ids < n_valid
    nv = jnp.where(real, jnp.clip(tot_of - (ids - start_of) * MOE_TILE, 0, MOE_TILE), 0)
    first = jnp.logical_and(real, ids == start_of)
    nxt = jnp.where(end_of < n_valid, expert_at(end_of), -1)
    geom = tuple(v.astype(I32) for v in (src, e_of, nv, first, nxt))
    return geom, (blk_start * MOE_TILE).astype(I32)


def kernel(x_prompt, x_sample, cache_k, cache_v, state_conv, state_h, p_prompt, p_sample, ln_in_g, ln_in_b, rel_bias, w_in, conv_w, conv_b, rg_wa, rg_ba, rg_wx, rg_bx, rg_lambda, attn_sinks, gn_rnn, gn_attn, w_out, ln1_g, ln1_b, router_w, router_b, moe_w1, moe_b1, moe_w2, moe_b2, ple_w_gate, ple_b_gate, ple_w_proj, ln2_g, ln2_b):
    depth = w_in.shape[0]
    bsz, t_len, _ = x_prompt.shape
    n_p = bsz * t_len
    n_s = x_sample.shape[0]
    wb = cache_k.shape[2]
    alpha = (2 * depth) ** 0.25

    dist = jnp.arange(WINDOW)[:, None] + WINDOW - jnp.arange(2 * WINDOW)[None, :]
    valid = (dist >= 0) & (dist < WINDOW)
    bias_prompt = jnp.where(valid[None], _bias_lookup(rel_bias, dist), NEG)
    bias_sample = _bias_lookup(rel_bias, WINDOW - 1 - jnp.arange(WINDOW))

    tm_p = min(ROW_TILE, n_p)
    n_tiles = n_p // tm_p + 1
    max_rows = (n_p + n_s) * TOP_K + n_tiles * N_EXPERTS * (SUBLANES - 1)
    n_blocks = -(-(max_rows + N_EXPERTS * (MOE_TILE - 1)) // MOE_TILE)

    xp = x_prompt.reshape(n_p, D_MODEL)
    xs = x_sample.reshape(n_s, D_MODEL)
    pp = p_prompt.reshape(depth, n_p, D_PLE)
    ps = p_sample.reshape(depth, n_s, D_PLE)
    to_feature_major = lambda c: jnp.transpose(c, (0, 1, 3, 4, 2)).reshape(depth, n_s, N_KV * HEAD_DIM, wb)
    to_position_major = lambda c: jnp.transpose(c.reshape(depth, n_s, N_KV, HEAD_DIM, wb), (0, 1, 4, 2, 3))
    ck = to_feature_major(cache_k)
    cv = to_feature_major(cache_v)
    new_caches = None
    row2 = lambda v: v.reshape(1, -1)
    kv0 = 2 * D_RNN + D_ATTN
    outs = {k: [] for k in ("kp", "vp", "cp", "hp", "cs", "hs")}

    vecs = lambda v: v.reshape(depth, 1, -1)
    rw_all = {"conv_w": conv_w, "conv_b": vecs(conv_b), "wa": _block_diag(rg_wa).astype(BF16),
              "ba": vecs(rg_ba), "wx": _block_diag(rg_wx).astype(BF16), "bx": vecs(rg_bx),
              "lam": vecs(rg_lambda)}
    lw_all = {"gn_rnn": vecs(gn_rnn), "gn_attn": vecs(gn_attn), "w_out": w_out.astype(BF16),
              "ln1_g": vecs(ln1_g), "ln1_b": vecs(ln1_b),
              "router_wt": _hi_lo_rows(jnp.swapaxes(router_w, 1, 2)),
              "router_b": router_b.reshape(depth, N_EXPERTS, 1), "alpha": alpha,
              "ple_w_gate": ple_w_gate.astype(BF16), "ple_b_gate": vecs(ple_b_gate),
              "ple_w_proj": ple_w_proj.astype(BF16), "ln2_g": vecs(ln2_g), "ln2_b": vecs(ln2_b)}
    w_in_b = w_in.astype(BF16)

    for l in range(depth):
        rw = dict(rw_all, layer=l)
        lw = dict(lw_all, layer=l)
        first = l == 0

        xp, zp = _in_proj(xp, row2(ln_in_g), row2(ln_in_b), w_in_b, l, first)
        zp3 = zp.reshape(bsz, t_len, D_IN)
        yp, hp = _rnn_prompt(zp3, rw)
        op = _attn_prompt(zp, bsz, t_len, bias_prompt, attn_sinks, l)
        wp = min(WINDOW, t_len)
        outs["kp"].append(zp3[:, t_len - wp:, kv0:kv0 + 128].reshape(bsz, wp, N_KV, HEAD_DIM))
        outs["vp"].append(zp3[:, t_len - wp:, kv0 + 128:kv0 + 256].reshape(bsz, wp, N_KV, HEAD_DIM))
        outs["cp"].append(zp3[:, t_len - (CONV_W - 1):, :D_RNN])
        outs["hp"].append(hp.reshape(bsz, D_RNN))

        xs, zs = _in_proj(xs, row2(ln_in_g), row2(ln_in_b), w_in_b, l, first)
        ys, hs = _rnn_sample(zs, state_conv, state_h, rw)
        os_, new_caches = _attn_sample(zs[:, 2 * D_RNN:kv0], zs[:, kv0:kv0 + 128], zs[:, kv0 + 128:], ck, cv,
                                       new_caches, l, bias_sample, attn_sinks)
        outs["cs"].append(jnp.concatenate([state_conv[l][:, 1:], zs[:, None, :D_RNN]], axis=1))
        outs["hs"].append(hs)

        x1p, rrow_p, rcol_p, cnt_p = _out_proj(xp, yp.reshape(n_p, D_RNN), op, lw)
        x1s, rrow_s, rcol_s, cnt_s = _out_proj(xs, ys, os_, lw)

        tot_p = jnp.sum(cnt_p, axis=(0, 1))
        tot = tot_p + jnp.sum(cnt_s, axis=(0, 1))
        geom, base_p = _block_geometry(tot, n_blocks)
        base_s = base_p + tot_p

        xbuf = _dispatch(x1p, rrow_p, cnt_p, base_p, None, n_blocks * MOE_TILE)
        xbuf = _dispatch(x1s, rrow_s, cnt_s, base_s, xbuf, n_blocks * MOE_TILE)
        ybuf = _moe_experts(xbuf, geom, moe_w1, moe_b1, moe_w2, moe_b2, l)
        xp = _final(x1p, pp, l, rcol_p, cnt_p, base_p, ybuf, lw)
        xs = _final(x1s, ps, l, rcol_s, cnt_s, base_s, ybuf, lw)

    st = lambda k: jnp.stack(outs[k])
    return (xp.reshape(bsz, t_len, D_MODEL), xs.reshape(n_s, 1, D_MODEL),
            st("kp"), st("vp"), st("cp"), st("hp"), to_position_major(new_caches[0]),
            to_position_major(new_caches[1]), st("cs"), st("hs"))
```

```python
import functools
import math

import jax
import jax.numpy as jnp
from jax import lax
from jax.experimental import pallas as pl
from jax.experimental.pallas import tpu as pltpu

F32 = jnp.float32
BF16 = jnp.bfloat16
I32 = jnp.int32
U32 = jnp.uint32

D_MODEL = 1024
D_RNN = 512
RNN_BLOCKS = 8
CONV_W = 4
LRU_C = 8.0
N_HEADS = 8
HEAD_DIM = 64
N_KV = 2
GQA_R = N_HEADS // N_KV
D_ATTN = N_HEADS * HEAD_DIM
WINDOW = 128
N_BUCKETS = 32
MAX_DIST = 128
N_EXPERTS = 32
TOP_K = 4
D_FF = 1024
SWIGLU_LIMIT = 7.0
SWIGLU_ALPHA = 1.702
D_PLE = 256
LN_EPS = 1e-5
D_IN = 2 * D_RNN + D_ATTN + 2 * N_KV * HEAD_DIM
NEG = -1e30

SUBLANES = 8
LANES = 128
ROW_TILE = 512
MOE_TILE = 1024
D_PACK = D_MODEL // 2
VMEM_LIMIT = 56 * 1024 * 1024
HI_MASK = 0xFFFF0000


def _cparams(*sem):
    return pltpu.CompilerParams(dimension_semantics=sem, vmem_limit_bytes=VMEM_LIMIT)


def _layer_norm(x, g, b):
    mu = jnp.mean(x, axis=-1, keepdims=True)
    xc = x - mu
    var = jnp.mean(xc * xc, axis=-1, keepdims=True)
    return xc * lax.rsqrt(var + LN_EPS) * g + b


def _rms_norm(x, g):
    return x * lax.rsqrt(jnp.mean(x * x, axis=-1, keepdims=True) + LN_EPS) * g


def _const_spec(shape):
    return pl.BlockSpec(shape, lambda *_: (0,) * len(shape))


def _layer_spec(layer, *tail):
    return pl.BlockSpec((None,) + tail, lambda *_: (layer,) + (0,) * len(tail))


def _in_proj_kernel(x_ref, g_ref, b_ref, w_ref, *out_refs, apply_ln):
    x = x_ref[...]
    if apply_ln:
        x = _layer_norm(x, g_ref[...], b_ref[...])
        out_refs[0][...] = x
    out_refs[-1][...] = jnp.dot(x.astype(BF16), w_ref[...], preferred_element_type=F32)


def _in_proj(x, ln_g, ln_b, w_in_bf16, layer, apply_ln):
    n = x.shape[0]
    tm = min(ROW_TILE, n)
    row = lambda w: pl.BlockSpec((tm, w), lambda i: (i, 0))
    out_shape = [jax.ShapeDtypeStruct((n, D_IN), F32)]
    out_specs = [row(D_IN)]
    if apply_ln:
        out_shape.insert(0, jax.ShapeDtypeStruct((n, D_MODEL), F32))
        out_specs.insert(0, row(D_MODEL))
    outs = pl.pallas_call(
        functools.partial(_in_proj_kernel, apply_ln=apply_ln),
        grid=(n // tm,),
        in_specs=[row(D_MODEL), _const_spec((1, D_MODEL)), _const_spec((1, D_MODEL)),
                  _layer_spec(layer, D_MODEL, D_IN)],
        out_specs=out_specs, out_shape=out_shape,
        compiler_params=_cparams("parallel"), name="in_proj",
    )(x, ln_g, ln_b, w_in_bf16)
    return (outs[0], outs[1]) if apply_ln else (x, outs[0])


def _rglru_coeffs(u, wa_ref, ba_ref, wx_ref, bx_ref, lam_ref):
    ub = u.astype(BF16)
    r = jax.nn.sigmoid(jnp.dot(ub, wa_ref[...], preferred_element_type=F32) + ba_ref[...])
    i = jax.nn.sigmoid(jnp.dot(ub, wx_ref[...], preferred_element_type=F32) + bx_ref[...])
    lam = -lam_ref[...]
    softplus = jnp.maximum(lam, 0.0) + jnp.log1p(jnp.exp(-jnp.abs(lam)))
    log_a = -LRU_C * r * softplus
    a = jnp.exp(log_a)
    b = jnp.sqrt(-jnp.tanh(log_a) * (a * a + 1.0)) * (i * u)
    return a, b


def _rnn_prompt_kernel(xr_ref, gate_ref, cw_ref, cb_ref, wa_ref, ba_ref, wx_ref, bx_ref, lam_ref,
                       y_ref, hlast_ref, xs_ref, a_ref, b_ref, carry_ref, *, tt):
    t = pl.program_id(1)

    @pl.when(t == 0)
    def _():
        xs_ref[0:SUBLANES, :] = jnp.zeros((SUBLANES, D_RNN), F32)
        carry_ref[...] = jnp.zeros((SUBLANES, D_RNN), F32)

    x = xr_ref[...]
    xs_ref[SUBLANES:SUBLANES + tt, :] = x
    cw = cw_ref[...]
    u = cb_ref[...] + x * cw[3:4]
    for j in range(CONV_W - 1):
        off = SUBLANES - (CONV_W - 1) + j
        u = u + xs_ref[off:off + tt, :] * cw[j:j + 1]
    xs_ref[0:SUBLANES, :] = x[tt - SUBLANES:tt, :]

    a, b = _rglru_coeffs(u, wa_ref, ba_ref, wx_ref, bx_ref, lam_ref)
    ng = tt // SUBLANES
    a = a.reshape(ng, SUBLANES, D_RNN)
    b = b.reshape(ng, SUBLANES, D_RNN)
    row = lax.broadcasted_iota(I32, (1, SUBLANES, 1), 1)
    for s in (1, 2, 4):
        a_sh = pltpu.roll(a, s, 1)
        b_sh = pltpu.roll(b, s, 1)
        m = row >= s
        b = jnp.where(m, a * b_sh + b, b)
        a = jnp.where(m, a * a_sh, a)
    a_ref[...] = a
    b_ref[...] = b

    def body(g, carry):
        h = a_ref[g] * carry + b_ref[g]
        b_ref[g] = h
        return jnp.broadcast_to(h[SUBLANES - 1:SUBLANES, :], (SUBLANES, D_RNN))

    carry = lax.fori_loop(0, ng, body, carry_ref[...])
    carry_ref[...] = carry
    h = b_ref[...].reshape(tt, D_RNN)
    y_ref[...] = h * jax.nn.gelu(gate_ref[...])
    hlast_ref[...] = carry[0:1, :]


def _rnn_prompt(z3, rw):
    bsz, t_len, _ = z3.shape
    tt = min(ROW_TILE, t_len)
    vec = _layer_spec(rw["layer"], 1, D_RNN)
    mat = _layer_spec(rw["layer"], D_RNN, D_RNN)
    return pl.pallas_call(
        functools.partial(_rnn_prompt_kernel, tt=tt),
        grid=(bsz, t_len // tt),
        in_specs=[pl.BlockSpec((None, tt, D_RNN), lambda b, t: (b, t, 0)),
                  pl.BlockSpec((None, tt, D_RNN), lambda b, t: (b, t, 1)),
                  _layer_spec(rw["layer"], CONV_W, D_RNN), vec, mat, vec, mat, vec, vec],
        out_specs=[pl.BlockSpec((None, tt, D_RNN), lambda b, t: (b, t, 0)),
                   pl.BlockSpec((None, 1, D_RNN), lambda b, t: (b, 0, 0))],
        out_shape=[jax.ShapeDtypeStruct((bsz, t_len, D_RNN), F32),
                   jax.ShapeDtypeStruct((bsz, 1, D_RNN), F32)],
        scratch_shapes=[pltpu.VMEM((tt + SUBLANES, D_RNN), F32),
                        pltpu.VMEM((tt // SUBLANES, SUBLANES, D_RNN), F32),
                        pltpu.VMEM((tt // SUBLANES, SUBLANES, D_RNN), F32),
                        pltpu.VMEM((SUBLANES, D_RNN), F32)],
        compiler_params=_cparams("parallel", "arbitrary"), name="rnn_prompt",
    )(z3, z3, rw["conv_w"], rw["conv_b"], rw["wa"], rw["ba"], rw["wx"], rw["bx"], rw["lam"])


def _rnn_sample_kernel(xr_ref, gate_ref, c0_ref, c1_ref, c2_ref, h0_ref, cw_ref, cb_ref, wa_ref, ba_ref,
                       wx_ref, bx_ref, lam_ref, y_ref, h_ref):
    x = xr_ref[...]
    cw = cw_ref[...]
    u = (cb_ref[...] + c0_ref[...] * cw[0:1] + c1_ref[...] * cw[1:2] + c2_ref[...] * cw[2:3]
         + x * cw[3:4])
    a, b = _rglru_coeffs(u, wa_ref, ba_ref, wx_ref, bx_ref, lam_ref)
    h = a * h0_ref[...] + b
    h_ref[...] = h
    y_ref[...] = h * jax.nn.gelu(gate_ref[...])


def _rnn_sample(z, conv_state, h0, rw):
    n = z.shape[0]
    layer = rw["layer"]
    vec = _layer_spec(layer, 1, D_RNN)
    mat = _layer_spec(layer, D_RNN, D_RNN)
    full = _const_spec((n, D_RNN))
    depth = conv_state.shape[0]
    conv_flat = conv_state.reshape(depth, n, (CONV_W - 1) * D_RNN)
    tap = lambda j: pl.BlockSpec((None, n, D_RNN), lambda i: (layer, 0, j))
    return pl.pallas_call(
        _rnn_sample_kernel,
        grid=(1,),
        in_specs=[pl.BlockSpec((n, D_RNN), lambda i: (0, 0)), pl.BlockSpec((n, D_RNN), lambda i: (0, 1)),
                  tap(0), tap(1), tap(2), _layer_spec(layer, n, D_RNN),
                  _layer_spec(layer, CONV_W, D_RNN), vec, mat, vec, mat, vec, vec],
        out_specs=[full, full],
        out_shape=[jax.ShapeDtypeStruct((n, D_RNN), F32)] * 2,
        compiler_params=_cparams("arbitrary"), name="rnn_sample",
    )(z, z, conv_flat, conv_flat, conv_flat, h0, rw["conv_w"], rw["conv_b"], rw["wa"], rw["ba"], rw["wx"], rw["bx"], rw["lam"])


def _rel_bucket(dist):
    n = jnp.maximum(dist, 0)
    max_exact = N_BUCKETS // 2
    nf = jnp.maximum(n, max_exact).astype(F32)
    large = max_exact + (jnp.log(nf / max_exact) / math.log(MAX_DIST / max_exact)
                         * (N_BUCKETS - max_exact)).astype(I32)
    large = jnp.minimum(large, N_BUCKETS - 1)
    return jnp.where(n < max_exact, n, large)


def _bias_lookup(rel_bias, dist):
    hot = (_rel_bucket(dist)[..., None] == jnp.arange(N_BUCKETS)).astype(F32)
    out = jnp.tensordot(hot, rel_bias.astype(F32), axes=1, precision=lax.Precision.HIGHEST)
    return jnp.moveaxis(out, -1, 0)


def _half_lane_variants(x, group):
    lo = lax.broadcasted_iota(I32, (1, 2 * HEAD_DIM), 1) < HEAD_DIM
    xr = pltpu.roll(x, HEAD_DIM, 1)
    zero = jnp.zeros_like(x)
    if group == 0:
        return jnp.where(lo, x, zero), jnp.where(lo, zero, xr)
    return jnp.where(lo, xr, zero), jnp.where(lo, zero, x)


def _attn_prompt_kernel(q_ref, kc_ref, kp_ref, vc_ref, vp_ref, bias_ref, sink_ref, o_ref, *, nq, layer):
    first = pl.program_id(1) == 0
    col = lax.broadcasted_iota(I32, (1, 2 * WINDOW), 1)
    hide_prev = jnp.logical_and(first, col < WINDOW)
    for j in range(nq):
        rows = slice(j * WINDOW, (j + 1) * WINDOW)
        before = slice((j - 1) * WINDOW, j * WINDOW)
        kk = jnp.concatenate([kp_ref[...] if j == 0 else kc_ref[before, :], kc_ref[rows, :]], axis=0)
        vv = jnp.concatenate([vp_ref[...] if j == 0 else vc_ref[before, :], vc_ref[rows, :]], axis=0)
        for g in range(N_KV):
            k_lo, k_hi = _half_lane_variants(kk, g)
            v_lo, v_hi = _half_lane_variants(vv, g)
            for pair in range(GQA_R // 2):
                p0 = g * (GQA_R // 2) + pair
                qp = (q_ref[rows, p0 * 128:(p0 + 1) * 128] * (HEAD_DIM ** -0.5)).astype(BF16)
                acc = jnp.zeros((WINDOW, 2 * HEAD_DIM), F32)
                for kx, vx, h in ((k_lo, v_lo, 2 * p0), (k_hi, v_hi, 2 * p0 + 1)):
                    s = lax.dot_general(qp, kx.astype(BF16), (((1,), (1,)), ((), ())),
                                        preferred_element_type=F32) + bias_ref[h]
                    if j == 0:
                        s = jnp.where(hide_prev, NEG, s)
                    sink = sink_ref[layer, h]
                    m = jnp.maximum(jnp.max(s, axis=-1, keepdims=True), sink)
                    p = jnp.exp(s - m)
                    denom = jnp.sum(p, axis=-1, keepdims=True) + jnp.exp(sink - m)
                    acc = acc + jnp.dot(p.astype(BF16), vx.astype(BF16), preferred_element_type=F32) / denom
                o_ref[rows, p0 * 128:(p0 + 1) * 128] = acc


def _attn_prompt(z, bsz, t_len, bias_prompt, sinks, layer):
    nb = t_len // WINDOW
    nq = min(4, nb)
    ns = nb // nq
    kcol = (2 * D_RNN + D_ATTN) // 128
    cur = lambda c: pl.BlockSpec((nq * WINDOW, 128), lambda b, i: (b * ns + i, c))
    prev = lambda c: pl.BlockSpec((WINDOW, 128), lambda b, i: (b * nb + jnp.maximum(i * nq - 1, 0), c))
    return pl.pallas_call(
        functools.partial(_attn_prompt_kernel, nq=nq, layer=layer),
        grid=(bsz, ns),
        in_specs=[pl.BlockSpec((nq * WINDOW, D_ATTN), lambda b, i: (b * ns + i, 2 * D_RNN // D_ATTN)),
                  cur(kcol), prev(kcol), cur(kcol + 1), prev(kcol + 1),
                  _const_spec((N_HEADS, WINDOW, 2 * WINDOW)),
                  pl.BlockSpec(memory_space=pltpu.SMEM)],
        out_specs=pl.BlockSpec((nq * WINDOW, D_ATTN), lambda b, i: (b * ns + i, 0)),
        out_shape=jax.ShapeDtypeStruct((bsz * t_len, D_ATTN), F32),
        compiler_params=_cparams("parallel", "arbitrary"), name="attn_prompt",
    )(z, z, z, z, z, bias_prompt, sinks)


def _attn_sample_kernel(q_ref, kn_ref, vn_ref, ck_ref, cv_ref, bias_ref, sink_ref, hmask_ref, *rest):
    o_ref, nk_ref, nv_ref = rest[-3:]
    pos = lax.broadcasted_iota(I32, (1, 1, WINDOW), 2)
    last = pos == WINDOW - 1
    bb = ck_ref.shape[0]
    newk = jnp.where(last, jnp.broadcast_to(kn_ref[...], (bb, 128, WINDOW)), pltpu.roll(ck_ref[...], WINDOW - 1, 2))
    newv = jnp.where(last, jnp.broadcast_to(vn_ref[...], (bb, 128, WINDOW)), pltpu.roll(cv_ref[...], WINDOW - 1, 2))
    nk_ref[...] = newk
    nv_ref[...] = newv
    s = lax.dot_general(q_ref[...].astype(BF16), newk.astype(BF16), (((2,), (1,)), ((0,), (0,))),
                        preferred_element_type=F32)
    s = s + bias_ref[...][None]
    sink = sink_ref[...][None]
    m = jnp.maximum(jnp.max(s, axis=-1, keepdims=True), sink)
    p = jnp.exp(s - m)
    denom = jnp.sum(p, axis=-1, keepdims=True) + jnp.exp(sink - m)
    o = lax.dot_general(p.astype(BF16), newv.astype(BF16), (((2,), (2,)), ((0,), (0,))),
                        preferred_element_type=F32)
    o_ref[...] = o / denom * hmask_ref[...][None]


def _attn_sample(q, k_new, v_new, k_cache, v_cache, new_caches, layer, bias_sample, sinks):
    n = q.shape[0]
    depth = k_cache.shape[0]
    bb = 16
    qh = q.reshape(n, N_KV, GQA_R, HEAD_DIM) * (HEAD_DIM ** -0.5)
    eye = jnp.eye(N_KV, dtype=F32)
    q8 = (qh[:, :, :, None, :] * eye[None, :, None, :, None]).reshape(n, N_HEADS, N_KV * HEAD_DIM)
    hmask = jnp.repeat(jnp.repeat(eye, GQA_R, axis=0), HEAD_DIM, axis=1)
    blk3 = lambda a, b: pl.BlockSpec((bb, a, b), lambda i: (i, 0, 0))
    cache = pl.BlockSpec((None, bb, 128, WINDOW), lambda i: (layer, i, 0, 0))
    in_specs = [blk3(N_HEADS, 128), blk3(128, 1), blk3(128, 1), cache, cache,
                _const_spec((N_HEADS, WINDOW)), _layer_spec(layer, N_HEADS, 1), _const_spec((N_HEADS, 128))]
    args = [q8, k_new.reshape(n, 128, 1), v_new.reshape(n, 128, 1), k_cache, v_cache, bias_sample,
            sinks.reshape(-1, N_HEADS, 1), hmask]
    aliases = {}
    if new_caches is not None:
        in_specs += [pl.BlockSpec(memory_space=pl.ANY)] * 2
        args += list(new_caches)
        aliases = {len(args) - 2: 1, len(args) - 1: 2}
    o8, nk, nv = pl.pallas_call(
        _attn_sample_kernel,
        grid=(n // bb,),
        in_specs=in_specs,
        out_specs=[blk3(N_HEADS, 128), cache, cache],
        out_shape=[jax.ShapeDtypeStruct((n, N_HEADS, 128), F32),
                   jax.ShapeDtypeStruct((depth, n, 128, WINDOW), F32),
                   jax.ShapeDtypeStruct((depth, n, 128, WINDOW), F32)],
        input_output_aliases=aliases,
        compiler_params=_cparams("arbitrary"), name="attn_sample",
    )(*args)
    o = o8.reshape(n, N_KV, GQA_R, N_KV, HEAD_DIM)
    o = jnp.stack([o[:, g, :, g, :] for g in range(N_KV)], axis=1).reshape(n, D_ATTN)
    return o, (nk, nv)


ONE_HOT_BLOCK = 256


def _sorted_rows(tm):
    return TOP_K * tm + N_EXPERTS * SUBLANES


def _out_proj_kernel(x_ref, yr_ref, o_ref, gnr_ref, gna_ref, w_ref, g1_ref, b1_ref, rwt_ref, rb_ref,
                     x1_ref, rrow_ref, rcol_ref, cnt_ref, *, tm, alpha):
    yn = _rms_norm(yr_ref[...], gnr_ref[...]).astype(BF16)
    on = _rms_norm(o_ref[...], gna_ref[...]).astype(BF16)
    mix = jnp.dot(jnp.concatenate([yn, on], axis=1), w_ref[...], preferred_element_type=F32)
    x1 = _layer_norm(alpha * x_ref[...] + mix, g1_ref[...], b1_ref[...])
    x1_ref[...] = x1

    x1h = x1.astype(BF16)
    x1l = (x1 - x1h.astype(F32)).astype(BF16)
    nt_dims = (((1,), (1,)), ((), ()))
    by_hi = lax.dot_general(rwt_ref[...], x1h, nt_dims, preferred_element_type=F32)
    by_lo = lax.dot_general(rwt_ref[0:N_EXPERTS, :], x1l, nt_dims, preferred_element_type=F32)
    logits = by_hi[0:N_EXPERTS] + by_hi[N_EXPERTS:] + by_lo + rb_ref[...]
    eidx = lax.broadcasted_iota(I32, (N_EXPERTS, tm), 0).astype(F32)
    work = logits
    vals, hots = [], []
    for _ in range(TOP_K):
        v = jnp.max(work, axis=0, keepdims=True)
        idx = jnp.min(jnp.where(work == v, eidx, float(N_EXPERTS)), axis=0, keepdims=True)
        hot = eidx == idx
        work = jnp.where(hot, -jnp.inf, work)
        vals.append(v)
        hots.append(hot.astype(F32))
    ex = [jnp.exp(v - vals[0]) for v in vals]
    tot = ex[0] + ex[1] + ex[2] + ex[3]
    tok_hot = hots[0] + hots[1] + hots[2] + hots[3]
    s_i = lax.broadcasted_iota(I32, (tm, tm), 0)
    t_i = lax.broadcasted_iota(I32, (tm, tm), 1)
    earlier = (s_i < t_i).astype(BF16)
    before = jnp.dot(tok_hot.astype(BF16), earlier, preferred_element_type=F32)
    cnt = jnp.sum(tok_hot, axis=1, keepdims=True)
    cnt8 = jnp.floor((cnt + (SUBLANES - 1)) * (1.0 / SUBLANES)) * SUBLANES
    cnt8_b = jnp.broadcast_to(cnt8, (N_EXPERTS, LANES))
    e_r = lax.broadcasted_iota(I32, (N_EXPERTS, N_EXPERTS), 0)
    e_c = lax.broadcasted_iota(I32, (N_EXPERTS, N_EXPERTS), 1)
    start = jnp.dot((e_c < e_r).astype(F32), cnt8_b, preferred_element_type=F32,
                    precision=lax.Precision.HIGHEST)[:, 0:1]
    slot = start + before
    rows = [jnp.sum(h * slot, axis=0, keepdims=True) for h in hots] + [e / tot for e in ex]
    rrow = jnp.concatenate(rows, axis=0)
    rrow_ref[...] = rrow
    rcol_ref[...] = rrow.T
    cnt_ref[...] = cnt8_b.T[0:1, 0:N_EXPERTS].astype(I32)


def _out_proj(x, y_rnn, o, lw):
    n = x.shape[0]
    tm = min(ROW_TILE, n)
    nt = n // tm
    row = lambda w: pl.BlockSpec((tm, w), lambda i: (i, 0))
    layer = lw["layer"]
    vec = lambda w: _layer_spec(layer, 1, w)
    return pl.pallas_call(
        functools.partial(_out_proj_kernel, tm=tm, alpha=lw["alpha"]),
        grid=(nt,),
        in_specs=[row(D_MODEL), row(D_RNN), row(D_ATTN), vec(D_RNN), vec(D_ATTN),
                  _layer_spec(layer, D_MODEL, D_MODEL), vec(D_MODEL), vec(D_MODEL),
                  _layer_spec(layer, 2 * N_EXPERTS, D_MODEL), _layer_spec(layer, N_EXPERTS, 1)],
        out_specs=[row(D_MODEL), pl.BlockSpec((None, 2 * TOP_K, tm), lambda i: (i, 0, 0)), row(2 * TOP_K),
                   pl.BlockSpec((None, 1, N_EXPERTS), lambda i: (i, 0, 0))],
        out_shape=[jax.ShapeDtypeStruct((n, D_MODEL), F32), jax.ShapeDtypeStruct((nt, 2 * TOP_K, tm), F32),
                   jax.ShapeDtypeStruct((n, 2 * TOP_K), F32), jax.ShapeDtypeStruct((nt, 1, N_EXPERTS), I32)],
        compiler_params=_cparams("parallel"), name="out_proj_route",
    )(x, y_rnn, o, lw["gn_rnn"], lw["gn_attn"], lw["w_out"], lw["ln1_g"], lw["ln1_b"],
      lw["router_wt"], lw["router_b"])


RUN_CHUNK = 64
_SMALL_PIECES = (32, 16, 8)
_MAX_PAD_ROWS = N_EXPERTS * SUBLANES
_PAD_PIECES = (128, 64, 32, 16, 8)


def _for_each_piece(n8, fn):
    def chunk(j, c):
        fn(pl.multiple_of(j * RUN_CHUNK, RUN_CHUNK), RUN_CHUNK)
        return c
    lax.fori_loop(0, n8 // RUN_CHUNK, chunk, 0)
    for s in _SMALL_PIECES:
        @pl.when((n8 & s) != 0)
        def _(s=s):
            fn(n8 & (-2 * s), s)


def _wait_tile_rows(n_rows, tm, wait_rows):
    wait_rows(TOP_K * tm)
    pad = n_rows - TOP_K * tm
    for s in _PAD_PIECES:
        @pl.when((pad & s) != 0)
        def _(s=s):
            wait_rows(s)


def _dispatch_kernel(cnt_ref, base_ref, x_ref, rrow_ref, *rest, tm):
    xbuf_ref, xs_ref, carry_ref, nrows_ref, sems = rest[-5:]
    nrow = _sorted_rows(tm)
    i = pl.program_id(0)
    slot = i % 2

    @pl.when(i == 0)
    def _():
        def init(e, c):
            carry_ref[e] = base_ref[e]
            return c
        lax.fori_loop(0, N_EXPERTS, init, 0)

    r = lax.broadcasted_iota(I32, (ONE_HOT_BLOCK, 1), 0).astype(F32).astype(BF16)
    blocks = []
    for c in range(nrow // ONE_HOT_BLOCK):
        hit = None
        for k in range(TOP_K):
            h = r == (rrow_ref[k:k + 1, :] - float(c * ONE_HOT_BLOCK)).astype(BF16)
            hit = h if hit is None else jnp.logical_or(hit, h)
        blocks.append(jnp.where(hit, jnp.ones((), BF16), jnp.zeros((), BF16)))
    perm = jnp.concatenate(blocks, axis=0)
    xs = jnp.dot(perm, x_ref[...].astype(BF16), preferred_element_type=F32)
    lo = lax.bitcast_convert_type(xs[:, :D_PACK], U32)
    hi = lax.bitcast_convert_type(xs[:, D_PACK:], U32)
    xs_ref[slot] = (hi & U32(HI_MASK)) | (lo >> 16)

    def run_copy(buf, src, dst, s):
        return pltpu.make_async_copy(xs_ref.at[buf, pl.ds(pl.multiple_of(src, SUBLANES), s)],
                                     xbuf_ref.at[pl.ds(pl.multiple_of(dst, SUBLANES), s)], sems.at[buf])

    def start(e, src):
        n8 = cnt_ref[0, e]
        dst = carry_ref[e]
        _for_each_piece(n8, lambda off, s: run_copy(slot, src + off, dst + off, s).start())
        carry_ref[e] = dst + n8
        return src + n8

    def wait_tile(buf):
        _wait_tile_rows(nrows_ref[buf], tm, lambda s: run_copy(buf, 0, 0, s).wait())

    nrows_ref[slot] = lax.fori_loop(0, N_EXPERTS, start, 0)

    @pl.when(i > 0)
    def _():
        wait_tile(1 - slot)

    @pl.when(i == pl.num_programs(0) - 1)
    def _():
        wait_tile(slot)


def _dispatch(x1, rrow, cnt8, base, xbuf, n_rows):
    n = x1.shape[0]
    nt, _, tm = rrow.shape
    in_specs = [pl.BlockSpec((None, 1, N_EXPERTS), lambda i: (i, 0, 0), memory_space=pltpu.SMEM),
                pl.BlockSpec(memory_space=pltpu.SMEM),
                pl.BlockSpec((tm, D_MODEL), lambda i: (i, 0)),
                pl.BlockSpec((None, 2 * TOP_K, tm), lambda i: (i, 0, 0))]
    args = [cnt8, base, x1, rrow]
    aliases = {}
    if xbuf is not None:
        in_specs.append(pl.BlockSpec(memory_space=pl.ANY))
        args.append(xbuf)
        aliases = {len(args) - 1: 0}
    return pl.pallas_call(
        functools.partial(_dispatch_kernel, tm=tm),
        grid=(nt,),
        in_specs=in_specs,
        out_specs=pl.BlockSpec(memory_space=pl.ANY),
        out_shape=jax.ShapeDtypeStruct((n_rows, D_PACK), U32),
        scratch_shapes=[pltpu.VMEM((2, _sorted_rows(tm), D_PACK), U32), pltpu.SMEM((N_EXPERTS,), I32),
                        pltpu.SMEM((2,), I32), pltpu.SemaphoreType.DMA((2,))],
        input_output_aliases=aliases,
        compiler_params=_cparams("arbitrary"), name="moe_dispatch",
    )(*args)


def _moe_kernel(br_ref, be_ref, nv_ref, first_ref, next_ref, x_ref, w1_hbm, b1_ref, w2_hbm, b2_ref, y_ref,
                w1f_ref, w2f_ref, w1s_ref, w2s_ref, sems, *, layer):
    i = pl.program_id(0)

    def load(e):
        return (pltpu.make_async_copy(w1_hbm.at[layer, e], w1f_ref, sems.at[0]),
                pltpu.make_async_copy(w2_hbm.at[layer, e], w2f_ref, sems.at[1]))

    @pl.when(jnp.logical_and(i == 0, first_ref[0] == 1))
    def _():
        for c in load(be_ref[0]):
            c.start()

    @pl.when(first_ref[i] == 1)
    def _():
        for c in load(be_ref[i]):
            c.wait()
        w1s_ref[...] = w1f_ref[...].astype(BF16)
        w2s_ref[...] = w2f_ref[...].astype(BF16)

        @pl.when(next_ref[i] >= 0)
        def _():
            for c in load(next_ref[i]):
                c.start()

    nv = nv_ref[i]

    def expert_rows(m):
        rows = lax.broadcasted_iota(I32, (m, 1), 0)
        xw = jnp.where(rows < nv, x_ref[0:m, :], U32(0))
        lo = lax.bitcast_convert_type(xw << 16, F32).astype(BF16)
        hi = lax.bitcast_convert_type(xw & U32(HI_MASK), F32).astype(BF16)
        xb = jnp.concatenate([lo, hi], axis=1)
        hdn = jnp.dot(xb, w1s_ref[...], preferred_element_type=F32) + b1_ref[...]
        g = jnp.minimum(hdn[:, :D_FF], SWIGLU_LIMIT)
        lin = jnp.clip(hdn[:, D_FF:], -SWIGLU_LIMIT, SWIGLU_LIMIT)
        act = g * jax.nn.sigmoid(SWIGLU_ALPHA * g) * (lin + 1.0)
        y_ref[0:m, :] = jnp.dot(act.astype(BF16), w2s_ref[...], preferred_element_type=F32) + b2_ref[...]

    quarter = MOE_TILE // 4
    for q in range(1, 5):
        @pl.when(jnp.logical_and(nv > (q - 1) * quarter, nv <= q * quarter))
        def _(q=q):
            expert_rows(q * quarter)


def _moe_experts(xbuf, geom, w1, b1, w2, b2, layer):
    n_blocks = xbuf.shape[0] // MOE_TILE
    bias = lambda w: pl.BlockSpec((None, None, 1, w), lambda i, br, be, *_: (layer, be[i], 0, 0))
    grid_spec = pltpu.PrefetchScalarGridSpec(
        num_scalar_prefetch=5,
        grid=(n_blocks,),
        in_specs=[pl.BlockSpec((MOE_TILE, D_PACK), lambda i, br, *_: (br[i], 0)),
                  pl.BlockSpec(memory_space=pl.ANY), bias(2 * D_FF),
                  pl.BlockSpec(memory_space=pl.ANY), bias(D_MODEL)],
        out_specs=pl.BlockSpec((MOE_TILE, D_MODEL), lambda i, br, *_: (br[i], 0)),
        scratch_shapes=[pltpu.VMEM((D_MODEL, 2 * D_FF), F32), pltpu.VMEM((D_FF, D_MODEL), F32),
                        pltpu.VMEM((D_MODEL, 2 * D_FF), BF16), pltpu.VMEM((D_FF, D_MODEL), BF16),
                        pltpu.SemaphoreType.DMA((2,))],
    )
    depth = w1.shape[0]
    return pl.pallas_call(
        functools.partial(_moe_kernel, layer=layer), grid_spec=grid_spec,
        out_shape=jax.ShapeDtypeStruct((xbuf.shape[0], D_MODEL), F32),
        compiler_params=_cparams("arbitrary"), name="moe_experts",
    )(*geom, xbuf, w1, b1.reshape(depth, N_EXPERTS, 1, 2 * D_FF), w2, b2.reshape(depth, N_EXPERTS, 1, D_MODEL))


def _final_kernel(cnt_ref, cnt_next_ref, base_ref, x_ref, p_ref, rcol_ref, wg_ref, bg_ref, wp_ref, g2_ref, b2_ref,
                  ybuf_ref, out_ref, ys_ref, carry_ref, nrows_ref, sems, *, tm, alpha):
    nrow = _sorted_rows(tm)
    i = pl.program_id(0)
    slot = i % 2

    def run_copy(buf, src, dst, s):
        return pltpu.make_async_copy(ybuf_ref.at[pl.ds(pl.multiple_of(src, SUBLANES), s)],
                                     ys_ref.at[buf, pl.ds(pl.multiple_of(dst, SUBLANES), s)], sems.at[buf])

    def gather_tile(counts_ref, buf):
        def start(e, dst):
            n8 = counts_ref[0, e]
            src = carry_ref[e]
            _for_each_piece(n8, lambda off, s: run_copy(buf, src + off, dst + off, s).start())
            carry_ref[e] = src + n8
            return dst + n8
        nrows_ref[buf] = lax.fori_loop(0, N_EXPERTS, start, 0)

    @pl.when(i == 0)
    def _():
        def init(e, c):
            carry_ref[e] = base_ref[e]
            return c
        lax.fori_loop(0, N_EXPERTS, init, 0)
        ys_ref[:, TOP_K * tm:, :] = jnp.zeros((2, nrow - TOP_K * tm, D_MODEL), F32)
        gather_tile(cnt_ref, 0)

    @pl.when(i + 1 < pl.num_programs(0))
    def _():
        gather_tile(cnt_next_ref, 1 - slot)

    x = x_ref[...]
    ple = (jax.nn.sigmoid(jnp.dot(x.astype(BF16), wg_ref[...], preferred_element_type=F32) + bg_ref[...])
           * jnp.dot(p_ref[...].astype(BF16), wp_ref[...], preferred_element_type=F32))
    col = lax.broadcasted_iota(I32, (1, ONE_HOT_BLOCK), 1).astype(F32).astype(BF16)
    rc = rcol_ref[...]
    gates = rc[:, TOP_K:].astype(BF16)
    blocks = []
    for c in range(nrow // ONE_HOT_BLOCK):
        blk = jnp.zeros((tm, ONE_HOT_BLOCK), BF16)
        for k in range(TOP_K):
            hit = col == (rc[:, k:k + 1] - float(c * ONE_HOT_BLOCK)).astype(BF16)
            blk = jnp.where(hit, gates[:, k:k + 1], blk)
        blocks.append(blk)
    gmat = jnp.concatenate(blocks, axis=1)
    base = alpha * x + ple

    _wait_tile_rows(nrows_ref[slot], tm, lambda s: run_copy(slot, 0, 0, s).wait())

    ffn = jnp.dot(gmat, ys_ref[slot].astype(BF16), preferred_element_type=F32)
    out_ref[...] = _layer_norm(base + ffn, g2_ref[...], b2_ref[...])


def _final(x1, p_all, layer, rcol, cnt8, base, ybuf, lw):
    n = x1.shape[0]
    nt = cnt8.shape[0]
    tm = n // nt
    row = lambda w: pl.BlockSpec((tm, w), lambda i: (i, 0))
    vec = lambda w: _layer_spec(layer, 1, w)
    return pl.pallas_call(
        functools.partial(_final_kernel, tm=tm, alpha=lw["alpha"]),
        grid=(nt,),
        in_specs=[pl.BlockSpec((None, 1, N_EXPERTS), lambda i: (i, 0, 0), memory_space=pltpu.SMEM),
                  pl.BlockSpec((None, 1, N_EXPERTS), lambda i: (jnp.minimum(i + 1, nt - 1), 0, 0),
                               memory_space=pltpu.SMEM),
                  pl.BlockSpec(memory_space=pltpu.SMEM),
                  row(D_MODEL), pl.BlockSpec((None, tm, D_PLE), lambda i: (layer, i, 0)), row(2 * TOP_K),
                  _layer_spec(layer, D_MODEL, D_MODEL), vec(D_MODEL), _layer_spec(layer, D_PLE, D_MODEL),
                  vec(D_MODEL), vec(D_MODEL), pl.BlockSpec(memory_space=pl.ANY)],
        out_specs=row(D_MODEL),
        out_shape=jax.ShapeDtypeStruct((n, D_MODEL), F32),
        scratch_shapes=[pltpu.VMEM((2, _sorted_rows(tm), D_MODEL), F32), pltpu.SMEM((N_EXPERTS,), I32),
                        pltpu.SMEM((2,), I32), pltpu.SemaphoreType.DMA((2,))],
        compiler_params=_cparams("arbitrary"), name="moe_combine_ln2",
    )(cnt8, cnt8, base, x1, p_all, rcol, lw["ple_w_gate"], lw["ple_b_gate"], lw["ple_w_proj"], lw["ln2_g"], lw["ln2_b"],
      ybuf)


def _block_diag(w):
    depth, nb, bw, _ = w.shape
    eye = jnp.eye(nb, dtype=w.dtype)
    return (w[:, :, :, None, :] * eye[None, :, None, :, None]).reshape(depth, nb * bw, nb * bw)


def _hi_lo_rows(w):
    hi = w.astype(BF16)
    lo = (w - hi.astype(F32)).astype(BF16)
    return jnp.concatenate([hi, lo], axis=-2)


def _block_geometry(tot, n_blocks):
    nblk = (tot + MOE_TILE - 1) // MOE_TILE
    blk_end = jnp.cumsum(nblk)
    blk_start = blk_end - nblk
    ids = jnp.arange(n_blocks, dtype=I32)
    n_valid = blk_end[-1]
    expert_at = lambda j: jnp.minimum(jnp.sum(j[:, None] >= blk_end[None, :], axis=1), N_EXPERTS - 1).astype(I32)
    src = jnp.clip(ids, 0, jnp.maximum(n_valid - 1, 0))
    e_of = expert_at(src)
    hot = e_of[:, None] == jnp.arange(N_EXPERTS)[None, :]
    pick = lambda v: jnp.sum(jnp.where(hot, v[None, :], 0), axis=1)
    start_of, end_of, tot_of = pick(blk_start), pick(blk_end), pick(tot)
    real = ids < n_valid
    nv = jnp.where(real, jnp.clip(tot_of - (ids - start_of) * MOE_TILE, 0, MOE_TILE), 0)
    first = jnp.logical_and(real, ids == start_of)
    nxt = jnp.where(end_of < n_valid, expert_at(end_of), -1)
    geom = tuple(v.astype(I32) for v in (src, e_of, nv, first, nxt))
    return geom, (blk_start * MOE_TILE).astype(I32)


def kernel(x_prompt, x_sample, cache_k, cache_v, state_conv, state_h, p_prompt, p_sample, ln_in_g, ln_in_b, rel_bias, w_in, conv_w, conv_b, rg_wa, rg_ba, rg_wx, rg_bx, rg_lambda, attn_sinks, gn_rnn, gn_attn, w_out, ln1_g, ln1_b, router_w, router_b, moe_w1, moe_b1, moe_w2, moe_b2, ple_w_gate, ple_b_gate, ple_w_proj, ln2_g, ln2_b):
    depth = w_in.shape[0]
    bsz, t_len, _ = x_prompt.shape
    n_p = bsz * t_len
    n_s = x_sample.shape[0]
    wb = cache_k.shape[2]
    alpha = (2 * depth) ** 0.25

    dist = jnp.arange(WINDOW)[:, None] + WINDOW - jnp.arange(2 * WINDOW)[None, :]
    valid = (dist >= 0) & (dist < WINDOW)
    bias_prompt = jnp.where(valid[None], _bias_lookup(rel_bias, dist), NEG)
    bias_sample = _bias_lookup(rel_bias, WINDOW - 1 - jnp.arange(WINDOW))

    tm_p = min(ROW_TILE, n_p)
    n_tiles = n_p // tm_p + 1
    max_rows = (n_p + n_s) * TOP_K + n_tiles * N_EXPERTS * (SUBLANES - 1)
    n_blocks = -(-(max_rows + N_EXPERTS * (MOE_TILE - 1)) // MOE_TILE)

    xp = x_prompt.reshape(n_p, D_MODEL)
    xs = x_sample.reshape(n_s, D_MODEL)
    pp = p_prompt.reshape(depth, n_p, D_PLE)
    ps = p_sample.reshape(depth, n_s, D_PLE)
    to_feature_major = lambda c: jnp.transpose(c, (0, 1, 3, 4, 2)).reshape(depth, n_s, N_KV * HEAD_DIM, wb)
    to_position_major = lambda c: jnp.transpose(c.reshape(depth, n_s, N_KV, HEAD_DIM, wb), (0, 1, 4, 2, 3))
    ck = to_feature_major(cache_k)
    cv = to_feature_major(cache_v)
    new_caches = None
    row2 = lambda v: v.reshape(1, -1)
    kv0 = 2 * D_RNN + D_ATTN
    outs = {k: [] for k in ("kp", "vp", "cp", "hp", "cs", "hs")}

    vecs = lambda v: v.reshape(depth, 1, -1)
    rw_all = {"conv_w": conv_w, "conv_b": vecs(conv_b), "wa": _block_diag(rg_wa).astype(BF16),
              "ba": vecs(rg_ba), "wx": _block_diag(rg_wx).astype(BF16), "bx": vecs(rg_bx),
              "lam": vecs(rg_lambda)}
    lw_all = {"gn_rnn": vecs(gn_rnn), "gn_attn": vecs(gn_attn), "w_out": w_out.astype(BF16),
              "ln1_g": vecs(ln1_g), "ln1_b": vecs(ln1_b),
              "router_wt": _hi_lo_rows(jnp.swapaxes(router_w, 1, 2)),
              "router_b": router_b.reshape(depth, N_EXPERTS, 1), "alpha": alpha,
              "ple_w_gate": ple_w_gate.astype(BF16), "ple_b_gate": vecs(ple_b_gate),
              "ple_w_proj": ple_w_proj.astype(BF16), "ln2_g": vecs(ln2_g), "ln2_b": vecs(ln2_b)}
    w_in_b = w_in.astype(BF16)

    for l in range(depth):
        rw = dict(rw_all, layer=l)
        lw = dict(lw_all, layer=l)
        first = l == 0

        xp, zp = _in_proj(xp, row2(ln_in_g), row2(ln_in_b), w_in_b, l, first)
        zp3 = zp.reshape(bsz, t_len, D_IN)
        yp, hp = _rnn_prompt(zp3, rw)
        op = _attn_prompt(zp, bsz, t_len, bias_prompt, attn_sinks, l)
        wp = min(WINDOW, t_len)
        outs["kp"].append(zp3[:, t_len - wp:, kv0:kv0 + 128].reshape(bsz, wp, N_KV, HEAD_DIM))
        outs["vp"].append(zp3[:, t_len - wp:, kv0 + 128:kv0 + 256].reshape(bsz, wp, N_KV, HEAD_DIM))
        outs["cp"].append(zp3[:, t_len - (CONV_W - 1):, :D_RNN])
        outs["hp"].append(hp.reshape(bsz, D_RNN))

        xs, zs = _in_proj(xs, row2(ln_in_g), row2(ln_in_b), w_in_b, l, first)
        ys, hs = _rnn_sample(zs, state_conv, state_h, rw)
        os_, new_caches = _attn_sample(zs[:, 2 * D_RNN:kv0], zs[:, kv0:kv0 + 128], zs[:, kv0 + 128:], ck, cv,
                                       new_caches, l, bias_sample, attn_sinks)
        outs["cs"].append(jnp.concatenate([state_conv[l][:, 1:], zs[:, None, :D_RNN]], axis=1))
        outs["hs"].append(hs)

        x1p, rrow_p, rcol_p, cnt_p = _out_proj(xp, yp.reshape(n_p, D_RNN), op, lw)
        x1s, rrow_s, rcol_s, cnt_s = _out_proj(xs, ys, os_, lw)

        tot_p = jnp.sum(cnt_p, axis=(0, 1))
        tot = tot_p + jnp.sum(cnt_s, axis=(0, 1))
        geom, base_p = _block_geometry(tot, n_blocks)
        base_s = base_p + tot_p

        xbuf = _dispatch(x1p, rrow_p, cnt_p, base_p, None, n_blocks * MOE_TILE)
        xbuf = _dispatch(x1s, rrow_s, cnt_s, base_s, xbuf, n_blocks * MOE_TILE)
        ybuf = _moe_experts(xbuf, geom, moe_w1, moe_b1, moe_w2, moe_b2, l)
        xp = _final(x1p, pp, l, rcol_p, cnt_p, base_p, ybuf, lw)
        xs = _final(x1s, ps, l, rcol_s, cnt_s, base_s, ybuf, lw)

    st = lambda k: jnp.stack(outs[k])
    return (xp.reshape(bsz, t_len, D_MODEL), xs.reshape(n_s, 1, D_MODEL),
            st("kp"), st("vp"), st("cp"), st("hp"), to_position_major(new_caches[0]),
            to_position_major(new_caches[1]), st("cs"), st("hs"))
```

```python
import functools
import math

import jax
import jax.numpy as jnp
from jax import lax
from jax.experimental import pallas as pl
from jax.experimental.pallas import tpu as pltpu

F32 = jnp.float32
BF16 = jnp.bfloat16
I32 = jnp.int32
U32 = jnp.uint32

D_MODEL = 1024
D_RNN = 512
RNN_BLOCKS = 8
CONV_W = 4
LRU_C = 8.0
N_HEADS = 8
HEAD_DIM = 64
N_KV = 2
GQA_R = N_HEADS // N_KV
D_ATTN = N_HEADS * HEAD_DIM
WINDOW = 128
N_BUCKETS = 32
MAX_DIST = 128
N_EXPERTS = 32
TOP_K = 4
D_FF = 1024
SWIGLU_LIMIT = 7.0
SWIGLU_ALPHA = 1.702
D_PLE = 256
LN_EPS = 1e-5
D_IN = 2 * D_RNN + D_ATTN + 2 * N_KV * HEAD_DIM
NEG = -1e30

SUBLANES = 8
LANES = 128
ROW_TILE = 512
MOE_TILE = 1024
D_PACK = D_MODEL // 2
VMEM_LIMIT = 56 * 1024 * 1024
HI_MASK = 0xFFFF0000


def _cparams(*sem):
    return pltpu.CompilerParams(dimension_semantics=sem, vmem_limit_bytes=VMEM_LIMIT)


def _layer_norm(x, g, b):
    mu = jnp.mean(x, axis=-1, keepdims=True)
    xc = x - mu
    var = jnp.mean(xc * xc, axis=-1, keepdims=True)
    return xc * lax.rsqrt(var + LN_EPS) * g + b


def _rms_norm(x, g):
    return x * lax.rsqrt(jnp.mean(x * x, axis=-1, keepdims=True) + LN_EPS) * g


def _const_spec(shape):
    return pl.BlockSpec(shape, lambda *_: (0,) * len(shape))


def _layer_spec(layer, *tail):
    return pl.BlockSpec((None,) + tail, lambda *_: (layer,) + (0,) * len(tail))


def _in_proj_kernel(x_ref, g_ref, b_ref, w_ref, *out_refs, apply_ln):
    x = x_ref[...]
    if apply_ln:
        x = _layer_norm(x, g_ref[...], b_ref[...])
        out_refs[0][...] = x
    out_refs[-1][...] = jnp.dot(x.astype(BF16), w_ref[...], preferred_element_type=F32)


def _in_proj(x, ln_g, ln_b, w_in_bf16, layer, apply_ln):
    n = x.shape[0]
    tm = min(ROW_TILE, n)
    row = lambda w: pl.BlockSpec((tm, w), lambda i: (i, 0))
    out_shape = [jax.ShapeDtypeStruct((n, D_IN), F32)]
    out_specs = [row(D_IN)]
    if apply_ln:
        out_shape.insert(0, jax.ShapeDtypeStruct((n, D_MODEL), F32))
        out_specs.insert(0, row(D_MODEL))
    outs = pl.pallas_call(
        functools.partial(_in_proj_kernel, apply_ln=apply_ln),
        grid=(n // tm,),
        in_specs=[row(D_MODEL), _const_spec((1, D_MODEL)), _const_spec((1, D_MODEL)),
                  _layer_spec(layer, D_MODEL, D_IN)],
        out_specs=out_specs, out_shape=out_shape,
        compiler_params=_cparams("parallel"), name="in_proj",
    )(x, ln_g, ln_b, w_in_bf16)
    return (outs[0], outs[1]) if apply_ln else (x, outs[0])


def _rglru_coeffs(u, wa_ref, ba_ref, wx_ref, bx_ref, lam_ref):
    ub = u.astype(BF16)
    r = jax.nn.sigmoid(jnp.dot(ub, wa_ref[...], preferred_element_type=F32) + ba_ref[...])
    i = jax.nn.sigmoid(jnp.dot(ub, wx_ref[...], preferred_element_type=F32) + bx_ref[...])
    lam = -lam_ref[...]
    softplus = jnp.maximum(lam, 0.0) + jnp.log1p(jnp.exp(-jnp.abs(lam)))
    log_a = -LRU_C * r * softplus
    a = jnp.exp(log_a)
    b = jnp.sqrt(-jnp.tanh(log_a) * (a * a + 1.0)) * (i * u)
    return a, b


def _rnn_prompt_kernel(xr_ref, gate_ref, cw_ref, cb_ref, wa_ref, ba_ref, wx_ref, bx_ref, lam_ref,
                       y_ref, hlast_ref, xs_ref, a_ref, b_ref, carry_ref, *, tt):
    t = pl.program_id(1)

    @pl.when(t == 0)
    def _():
        xs_ref[0:SUBLANES, :] = jnp.zeros((SUBLANES, D_RNN), F32)
        carry_ref[...] = jnp.zeros((SUBLANES, D_RNN), F32)

    x = xr_ref[...]
    xs_ref[SUBLANES:SUBLANES + tt, :] = x
    cw = cw_ref[...]
    u = cb_ref[...] + x * cw[3:4]
    for j in range(CONV_W - 1):
        off = SUBLANES - (CONV_W - 1) + j
        u = u + xs_ref[off:off + tt, :] * cw[j:j + 1]
    xs_ref[0:SUBLANES, :] = x[tt - SUBLANES:tt, :]

    a, b = _rglru_coeffs(u, wa_ref, ba_ref, wx_ref, bx_ref, lam_ref)
    ng = tt // SUBLANES
    a = a.reshape(ng, SUBLANES, D_RNN)
    b = b.reshape(ng, SUBLANES, D_RNN)
    row = lax.broadcasted_iota(I32, (1, SUBLANES, 1), 1)
    for s in (1, 2, 4):
        a_sh = pltpu.roll(a, s, 1)
        b_sh = pltpu.roll(b, s, 1)
        m = row >= s
        b = jnp.where(m, a * b_sh + b, b)
        a = jnp.where(m, a * a_sh, a)
    a_ref[...] = a
    b_ref[...] = b

    def body(g, carry):
        h = a_ref[g] * carry + b_ref[g]
        b_ref[g] = h
        return jnp.broadcast_to(h[SUBLANES - 1:SUBLANES, :], (SUBLANES, D_RNN))

    carry = lax.fori_loop(0, ng, body, carry_ref[...], unroll=True)
    carry_ref[...] = carry
    h = b_ref[...].reshape(tt, D_RNN)
    y_ref[...] = h * jax.nn.gelu(gate_ref[...])
    hlast_ref[...] = carry[0:1, :]


def _rnn_prompt(z3, rw):
    bsz, t_len, _ = z3.shape
    tt = min(ROW_TILE, t_len)
    vec = _layer_spec(rw["layer"], 1, D_RNN)
    mat = _layer_spec(rw["layer"], D_RNN, D_RNN)
    return pl.pallas_call(
        functools.partial(_rnn_prompt_kernel, tt=tt),
        grid=(bsz, t_len // tt),
        in_specs=[pl.BlockSpec((None, tt, D_RNN), lambda b, t: (b, t, 0)),
                  pl.BlockSpec((None, tt, D_RNN), lambda b, t: (b, t, 1)),
                  _layer_spec(rw["layer"], CONV_W, D_RNN), vec, mat, vec, mat, vec, vec],
        out_specs=[pl.BlockSpec((None, tt, D_RNN), lambda b, t: (b, t, 0)),
                   pl.BlockSpec((None, 1, D_RNN), lambda b, t: (b, 0, 0))],
        out_shape=[jax.ShapeDtypeStruct((bsz, t_len, D_RNN), F32),
                   jax.ShapeDtypeStruct((bsz, 1, D_RNN), F32)],
        scratch_shapes=[pltpu.VMEM((tt + SUBLANES, D_RNN), F32),
                        pltpu.VMEM((tt // SUBLANES, SUBLANES, D_RNN), F32),
                        pltpu.VMEM((tt // SUBLANES, SUBLANES, D_RNN), F32),
                        pltpu.VMEM((SUBLANES, D_RNN), F32)],
        compiler_params=_cparams("parallel", "arbitrary"), name="rnn_prompt",
    )(z3, z3, rw["conv_w"], rw["conv_b"], rw["wa"], rw["ba"], rw["wx"], rw["bx"], rw["lam"])


def _rnn_sample_kernel(xr_ref, gate_ref, c0_ref, c1_ref, c2_ref, h0_ref, cw_ref, cb_ref, wa_ref, ba_ref,
                       wx_ref, bx_ref, lam_ref, y_ref, h_ref):
    x = xr_ref[...]
    cw = cw_ref[...]
    u = (cb_ref[...] + c0_ref[...] * cw[0:1] + c1_ref[...] * cw[1:2] + c2_ref[...] * cw[2:3]
         + x * cw[3:4])
    a, b = _rglru_coeffs(u, wa_ref, ba_ref, wx_ref, bx_ref, lam_ref)
    h = a * h0_ref[...] + b
    h_ref[...] = h
    y_ref[...] = h * jax.nn.gelu(gate_ref[...])


def _rnn_sample(z, conv_state, h0, rw):
    n = z.shape[0]
    layer = rw["layer"]
    vec = _layer_spec(layer, 1, D_RNN)
    mat = _layer_spec(layer, D_RNN, D_RNN)
    full = _const_spec((n, D_RNN))
    depth = conv_state.shape[0]
    conv_flat = conv_state.reshape(depth, n, (CONV_W - 1) * D_RNN)
    tap = lambda j: pl.BlockSpec((None, n, D_RNN), lambda i: (layer, 0, j))
    return pl.pallas_call(
        _rnn_sample_kernel,
        grid=(1,),
        in_specs=[pl.BlockSpec((n, D_RNN), lambda i: (0, 0)), pl.BlockSpec((n, D_RNN), lambda i: (0, 1)),
                  tap(0), tap(1), tap(2), _layer_spec(layer, n, D_RNN),
                  _layer_spec(layer, CONV_W, D_RNN), vec, mat, vec, mat, vec, vec],
        out_specs=[full, full],
        out_shape=[jax.ShapeDtypeStruct((n, D_RNN), F32)] * 2,
        compiler_params=_cparams("arbitrary"), name="rnn_sample",
    )(z, z, conv_flat, conv_flat, conv_flat, h0, rw["conv_w"], rw["conv_b"], rw["wa"], rw["ba"], rw["wx"], rw["bx"], rw["lam"])


def _rel_bucket(dist):
    n = jnp.maximum(dist, 0)
    max_exact = N_BUCKETS // 2
    nf = jnp.maximum(n, max_exact).astype(F32)
    large = max_exact + (jnp.log(nf / max_exact) / math.log(MAX_DIST / max_exact)
                         * (N_BUCKETS - max_exact)).astype(I32)
    large = jnp.minimum(large, N_BUCKETS - 1)
    return jnp.where(n < max_exact, n, large)


def _bias_lookup(rel_bias, dist):
    hot = (_rel_bucket(dist)[..., None] == jnp.arange(N_BUCKETS)).astype(F32)
    out = jnp.tensordot(hot, rel_bias.astype(F32), axes=1, precision=lax.Precision.HIGHEST)
    return jnp.moveaxis(out, -1, 0)


def _half_lane_variants(x, group):
    lo = lax.broadcasted_iota(I32, (1, 2 * HEAD_DIM), 1) < HEAD_DIM
    xr = pltpu.roll(x, HEAD_DIM, 1)
    zero = jnp.zeros_like(x)
    if group == 0:
        return jnp.where(lo, x, zero), jnp.where(lo, zero, xr)
    return jnp.where(lo, xr, zero), jnp.where(lo, zero, x)


def _attn_prompt_kernel(q_ref, kc_ref, kp_ref, vc_ref, vp_ref, bias_ref, sink_ref, o_ref, *, nq, layer):
    first = pl.program_id(1) == 0
    col = lax.broadcasted_iota(I32, (1, 2 * WINDOW), 1)
    hide_prev = jnp.logical_and(first, col < WINDOW)
    for j in range(nq):
        rows = slice(j * WINDOW, (j + 1) * WINDOW)
        before = slice((j - 1) * WINDOW, j * WINDOW)
        kk = jnp.concatenate([kp_ref[...] if j == 0 else kc_ref[before, :], kc_ref[rows, :]], axis=0)
        vv = jnp.concatenate([vp_ref[...] if j == 0 else vc_ref[before, :], vc_ref[rows, :]], axis=0)
        for g in range(N_KV):
            k_lo, k_hi = _half_lane_variants(kk, g)
            v_lo, v_hi = _half_lane_variants(vv, g)
            for pair in range(GQA_R // 2):
                p0 = g * (GQA_R // 2) + pair
                qp = (q_ref[rows, p0 * 128:(p0 + 1) * 128] * (HEAD_DIM ** -0.5)).astype(BF16)
                acc = jnp.zeros((WINDOW, 2 * HEAD_DIM), F32)
                for kx, vx, h in ((k_lo, v_lo, 2 * p0), (k_hi, v_hi, 2 * p0 + 1)):
                    s = lax.dot_general(qp, kx.astype(BF16), (((1,), (1,)), ((), ())),
                                        preferred_element_type=F32) + bias_ref[h]
                    if j == 0:
                        s = jnp.where(hide_prev, NEG, s)
                    sink = sink_ref[layer, h]
                    m = jnp.maximum(jnp.max(s, axis=-1, keepdims=True), sink)
                    p = jnp.exp(s - m)
                    denom = jnp.sum(p, axis=-1, keepdims=True) + jnp.exp(sink - m)
                    acc = acc + jnp.dot(p.astype(BF16), vx.astype(BF16), preferred_element_type=F32) / denom
                o_ref[rows, p0 * 128:(p0 + 1) * 128] = acc


def _attn_prompt(z, bsz, t_len, bias_prompt, sinks, layer):
    nb = t_len // WINDOW
    nq = min(4, nb)
    ns = nb // nq
    kcol = (2 * D_RNN + D_ATTN) // 128
    cur = lambda c: pl.BlockSpec((nq * WINDOW, 128), lambda b, i: (b * ns + i, c))
    prev = lambda c: pl.BlockSpec((WINDOW, 128), lambda b, i: (b * nb + jnp.maximum(i * nq - 1, 0), c))
    return pl.pallas_call(
        functools.partial(_attn_prompt_kernel, nq=nq, layer=layer),
        grid=(bsz, ns),
        in_specs=[pl.BlockSpec((nq * WINDOW, D_ATTN), lambda b, i: (b * ns + i, 2 * D_RNN // D_ATTN)),
                  cur(kcol), prev(kcol), cur(kcol + 1), prev(kcol + 1),
                  _const_spec((N_HEADS, WINDOW, 2 * WINDOW)),
                  pl.BlockSpec(memory_space=pltpu.SMEM)],
        out_specs=pl.BlockSpec((nq * WINDOW, D_ATTN), lambda b, i: (b * ns + i, 0)),
        out_shape=jax.ShapeDtypeStruct((bsz * t_len, D_ATTN), F32),
        compiler_params=_cparams("parallel", "arbitrary"), name="attn_prompt",
    )(z, z, z, z, z, bias_prompt, sinks)


def _attn_sample_kernel(q_ref, kn_ref, vn_ref, ck_ref, cv_ref, bias_ref, sink_ref, hmask_ref, *rest):
    o_ref, nk_ref, nv_ref = rest[-3:]
    pos = lax.broadcasted_iota(I32, (1, 1, WINDOW), 2)
    last = pos == WINDOW - 1
    bb = ck_ref.shape[0]
    newk = jnp.where(last, jnp.broadcast_to(kn_ref[...], (bb, 128, WINDOW)), pltpu.roll(ck_ref[...], WINDOW - 1, 2))
    newv = jnp.where(last, jnp.broadcast_to(vn_ref[...], (bb, 128, WINDOW)), pltpu.roll(cv_ref[...], WINDOW - 1, 2))
    nk_ref[...] = newk
    nv_ref[...] = newv
    s = lax.dot_general(q_ref[...].astype(BF16), newk.astype(BF16), (((2,), (1,)), ((0,), (0,))),
                        preferred_element_type=F32)
    s = s + bias_ref[...][None]
    sink = sink_ref[...][None]
    m = jnp.maximum(jnp.max(s, axis=-1, keepdims=True), sink)
    p = jnp.exp(s - m)
    denom = jnp.sum(p, axis=-1, keepdims=True) + jnp.exp(sink - m)
    o = lax.dot_general(p.astype(BF16), newv.astype(BF16), (((2,), (2,)), ((0,), (0,))),
                        preferred_element_type=F32)
    o_ref[...] = o / denom * hmask_ref[...][None]


def _attn_sample(q, k_new, v_new, k_cache, v_cache, new_caches, layer, bias_sample, sinks):
    n = q.shape[0]
    depth = k_cache.shape[0]
    bb = 16
    qh = q.reshape(n, N_KV, GQA_R, HEAD_DIM) * (HEAD_DIM ** -0.5)
    eye = jnp.eye(N_KV, dtype=F32)
    q8 = (qh[:, :, :, None, :] * eye[None, :, None, :, None]).reshape(n, N_HEADS, N_KV * HEAD_DIM)
    hmask = jnp.repeat(jnp.repeat(eye, GQA_R, axis=0), HEAD_DIM, axis=1)
    blk3 = lambda a, b: pl.BlockSpec((bb, a, b), lambda i: (i, 0, 0))
    cache = pl.BlockSpec((None, bb, 128, WINDOW), lambda i: (layer, i, 0, 0))
    in_specs = [blk3(N_HEADS, 128), blk3(128, 1), blk3(128, 1), cache, cache,
                _const_spec((N_HEADS, WINDOW)), _layer_spec(layer, N_HEADS, 1), _const_spec((N_HEADS, 128))]
    args = [q8, k_new.reshape(n, 128, 1), v_new.reshape(n, 128, 1), k_cache, v_cache, bias_sample,
            sinks.reshape(-1, N_HEADS, 1), hmask]
    aliases = {}
    if new_caches is not None:
        in_specs += [pl.BlockSpec(memory_space=pl.ANY)] * 2
        args += list(new_caches)
        aliases = {len(args) - 2: 1, len(args) - 1: 2}
    o8, nk, nv = pl.pallas_call(
        _attn_sample_kernel,
        grid=(n // bb,),
        in_specs=in_specs,
        out_specs=[blk3(N_HEADS, 128), cache, cache],
        out_shape=[jax.ShapeDtypeStruct((n, N_HEADS, 128), F32),
                   jax.ShapeDtypeStruct((depth, n, 128, WINDOW), F32),
                   jax.ShapeDtypeStruct((depth, n, 128, WINDOW), F32)],
        input_output_aliases=aliases,
        compiler_params=_cparams("arbitrary"), name="attn_sample",
    )(*args)
    o = o8.reshape(n, N_KV, GQA_R, N_KV, HEAD_DIM)
    o = jnp.stack([o[:, g, :, g, :] for g in range(N_KV)], axis=1).reshape(n, D_ATTN)
    return o, (nk, nv)


ONE_HOT_BLOCK = 256


def _sorted_rows(tm):
    return TOP_K * tm + N_EXPERTS * SUBLANES


def _out_proj_kernel(x_ref, yr_ref, o_ref, gnr_ref, gna_ref, w_ref, g1_ref, b1_ref, rwt_ref, rb_ref,
                     x1_ref, rrow_ref, rcol_ref, cnt_ref, *, tm, alpha):
    yn = _rms_norm(yr_ref[...], gnr_ref[...]).astype(BF16)
    on = _rms_norm(o_ref[...], gna_ref[...]).astype(BF16)
    mix = jnp.dot(jnp.concatenate([yn, on], axis=1), w_ref[...], preferred_element_type=F32)
    x1 = _layer_norm(alpha * x_ref[...] + mix, g1_ref[...], b1_ref[...])
    x1_ref[...] = x1

    x1h = x1.astype(BF16)
    x1l = (x1 - x1h.astype(F32)).astype(BF16)
    nt_dims = (((1,), (1,)), ((), ()))
    by_hi = lax.dot_general(rwt_ref[...], x1h, nt_dims, preferred_element_type=F32)
    by_lo = lax.dot_general(rwt_ref[0:N_EXPERTS, :], x1l, nt_dims, preferred_element_type=F32)
    logits = by_hi[0:N_EXPERTS] + by_hi[N_EXPERTS:] + by_lo + rb_ref[...]
    eidx = lax.broadcasted_iota(I32, (N_EXPERTS, tm), 0).astype(F32)
    work = logits
    vals, hots = [], []
    for _ in range(TOP_K):
        v = jnp.max(work, axis=0, keepdims=True)
        idx = jnp.min(jnp.where(work == v, eidx, float(N_EXPERTS)), axis=0, keepdims=True)
        hot = eidx == idx
        work = jnp.where(hot, -jnp.inf, work)
        vals.append(v)
        hots.append(hot.astype(F32))
    ex = [jnp.exp(v - vals[0]) for v in vals]
    tot = ex[0] + ex[1] + ex[2] + ex[3]
    tok_hot = hots[0] + hots[1] + hots[2] + hots[3]
    s_i = lax.broadcasted_iota(I32, (tm, tm), 0)
    t_i = lax.broadcasted_iota(I32, (tm, tm), 1)
    earlier = (s_i < t_i).astype(BF16)
    before = jnp.dot(tok_hot.astype(BF16), earlier, preferred_element_type=F32)
    cnt = jnp.sum(tok_hot, axis=1, keepdims=True)
    cnt8 = jnp.floor((cnt + (SUBLANES - 1)) * (1.0 / SUBLANES)) * SUBLANES
    cnt8_b = jnp.broadcast_to(cnt8, (N_EXPERTS, LANES))
    e_r = lax.broadcasted_iota(I32, (N_EXPERTS, N_EXPERTS), 0)
    e_c = lax.broadcasted_iota(I32, (N_EXPERTS, N_EXPERTS), 1)
    start = jnp.dot((e_c < e_r).astype(F32), cnt8_b, preferred_element_type=F32,
                    precision=lax.Precision.HIGHEST)[:, 0:1]
    slot = start + before
    rows = [jnp.sum(h * slot, axis=0, keepdims=True) for h in hots] + [e / tot for e in ex]
    rrow = jnp.concatenate(rows, axis=0)
    rrow_ref[...] = rrow
    rcol_ref[...] = rrow.T
    cnt_ref[...] = cnt8_b.T[0:1, 0:N_EXPERTS].astype(I32)


def _out_proj(x, y_rnn, o, lw):
    n = x.shape[0]
    tm = min(ROW_TILE, n)
    nt = n // tm
    row = lambda w: pl.BlockSpec((tm, w), lambda i: (i, 0))
    layer = lw["layer"]
    vec = lambda w: _layer_spec(layer, 1, w)
    return pl.pallas_call(
        functools.partial(_out_proj_kernel, tm=tm, alpha=lw["alpha"]),
        grid=(nt,),
        in_specs=[row(D_MODEL), row(D_RNN), row(D_ATTN), vec(D_RNN), vec(D_ATTN),
                  _layer_spec(layer, D_MODEL, D_MODEL), vec(D_MODEL), vec(D_MODEL),
                  _layer_spec(layer, 2 * N_EXPERTS, D_MODEL), _layer_spec(layer, N_EXPERTS, 1)],
        out_specs=[row(D_MODEL), pl.BlockSpec((None, 2 * TOP_K, tm), lambda i: (i, 0, 0)), row(2 * TOP_K),
                   pl.BlockSpec((None, 1, N_EXPERTS), lambda i: (i, 0, 0))],
        out_shape=[jax.ShapeDtypeStruct((n, D_MODEL), F32), jax.ShapeDtypeStruct((nt, 2 * TOP_K, tm), F32),
                   jax.ShapeDtypeStruct((n, 2 * TOP_K), F32), jax.ShapeDtypeStruct((nt, 1, N_EXPERTS), I32)],
        compiler_params=_cparams("parallel"), name="out_proj_route",
    )(x, y_rnn, o, lw["gn_rnn"], lw["gn_attn"], lw["w_out"], lw["ln1_g"], lw["ln1_b"],
      lw["router_wt"], lw["router_b"])


RUN_CHUNK = 64
_SMALL_PIECES = (32, 16, 8)
_MAX_PAD_ROWS = N_EXPERTS * SUBLANES
_PAD_PIECES = (128, 64, 32, 16, 8)


def _for_each_piece(n8, fn):
    def chunk(j, c):
        fn(pl.multiple_of(j * RUN_CHUNK, RUN_CHUNK), RUN_CHUNK)
        return c
    lax.fori_loop(0, n8 // RUN_CHUNK, chunk, 0)
    for s in _SMALL_PIECES:
        @pl.when((n8 & s) != 0)
        def _(s=s):
            fn(n8 & (-2 * s), s)


def _wait_tile_rows(n_rows, tm, wait_rows):
    wait_rows(TOP_K * tm)
    pad = n_rows - TOP_K * tm
    for s in _PAD_PIECES:
        @pl.when((pad & s) != 0)
        def _(s=s):
            wait_rows(s)


def _dispatch_kernel(cnt_ref, base_ref, x_ref, rrow_ref, *rest, tm):
    xbuf_ref, xs_ref, carry_ref, nrows_ref, sems = rest[-5:]
    nrow = _sorted_rows(tm)
    i = pl.program_id(0)
    slot = i % 2

    @pl.when(i == 0)
    def _():
        def init(e, c):
            carry_ref[e] = base_ref[e]
            return c
        lax.fori_loop(0, N_EXPERTS, init, 0)

    r = lax.broadcasted_iota(I32, (ONE_HOT_BLOCK, 1), 0).astype(F32).astype(BF16)
    blocks = []
    for c in range(nrow // ONE_HOT_BLOCK):
        hit = None
        for k in range(TOP_K):
            h = r == (rrow_ref[k:k + 1, :] - float(c * ONE_HOT_BLOCK)).astype(BF16)
            hit = h if hit is None else jnp.logical_or(hit, h)
        blocks.append(jnp.where(hit, jnp.ones((), BF16), jnp.zeros((), BF16)))
    perm = jnp.concatenate(blocks, axis=0)
    xs = jnp.dot(perm, x_ref[...].astype(BF16), preferred_element_type=F32)
    lo = lax.bitcast_convert_type(xs[:, :D_PACK], U32)
    hi = lax.bitcast_convert_type(xs[:, D_PACK:], U32)
    xs_ref[slot] = (hi & U32(HI_MASK)) | (lo >> 16)

    def run_copy(buf, src, dst, s):
        return pltpu.make_async_copy(xs_ref.at[buf, pl.ds(pl.multiple_of(src, SUBLANES), s)],
                                     xbuf_ref.at[pl.ds(pl.multiple_of(dst, SUBLANES), s)], sems.at[buf])

    def start(e, src):
        n8 = cnt_ref[0, e]
        dst = carry_ref[e]
        _for_each_piece(n8, lambda off, s: run_copy(slot, src + off, dst + off, s).start())
        carry_ref[e] = dst + n8
        return src + n8

    def wait_tile(buf):
        _wait_tile_rows(nrows_ref[buf], tm, lambda s: run_copy(buf, 0, 0, s).wait())

    nrows_ref[slot] = lax.fori_loop(0, N_EXPERTS, start, 0)

    @pl.when(i > 0)
    def _():
        wait_tile(1 - slot)

    @pl.when(i == pl.num_programs(0) - 1)
    def _():
        wait_tile(slot)


def _dispatch(x1, rrow, cnt8, base, xbuf, n_rows):
    n = x1.shape[0]
    nt, _, tm = rrow.shape
    in_specs = [pl.BlockSpec((None, 1, N_EXPERTS), lambda i: (i, 0, 0), memory_space=pltpu.SMEM),
                pl.BlockSpec(memory_space=pltpu.SMEM),
                pl.BlockSpec((tm, D_MODEL), lambda i: (i, 0)),
                pl.BlockSpec((None, 2 * TOP_K, tm), lambda i: (i, 0, 0))]
    args = [cnt8, base, x1, rrow]
    aliases = {}
    if xbuf is not None:
        in_specs.append(pl.BlockSpec(memory_space=pl.ANY))
        args.append(xbuf)
        aliases = {len(args) - 1: 0}
    return pl.pallas_call(
        functools.partial(_dispatch_kernel, tm=tm),
        grid=(nt,),
        in_specs=in_specs,
        out_specs=pl.BlockSpec(memory_space=pl.ANY),
        out_shape=jax.ShapeDtypeStruct((n_rows, D_PACK), U32),
        scratch_shapes=[pltpu.VMEM((2, _sorted_rows(tm), D_PACK), U32), pltpu.SMEM((N_EXPERTS,), I32),
                        pltpu.SMEM((2,), I32), pltpu.SemaphoreType.DMA((2,))],
        input_output_aliases=aliases,
        compiler_params=_cparams("arbitrary"), name="moe_dispatch",
    )(*args)


def _moe_kernel(br_ref, be_ref, nv_ref, first_ref, next_ref, x_ref, w1_hbm, b1_ref, w2_hbm, b2_ref, y_ref,
                w1f_ref, w2f_ref, w1s_ref, w2s_ref, sems, *, layer):
    i = pl.program_id(0)

    def load(e):
        return (pltpu.make_async_copy(w1_hbm.at[layer, e], w1f_ref, sems.at[0]),
                pltpu.make_async_copy(w2_hbm.at[layer, e], w2f_ref, sems.at[1]))

    @pl.when(jnp.logical_and(i == 0, first_ref[0] == 1))
    def _():
        for c in load(be_ref[0]):
            c.start()

    nv = nv_ref[i]
    quarter = MOE_TILE // 4
    is_first = first_ref[i] == 1
    fresh_full = jnp.logical_and(is_first, nv > 3 * quarter)

    def start_next_load():
        @pl.when(next_ref[i] >= 0)
        def _():
            for c in load(next_ref[i]):
                c.start()

    def expert_rows(m, fresh):
        if fresh:
            w1 = w1f_ref[...].astype(BF16)
            w2 = w2f_ref[...].astype(BF16)
            w1s_ref[...] = w1
            w2s_ref[...] = w2
        else:
            w1 = w1s_ref[...]
            w2 = w2s_ref[...]
        rows = lax.broadcasted_iota(I32, (m, 1), 0)
        xw = jnp.where(rows < nv, x_ref[0:m, :], U32(0))
        lo = lax.bitcast_convert_type(xw << 16, F32).astype(BF16)
        hi = lax.bitcast_convert_type(xw & U32(HI_MASK), F32).astype(BF16)
        xb = jnp.concatenate([lo, hi], axis=1)
        hdn = jnp.dot(xb, w1, preferred_element_type=F32) + b1_ref[...]
        g = jnp.minimum(hdn[:, :D_FF], SWIGLU_LIMIT)
        lin = jnp.clip(hdn[:, D_FF:], -SWIGLU_LIMIT, SWIGLU_LIMIT)
        act = g * jax.nn.sigmoid(SWIGLU_ALPHA * g) * (lin + 1.0)
        y_ref[0:m, :] = jnp.dot(act.astype(BF16), w2, preferred_element_type=F32) + b2_ref[...]

    @pl.when(is_first)
    def _():
        for c in load(be_ref[i]):
            c.wait()

    @pl.when(fresh_full)
    def _():
        expert_rows(MOE_TILE, True)
        start_next_load()

    @pl.when(jnp.logical_and(is_first, jnp.logical_not(fresh_full)))
    def _():
        w1s_ref[...] = w1f_ref[...].astype(BF16)
        w2s_ref[...] = w2f_ref[...].astype(BF16)
        start_next_load()

    for q in range(1, 5):
        @pl.when(jnp.logical_and(jnp.logical_not(fresh_full),
                                 jnp.logical_and(nv > (q - 1) * quarter, nv <= q * quarter)))
        def _(q=q):
            expert_rows(q * quarter, False)


def _moe_experts(xbuf, geom, w1, b1, w2, b2, layer):
    n_blocks = xbuf.shape[0] // MOE_TILE
    bias = lambda w: pl.BlockSpec((None, None, 1, w), lambda i, br, be, *_: (layer, be[i], 0, 0))
    grid_spec = pltpu.PrefetchScalarGridSpec(
        num_scalar_prefetch=5,
        grid=(n_blocks,),
        in_specs=[pl.BlockSpec((MOE_TILE, D_PACK), lambda i, br, *_: (br[i], 0)),
                  pl.BlockSpec(memory_space=pl.ANY), bias(2 * D_FF),
                  pl.BlockSpec(memory_space=pl.ANY), bias(D_MODEL)],
        out_specs=pl.BlockSpec((MOE_TILE, D_MODEL), lambda i, br, *_: (br[i], 0)),
        scratch_shapes=[pltpu.VMEM((D_MODEL, 2 * D_FF), F32), pltpu.VMEM((D_FF, D_MODEL), F32),
                        pltpu.VMEM((D_MODEL, 2 * D_FF), BF16), pltpu.VMEM((D_FF, D_MODEL), BF16),
                        pltpu.SemaphoreType.DMA((2,))],
    )
    depth = w1.shape[0]
    return pl.pallas_call(
        functools.partial(_moe_kernel, layer=layer), grid_spec=grid_spec,
        out_shape=jax.ShapeDtypeStruct((xbuf.shape[0], D_MODEL), F32),
        compiler_params=_cparams("arbitrary"), name="moe_experts",
    )(*geom, xbuf, w1, b1.reshape(depth, N_EXPERTS, 1, 2 * D_FF), w2, b2.reshape(depth, N_EXPERTS, 1, D_MODEL))


def _final_kernel(cnt_ref, cnt_next_ref, base_ref, x_ref, p_ref, rcol_ref, wg_ref, bg_ref, wp_ref, g2_ref, b2_ref,
                  ybuf_ref, out_ref, ys_ref, carry_ref, nrows_ref, sems, *, tm, alpha):
    nrow = _sorted_rows(tm)
    i = pl.program_id(0)
    slot = i % 2

    def run_copy(buf, src, dst, s):
        return pltpu.make_async_copy(ybuf_ref.at[pl.ds(pl.multiple_of(src, SUBLANES), s)],
                                     ys_ref.at[buf, pl.ds(pl.multiple_of(dst, SUBLANES), s)], sems.at[buf])

    def gather_tile(counts_ref, buf):
        def start(e, dst):
            n8 = counts_ref[0, e]
            src = carry_ref[e]
            _for_each_piece(n8, lambda off, s: run_copy(buf, src + off, dst + off, s).start())
            carry_ref[e] = src + n8
            return dst + n8
        nrows_ref[buf] = lax.fori_loop(0, N_EXPERTS, start, 0)

    @pl.when(i == 0)
    def _():
        def init(e, c):
            carry_ref[e] = base_ref[e]
            return c
        lax.fori_loop(0, N_EXPERTS, init, 0)
        ys_ref[:, TOP_K * tm:, :] = jnp.zeros((2, nrow - TOP_K * tm, D_MODEL), F32)
        gather_tile(cnt_ref, 0)

    @pl.when(i + 1 < pl.num_programs(0))
    def _():
        gather_tile(cnt_next_ref, 1 - slot)

    x = x_ref[...]
    ple = (jax.nn.sigmoid(jnp.dot(x.astype(BF16), wg_ref[...], preferred_element_type=F32) + bg_ref[...])
           * jnp.dot(p_ref[...].astype(BF16), wp_ref[...], preferred_element_type=F32))
    col = lax.broadcasted_iota(I32, (1, ONE_HOT_BLOCK), 1).astype(F32).astype(BF16)
    rc = rcol_ref[...]
    gates = rc[:, TOP_K:].astype(BF16)
    blocks = []
    for c in range(nrow // ONE_HOT_BLOCK):
        blk = jnp.zeros((tm, ONE_HOT_BLOCK), BF16)
        for k in range(TOP_K):
            hit = col == (rc[:, k:k + 1] - float(c * ONE_HOT_BLOCK)).astype(BF16)
            blk = jnp.where(hit, gates[:, k:k + 1], blk)
        blocks.append(blk)
    gmat = jnp.concatenate(blocks, axis=1)
    base = alpha * x + ple

    _wait_tile_rows(nrows_ref[slot], tm, lambda s: run_copy(slot, 0, 0, s).wait())

    ffn = jnp.dot(gmat, ys_ref[slot].astype(BF16), preferred_element_type=F32)
    out_ref[...] = _layer_norm(base + ffn, g2_ref[...], b2_ref[...])


def _final(x1, p_all, layer, rcol, cnt8, base, ybuf, lw):
    n = x1.shape[0]
    nt = cnt8.shape[0]
    tm = n // nt
    row = lambda w: pl.BlockSpec((tm, w), lambda i: (i, 0))
    vec = lambda w: _layer_spec(layer, 1, w)
    return pl.pallas_call(
        functools.partial(_final_kernel, tm=tm, alpha=lw["alpha"]),
        grid=(nt,),
        in_specs=[pl.BlockSpec((None, 1, N_EXPERTS), lambda i: (i, 0, 0), memory_space=pltpu.SMEM),
                  pl.BlockSpec((None, 1, N_EXPERTS), lambda i: (jnp.minimum(i + 1, nt - 1), 0, 0),
                               memory_space=pltpu.SMEM),
                  pl.BlockSpec(memory_space=pltpu.SMEM),
                  row(D_MODEL), pl.BlockSpec((None, tm, D_PLE), lambda i: (layer, i, 0)), row(2 * TOP_K),
                  _layer_spec(layer, D_MODEL, D_MODEL), vec(D_MODEL), _layer_spec(layer, D_PLE, D_MODEL),
                  vec(D_MODEL), vec(D_MODEL), pl.BlockSpec(memory_space=pl.ANY)],
        out_specs=row(D_MODEL),
        out_shape=jax.ShapeDtypeStruct((n, D_MODEL), F32),
        scratch_shapes=[pltpu.VMEM((2, _sorted_rows(tm), D_MODEL), F32), pltpu.SMEM((N_EXPERTS,), I32),
                        pltpu.SMEM((2,), I32), pltpu.SemaphoreType.DMA((2,))],
        compiler_params=_cparams("arbitrary"), name="moe_combine_ln2",
    )(cnt8, cnt8, base, x1, p_all, rcol, lw["ple_w_gate"], lw["ple_b_gate"], lw["ple_w_proj"], lw["ln2_g"], lw["ln2_b"],
      ybuf)


def _block_diag(w):
    depth, nb, bw, _ = w.shape
    eye = jnp.eye(nb, dtype=w.dtype)
    return (w[:, :, :, None, :] * eye[None, :, None, :, None]).reshape(depth, nb * bw, nb * bw)


def _hi_lo_rows(w):
    hi = w.astype(BF16)
    lo = (w - hi.astype(F32)).astype(BF16)
    return jnp.concatenate([hi, lo], axis=-2)


def _block_geometry(tot, n_blocks):
    nblk = (tot + MOE_TILE - 1) // MOE_TILE
    blk_end = jnp.cumsum(nblk)
    blk_start = blk_end - nblk
    ids = jnp.arange(n_blocks, dtype=I32)
    n_valid = blk_end[-1]
    expert_at = lambda j: jnp.minimum(jnp.sum(j[:, None] >= blk_end[None, :], axis=1), N_EXPERTS - 1).astype(I32)
    src = jnp.clip(ids, 0, jnp.maximum(n_valid - 1, 0))
    e_of = expert_at(src)
    hot = e_of[:, None] == jnp.arange(N_EXPERTS)[None, :]
    pick = lambda v: jnp.sum(jnp.where(hot, v[None, :], 0), axis=1)
    start_of, end_of, tot_of = pick(blk_start), pick(blk_end), pick(tot)
    real = ids < n_valid
    nv = jnp.where(real, jnp.clip(tot_of - (ids - start_of) * MOE_TILE, 0, MOE_TILE), 0)
    first = jnp.logical_and(real, ids == start_of)
    nxt = jnp.where(end_of < n_valid, expert_at(end_of), -1)
    geom = tuple(v.astype(I32) for v in (src, e_of, nv, first, nxt))
    return geom, (blk_start * MOE_TILE).astype(I32)


def kernel(x_prompt, x_sample, cache_k, cache_v, state_conv, state_h, p_prompt, p_sample, ln_in_g, ln_in_b, rel_bias, w_in, conv_w, conv_b, rg_wa, rg_ba, rg_wx, rg_bx, rg_lambda, attn_sinks, gn_rnn, gn_attn, w_out, ln1_g, ln1_b, router_w, router_b, moe_w1, moe_b1, moe_w2, moe_b2, ple_w_gate, ple_b_gate, ple_w_proj, ln2_g, ln2_b):
    depth = w_in.shape[0]
    bsz, t_len, _ = x_prompt.shape
    n_p = bsz * t_len
    n_s = x_sample.shape[0]
    wb = cache_k.shape[2]
    alpha = (2 * depth) ** 0.25

    dist = jnp.arange(WINDOW)[:, None] + WINDOW - jnp.arange(2 * WINDOW)[None, :]
    valid = (dist >= 0) & (dist < WINDOW)
    bias_prompt = jnp.where(valid[None], _bias_lookup(rel_bias, dist), NEG)
    bias_sample = _bias_lookup(rel_bias, WINDOW - 1 - jnp.arange(WINDOW))

    tm_p = min(ROW_TILE, n_p)
    n_tiles = n_p // tm_p + 1
    max_rows = (n_p + n_s) * TOP_K + n_tiles * N_EXPERTS * (SUBLANES - 1)
    n_blocks = -(-(max_rows + N_EXPERTS * (MOE_TILE - 1)) // MOE_TILE)

    xp = x_prompt.reshape(n_p, D_MODEL)
    xs = x_sample.reshape(n_s, D_MODEL)
    pp = p_prompt.reshape(depth, n_p, D_PLE)
    ps = p_sample.reshape(depth, n_s, D_PLE)
    to_feature_major = lambda c: jnp.transpose(c, (0, 1, 3, 4, 2)).reshape(depth, n_s, N_KV * HEAD_DIM, wb)
    to_position_major = lambda c: jnp.transpose(c.reshape(depth, n_s, N_KV, HEAD_DIM, wb), (0, 1, 4, 2, 3))
    ck = to_feature_major(cache_k)
    cv = to_feature_major(cache_v)
    new_caches = None
    row2 = lambda v: v.reshape(1, -1)
    kv0 = 2 * D_RNN + D_ATTN
    outs = {k: [] for k in ("kp", "vp", "cp", "hp", "cs", "hs")}

    vecs = lambda v: v.reshape(depth, 1, -1)
    rw_all = {"conv_w": conv_w, "conv_b": vecs(conv_b), "wa": _block_diag(rg_wa).astype(BF16),
              "ba": vecs(rg_ba), "wx": _block_diag(rg_wx).astype(BF16), "bx": vecs(rg_bx),
              "lam": vecs(rg_lambda)}
    lw_all = {"gn_rnn": vecs(gn_rnn), "gn_attn": vecs(gn_attn), "w_out": w_out.astype(BF16),
              "ln1_g": vecs(ln1_g), "ln1_b": vecs(ln1_b),
              "router_wt": _hi_lo_rows(jnp.swapaxes(router_w, 1, 2)),
              "router_b": router_b.reshape(depth, N_EXPERTS, 1), "alpha": alpha,
              "ple_w_gate": ple_w_gate.astype(BF16), "ple_b_gate": vecs(ple_b_gate),
              "ple_w_proj": ple_w_proj.astype(BF16), "ln2_g": vecs(ln2_g), "ln2_b": vecs(ln2_b)}
    w_in_b = w_in.astype(BF16)

    for l in range(depth):
        rw = dict(rw_all, layer=l)
        lw = dict(lw_all, layer=l)
        first = l == 0

        xp, zp = _in_proj(xp, row2(ln_in_g), row2(ln_in_b), w_in_b, l, first)
        zp3 = zp.reshape(bsz, t_len, D_IN)
        yp, hp = _rnn_prompt(zp3, rw)
        op = _attn_prompt(zp, bsz, t_len, bias_prompt, attn_sinks, l)
        wp = min(WINDOW, t_len)
        outs["kp"].append(zp3[:, t_len - wp:, kv0:kv0 + 128].reshape(bsz, wp, N_KV, HEAD_DIM))
        outs["vp"].append(zp3[:, t_len - wp:, kv0 + 128:kv0 + 256].reshape(bsz, wp, N_KV, HEAD_DIM))
        outs["cp"].append(zp3[:, t_len - (CONV_W - 1):, :D_RNN])
        outs["hp"].append(hp.reshape(bsz, D_RNN))

        xs, zs = _in_proj(xs, row2(ln_in_g), row2(ln_in_b), w_in_b, l, first)
        ys, hs = _rnn_sample(zs, state_conv, state_h, rw)
        os_, new_caches = _attn_sample(zs[:, 2 * D_RNN:kv0], zs[:, kv0:kv0 + 128], zs[:, kv0 + 128:], ck, cv,
                                       new_caches, l, bias_sample, attn_sinks)
        outs["cs"].append(jnp.concatenate([state_conv[l][:, 1:], zs[:, None, :D_RNN]], axis=1))
        outs["hs"].append(hs)

        x1p, rrow_p, rcol_p, cnt_p = _out_proj(xp, yp.reshape(n_p, D_RNN), op, lw)
        x1s, rrow_s, rcol_s, cnt_s = _out_proj(xs, ys, os_, lw)

        tot_p = jnp.sum(cnt_p, axis=(0, 1))
        tot = tot_p + jnp.sum(cnt_s, axis=(0, 1))
        geom, base_p = _block_geometry(tot, n_blocks)
        base_s = base_p + tot_p

        xbuf = _dispatch(x1p, rrow_p, cnt_p, base_p, None, n_blocks * MOE_TILE)
        xbuf = _dispatch(x1s, rrow_s, cnt_s, base_s, xbuf, n_blocks * MOE_TILE)
        ybuf = _moe_experts(xbuf, geom, moe_w1, moe_b1, moe_w2, moe_b2, l)
        xp = _final(x1p, pp, l, rcol_p, cnt_p, base_p, ybuf, lw)
        xs = _final(x1s, ps, l, rcol_s, cnt_s, base_s, ybuf, lw)

    st = lambda k: jnp.stack(outs[k])
    return (xp.reshape(bsz, t_len, D_MODEL), xs.reshape(n_s, 1, D_MODEL),
            st("kp"), st("vp"), st("cp"), st("hp"), to_position_major(new_caches[0]),
            to_position_major(new_caches[1]), st("cs"), st("hs"))
```

```python
import functools
import math

import jax
import jax.numpy as jnp
from jax import lax
from jax.experimental import pallas as pl
from jax.experimental.pallas import tpu as pltpu

F32 = jnp.float32
BF16 = jnp.bfloat16
I32 = jnp.int32
U32 = jnp.uint32

D_MODEL = 1024
D_RNN = 512
RNN_BLOCKS = 8
CONV_W = 4
LRU_C = 8.0
N_HEADS = 8
HEAD_DIM = 64
N_KV = 2
GQA_R = N_HEADS // N_KV
D_ATTN = N_HEADS * HEAD_DIM
WINDOW = 128
N_BUCKETS = 32
MAX_DIST = 128
N_EXPERTS = 32
TOP_K = 4
D_FF = 1024
SWIGLU_LIMIT = 7.0
SWIGLU_ALPHA = 1.702
D_PLE = 256
LN_EPS = 1e-5
D_IN = 2 * D_RNN + D_ATTN + 2 * N_KV * HEAD_DIM
NEG = -1e30

SUBLANES = 8
LANES = 128
ROW_TILE = 512
MOE_TILE = 1024
D_PACK = D_MODEL // 2
VMEM_LIMIT = 56 * 1024 * 1024
HI_MASK = 0xFFFF0000


def _cparams(*sem):
    return pltpu.CompilerParams(dimension_semantics=sem, vmem_limit_bytes=VMEM_LIMIT)


def _layer_norm(x, g, b):
    mu = jnp.mean(x, axis=-1, keepdims=True)
    xc = x - mu
    var = jnp.mean(xc * xc, axis=-1, keepdims=True)
    return xc * lax.rsqrt(var + LN_EPS) * g + b


def _rms_norm(x, g):
    return x * lax.rsqrt(jnp.mean(x * x, axis=-1, keepdims=True) + LN_EPS) * g


def _const_spec(shape):
    return pl.BlockSpec(shape, lambda *_: (0,) * len(shape))


def _layer_spec(layer, *tail):
    return pl.BlockSpec((None,) + tail, lambda *_: (layer,) + (0,) * len(tail))


def _in_proj_kernel(x_ref, g_ref, b_ref, w_ref, *out_refs, apply_ln):
    x = x_ref[...]
    if apply_ln:
        x = _layer_norm(x, g_ref[...], b_ref[...])
        out_refs[0][...] = x
    out_refs[-1][...] = jnp.dot(x.astype(BF16), w_ref[...], preferred_element_type=F32)


def _in_proj(x, ln_g, ln_b, w_in_bf16, layer, apply_ln):
    n = x.shape[0]
    tm = min(2 * ROW_TILE, n)
    row = lambda w: pl.BlockSpec((tm, w), lambda i: (i, 0))
    out_shape = [jax.ShapeDtypeStruct((n, D_IN), F32)]
    out_specs = [row(D_IN)]
    if apply_ln:
        out_shape.insert(0, jax.ShapeDtypeStruct((n, D_MODEL), F32))
        out_specs.insert(0, row(D_MODEL))
    outs = pl.pallas_call(
        functools.partial(_in_proj_kernel, apply_ln=apply_ln),
        grid=(n // tm,),
        in_specs=[row(D_MODEL), _const_spec((1, D_MODEL)), _const_spec((1, D_MODEL)),
                  _layer_spec(layer, D_MODEL, D_IN)],
        out_specs=out_specs, out_shape=out_shape,
        compiler_params=_cparams("parallel"), name="in_proj",
    )(x, ln_g, ln_b, w_in_bf16)
    return (outs[0], outs[1]) if apply_ln else (x, outs[0])


def _rglru_coeffs(u, wa_ref, ba_ref, wx_ref, bx_ref, lam_ref):
    ub = u.astype(BF16)
    r = jax.nn.sigmoid(jnp.dot(ub, wa_ref[...], preferred_element_type=F32) + ba_ref[...])
    i = jax.nn.sigmoid(jnp.dot(ub, wx_ref[...], preferred_element_type=F32) + bx_ref[...])
    lam = -lam_ref[...]
    softplus = jnp.maximum(lam, 0.0) + jnp.log1p(jnp.exp(-jnp.abs(lam)))
    log_a = -LRU_C * r * softplus
    a = jnp.exp(log_a)
    b = jnp.sqrt(-jnp.tanh(log_a) * (a * a + 1.0)) * (i * u)
    return a, b


def _rnn_prompt_kernel(xr_ref, gate_ref, cw_ref, cb_ref, wa_ref, ba_ref, wx_ref, bx_ref, lam_ref,
                       y_ref, hlast_ref, xs_ref, a_ref, b_ref, carry_ref, *, tt):
    t = pl.program_id(1)

    @pl.when(t == 0)
    def _():
        xs_ref[0:SUBLANES, :] = jnp.zeros((SUBLANES, D_RNN), F32)
        carry_ref[...] = jnp.zeros((SUBLANES, D_RNN), F32)

    x = xr_ref[...]
    xs_ref[SUBLANES:SUBLANES + tt, :] = x
    cw = cw_ref[...]
    u = cb_ref[...] + x * cw[3:4]
    for j in range(CONV_W - 1):
        off = SUBLANES - (CONV_W - 1) + j
        u = u + xs_ref[off:off + tt, :] * cw[j:j + 1]
    xs_ref[0:SUBLANES, :] = x[tt - SUBLANES:tt, :]

    a, b = _rglru_coeffs(u, wa_ref, ba_ref, wx_ref, bx_ref, lam_ref)
    ng = tt // SUBLANES
    a = a.reshape(ng, SUBLANES, D_RNN)
    b = b.reshape(ng, SUBLANES, D_RNN)
    row = lax.broadcasted_iota(I32, (1, SUBLANES, 1), 1)
    for s in (1, 2, 4):
        a_sh = pltpu.roll(a, s, 1)
        b_sh = pltpu.roll(b, s, 1)
        m = row >= s
        b = jnp.where(m, a * b_sh + b, b)
        a = jnp.where(m, a * a_sh, a)
    a_ref[...] = a
    b_ref[...] = b

    def body(g, carry):
        h = a_ref[g] * carry + b_ref[g]
        b_ref[g] = h
        return jnp.broadcast_to(h[SUBLANES - 1:SUBLANES, :], (SUBLANES, D_RNN))

    carry = lax.fori_loop(0, ng, body, carry_ref[...], unroll=True)
    carry_ref[...] = carry
    h = b_ref[...].reshape(tt, D_RNN)
    y_ref[...] = h * jax.nn.gelu(gate_ref[...])
    hlast_ref[...] = carry[0:1, :]


def _rnn_prompt(z3, rw):
    bsz, t_len, _ = z3.shape
    tt = min(ROW_TILE, t_len)
    vec = _layer_spec(rw["layer"], 1, D_RNN)
    mat = _layer_spec(rw["layer"], D_RNN, D_RNN)
    return pl.pallas_call(
        functools.partial(_rnn_prompt_kernel, tt=tt),
        grid=(bsz, t_len // tt),
        in_specs=[pl.BlockSpec((None, tt, D_RNN), lambda b, t: (b, t, 0)),
                  pl.BlockSpec((None, tt, D_RNN), lambda b, t: (b, t, 1)),
                  _layer_spec(rw["layer"], CONV_W, D_RNN), vec, mat, vec, mat, vec, vec],
        out_specs=[pl.BlockSpec((None, tt, D_RNN), lambda b, t: (b, t, 0)),
                   pl.BlockSpec((None, 1, D_RNN), lambda b, t: (b, 0, 0))],
        out_shape=[jax.ShapeDtypeStruct((bsz, t_len, D_RNN), F32),
                   jax.ShapeDtypeStruct((bsz, 1, D_RNN), F32)],
        scratch_shapes=[pltpu.VMEM((tt + SUBLANES, D_RNN), F32),
                        pltpu.VMEM((tt // SUBLANES, SUBLANES, D_RNN), F32),
                        pltpu.VMEM((tt // SUBLANES, SUBLANES, D_RNN), F32),
                        pltpu.VMEM((SUBLANES, D_RNN), F32)],
        compiler_params=_cparams("parallel", "arbitrary"), name="rnn_prompt",
    )(z3, z3, rw["conv_w"], rw["conv_b"], rw["wa"], rw["ba"], rw["wx"], rw["bx"], rw["lam"])


def _rnn_sample_kernel(xr_ref, gate_ref, c0_ref, c1_ref, c2_ref, h0_ref, cw_ref, cb_ref, wa_ref, ba_ref,
                       wx_ref, bx_ref, lam_ref, y_ref, h_ref):
    x = xr_ref[...]
    cw = cw_ref[...]
    u = (cb_ref[...] + c0_ref[...] * cw[0:1] + c1_ref[...] * cw[1:2] + c2_ref[...] * cw[2:3]
         + x * cw[3:4])
    a, b = _rglru_coeffs(u, wa_ref, ba_ref, wx_ref, bx_ref, lam_ref)
    h = a * h0_ref[...] + b
    h_ref[...] = h
    y_ref[...] = h * jax.nn.gelu(gate_ref[...])


def _rnn_sample(z, conv_state, h0, rw):
    n = z.shape[0]
    layer = rw["layer"]
    vec = _layer_spec(layer, 1, D_RNN)
    mat = _layer_spec(layer, D_RNN, D_RNN)
    full = _const_spec((n, D_RNN))
    depth = conv_state.shape[0]
    conv_flat = conv_state.reshape(depth, n, (CONV_W - 1) * D_RNN)
    tap = lambda j: pl.BlockSpec((None, n, D_RNN), lambda i: (layer, 0, j))
    return pl.pallas_call(
        _rnn_sample_kernel,
        grid=(1,),
        in_specs=[pl.BlockSpec((n, D_RNN), lambda i: (0, 0)), pl.BlockSpec((n, D_RNN), lambda i: (0, 1)),
                  tap(0), tap(1), tap(2), _layer_spec(layer, n, D_RNN),
                  _layer_spec(layer, CONV_W, D_RNN), vec, mat, vec, mat, vec, vec],
        out_specs=[full, full],
        out_shape=[jax.ShapeDtypeStruct((n, D_RNN), F32)] * 2,
        compiler_params=_cparams("arbitrary"), name="rnn_sample",
    )(z, z, conv_flat, conv_flat, conv_flat, h0, rw["conv_w"], rw["conv_b"], rw["wa"], rw["ba"], rw["wx"], rw["bx"], rw["lam"])


def _rel_bucket(dist):
    n = jnp.maximum(dist, 0)
    max_exact = N_BUCKETS // 2
    nf = jnp.maximum(n, max_exact).astype(F32)
    large = max_exact + (jnp.log(nf / max_exact) / math.log(MAX_DIST / max_exact)
                         * (N_BUCKETS - max_exact)).astype(I32)
    large = jnp.minimum(large, N_BUCKETS - 1)
    return jnp.where(n < max_exact, n, large)


def _bias_lookup(rel_bias, dist):
    hot = (_rel_bucket(dist)[..., None] == jnp.arange(N_BUCKETS)).astype(F32)
    out = jnp.tensordot(hot, rel_bias.astype(F32), axes=1, precision=lax.Precision.HIGHEST)
    return jnp.moveaxis(out, -1, 0)


def _half_lane_variants(x, group):
    lo = lax.broadcasted_iota(I32, (1, 2 * HEAD_DIM), 1) < HEAD_DIM
    xr = pltpu.roll(x, HEAD_DIM, 1)
    zero = jnp.zeros_like(x)
    if group == 0:
        return jnp.where(lo, x, zero), jnp.where(lo, zero, xr)
    return jnp.where(lo, xr, zero), jnp.where(lo, zero, x)


def _attn_prompt_kernel(q_ref, kc_ref, kp_ref, vc_ref, vp_ref, bias_ref, sink_ref, o_ref, *, nq, layer):
    first = pl.program_id(1) == 0
    col = lax.broadcasted_iota(I32, (1, 2 * WINDOW), 1)
    hide_prev = jnp.logical_and(first, col < WINDOW)
    for j in range(nq):
        rows = slice(j * WINDOW, (j + 1) * WINDOW)
        before = slice((j - 1) * WINDOW, j * WINDOW)
        kk = jnp.concatenate([kp_ref[...] if j == 0 else kc_ref[before, :], kc_ref[rows, :]], axis=0)
        vv = jnp.concatenate([vp_ref[...] if j == 0 else vc_ref[before, :], vc_ref[rows, :]], axis=0)
        for g in range(N_KV):
            k_lo, k_hi = _half_lane_variants(kk, g)
            v_lo, v_hi = _half_lane_variants(vv, g)
            for pair in range(GQA_R // 2):
                p0 = g * (GQA_R // 2) + pair
                qp = (q_ref[rows, p0 * 128:(p0 + 1) * 128] * (HEAD_DIM ** -0.5)).astype(BF16)
                acc = jnp.zeros((WINDOW, 2 * HEAD_DIM), F32)
                for kx, vx, h in ((k_lo, v_lo, 2 * p0), (k_hi, v_hi, 2 * p0 + 1)):
                    s = lax.dot_general(qp, kx.astype(BF16), (((1,), (1,)), ((), ())),
                                        preferred_element_type=F32) + bias_ref[h]
                    if j == 0:
                        s = jnp.where(hide_prev, NEG, s)
                    sink = sink_ref[layer, h]
                    m = jnp.maximum(jnp.max(s, axis=-1, keepdims=True), sink)
                    p = jnp.exp(s - m)
                    denom = jnp.sum(p, axis=-1, keepdims=True) + jnp.exp(sink - m)
                    acc = acc + jnp.dot(p.astype(BF16), vx.astype(BF16), preferred_element_type=F32) / denom
                o_ref[rows, p0 * 128:(p0 + 1) * 128] = acc


def _attn_prompt(z, bsz, t_len, bias_prompt, sinks, layer):
    nb = t_len // WINDOW
    nq = min(4, nb)
    ns = nb // nq
    kcol = (2 * D_RNN + D_ATTN) // 128
    cur = lambda c: pl.BlockSpec((nq * WINDOW, 128), lambda b, i: (b * ns + i, c))
    prev = lambda c: pl.BlockSpec((WINDOW, 128), lambda b, i: (b * nb + jnp.maximum(i * nq - 1, 0), c))
    return pl.pallas_call(
        functools.partial(_attn_prompt_kernel, nq=nq, layer=layer),
        grid=(bsz, ns),
        in_specs=[pl.BlockSpec((nq * WINDOW, D_ATTN), lambda b, i: (b * ns + i, 2 * D_RNN // D_ATTN)),
                  cur(kcol), prev(kcol), cur(kcol + 1), prev(kcol + 1),
                  _const_spec((N_HEADS, WINDOW, 2 * WINDOW)),
                  pl.BlockSpec(memory_space=pltpu.SMEM)],
        out_specs=pl.BlockSpec((nq * WINDOW, D_ATTN), lambda b, i: (b * ns + i, 0)),
        out_shape=jax.ShapeDtypeStruct((bsz * t_len, D_ATTN), F32),
        compiler_params=_cparams("parallel", "arbitrary"), name="attn_prompt",
    )(z, z, z, z, z, bias_prompt, sinks)


def _attn_sample_kernel(q_ref, kn_ref, vn_ref, ck_ref, cv_ref, bias_ref, sink_ref, hmask_ref, *rest):
    o_ref, nk_ref, nv_ref = rest[-3:]
    pos = lax.broadcasted_iota(I32, (1, 1, WINDOW), 2)
    last = pos == WINDOW - 1
    bb = ck_ref.shape[0]
    newk = jnp.where(last, jnp.broadcast_to(kn_ref[...], (bb, 128, WINDOW)), pltpu.roll(ck_ref[...], WINDOW - 1, 2))
    newv = jnp.where(last, jnp.broadcast_to(vn_ref[...], (bb, 128, WINDOW)), pltpu.roll(cv_ref[...], WINDOW - 1, 2))
    nk_ref[...] = newk
    nv_ref[...] = newv
    s = lax.dot_general(q_ref[...].astype(BF16), newk.astype(BF16), (((2,), (1,)), ((0,), (0,))),
                        preferred_element_type=F32)
    s = s + bias_ref[...][None]
    sink = sink_ref[...][None]
    m = jnp.maximum(jnp.max(s, axis=-1, keepdims=True), sink)
    p = jnp.exp(s - m)
    denom = jnp.sum(p, axis=-1, keepdims=True) + jnp.exp(sink - m)
    o = lax.dot_general(p.astype(BF16), newv.astype(BF16), (((2,), (2,)), ((0,), (0,))),
                        preferred_element_type=F32)
    o_ref[...] = o / denom * hmask_ref[...][None]


def _attn_sample(q, k_new, v_new, k_cache, v_cache, new_caches, layer, bias_sample, sinks):
    n = q.shape[0]
    depth = k_cache.shape[0]
    bb = 32
    qh = q.reshape(n, N_KV, GQA_R, HEAD_DIM) * (HEAD_DIM ** -0.5)
    eye = jnp.eye(N_KV, dtype=F32)
    q8 = (qh[:, :, :, None, :] * eye[None, :, None, :, None]).reshape(n, N_HEADS, N_KV * HEAD_DIM)
    hmask = jnp.repeat(jnp.repeat(eye, GQA_R, axis=0), HEAD_DIM, axis=1)
    blk3 = lambda a, b: pl.BlockSpec((bb, a, b), lambda i: (i, 0, 0))
    cache = pl.BlockSpec((None, bb, 128, WINDOW), lambda i: (layer, i, 0, 0))
    in_specs = [blk3(N_HEADS, 128), blk3(128, 1), blk3(128, 1), cache, cache,
                _const_spec((N_HEADS, WINDOW)), _layer_spec(layer, N_HEADS, 1), _const_spec((N_HEADS, 128))]
    args = [q8, k_new.reshape(n, 128, 1), v_new.reshape(n, 128, 1), k_cache, v_cache, bias_sample,
            sinks.reshape(-1, N_HEADS, 1), hmask]
    aliases = {}
    if new_caches is not None:
        in_specs += [pl.BlockSpec(memory_space=pl.ANY)] * 2
        args += list(new_caches)
        aliases = {len(args) - 2: 1, len(args) - 1: 2}
    o8, nk, nv = pl.pallas_call(
        _attn_sample_kernel,
        grid=(n // bb,),
        in_specs=in_specs,
        out_specs=[blk3(N_HEADS, 128), cache, cache],
        out_shape=[jax.ShapeDtypeStruct((n, N_HEADS, 128), F32),
                   jax.ShapeDtypeStruct((depth, n, 128, WINDOW), F32),
                   jax.ShapeDtypeStruct((depth, n, 128, WINDOW), F32)],
        input_output_aliases=aliases,
        compiler_params=_cparams("arbitrary"), name="attn_sample",
    )(*args)
    o = o8.reshape(n, N_KV, GQA_R, N_KV, HEAD_DIM)
    o = jnp.stack([o[:, g, :, g, :] for g in range(N_KV)], axis=1).reshape(n, D_ATTN)
    return o, (nk, nv)


ONE_HOT_BLOCK = 256


def _sorted_rows(tm):
    return TOP_K * tm + N_EXPERTS * SUBLANES


def _out_proj_kernel(x_ref, yr_ref, o_ref, gnr_ref, gna_ref, w_ref, g1_ref, b1_ref, rwt_ref, rb_ref,
                     x1_ref, rrow_ref, rcol_ref, cnt_ref, *, tm, alpha):
    yn = _rms_norm(yr_ref[...], gnr_ref[...]).astype(BF16)
    on = _rms_norm(o_ref[...], gna_ref[...]).astype(BF16)
    mix = jnp.dot(jnp.concatenate([yn, on], axis=1), w_ref[...], preferred_element_type=F32)
    x1 = _layer_norm(alpha * x_ref[...] + mix, g1_ref[...], b1_ref[...])
    x1_ref[...] = x1

    x1h = x1.astype(BF16)
    x1l = (x1 - x1h.astype(F32)).astype(BF16)
    nt_dims = (((1,), (1,)), ((), ()))
    by_hi = lax.dot_general(rwt_ref[...], x1h, nt_dims, preferred_element_type=F32)
    by_lo = lax.dot_general(rwt_ref[0:N_EXPERTS, :], x1l, nt_dims, preferred_element_type=F32)
    logits = by_hi[0:N_EXPERTS] + by_hi[N_EXPERTS:] + by_lo + rb_ref[...]
    eidx = lax.broadcasted_iota(I32, (N_EXPERTS, tm), 0).astype(F32)
    work = logits
    vals, hots = [], []
    for _ in range(TOP_K):
        v = jnp.max(work, axis=0, keepdims=True)
        idx = jnp.min(jnp.where(work == v, eidx, float(N_EXPERTS)), axis=0, keepdims=True)
        hot = eidx == idx
        work = jnp.where(hot, -jnp.inf, work)
        vals.append(v)
        hots.append(hot.astype(F32))
    ex = [jnp.exp(v - vals[0]) for v in vals]
    tot = ex[0] + ex[1] + ex[2] + ex[3]
    tok_hot = hots[0] + hots[1] + hots[2] + hots[3]
    s_i = lax.broadcasted_iota(I32, (tm, tm), 0)
    t_i = lax.broadcasted_iota(I32, (tm, tm), 1)
    earlier = (s_i < t_i).astype(BF16)
    before = jnp.dot(tok_hot.astype(BF16), earlier, preferred_element_type=F32)
    cnt = jnp.sum(tok_hot, axis=1, keepdims=True)
    cnt8 = jnp.floor((cnt + (SUBLANES - 1)) * (1.0 / SUBLANES)) * SUBLANES
    cnt8_b = jnp.broadcast_to(cnt8, (N_EXPERTS, LANES))
    e_r = lax.broadcasted_iota(I32, (N_EXPERTS, N_EXPERTS), 0)
    e_c = lax.broadcasted_iota(I32, (N_EXPERTS, N_EXPERTS), 1)
    start = jnp.dot((e_c < e_r).astype(F32), cnt8_b, preferred_element_type=F32,
                    precision=lax.Precision.HIGHEST)[:, 0:1]
    slot = start + before
    rows = [jnp.sum(h * slot, axis=0, keepdims=True) for h in hots] + [e / tot for e in ex]
    rrow = jnp.concatenate(rows, axis=0)
    rrow_ref[...] = rrow
    rcol_ref[...] = rrow.T
    cnt_ref[...] = cnt8_b.T[0:1, 0:N_EXPERTS].astype(I32)


def _out_proj(x, y_rnn, o, lw):
    n = x.shape[0]
    tm = min(ROW_TILE, n)
    nt = n // tm
    row = lambda w: pl.BlockSpec((tm, w), lambda i: (i, 0))
    layer = lw["layer"]
    vec = lambda w: _layer_spec(layer, 1, w)
    return pl.pallas_call(
        functools.partial(_out_proj_kernel, tm=tm, alpha=lw["alpha"]),
        grid=(nt,),
        in_specs=[row(D_MODEL), row(D_RNN), row(D_ATTN), vec(D_RNN), vec(D_ATTN),
                  _layer_spec(layer, D_MODEL, D_MODEL), vec(D_MODEL), vec(D_MODEL),
                  _layer_spec(layer, 2 * N_EXPERTS, D_MODEL), _layer_spec(layer, N_EXPERTS, 1)],
        out_specs=[row(D_MODEL), pl.BlockSpec((None, 2 * TOP_K, tm), lambda i: (i, 0, 0)), row(2 * TOP_K),
                   pl.BlockSpec((None, 1, N_EXPERTS), lambda i: (i, 0, 0))],
        out_shape=[jax.ShapeDtypeStruct((n, D_MODEL), F32), jax.ShapeDtypeStruct((nt, 2 * TOP_K, tm), F32),
                   jax.ShapeDtypeStruct((n, 2 * TOP_K), F32), jax.ShapeDtypeStruct((nt, 1, N_EXPERTS), I32)],
        compiler_params=_cparams("parallel"), name="out_proj_route",
    )(x, y_rnn, o, lw["gn_rnn"], lw["gn_attn"], lw["w_out"], lw["ln1_g"], lw["ln1_b"],
      lw["router_wt"], lw["router_b"])


RUN_CHUNK = 64
_SMALL_PIECES = (32, 16, 8)
_PAD_PIECES = (128, 64, 32, 16, 8)


def _for_each_piece(n8, fn):
    def chunk(j, c):
        fn(pl.multiple_of(j * RUN_CHUNK, RUN_CHUNK), RUN_CHUNK)
        return c
    lax.fori_loop(0, n8 // RUN_CHUNK, chunk, 0)
    for s in _SMALL_PIECES:
        @pl.when((n8 & s) != 0)
        def _(s=s):
            fn(n8 & (-2 * s), s)


def _wait_tile_rows(n_rows, tm, wait_rows):
    wait_rows(TOP_K * tm)
    pad = n_rows - TOP_K * tm
    for s in _PAD_PIECES:
        @pl.when((pad & s) != 0)
        def _(s=s):
            wait_rows(s)


def _dispatch_kernel(cnt_ref, base_ref, x_ref, rrow_ref, *rest, tm):
    xbuf_ref, xs_ref, carry_ref, nrows_ref, sems = rest[-5:]
    nrow = _sorted_rows(tm)
    i = pl.program_id(0)
    slot = i % 2

    @pl.when(i == 0)
    def _():
        def init(e, c):
            carry_ref[e] = base_ref[e]
            return c
        lax.fori_loop(0, N_EXPERTS, init, 0)

    r = lax.broadcasted_iota(I32, (ONE_HOT_BLOCK, 1), 0).astype(F32).astype(BF16)
    blocks = []
    for c in range(nrow // ONE_HOT_BLOCK):
        hit = None
        for k in range(TOP_K):
            h = r == (rrow_ref[k:k + 1, :] - float(c * ONE_HOT_BLOCK)).astype(BF16)
            hit = h if hit is None else jnp.logical_or(hit, h)
        blocks.append(jnp.where(hit, jnp.ones((), BF16), jnp.zeros((), BF16)))
    perm = jnp.concatenate(blocks, axis=0)
    xs = jnp.dot(perm, x_ref[...].astype(BF16), preferred_element_type=F32)
    lo = lax.bitcast_convert_type(xs[:, :D_PACK], U32)
    hi = lax.bitcast_convert_type(xs[:, D_PACK:], U32)
    xs_ref[slot] = (hi & U32(HI_MASK)) | (lo >> 16)

    def run_copy(buf, src, dst, s):
        return pltpu.make_async_copy(xs_ref.at[buf, pl.ds(pl.multiple_of(src, SUBLANES), s)],
                                     xbuf_ref.at[pl.ds(pl.multiple_of(dst, SUBLANES), s)], sems.at[buf])

    def start(e, src):
        n8 = cnt_ref[0, e]
        dst = carry_ref[e]
        _for_each_piece(n8, lambda off, s: run_copy(slot, src + off, dst + off, s).start())
        carry_ref[e] = dst + n8
        return src + n8

    def wait_tile(buf):
        _wait_tile_rows(nrows_ref[buf], tm, lambda s: run_copy(buf, 0, 0, s).wait())

    nrows_ref[slot] = lax.fori_loop(0, N_EXPERTS, start, 0)

    @pl.when(i > 0)
    def _():
        wait_tile(1 - slot)

    @pl.when(i == pl.num_programs(0) - 1)
    def _():
        wait_tile(slot)


def _dispatch(x1, rrow, cnt8, base, xbuf, n_rows):
    n = x1.shape[0]
    nt, _, tm = rrow.shape
    in_specs = [pl.BlockSpec((None, 1, N_EXPERTS), lambda i: (i, 0, 0), memory_space=pltpu.SMEM),
                pl.BlockSpec(memory_space=pltpu.SMEM),
                pl.BlockSpec((tm, D_MODEL), lambda i: (i, 0)),
                pl.BlockSpec((None, 2 * TOP_K, tm), lambda i: (i, 0, 0))]
    args = [cnt8, base, x1, rrow]
    aliases = {}
    if xbuf is not None:
        in_specs.append(pl.BlockSpec(memory_space=pl.ANY))
        args.append(xbuf)
        aliases = {len(args) - 1: 0}
    return pl.pallas_call(
        functools.partial(_dispatch_kernel, tm=tm),
        grid=(nt,),
        in_specs=in_specs,
        out_specs=pl.BlockSpec(memory_space=pl.ANY),
        out_shape=jax.ShapeDtypeStruct((n_rows, D_PACK), U32),
        scratch_shapes=[pltpu.VMEM((2, _sorted_rows(tm), D_PACK), U32), pltpu.SMEM((N_EXPERTS,), I32),
                        pltpu.SMEM((2,), I32), pltpu.SemaphoreType.DMA((2,))],
        input_output_aliases=aliases,
        compiler_params=_cparams("arbitrary"), name="moe_dispatch",
    )(*args)


def _moe_kernel(br_ref, be_ref, nv_ref, first_ref, next_ref, x_ref, w1_hbm, b1_ref, w2_hbm, b2_ref, y_ref,
                w1f_ref, w2f_ref, w1s_ref, w2s_ref, sems, *, layer):
    i = pl.program_id(0)

    def load(e):
        return (pltpu.make_async_copy(w1_hbm.at[layer, e], w1f_ref, sems.at[0]),
                pltpu.make_async_copy(w2_hbm.at[layer, e], w2f_ref, sems.at[1]))

    @pl.when(jnp.logical_and(i == 0, first_ref[0] == 1))
    def _():
        for c in load(be_ref[0]):
            c.start()

    nv = nv_ref[i]
    quarter = MOE_TILE // 4
    is_first = first_ref[i] == 1
    fresh_full = jnp.logical_and(is_first, nv > 3 * quarter)

    def start_next_load():
        @pl.when(next_ref[i] >= 0)
        def _():
            for c in load(next_ref[i]):
                c.start()

    def expert_rows(m, fresh):
        if fresh:
            w1 = w1f_ref[...].astype(BF16)
            w2 = w2f_ref[...].astype(BF16)
            w1s_ref[...] = w1
            w2s_ref[...] = w2
        else:
            w1 = w1s_ref[...]
            w2 = w2s_ref[...]
        rows = lax.broadcasted_iota(I32, (m, 1), 0)
        xw = jnp.where(rows < nv, x_ref[0:m, :], U32(0))
        lo = lax.bitcast_convert_type(xw << 16, F32).astype(BF16)
        hi = lax.bitcast_convert_type(xw & U32(HI_MASK), F32).astype(BF16)
        xb = jnp.concatenate([lo, hi], axis=1)
        hdn = jnp.dot(xb, w1, preferred_element_type=F32) + b1_ref[...]
        g = jnp.minimum(hdn[:, :D_FF], SWIGLU_LIMIT)
        lin = jnp.clip(hdn[:, D_FF:], -SWIGLU_LIMIT, SWIGLU_LIMIT)
        act = g * jax.nn.sigmoid(SWIGLU_ALPHA * g) * (lin + 1.0)
        y_ref[0:m, :] = jnp.dot(act.astype(BF16), w2, preferred_element_type=F32) + b2_ref[...]

    @pl.when(is_first)
    def _():
        for c in load(be_ref[i]):
            c.wait()

    @pl.when(fresh_full)
    def _():
        expert_rows(MOE_TILE, True)
        start_next_load()

    @pl.when(jnp.logical_and(is_first, jnp.logical_not(fresh_full)))
    def _():
        w1s_ref[...] = w1f_ref[...].astype(BF16)
        w2s_ref[...] = w2f_ref[...].astype(BF16)
        start_next_load()

    for q in range(1, 5):
        @pl.when(jnp.logical_and(jnp.logical_not(fresh_full),
                                 jnp.logical_and(nv > (q - 1) * quarter, nv <= q * quarter)))
        def _(q=q):
            expert_rows(q * quarter, False)


def _moe_experts(xbuf, geom, w1, b1, w2, b2, layer):
    n_blocks = xbuf.shape[0] // MOE_TILE
    bias = lambda w: pl.BlockSpec((None, None, 1, w), lambda i, br, be, *_: (layer, be[i], 0, 0))
    grid_spec = pltpu.PrefetchScalarGridSpec(
        num_scalar_prefetch=5,
        grid=(n_blocks,),
        in_specs=[pl.BlockSpec((MOE_TILE, D_PACK), lambda i, br, *_: (br[i], 0)),
                  pl.BlockSpec(memory_space=pl.ANY), bias(2 * D_FF),
                  pl.BlockSpec(memory_space=pl.ANY), bias(D_MODEL)],
        out_specs=pl.BlockSpec((MOE_TILE, D_MODEL), lambda i, br, *_: (br[i], 0)),
        scratch_shapes=[pltpu.VMEM((D_MODEL, 2 * D_FF), F32), pltpu.VMEM((D_FF, D_MODEL), F32),
                        pltpu.VMEM((D_MODEL, 2 * D_FF), BF16), pltpu.VMEM((D_FF, D_MODEL), BF16),
                        pltpu.SemaphoreType.DMA((2,))],
    )
    depth = w1.shape[0]
    return pl.pallas_call(
        functools.partial(_moe_kernel, layer=layer), grid_spec=grid_spec,
        out_shape=jax.ShapeDtypeStruct((xbuf.shape[0], D_MODEL), F32),
        compiler_params=_cparams("arbitrary"), name="moe_experts",
    )(*geom, xbuf, w1, b1.reshape(depth, N_EXPERTS, 1, 2 * D_FF), w2, b2.reshape(depth, N_EXPERTS, 1, D_MODEL))


def _final_kernel(cnt_ref, cnt_next_ref, base_ref, x_ref, p_ref, rcol_ref, wg_ref, bg_ref, wp_ref, g2_ref, b2_ref,
                  ybuf_ref, out_ref, ys_ref, carry_ref, nrows_ref, sems, *, tm, alpha):
    nrow = _sorted_rows(tm)
    i = pl.program_id(0)
    slot = i % 2

    def run_copy(buf, src, dst, s):
        return pltpu.make_async_copy(ybuf_ref.at[pl.ds(pl.multiple_of(src, SUBLANES), s)],
                                     ys_ref.at[buf, pl.ds(pl.multiple_of(dst, SUBLANES), s)], sems.at[buf])

    def gather_tile(counts_ref, buf):
        def start(e, dst):
            n8 = counts_ref[0, e]
            src = carry_ref[e]
            _for_each_piece(n8, lambda off, s: run_copy(buf, src + off, dst + off, s).start())
            carry_ref[e] = src + n8
            return dst + n8
        nrows_ref[buf] = lax.fori_loop(0, N_EXPERTS, start, 0)

    @pl.when(i == 0)
    def _():
        def init(e, c):
            carry_ref[e] = base_ref[e]
            return c
        lax.fori_loop(0, N_EXPERTS, init, 0)
        ys_ref[:, TOP_K * tm:, :] = jnp.zeros((2, nrow - TOP_K * tm, D_MODEL), F32)
        gather_tile(cnt_ref, 0)

    @pl.when(i + 1 < pl.num_programs(0))
    def _():
        gather_tile(cnt_next_ref, 1 - slot)

    x = x_ref[...]
    ple = (jax.nn.sigmoid(jnp.dot(x.astype(BF16), wg_ref[...], preferred_element_type=F32) + bg_ref[...])
           * jnp.dot(p_ref[...].astype(BF16), wp_ref[...], preferred_element_type=F32))
    col = lax.broadcasted_iota(I32, (1, ONE_HOT_BLOCK), 1).astype(F32).astype(BF16)
    rc = rcol_ref[...]
    gates = rc[:, TOP_K:].astype(BF16)
    blocks = []
    for c in range(nrow // ONE_HOT_BLOCK):
        blk = jnp.zeros((tm, ONE_HOT_BLOCK), BF16)
        for k in range(TOP_K):
            hit = col == (rc[:, k:k + 1] - float(c * ONE_HOT_BLOCK)).astype(BF16)
            blk = jnp.where(hit, gates[:, k:k + 1], blk)
        blocks.append(blk)
    gmat = jnp.concatenate(blocks, axis=1)
    base = alpha * x + ple

    _wait_tile_rows(nrows_ref[slot], tm, lambda s: run_copy(slot, 0, 0, s).wait())

    ffn = jnp.dot(gmat, ys_ref[slot].astype(BF16), preferred_element_type=F32)
    out_ref[...] = _layer_norm(base + ffn, g2_ref[...], b2_ref[...])


def _final(x1, p_all, layer, rcol, cnt8, base, ybuf, lw):
    n = x1.shape[0]
    nt = cnt8.shape[0]
    tm = n // nt
    row = lambda w: pl.BlockSpec((tm, w), lambda i: (i, 0))
    vec = lambda w: _layer_spec(layer, 1, w)
    return pl.pallas_call(
        functools.partial(_final_kernel, tm=tm, alpha=lw["alpha"]),
        grid=(nt,),
        in_specs=[pl.BlockSpec((None, 1, N_EXPERTS), lambda i: (i, 0, 0), memory_space=pltpu.SMEM),
                  pl.BlockSpec((None, 1, N_EXPERTS), lambda i: (jnp.minimum(i + 1, nt - 1), 0, 0),
                               memory_space=pltpu.SMEM),
                  pl.BlockSpec(memory_space=pltpu.SMEM),
                  row(D_MODEL), pl.BlockSpec((None, tm, D_PLE), lambda i: (layer, i, 0)), row(2 * TOP_K),
                  _layer_spec(layer, D_MODEL, D_MODEL), vec(D_MODEL), _layer_spec(layer, D_PLE, D_MODEL),
                  vec(D_MODEL), vec(D_MODEL), pl.BlockSpec(memory_space=pl.ANY)],
        out_specs=row(D_MODEL),
        out_shape=jax.ShapeDtypeStruct((n, D_MODEL), F32),
        scratch_shapes=[pltpu.VMEM((2, _sorted_rows(tm), D_MODEL), F32), pltpu.SMEM((N_EXPERTS,), I32),
                        pltpu.SMEM((2,), I32), pltpu.SemaphoreType.DMA((2,))],
        compiler_params=_cparams("arbitrary"), name="moe_combine_ln2",
    )(cnt8, cnt8, base, x1, p_all, rcol, lw["ple_w_gate"], lw["ple_b_gate"], lw["ple_w_proj"], lw["ln2_g"], lw["ln2_b"],
      ybuf)


def _block_diag(w):
    depth, nb, bw, _ = w.shape
    eye = jnp.eye(nb, dtype=w.dtype)
    return (w[:, :, :, None, :] * eye[None, :, None, :, None]).reshape(depth, nb * bw, nb * bw)


def _hi_lo_rows(w):
    hi = w.astype(BF16)
    lo = (w - hi.astype(F32)).astype(BF16)
    return jnp.concatenate([hi, lo], axis=-2)


def _block_geometry(tot, n_blocks):
    nblk = (tot + MOE_TILE - 1) // MOE_TILE
    blk_end = jnp.cumsum(nblk)
    blk_start = blk_end - nblk
    ids = jnp.arange(n_blocks, dtype=I32)
    n_valid = blk_end[-1]
    expert_at = lambda j: jnp.minimum(jnp.sum(j[:, None] >= blk_end[None, :], axis=1), N_EXPERTS - 1).astype(I32)
    src = jnp.clip(ids, 0, jnp.maximum(n_valid - 1, 0))
    e_of = expert_at(src)
    hot = e_of[:, None] == jnp.arange(N_EXPERTS)[None, :]
    pick = lambda v: jnp.sum(jnp.where(hot, v[None, :], 0), axis=1)
    start_of, end_of, tot_of = pick(blk_start), pick(blk_end), pick(tot)
    real = ids < n_valid
    nv = jnp.where(real, jnp.clip(tot_of - (ids - start_of) * MOE_TILE, 0, MOE_TILE), 0)
    first = jnp.logical_and(real, ids == start_of)
    nxt = jnp.where(end_of < n_valid, expert_at(end_of), -1)
    geom = tuple(v.astype(I32) for v in (src, e_of, nv, first, nxt))
    return geom, (blk_start * MOE_TILE).astype(I32)


def kernel(x_prompt, x_sample, cache_k, cache_v, state_conv, state_h, p_prompt, p_sample, ln_in_g, ln_in_b, rel_bias, w_in, conv_w, conv_b, rg_wa, rg_ba, rg_wx, rg_bx, rg_lambda, attn_sinks, gn_rnn, gn_attn, w_out, ln1_g, ln1_b, router_w, router_b, moe_w1, moe_b1, moe_w2, moe_b2, ple_w_gate, ple_b_gate, ple_w_proj, ln2_g, ln2_b):
    depth = w_in.shape[0]
    bsz, t_len, _ = x_prompt.shape
    n_p = bsz * t_len
    n_s = x_sample.shape[0]
    wb = cache_k.shape[2]
    alpha = (2 * depth) ** 0.25

    dist = jnp.arange(WINDOW)[:, None] + WINDOW - jnp.arange(2 * WINDOW)[None, :]
    valid = (dist >= 0) & (dist < WINDOW)
    bias_prompt = jnp.where(valid[None], _bias_lookup(rel_bias, dist), NEG)
    bias_sample = _bias_lookup(rel_bias, WINDOW - 1 - jnp.arange(WINDOW))

    tm_p = min(ROW_TILE, n_p)
    n_tiles = n_p // tm_p + 1
    max_rows = (n_p + n_s) * TOP_K + n_tiles * N_EXPERTS * (SUBLANES - 1)
    n_blocks = -(-(max_rows + N_EXPERTS * (MOE_TILE - 1)) // MOE_TILE)

    xp = x_prompt.reshape(n_p, D_MODEL)
    xs = x_sample.reshape(n_s, D_MODEL)
    pp = p_prompt.reshape(depth, n_p, D_PLE)
    ps = p_sample.reshape(depth, n_s, D_PLE)
    to_feature_major = lambda c: jnp.transpose(c, (0, 1, 3, 4, 2)).reshape(depth, n_s, N_KV * HEAD_DIM, wb)
    to_position_major = lambda c: jnp.transpose(c.reshape(depth, n_s, N_KV, HEAD_DIM, wb), (0, 1, 4, 2, 3))
    ck = to_feature_major(cache_k)
    cv = to_feature_major(cache_v)
    new_caches = None
    row2 = lambda v: v.reshape(1, -1)
    kv0 = 2 * D_RNN + D_ATTN
    outs = {k: [] for k in ("kp", "vp", "cp", "hp", "cs", "hs")}

    vecs = lambda v: v.reshape(depth, 1, -1)
    rw_all = {"conv_w": conv_w, "conv_b": vecs(conv_b), "wa": _block_diag(rg_wa).astype(BF16),
              "ba": vecs(rg_ba), "wx": _block_diag(rg_wx).astype(BF16), "bx": vecs(rg_bx),
              "lam": vecs(rg_lambda)}
    lw_all = {"gn_rnn": vecs(gn_rnn), "gn_attn": vecs(gn_attn), "w_out": w_out.astype(BF16),
              "ln1_g": vecs(ln1_g), "ln1_b": vecs(ln1_b),
              "router_wt": _hi_lo_rows(jnp.swapaxes(router_w, 1, 2)),
              "router_b": router_b.reshape(depth, N_EXPERTS, 1), "alpha": alpha,
              "ple_w_gate": ple_w_gate.astype(BF16), "ple_b_gate": vecs(ple_b_gate),
              "ple_w_proj": ple_w_proj.astype(BF16), "ln2_g": vecs(ln2_g), "ln2_b": vecs(ln2_b)}
    w_in_b = w_in.astype(BF16)

    for l in range(depth):
        rw = dict(rw_all, layer=l)
        lw = dict(lw_all, layer=l)
        first = l == 0

        xp, zp = _in_proj(xp, row2(ln_in_g), row2(ln_in_b), w_in_b, l, first)
        zp3 = zp.reshape(bsz, t_len, D_IN)
        yp, hp = _rnn_prompt(zp3, rw)
        op = _attn_prompt(zp, bsz, t_len, bias_prompt, attn_sinks, l)
        wp = min(WINDOW, t_len)
        outs["kp"].append(zp3[:, t_len - wp:, kv0:kv0 + 128].reshape(bsz, wp, N_KV, HEAD_DIM))
        outs["vp"].append(zp3[:, t_len - wp:, kv0 + 128:kv0 + 256].reshape(bsz, wp, N_KV, HEAD_DIM))
        outs["cp"].append(zp3[:, t_len - (CONV_W - 1):, :D_RNN])
        outs["hp"].append(hp.reshape(bsz, D_RNN))

        xs, zs = _in_proj(xs, row2(ln_in_g), row2(ln_in_b), w_in_b, l, first)
        ys, hs = _rnn_sample(zs, state_conv, state_h, rw)
        os_, new_caches = _attn_sample(zs[:, 2 * D_RNN:kv0], zs[:, kv0:kv0 + 128], zs[:, kv0 + 128:], ck, cv,
                                       new_caches, l, bias_sample, attn_sinks)
        outs["cs"].append(jnp.concatenate([state_conv[l][:, 1:], zs[:, None, :D_RNN]], axis=1))
        outs["hs"].append(hs)

        x1p, rrow_p, rcol_p, cnt_p = _out_proj(xp, yp.reshape(n_p, D_RNN), op, lw)
        x1s, rrow_s, rcol_s, cnt_s = _out_proj(xs, ys, os_, lw)

        tot_p = jnp.sum(cnt_p, axis=(0, 1))
        tot = tot_p + jnp.sum(cnt_s, axis=(0, 1))
        geom, base_p = _block_geometry(tot, n_blocks)
        base_s = base_p + tot_p

        xbuf = _dispatch(x1p, rrow_p, cnt_p, base_p, None, n_blocks * MOE_TILE)
        xbuf = _dispatch(x1s, rrow_s, cnt_s, base_s, xbuf, n_blocks * MOE_TILE)
        ybuf = _moe_experts(xbuf, geom, moe_w1, moe_b1, moe_w2, moe_b2, l)
        xp = _final(x1p, pp, l, rcol_p, cnt_p, base_p, ybuf, lw)
        xs = _final(x1s, ps, l, rcol_s, cnt_s, base_s, ybuf, lw)

    st = lambda k: jnp.stack(outs[k])
    return (xp.reshape(bsz, t_len, D_MODEL), xs.reshape(n_s, 1, D_MODEL),
            st("kp"), st("vp"), st("cp"), st("hp"), to_position_major(new_caches[0]),
            to_position_major(new_caches[1]), st("cs"), st("hs"))
```

```python
import functools
import math

import jax
import jax.numpy as jnp
from jax import lax
from jax.experimental import pallas as pl
from jax.experimental.pallas import tpu as pltpu

F32 = jnp.float32
BF16 = jnp.bfloat16
I32 = jnp.int32
U32 = jnp.uint32

D_MODEL = 1024
D_RNN = 512
RNN_BLOCKS = 8
CONV_W = 4
LRU_C = 8.0
N_HEADS = 8
HEAD_DIM = 64
N_KV = 2
GQA_R = N_HEADS // N_KV
D_ATTN = N_HEADS * HEAD_DIM
WINDOW = 128
N_BUCKETS = 32
MAX_DIST = 128
N_EXPERTS = 32
TOP_K = 4
D_FF = 1024
SWIGLU_LIMIT = 7.0
SWIGLU_ALPHA = 1.702
D_PLE = 256
LN_EPS = 1e-5
D_IN = 2 * D_RNN + D_ATTN + 2 * N_KV * HEAD_DIM
NEG = -1e30

SUBLANES = 8
LANES = 128
ROW_TILE = 512
MOE_TILE = 1024
D_PACK = D_MODEL // 2
VMEM_LIMIT = 56 * 1024 * 1024
HI_MASK = 0xFFFF0000


def _cparams(*sem):
    return pltpu.CompilerParams(dimension_semantics=sem, vmem_limit_bytes=VMEM_LIMIT)


def _layer_norm(x, g, b):
    mu = jnp.mean(x, axis=-1, keepdims=True)
    xc = x - mu
    var = jnp.mean(xc * xc, axis=-1, keepdims=True)
    return xc * lax.rsqrt(var + LN_EPS) * g + b


def _rms_norm(x, g):
    return x * lax.rsqrt(jnp.mean(x * x, axis=-1, keepdims=True) + LN_EPS) * g


def _const_spec(shape):
    return pl.BlockSpec(shape, lambda *_: (0,) * len(shape))


def _layer_spec(layer, *tail):
    return pl.BlockSpec((None,) + tail, lambda *_: (layer,) + (0,) * len(tail))


def _in_proj_kernel(x_ref, g_ref, b_ref, w_ref, *out_refs, apply_ln):
    x = x_ref[...]
    if apply_ln:
        x = _layer_norm(x, g_ref[...], b_ref[...])
        out_refs[0][...] = x
    out_refs[-1][...] = jnp.dot(x.astype(BF16), w_ref[...], preferred_element_type=F32)


def _in_proj(x, ln_g, ln_b, w_in_bf16, layer, apply_ln):
    n = x.shape[0]
    tm = min(2 * ROW_TILE, n)
    row = lambda w: pl.BlockSpec((tm, w), lambda i: (i, 0))
    out_shape = [jax.ShapeDtypeStruct((n, D_IN), F32)]
    out_specs = [row(D_IN)]
    if apply_ln:
        out_shape.insert(0, jax.ShapeDtypeStruct((n, D_MODEL), F32))
        out_specs.insert(0, row(D_MODEL))
    outs = pl.pallas_call(
        functools.partial(_in_proj_kernel, apply_ln=apply_ln),
        grid=(n // tm,),
        in_specs=[row(D_MODEL), _const_spec((1, D_MODEL)), _const_spec((1, D_MODEL)),
                  _layer_spec(layer, D_MODEL, D_IN)],
        out_specs=out_specs, out_shape=out_shape,
        compiler_params=_cparams("parallel"), name="in_proj",
    )(x, ln_g, ln_b, w_in_bf16)
    return (outs[0], outs[1]) if apply_ln else (x, outs[0])


def _rglru_coeffs(u, wa_ref, ba_ref, wx_ref, bx_ref, lam_ref):
    ub = u.astype(BF16)
    r = jax.nn.sigmoid(jnp.dot(ub, wa_ref[...], preferred_element_type=F32) + ba_ref[...])
    i = jax.nn.sigmoid(jnp.dot(ub, wx_ref[...], preferred_element_type=F32) + bx_ref[...])
    lam = -lam_ref[...]
    softplus = jnp.maximum(lam, 0.0) + jnp.log1p(jnp.exp(-jnp.abs(lam)))
    log_a = -LRU_C * r * softplus
    a = jnp.exp(log_a)
    b = jnp.sqrt(-jnp.tanh(log_a) * (a * a + 1.0)) * (i * u)
    return a, b


def _rnn_prompt_kernel(xr_ref, gate_ref, cw_ref, cb_ref, wa_ref, ba_ref, wx_ref, bx_ref, lam_ref,
                       y_ref, hlast_ref, xs_ref, a_ref, b_ref, carry_ref, *, tt):
    t = pl.program_id(1)

    @pl.when(t == 0)
    def _():
        xs_ref[0:SUBLANES, :] = jnp.zeros((SUBLANES, D_RNN), F32)
        carry_ref[...] = jnp.zeros((SUBLANES, D_RNN), F32)

    x = xr_ref[...]
    xs_ref[SUBLANES:SUBLANES + tt, :] = x
    cw = cw_ref[...]
    u = cb_ref[...] + x * cw[3:4]
    for j in range(CONV_W - 1):
        off = SUBLANES - (CONV_W - 1) + j
        u = u + xs_ref[off:off + tt, :] * cw[j:j + 1]
    xs_ref[0:SUBLANES, :] = x[tt - SUBLANES:tt, :]

    a, b = _rglru_coeffs(u, wa_ref, ba_ref, wx_ref, bx_ref, lam_ref)
    ng = tt // SUBLANES
    a = a.reshape(ng, SUBLANES, D_RNN)
    b = b.reshape(ng, SUBLANES, D_RNN)
    row = lax.broadcasted_iota(I32, (1, SUBLANES, 1), 1)
    for s in (1, 2, 4):
        a_sh = pltpu.roll(a, s, 1)
        b_sh = pltpu.roll(b, s, 1)
        m = row >= s
        b = jnp.where(m, a * b_sh + b, b)
        a = jnp.where(m, a * a_sh, a)
    a_ref[...] = a
    b_ref[...] = b

    def body(g, carry):
        h = a_ref[g] * carry + b_ref[g]
        b_ref[g] = h
        return jnp.broadcast_to(h[SUBLANES - 1:SUBLANES, :], (SUBLANES, D_RNN))

    carry = lax.fori_loop(0, ng, body, carry_ref[...], unroll=True)
    carry_ref[...] = carry
    h = b_ref[...].reshape(tt, D_RNN)
    y_ref[...] = h * jax.nn.gelu(gate_ref[...])
    hlast_ref[...] = carry[0:1, :]


def _rnn_prompt(z3, rw):
    bsz, t_len, _ = z3.shape
    tt = min(2 * ROW_TILE, t_len)
    vec = _layer_spec(rw["layer"], 1, D_RNN)
    mat = _layer_spec(rw["layer"], D_RNN, D_RNN)
    return pl.pallas_call(
        functools.partial(_rnn_prompt_kernel, tt=tt),
        grid=(bsz, t_len // tt),
        in_specs=[pl.BlockSpec((None, tt, D_RNN), lambda b, t: (b, t, 0)),
                  pl.BlockSpec((None, tt, D_RNN), lambda b, t: (b, t, 1)),
                  _layer_spec(rw["layer"], CONV_W, D_RNN), vec, mat, vec, mat, vec, vec],
        out_specs=[pl.BlockSpec((None, tt, D_RNN), lambda b, t: (b, t, 0)),
                   pl.BlockSpec((None, 1, D_RNN), lambda b, t: (b, 0, 0))],
        out_shape=[jax.ShapeDtypeStruct((bsz, t_len, D_RNN), F32),
                   jax.ShapeDtypeStruct((bsz, 1, D_RNN), F32)],
        scratch_shapes=[pltpu.VMEM((tt + SUBLANES, D_RNN), F32),
                        pltpu.VMEM((tt // SUBLANES, SUBLANES, D_RNN), F32),
                        pltpu.VMEM((tt // SUBLANES, SUBLANES, D_RNN), F32),
                        pltpu.VMEM((SUBLANES, D_RNN), F32)],
        compiler_params=_cparams("parallel", "arbitrary"), name="rnn_prompt",
    )(z3, z3, rw["conv_w"], rw["conv_b"], rw["wa"], rw["ba"], rw["wx"], rw["bx"], rw["lam"])


def _rnn_sample_kernel(xr_ref, gate_ref, c0_ref, c1_ref, c2_ref, h0_ref, cw_ref, cb_ref, wa_ref, ba_ref,
                       wx_ref, bx_ref, lam_ref, y_ref, h_ref):
    x = xr_ref[...]
    cw = cw_ref[...]
    u = (cb_ref[...] + c0_ref[...] * cw[0:1] + c1_ref[...] * cw[1:2] + c2_ref[...] * cw[2:3]
         + x * cw[3:4])
    a, b = _rglru_coeffs(u, wa_ref, ba_ref, wx_ref, bx_ref, lam_ref)
    h = a * h0_ref[...] + b
    h_ref[...] = h
    y_ref[...] = h * jax.nn.gelu(gate_ref[...])


def _rnn_sample(z, conv_state, h0, rw):
    n = z.shape[0]
    layer = rw["layer"]
    vec = _layer_spec(layer, 1, D_RNN)
    mat = _layer_spec(layer, D_RNN, D_RNN)
    full = _const_spec((n, D_RNN))
    depth = conv_state.shape[0]
    conv_flat = conv_state.reshape(depth, n, (CONV_W - 1) * D_RNN)
    tap = lambda j: pl.BlockSpec((None, n, D_RNN), lambda i: (layer, 0, j))
    return pl.pallas_call(
        _rnn_sample_kernel,
        grid=(1,),
        in_specs=[pl.BlockSpec((n, D_RNN), lambda i: (0, 0)), pl.BlockSpec((n, D_RNN), lambda i: (0, 1)),
                  tap(0), tap(1), tap(2), _layer_spec(layer, n, D_RNN),
                  _layer_spec(layer, CONV_W, D_RNN), vec, mat, vec, mat, vec, vec],
        out_specs=[full, full],
        out_shape=[jax.ShapeDtypeStruct((n, D_RNN), F32)] * 2,
        compiler_params=_cparams("arbitrary"), name="rnn_sample",
    )(z, z, conv_flat, conv_flat, conv_flat, h0, rw["conv_w"], rw["conv_b"], rw["wa"], rw["ba"], rw["wx"], rw["bx"], rw["lam"])


def _rel_bucket(dist):
    n = jnp.maximum(dist, 0)
    max_exact = N_BUCKETS // 2
    nf = jnp.maximum(n, max_exact).astype(F32)
    large = max_exact + (jnp.log(nf / max_exact) / math.log(MAX_DIST / max_exact)
                         * (N_BUCKETS - max_exact)).astype(I32)
    large = jnp.minimum(large, N_BUCKETS - 1)
    return jnp.where(n < max_exact, n, large)


def _bias_lookup(rel_bias, dist):
    hot = (_rel_bucket(dist)[..., None] == jnp.arange(N_BUCKETS)).astype(F32)
    out = jnp.tensordot(hot, rel_bias.astype(F32), axes=1, precision=lax.Precision.HIGHEST)
    return jnp.moveaxis(out, -1, 0)


def _half_lane_variants(x, group):
    lo = lax.broadcasted_iota(I32, (1, 2 * HEAD_DIM), 1) < HEAD_DIM
    xr = pltpu.roll(x, HEAD_DIM, 1)
    zero = jnp.zeros_like(x)
    if group == 0:
        return jnp.where(lo, x, zero), jnp.where(lo, zero, xr)
    return jnp.where(lo, xr, zero), jnp.where(lo, zero, x)


def _attn_prompt_kernel(q_ref, kc_ref, kp_ref, vc_ref, vp_ref, bias_ref, sink_ref, o_ref, *, nq, layer):
    first = pl.program_id(1) == 0
    col = lax.broadcasted_iota(I32, (1, 2 * WINDOW), 1)
    hide_prev = jnp.logical_and(first, col < WINDOW)
    for j in range(nq):
        rows = slice(j * WINDOW, (j + 1) * WINDOW)
        before = slice((j - 1) * WINDOW, j * WINDOW)
        kk = jnp.concatenate([kp_ref[...] if j == 0 else kc_ref[before, :], kc_ref[rows, :]], axis=0)
        vv = jnp.concatenate([vp_ref[...] if j == 0 else vc_ref[before, :], vc_ref[rows, :]], axis=0)
        for g in range(N_KV):
            k_lo, k_hi = _half_lane_variants(kk, g)
            v_lo, v_hi = _half_lane_variants(vv, g)
            for pair in range(GQA_R // 2):
                p0 = g * (GQA_R // 2) + pair
                qp = (q_ref[rows, p0 * 128:(p0 + 1) * 128] * (HEAD_DIM ** -0.5)).astype(BF16)
                acc = jnp.zeros((WINDOW, 2 * HEAD_DIM), F32)
                for kx, vx, h in ((k_lo, v_lo, 2 * p0), (k_hi, v_hi, 2 * p0 + 1)):
                    s = lax.dot_general(qp, kx.astype(BF16), (((1,), (1,)), ((), ())),
                                        preferred_element_type=F32) + bias_ref[h]
                    if j == 0:
                        s = jnp.where(hide_prev, NEG, s)
                    sink = sink_ref[layer, h]
                    m = jnp.maximum(jnp.max(s, axis=-1, keepdims=True), sink)
                    p = jnp.exp(s - m)
                    denom = jnp.sum(p, axis=-1, keepdims=True) + jnp.exp(sink - m)
                    acc = acc + jnp.dot(p.astype(BF16), vx.astype(BF16), preferred_element_type=F32) / denom
                o_ref[rows, p0 * 128:(p0 + 1) * 128] = acc


def _attn_prompt(z, bsz, t_len, bias_prompt, sinks, layer):
    nb = t_len // WINDOW
    nq = min(8, nb)
    ns = nb // nq
    kcol = (2 * D_RNN + D_ATTN) // 128
    cur = lambda c: pl.BlockSpec((nq * WINDOW, 128), lambda b, i: (b * ns + i, c))
    prev = lambda c: pl.BlockSpec((WINDOW, 128), lambda b, i: (b * nb + jnp.maximum(i * nq - 1, 0), c))
    return pl.pallas_call(
        functools.partial(_attn_prompt_kernel, nq=nq, layer=layer),
        grid=(bsz, ns),
        in_specs=[pl.BlockSpec((nq * WINDOW, D_ATTN), lambda b, i: (b * ns + i, 2 * D_RNN // D_ATTN)),
                  cur(kcol), prev(kcol), cur(kcol + 1), prev(kcol + 1),
                  _const_spec((N_HEADS, WINDOW, 2 * WINDOW)),
                  pl.BlockSpec(memory_space=pltpu.SMEM)],
        out_specs=pl.BlockSpec((nq * WINDOW, D_ATTN), lambda b, i: (b * ns + i, 0)),
        out_shape=jax.ShapeDtypeStruct((bsz * t_len, D_ATTN), F32),
        compiler_params=_cparams("parallel", "arbitrary"), name="attn_prompt",
    )(z, z, z, z, z, bias_prompt, sinks)


def _attn_sample_kernel(q_ref, kn_ref, vn_ref, ck_ref, cv_ref, bias_ref, sink_ref, hmask_ref, *rest):
    o_ref, nk_ref, nv_ref = rest[-3:]
    pos = lax.broadcasted_iota(I32, (1, 1, WINDOW), 2)
    last = pos == WINDOW - 1
    bb = ck_ref.shape[0]
    newk = jnp.where(last, jnp.broadcast_to(kn_ref[...], (bb, 128, WINDOW)), pltpu.roll(ck_ref[...], WINDOW - 1, 2))
    newv = jnp.where(last, jnp.broadcast_to(vn_ref[...], (bb, 128, WINDOW)), pltpu.roll(cv_ref[...], WINDOW - 1, 2))
    nk_ref[...] = newk
    nv_ref[...] = newv
    s = lax.dot_general(q_ref[...].astype(BF16), newk.astype(BF16), (((2,), (1,)), ((0,), (0,))),
                        preferred_element_type=F32)
    s = s + bias_ref[...][None]
    sink = sink_ref[...][None]
    m = jnp.maximum(jnp.max(s, axis=-1, keepdims=True), sink)
    p = jnp.exp(s - m)
    denom = jnp.sum(p, axis=-1, keepdims=True) + jnp.exp(sink - m)
    o = lax.dot_general(p.astype(BF16), newv.astype(BF16), (((2,), (2,)), ((0,), (0,))),
                        preferred_element_type=F32)
    o_ref[...] = o / denom * hmask_ref[...][None]


def _attn_sample(q, k_new, v_new, k_cache, v_cache, new_caches, layer, bias_sample, sinks):
    n = q.shape[0]
    depth = k_cache.shape[0]
    bb = 32
    qh = q.reshape(n, N_KV, GQA_R, HEAD_DIM) * (HEAD_DIM ** -0.5)
    eye = jnp.eye(N_KV, dtype=F32)
    q8 = (qh[:, :, :, None, :] * eye[None, :, None, :, None]).reshape(n, N_HEADS, N_KV * HEAD_DIM)
    hmask = jnp.repeat(jnp.repeat(eye, GQA_R, axis=0), HEAD_DIM, axis=1)
    blk3 = lambda a, b: pl.BlockSpec((bb, a, b), lambda i: (i, 0, 0))
    cache = pl.BlockSpec((None, bb, 128, WINDOW), lambda i: (layer, i, 0, 0))
    in_specs = [blk3(N_HEADS, 128), blk3(128, 1), blk3(128, 1), cache, cache,
                _const_spec((N_HEADS, WINDOW)), _layer_spec(layer, N_HEADS, 1), _const_spec((N_HEADS, 128))]
    args = [q8, k_new.reshape(n, 128, 1), v_new.reshape(n, 128, 1), k_cache, v_cache, bias_sample,
            sinks.reshape(-1, N_HEADS, 1), hmask]
    aliases = {}
    if new_caches is not None:
        in_specs += [pl.BlockSpec(memory_space=pl.ANY)] * 2
        args += list(new_caches)
        aliases = {len(args) - 2: 1, len(args) - 1: 2}
    o8, nk, nv = pl.pallas_call(
        _attn_sample_kernel,
        grid=(n // bb,),
        in_specs=in_specs,
        out_specs=[blk3(N_HEADS, 128), cache, cache],
        out_shape=[jax.ShapeDtypeStruct((n, N_HEADS, 128), F32),
                   jax.ShapeDtypeStruct((depth, n, 128, WINDOW), F32),
                   jax.ShapeDtypeStruct((depth, n, 128, WINDOW), F32)],
        input_output_aliases=aliases,
        compiler_params=_cparams("arbitrary"), name="attn_sample",
    )(*args)
    o = o8.reshape(n, N_KV, GQA_R, N_KV, HEAD_DIM)
    o = jnp.stack([o[:, g, :, g, :] for g in range(N_KV)], axis=1).reshape(n, D_ATTN)
    return o, (nk, nv)


ONE_HOT_BLOCK = 256


def _sorted_rows(tm):
    return TOP_K * tm + N_EXPERTS * SUBLANES


def _out_proj_kernel(x_ref, yr_ref, o_ref, gnr_ref, gna_ref, w_ref, g1_ref, b1_ref, rwt_ref, rb_ref,
                     x1_ref, rrow_ref, rcol_ref, cnt_ref, *, tm, alpha):
    yn = _rms_norm(yr_ref[...], gnr_ref[...]).astype(BF16)
    on = _rms_norm(o_ref[...], gna_ref[...]).astype(BF16)
    mix = jnp.dot(jnp.concatenate([yn, on], axis=1), w_ref[...], preferred_element_type=F32)
    x1 = _layer_norm(alpha * x_ref[...] + mix, g1_ref[...], b1_ref[...])
    x1_ref[...] = x1

    x1h = x1.astype(BF16)
    x1l = (x1 - x1h.astype(F32)).astype(BF16)
    nt_dims = (((1,), (1,)), ((), ()))
    by_hi = lax.dot_general(rwt_ref[...], x1h, nt_dims, preferred_element_type=F32)
    by_lo = lax.dot_general(rwt_ref[0:N_EXPERTS, :], x1l, nt_dims, preferred_element_type=F32)
    logits = by_hi[0:N_EXPERTS] + by_hi[N_EXPERTS:] + by_lo + rb_ref[...]
    eidx = lax.broadcasted_iota(I32, (N_EXPERTS, tm), 0).astype(F32)
    work = logits
    vals, hots = [], []
    for _ in range(TOP_K):
        v = jnp.max(work, axis=0, keepdims=True)
        idx = jnp.min(jnp.where(work == v, eidx, float(N_EXPERTS)), axis=0, keepdims=True)
        hot = eidx == idx
        work = jnp.where(hot, -jnp.inf, work)
        vals.append(v)
        hots.append(hot.astype(F32))
    ex = [jnp.exp(v - vals[0]) for v in vals]
    tot = ex[0] + ex[1] + ex[2] + ex[3]
    tok_hot = hots[0] + hots[1] + hots[2] + hots[3]
    s_i = lax.broadcasted_iota(I32, (tm, tm), 0)
    t_i = lax.broadcasted_iota(I32, (tm, tm), 1)
    earlier = (s_i < t_i).astype(BF16)
    before = jnp.dot(tok_hot.astype(BF16), earlier, preferred_element_type=F32)
    cnt = jnp.sum(tok_hot, axis=1, keepdims=True)
    cnt8 = jnp.floor((cnt + (SUBLANES - 1)) * (1.0 / SUBLANES)) * SUBLANES
    cnt8_b = jnp.broadcast_to(cnt8, (N_EXPERTS, LANES))
    e_r = lax.broadcasted_iota(I32, (N_EXPERTS, N_EXPERTS), 0)
    e_c = lax.broadcasted_iota(I32, (N_EXPERTS, N_EXPERTS), 1)
    start = jnp.dot((e_c < e_r).astype(F32), cnt8_b, preferred_element_type=F32,
                    precision=lax.Precision.HIGHEST)[:, 0:1]
    slot = start + before
    rows = [jnp.sum(h * slot, axis=0, keepdims=True) for h in hots] + [e / tot for e in ex]
    rrow = jnp.concatenate(rows, axis=0)
    rrow_ref[...] = rrow
    rcol_ref[...] = rrow.T
    cnt_ref[...] = cnt8_b.T[0:1, 0:N_EXPERTS].astype(I32)


def _out_proj(x, y_rnn, o, lw):
    n = x.shape[0]
    tm = min(ROW_TILE, n)
    nt = n // tm
    row = lambda w: pl.BlockSpec((tm, w), lambda i: (i, 0))
    layer = lw["layer"]
    vec = lambda w: _layer_spec(layer, 1, w)
    return pl.pallas_call(
        functools.partial(_out_proj_kernel, tm=tm, alpha=lw["alpha"]),
        grid=(nt,),
        in_specs=[row(D_MODEL), row(D_RNN), row(D_ATTN), vec(D_RNN), vec(D_ATTN),
                  _layer_spec(layer, D_MODEL, D_MODEL), vec(D_MODEL), vec(D_MODEL),
                  _layer_spec(layer, 2 * N_EXPERTS, D_MODEL), _layer_spec(layer, N_EXPERTS, 1)],
        out_specs=[row(D_MODEL), pl.BlockSpec((None, 2 * TOP_K, tm), lambda i: (i, 0, 0)), row(2 * TOP_K),
                   pl.BlockSpec((None, 1, N_EXPERTS), lambda i: (i, 0, 0))],
        out_shape=[jax.ShapeDtypeStruct((n, D_MODEL), F32), jax.ShapeDtypeStruct((nt, 2 * TOP_K, tm), F32),
                   jax.ShapeDtypeStruct((n, 2 * TOP_K), F32), jax.ShapeDtypeStruct((nt, 1, N_EXPERTS), I32)],
        compiler_params=_cparams("parallel"), name="out_proj_route",
    )(x, y_rnn, o, lw["gn_rnn"], lw["gn_attn"], lw["w_out"], lw["ln1_g"], lw["ln1_b"],
      lw["router_wt"], lw["router_b"])


RUN_CHUNK = 64
_SMALL_PIECES = (32, 16, 8)
_PAD_PIECES = (128, 64, 32, 16, 8)


def _for_each_piece(n8, fn):
    def chunk(j, c):
        fn(pl.multiple_of(j * RUN_CHUNK, RUN_CHUNK), RUN_CHUNK)
        return c
    lax.fori_loop(0, n8 // RUN_CHUNK, chunk, 0)
    for s in _SMALL_PIECES:
        @pl.when((n8 & s) != 0)
        def _(s=s):
            fn(n8 & (-2 * s), s)


def _wait_tile_rows(n_rows, tm, wait_rows):
    wait_rows(TOP_K * tm)
    pad = n_rows - TOP_K * tm
    for s in _PAD_PIECES:
        @pl.when((pad & s) != 0)
        def _(s=s):
            wait_rows(s)


def _dispatch_kernel(cnt_ref, base_ref, x_ref, rrow_ref, *rest, tm):
    xbuf_ref, xs_ref, carry_ref, nrows_ref, sems = rest[-5:]
    nrow = _sorted_rows(tm)
    i = pl.program_id(0)
    slot = i % 2

    @pl.when(i == 0)
    def _():
        def init(e, c):
            carry_ref[e] = base_ref[e]
            return c
        lax.fori_loop(0, N_EXPERTS, init, 0)

    r = lax.broadcasted_iota(I32, (ONE_HOT_BLOCK, 1), 0).astype(F32).astype(BF16)
    blocks = []
    for c in range(nrow // ONE_HOT_BLOCK):
        hit = None
        for k in range(TOP_K):
            h = r == (rrow_ref[k:k + 1, :] - float(c * ONE_HOT_BLOCK)).astype(BF16)
            hit = h if hit is None else jnp.logical_or(hit, h)
        blocks.append(jnp.where(hit, jnp.ones((), BF16), jnp.zeros((), BF16)))
    perm = jnp.concatenate(blocks, axis=0)
    xs = jnp.dot(perm, x_ref[...].astype(BF16), preferred_element_type=F32)
    lo = lax.bitcast_convert_type(xs[:, :D_PACK], U32)
    hi = lax.bitcast_convert_type(xs[:, D_PACK:], U32)
    xs_ref[slot] = (hi & U32(HI_MASK)) | (lo >> 16)

    def run_copy(buf, src, dst, s):
        return pltpu.make_async_copy(xs_ref.at[buf, pl.ds(pl.multiple_of(src, SUBLANES), s)],
                                     xbuf_ref.at[pl.ds(pl.multiple_of(dst, SUBLANES), s)], sems.at[buf])

    def start(e, src):
        n8 = cnt_ref[0, e]
        dst = carry_ref[e]
        _for_each_piece(n8, lambda off, s: run_copy(slot, src + off, dst + off, s).start())
        carry_ref[e] = dst + n8
        return src + n8

    def wait_tile(buf):
        _wait_tile_rows(nrows_ref[buf], tm, lambda s: run_copy(buf, 0, 0, s).wait())

    nrows_ref[slot] = lax.fori_loop(0, N_EXPERTS, start, 0)

    @pl.when(i > 0)
    def _():
        wait_tile(1 - slot)

    @pl.when(i == pl.num_programs(0) - 1)
    def _():
        wait_tile(slot)


def _dispatch(x1, rrow, cnt8, base, xbuf, n_rows):
    n = x1.shape[0]
    nt, _, tm = rrow.shape
    in_specs = [pl.BlockSpec((None, 1, N_EXPERTS), lambda i: (i, 0, 0), memory_space=pltpu.SMEM),
                pl.BlockSpec(memory_space=pltpu.SMEM),
                pl.BlockSpec((tm, D_MODEL), lambda i: (i, 0)),
                pl.BlockSpec((None, 2 * TOP_K, tm), lambda i: (i, 0, 0))]
    args = [cnt8, base, x1, rrow]
    aliases = {}
    if xbuf is not None:
        in_specs.append(pl.BlockSpec(memory_space=pl.ANY))
        args.append(xbuf)
        aliases = {len(args) - 1: 0}
    return pl.pallas_call(
        functools.partial(_dispatch_kernel, tm=tm),
        grid=(nt,),
        in_specs=in_specs,
        out_specs=pl.BlockSpec(memory_space=pl.ANY),
        out_shape=jax.ShapeDtypeStruct((n_rows, D_PACK), U32),
        scratch_shapes=[pltpu.VMEM((2, _sorted_rows(tm), D_PACK), U32), pltpu.SMEM((N_EXPERTS,), I32),
                        pltpu.SMEM((2,), I32), pltpu.SemaphoreType.DMA((2,))],
        input_output_aliases=aliases,
        compiler_params=_cparams("arbitrary"), name="moe_dispatch",
    )(*args)


def _moe_kernel(br_ref, be_ref, nv_ref, first_ref, next_ref, x_ref, w1_hbm, b1_ref, w2_hbm, b2_ref, y_ref,
                w1f_ref, w2f_ref, w1s_ref, w2s_ref, sems, *, layer):
    i = pl.program_id(0)

    def load(e):
        return (pltpu.make_async_copy(w1_hbm.at[layer, e], w1f_ref, sems.at[0]),
                pltpu.make_async_copy(w2_hbm.at[layer, e], w2f_ref, sems.at[1]))

    @pl.when(jnp.logical_and(i == 0, first_ref[0] == 1))
    def _():
        for c in load(be_ref[0]):
            c.start()

    nv = nv_ref[i]
    quarter = MOE_TILE // 4
    is_first = first_ref[i] == 1
    fresh_full = jnp.logical_and(is_first, nv > 3 * quarter)

    def start_next_load():
        @pl.when(next_ref[i] >= 0)
        def _():
            for c in load(next_ref[i]):
                c.start()

    def expert_rows(m, fresh):
        if fresh:
            w1 = w1f_ref[...].astype(BF16)
            w2 = w2f_ref[...].astype(BF16)
            w1s_ref[...] = w1
            w2s_ref[...] = w2
        else:
            w1 = w1s_ref[...]
            w2 = w2s_ref[...]
        rows = lax.broadcasted_iota(I32, (m, 1), 0)
        xw = jnp.where(rows < nv, x_ref[0:m, :], U32(0))
        lo = lax.bitcast_convert_type(xw << 16, F32).astype(BF16)
        hi = lax.bitcast_convert_type(xw & U32(HI_MASK), F32).astype(BF16)
        xb = jnp.concatenate([lo, hi], axis=1)
        hdn = jnp.dot(xb, w1, preferred_element_type=F32) + b1_ref[...]
        g = jnp.minimum(hdn[:, :D_FF], SWIGLU_LIMIT)
        lin = jnp.clip(hdn[:, D_FF:], -SWIGLU_LIMIT, SWIGLU_LIMIT)
        act = g * jax.nn.sigmoid(SWIGLU_ALPHA * g) * (lin + 1.0)
        y_ref[0:m, :] = jnp.dot(act.astype(BF16), w2, preferred_element_type=F32) + b2_ref[...]

    @pl.when(is_first)
    def _():
        for c in load(be_ref[i]):
            c.wait()

    @pl.when(fresh_full)
    def _():
        expert_rows(MOE_TILE, True)
        start_next_load()

    @pl.when(jnp.logical_and(is_first, jnp.logical_not(fresh_full)))
    def _():
        w1s_ref[...] = w1f_ref[...].astype(BF16)
        w2s_ref[...] = w2f_ref[...].astype(BF16)
        start_next_load()

    for q in range(1, 5):
        @pl.when(jnp.logical_and(jnp.logical_not(fresh_full),
                                 jnp.logical_and(nv > (q - 1) * quarter, nv <= q * quarter)))
        def _(q=q):
            expert_rows(q * quarter, False)


def _moe_experts(xbuf, geom, w1, b1, w2, b2, layer):
    n_blocks = xbuf.shape[0] // MOE_TILE
    bias = lambda w: pl.BlockSpec((None, None, 1, w), lambda i, br, be, *_: (layer, be[i], 0, 0))
    grid_spec = pltpu.PrefetchScalarGridSpec(
        num_scalar_prefetch=5,
        grid=(n_blocks,),
        in_specs=[pl.BlockSpec((MOE_TILE, D_PACK), lambda i, br, *_: (br[i], 0)),
                  pl.BlockSpec(memory_space=pl.ANY), bias(2 * D_FF),
                  pl.BlockSpec(memory_space=pl.ANY), bias(D_MODEL)],
        out_specs=pl.BlockSpec((MOE_TILE, D_MODEL), lambda i, br, *_: (br[i], 0)),
        scratch_shapes=[pltpu.VMEM((D_MODEL, 2 * D_FF), F32), pltpu.VMEM((D_FF, D_MODEL), F32),
                        pltpu.VMEM((D_MODEL, 2 * D_FF), BF16), pltpu.VMEM((D_FF, D_MODEL), BF16),
                        pltpu.SemaphoreType.DMA((2,))],
    )
    depth = w1.shape[0]
    return pl.pallas_call(
        functools.partial(_moe_kernel, layer=layer), grid_spec=grid_spec,
        out_shape=jax.ShapeDtypeStruct((xbuf.shape[0], D_MODEL), F32),
        compiler_params=_cparams("arbitrary"), name="moe_experts",
    )(*geom, xbuf, w1, b1.reshape(depth, N_EXPERTS, 1, 2 * D_FF), w2, b2.reshape(depth, N_EXPERTS, 1, D_MODEL))


def _final_kernel(cnt_ref, cnt_next_ref, base_ref, x_ref, p_ref, rcol_ref, wg_ref, bg_ref, wp_ref, g2_ref, b2_ref,
                  ybuf_ref, out_ref, ys_ref, carry_ref, nrows_ref, sems, *, tm, alpha):
    nrow = _sorted_rows(tm)
    i = pl.program_id(0)
    slot = i % 2

    def run_copy(buf, src, dst, s):
        return pltpu.make_async_copy(ybuf_ref.at[pl.ds(pl.multiple_of(src, SUBLANES), s)],
                                     ys_ref.at[buf, pl.ds(pl.multiple_of(dst, SUBLANES), s)], sems.at[buf])

    def gather_tile(counts_ref, buf):
        def start(e, dst):
            n8 = counts_ref[0, e]
            src = carry_ref[e]
            _for_each_piece(n8, lambda off, s: run_copy(buf, src + off, dst + off, s).start())
            carry_ref[e] = src + n8
            return dst + n8
        nrows_ref[buf] = lax.fori_loop(0, N_EXPERTS, start, 0)

    @pl.when(i == 0)
    def _():
        def init(e, c):
            carry_ref[e] = base_ref[e]
            return c
        lax.fori_loop(0, N_EXPERTS, init, 0)
        ys_ref[:, TOP_K * tm:, :] = jnp.zeros((2, nrow - TOP_K * tm, D_MODEL), F32)
        gather_tile(cnt_ref, 0)

    @pl.when(i + 1 < pl.num_programs(0))
    def _():
        gather_tile(cnt_next_ref, 1 - slot)

    x = x_ref[...]
    ple = (jax.nn.sigmoid(jnp.dot(x.astype(BF16), wg_ref[...], preferred_element_type=F32) + bg_ref[...])
           * jnp.dot(p_ref[...].astype(BF16), wp_ref[...], preferred_element_type=F32))
    col = lax.broadcasted_iota(I32, (1, ONE_HOT_BLOCK), 1).astype(F32).astype(BF16)
    rc = rcol_ref[...]
    gates = rc[:, TOP_K:].astype(BF16)
    blocks = []
    for c in range(nrow // ONE_HOT_BLOCK):
        blk = jnp.zeros((tm, ONE_HOT_BLOCK), BF16)
        for k in range(TOP_K):
            hit = col == (rc[:, k:k + 1] - float(c * ONE_HOT_BLOCK)).astype(BF16)
            blk = jnp.where(hit, gates[:, k:k + 1], blk)
        blocks.append(blk)
    gmat = jnp.concatenate(blocks, axis=1)
    base = alpha * x + ple

    _wait_tile_rows(nrows_ref[slot], tm, lambda s: run_copy(slot, 0, 0, s).wait())

    ffn = jnp.dot(gmat, ys_ref[slot].astype(BF16), preferred_element_type=F32)
    out_ref[...] = _layer_norm(base + ffn, g2_ref[...], b2_ref[...])


def _final(x1, p_all, layer, rcol, cnt8, base, ybuf, lw):
    n = x1.shape[0]
    nt = cnt8.shape[0]
    tm = n // nt
    row = lambda w: pl.BlockSpec((tm, w), lambda i: (i, 0))
    vec = lambda w: _layer_spec(layer, 1, w)
    return pl.pallas_call(
        functools.partial(_final_kernel, tm=tm, alpha=lw["alpha"]),
        grid=(nt,),
        in_specs=[pl.BlockSpec((None, 1, N_EXPERTS), lambda i: (i, 0, 0), memory_space=pltpu.SMEM),
                  pl.BlockSpec((None, 1, N_EXPERTS), lambda i: (jnp.minimum(i + 1, nt - 1), 0, 0),
                               memory_space=pltpu.SMEM),
                  pl.BlockSpec(memory_space=pltpu.SMEM),
                  row(D_MODEL), pl.BlockSpec((None, tm, D_PLE), lambda i: (layer, i, 0)), row(2 * TOP_K),
                  _layer_spec(layer, D_MODEL, D_MODEL), vec(D_MODEL), _layer_spec(layer, D_PLE, D_MODEL),
                  vec(D_MODEL), vec(D_MODEL), pl.BlockSpec(memory_space=pl.ANY)],
        out_specs=row(D_MODEL),
        out_shape=jax.ShapeDtypeStruct((n, D_MODEL), F32),
        scratch_shapes=[pltpu.VMEM((2, _sorted_rows(tm), D_MODEL), F32), pltpu.SMEM((N_EXPERTS,), I32),
                        pltpu.SMEM((2,), I32), pltpu.SemaphoreType.DMA((2,))],
        compiler_params=_cparams("arbitrary"), name="moe_combine_ln2",
    )(cnt8, cnt8, base, x1, p_all, rcol, lw["ple_w_gate"], lw["ple_b_gate"], lw["ple_w_proj"], lw["ln2_g"], lw["ln2_b"],
      ybuf)


def _block_diag(w):
    depth, nb, bw, _ = w.shape
    eye = jnp.eye(nb, dtype=w.dtype)
    return (w[:, :, :, None, :] * eye[None, :, None, :, None]).reshape(depth, nb * bw, nb * bw)


def _hi_lo_rows(w):
    hi = w.astype(BF16)
    lo = (w - hi.astype(F32)).astype(BF16)
    return jnp.concatenate([hi, lo], axis=-2)


def _block_geometry(tot, n_blocks):
    nblk = (tot + MOE_TILE - 1) // MOE_TILE
    blk_end = jnp.cumsum(nblk)
    blk_start = blk_end - nblk
    ids = jnp.arange(n_blocks, dtype=I32)
    n_valid = blk_end[-1]
    expert_at = lambda j: jnp.minimum(jnp.sum(j[:, None] >= blk_end[None, :], axis=1), N_EXPERTS - 1).astype(I32)
    src = jnp.clip(ids, 0, jnp.maximum(n_valid - 1, 0))
    e_of = expert_at(src)
    hot = e_of[:, None] == jnp.arange(N_EXPERTS)[None, :]
    pick = lambda v: jnp.sum(jnp.where(hot, v[None, :], 0), axis=1)
    start_of, end_of, tot_of = pick(blk_start), pick(blk_end), pick(tot)
    real = ids < n_valid
    nv = jnp.where(real, jnp.clip(tot_of - (ids - start_of) * MOE_TILE, 0, MOE_TILE), 0)
    first = jnp.logical_and(real, ids == start_of)
    nxt = jnp.where(end_of < n_valid, expert_at(end_of), -1)
    geom = tuple(v.astype(I32) for v in (src, e_of, nv, first, nxt))
    return geom, (blk_start * MOE_TILE).astype(I32)


def kernel(x_prompt, x_sample, cache_k, cache_v, state_conv, state_h, p_prompt, p_sample, ln_in_g, ln_in_b, rel_bias, w_in, conv_w, conv_b, rg_wa, rg_ba, rg_wx, rg_bx, rg_lambda, attn_sinks, gn_rnn, gn_attn, w_out, ln1_g, ln1_b, router_w, router_b, moe_w1, moe_b1, moe_w2, moe_b2, ple_w_gate, ple_b_gate, ple_w_proj, ln2_g, ln2_b):
    depth = w_in.shape[0]
    bsz, t_len, _ = x_prompt.shape
    n_p = bsz * t_len
    n_s = x_sample.shape[0]
    wb = cache_k.shape[2]
    alpha = (2 * depth) ** 0.25

    dist = jnp.arange(WINDOW)[:, None] + WINDOW - jnp.arange(2 * WINDOW)[None, :]
    valid = (dist >= 0) & (dist < WINDOW)
    bias_prompt = jnp.where(valid[None], _bias_lookup(rel_bias, dist), NEG)
    bias_sample = _bias_lookup(rel_bias, WINDOW - 1 - jnp.arange(WINDOW))

    tm_p = min(ROW_TILE, n_p)
    n_tiles = n_p // tm_p + 1
    max_rows = (n_p + n_s) * TOP_K + n_tiles * N_EXPERTS * (SUBLANES - 1)
    n_blocks = -(-(max_rows + N_EXPERTS * (MOE_TILE - 1)) // MOE_TILE)

    xp = x_prompt.reshape(n_p, D_MODEL)
    xs = x_sample.reshape(n_s, D_MODEL)
    pp = p_prompt.reshape(depth, n_p, D_PLE)
    ps = p_sample.reshape(depth, n_s, D_PLE)
    to_feature_major = lambda c: jnp.transpose(c, (0, 1, 3, 4, 2)).reshape(depth, n_s, N_KV * HEAD_DIM, wb)
    to_position_major = lambda c: jnp.transpose(c.reshape(depth, n_s, N_KV, HEAD_DIM, wb), (0, 1, 4, 2, 3))
    ck = to_feature_major(cache_k)
    cv = to_feature_major(cache_v)
    new_caches = None
    row2 = lambda v: v.reshape(1, -1)
    kv0 = 2 * D_RNN + D_ATTN
    outs = {k: [] for k in ("kp", "vp", "cp", "hp", "cs", "hs")}

    vecs = lambda v: v.reshape(depth, 1, -1)
    rw_all = {"conv_w": conv_w, "conv_b": vecs(conv_b), "wa": _block_diag(rg_wa).astype(BF16),
              "ba": vecs(rg_ba), "wx": _block_diag(rg_wx).astype(BF16), "bx": vecs(rg_bx),
              "lam": vecs(rg_lambda)}
    lw_all = {"gn_rnn": vecs(gn_rnn), "gn_attn": vecs(gn_attn), "w_out": w_out.astype(BF16),
              "ln1_g": vecs(ln1_g), "ln1_b": vecs(ln1_b),
              "router_wt": _hi_lo_rows(jnp.swapaxes(router_w, 1, 2)),
              "router_b": router_b.reshape(depth, N_EXPERTS, 1), "alpha": alpha,
              "ple_w_gate": ple_w_gate.astype(BF16), "ple_b_gate": vecs(ple_b_gate),
              "ple_w_proj": ple_w_proj.astype(BF16), "ln2_g": vecs(ln2_g), "ln2_b": vecs(ln2_b)}
    w_in_b = w_in.astype(BF16)

    for l in range(depth):
        rw = dict(rw_all, layer=l)
        lw = dict(lw_all, layer=l)
        first = l == 0

        xp, zp = _in_proj(xp, row2(ln_in_g), row2(ln_in_b), w_in_b, l, first)
        zp3 = zp.reshape(bsz, t_len, D_IN)
        yp, hp = _rnn_prompt(zp3, rw)
        op = _attn_prompt(zp, bsz, t_len, bias_prompt, attn_sinks, l)
        wp = min(WINDOW, t_len)
        outs["kp"].append(zp3[:, t_len - wp:, kv0:kv0 + 128].reshape(bsz, wp, N_KV, HEAD_DIM))
        outs["vp"].append(zp3[:, t_len - wp:, kv0 + 128:kv0 + 256].reshape(bsz, wp, N_KV, HEAD_DIM))
        outs["cp"].append(zp3[:, t_len - (CONV_W - 1):, :D_RNN])
        outs["hp"].append(hp.reshape(bsz, D_RNN))

        xs, zs = _in_proj(xs, row2(ln_in_g), row2(ln_in_b), w_in_b, l, first)
        ys, hs = _rnn_sample(zs, state_conv, state_h, rw)
        os_, new_caches = _attn_sample(zs[:, 2 * D_RNN:kv0], zs[:, kv0:kv0 + 128], zs[:, kv0 + 128:], ck, cv,
                                       new_caches, l, bias_sample, attn_sinks)
        outs["cs"].append(jnp.concatenate([state_conv[l][:, 1:], zs[:, None, :D_RNN]], axis=1))
        outs["hs"].append(hs)

        x1p, rrow_p, rcol_p, cnt_p = _out_proj(xp, yp.reshape(n_p, D_RNN), op, lw)
        x1s, rrow_s, rcol_s, cnt_s = _out_proj(xs, ys, os_, lw)

        tot_p = jnp.sum(cnt_p, axis=(0, 1))
        tot = tot_p + jnp.sum(cnt_s, axis=(0, 1))
        geom, base_p = _block_geometry(tot, n_blocks)
        base_s = base_p + tot_p

        xbuf = _dispatch(x1p, rrow_p, cnt_p, base_p, None, n_blocks * MOE_TILE)
        xbuf = _dispatch(x1s, rrow_s, cnt_s, base_s, xbuf, n_blocks * MOE_TILE)
        ybuf = _moe_experts(xbuf, geom, moe_w1, moe_b1, moe_w2, moe_b2, l)
        xp = _final(x1p, pp, l, rcol_p, cnt_p, base_p, ybuf, lw)
        xs = _final(x1s, ps, l, rcol_s, cnt_s, base_s, ybuf, lw)

    st = lambda k: jnp.stack(outs[k])
    return (xp.reshape(bsz, t_len, D_MODEL), xs.reshape(n_s, 1, D_MODEL),
            st("kp"), st("vp"), st("cp"), st("hp"), to_position_major(new_caches[0]),
            to_position_major(new_caches[1]), st("cs"), st("hs"))
```

```python
import functools
import math

import jax
import jax.numpy as jnp
from jax import lax
from jax.experimental import pallas as pl
from jax.experimental.pallas import tpu as pltpu

F32 = jnp.float32
BF16 = jnp.bfloat16
I32 = jnp.int32
U32 = jnp.uint32

D_MODEL = 1024
D_RNN = 512
RNN_BLOCKS = 8
CONV_W = 4
LRU_C = 8.0
N_HEADS = 8
HEAD_DIM = 64
N_KV = 2
GQA_R = N_HEADS // N_KV
D_ATTN = N_HEADS * HEAD_DIM
WINDOW = 128
N_BUCKETS = 32
MAX_DIST = 128
N_EXPERTS = 32
TOP_K = 4
D_FF = 1024
SWIGLU_LIMIT = 7.0
SWIGLU_ALPHA = 1.702
D_PLE = 256
LN_EPS = 1e-5
D_IN = 2 * D_RNN + D_ATTN + 2 * N_KV * HEAD_DIM
NEG = -1e30

SUBLANES = 8
LANES = 128
ROW_TILE = 512
MOE_TILE = 1024
D_PACK = D_MODEL // 2
VMEM_LIMIT = 56 * 1024 * 1024
HI_MASK = 0xFFFF0000


def _cparams(*sem):
    return pltpu.CompilerParams(dimension_semantics=sem, vmem_limit_bytes=VMEM_LIMIT)


def _layer_norm(x, g, b):
    mu = jnp.mean(x, axis=-1, keepdims=True)
    xc = x - mu
    var = jnp.mean(xc * xc, axis=-1, keepdims=True)
    return xc * lax.rsqrt(var + LN_EPS) * g + b


def _rms_norm(x, g):
    return x * lax.rsqrt(jnp.mean(x * x, axis=-1, keepdims=True) + LN_EPS) * g


def _const_spec(shape):
    return pl.BlockSpec(shape, lambda *_: (0,) * len(shape))


def _layer_spec(layer, *tail):
    return pl.BlockSpec((None,) + tail, lambda *_: (layer,) + (0,) * len(tail))


def _in_proj_kernel(x_ref, g_ref, b_ref, w_ref, *out_refs, apply_ln):
    x = x_ref[...]
    if apply_ln:
        x = _layer_norm(x, g_ref[...], b_ref[...])
        out_refs[0][...] = x
    out_refs[-1][...] = jnp.dot(x.astype(BF16), w_ref[...], preferred_element_type=F32)


def _in_proj(x, ln_g, ln_b, w_in_bf16, layer, apply_ln):
    n = x.shape[0]
    tm = min(2 * ROW_TILE, n)
    row = lambda w: pl.BlockSpec((tm, w), lambda i: (i, 0))
    out_shape = [jax.ShapeDtypeStruct((n, D_IN), F32)]
    out_specs = [row(D_IN)]
    if apply_ln:
        out_shape.insert(0, jax.ShapeDtypeStruct((n, D_MODEL), F32))
        out_specs.insert(0, row(D_MODEL))
    outs = pl.pallas_call(
        functools.partial(_in_proj_kernel, apply_ln=apply_ln),
        grid=(n // tm,),
        in_specs=[row(D_MODEL), _const_spec((1, D_MODEL)), _const_spec((1, D_MODEL)),
                  _layer_spec(layer, D_MODEL, D_IN)],
        out_specs=out_specs, out_shape=out_shape,
        compiler_params=_cparams("parallel"), name="in_proj",
    )(x, ln_g, ln_b, w_in_bf16)
    return (outs[0], outs[1]) if apply_ln else (x, outs[0])


def _rglru_coeffs(u, wa_ref, ba_ref, wx_ref, bx_ref, lam_ref):
    ub = u.astype(BF16)
    r = jax.nn.sigmoid(jnp.dot(ub, wa_ref[...], preferred_element_type=F32) + ba_ref[...])
    i = jax.nn.sigmoid(jnp.dot(ub, wx_ref[...], preferred_element_type=F32) + bx_ref[...])
    lam = -lam_ref[...]
    softplus = jnp.maximum(lam, 0.0) + jnp.log1p(jnp.exp(-jnp.abs(lam)))
    log_a = -LRU_C * r * softplus
    a = jnp.exp(log_a)
    b = jnp.sqrt(-jnp.tanh(log_a) * (a * a + 1.0)) * (i * u)
    return a, b


def _rnn_prompt_kernel(xr_ref, gate_ref, cw_ref, cb_ref, wa_ref, ba_ref, wx_ref, bx_ref, lam_ref,
                       y_ref, hlast_ref, xs_ref, a_ref, b_ref, carry_ref, *, tt):
    t = pl.program_id(1)

    @pl.when(t == 0)
    def _():
        xs_ref[0:SUBLANES, :] = jnp.zeros((SUBLANES, D_RNN), F32)
        carry_ref[...] = jnp.zeros((SUBLANES, D_RNN), F32)

    x = xr_ref[...]
    xs_ref[SUBLANES:SUBLANES + tt, :] = x
    cw = cw_ref[...]
    u = cb_ref[...] + x * cw[3:4]
    for j in range(CONV_W - 1):
        off = SUBLANES - (CONV_W - 1) + j
        u = u + xs_ref[off:off + tt, :] * cw[j:j + 1]
    xs_ref[0:SUBLANES, :] = x[tt - SUBLANES:tt, :]

    a, b = _rglru_coeffs(u, wa_ref, ba_ref, wx_ref, bx_ref, lam_ref)
    ng = tt // SUBLANES
    a = a.reshape(ng, SUBLANES, D_RNN)
    b = b.reshape(ng, SUBLANES, D_RNN)
    row = lax.broadcasted_iota(I32, (1, SUBLANES, 1), 1)
    for s in (1, 2, 4):
        a_sh = pltpu.roll(a, s, 1)
        b_sh = pltpu.roll(b, s, 1)
        m = row >= s
        b = jnp.where(m, a * b_sh + b, b)
        a = jnp.where(m, a * a_sh, a)
    a_ref[...] = a
    b_ref[...] = b

    def body(g, carry):
        h = a_ref[g] * carry + b_ref[g]
        b_ref[g] = h
        return jnp.broadcast_to(h[SUBLANES - 1:SUBLANES, :], (SUBLANES, D_RNN))

    carry = lax.fori_loop(0, ng, body, carry_ref[...], unroll=True)
    carry_ref[...] = carry
    h = b_ref[...].reshape(tt, D_RNN)
    y_ref[...] = h * jax.nn.gelu(gate_ref[...])
    hlast_ref[...] = carry[0:1, :]


def _rnn_prompt(z3, rw):
    bsz, t_len, _ = z3.shape
    tt = min(2 * ROW_TILE, t_len)
    vec = _layer_spec(rw["layer"], 1, D_RNN)
    mat = _layer_spec(rw["layer"], D_RNN, D_RNN)
    return pl.pallas_call(
        functools.partial(_rnn_prompt_kernel, tt=tt),
        grid=(bsz, t_len // tt),
        in_specs=[pl.BlockSpec((None, tt, D_RNN), lambda b, t: (b, t, 0)),
                  pl.BlockSpec((None, tt, D_RNN), lambda b, t: (b, t, 1)),
                  _layer_spec(rw["layer"], CONV_W, D_RNN), vec, mat, vec, mat, vec, vec],
        out_specs=[pl.BlockSpec((None, tt, D_RNN), lambda b, t: (b, t, 0)),
                   pl.BlockSpec((None, 1, D_RNN), lambda b, t: (b, 0, 0))],
        out_shape=[jax.ShapeDtypeStruct((bsz, t_len, D_RNN), F32),
                   jax.ShapeDtypeStruct((bsz, 1, D_RNN), F32)],
        scratch_shapes=[pltpu.VMEM((tt + SUBLANES, D_RNN), F32),
                        pltpu.VMEM((tt // SUBLANES, SUBLANES, D_RNN), F32),
                        pltpu.VMEM((tt // SUBLANES, SUBLANES, D_RNN), F32),
                        pltpu.VMEM((SUBLANES, D_RNN), F32)],
        compiler_params=_cparams("parallel", "arbitrary"), name="rnn_prompt",
    )(z3, z3, rw["conv_w"], rw["conv_b"], rw["wa"], rw["ba"], rw["wx"], rw["bx"], rw["lam"])


def _rnn_sample_kernel(xr_ref, gate_ref, c0_ref, c1_ref, c2_ref, h0_ref, cw_ref, cb_ref, wa_ref, ba_ref,
                       wx_ref, bx_ref, lam_ref, y_ref, h_ref):
    x = xr_ref[...]
    cw = cw_ref[...]
    u = (cb_ref[...] + c0_ref[...] * cw[0:1] + c1_ref[...] * cw[1:2] + c2_ref[...] * cw[2:3]
         + x * cw[3:4])
    a, b = _rglru_coeffs(u, wa_ref, ba_ref, wx_ref, bx_ref, lam_ref)
    h = a * h0_ref[...] + b
    h_ref[...] = h
    y_ref[...] = h * jax.nn.gelu(gate_ref[...])


def _rnn_sample(z, conv_state, h0, rw):
    n = z.shape[0]
    layer = rw["layer"]
    vec = _layer_spec(layer, 1, D_RNN)
    mat = _layer_spec(layer, D_RNN, D_RNN)
    full = _const_spec((n, D_RNN))
    depth = conv_state.shape[0]
    conv_flat = conv_state.reshape(depth, n, (CONV_W - 1) * D_RNN)
    tap = lambda j: pl.BlockSpec((None, n, D_RNN), lambda i: (layer, 0, j))
    return pl.pallas_call(
        _rnn_sample_kernel,
        grid=(1,),
        in_specs=[pl.BlockSpec((n, D_RNN), lambda i: (0, 0)), pl.BlockSpec((n, D_RNN), lambda i: (0, 1)),
                  tap(0), tap(1), tap(2), _layer_spec(layer, n, D_RNN),
                  _layer_spec(layer, CONV_W, D_RNN), vec, mat, vec, mat, vec, vec],
        out_specs=[full, full],
        out_shape=[jax.ShapeDtypeStruct((n, D_RNN), F32)] * 2,
        compiler_params=_cparams("arbitrary"), name="rnn_sample",
    )(z, z, conv_flat, conv_flat, conv_flat, h0, rw["conv_w"], rw["conv_b"], rw["wa"], rw["ba"], rw["wx"], rw["bx"], rw["lam"])


def _rel_bucket(dist):
    n = jnp.maximum(dist, 0)
    max_exact = N_BUCKETS // 2
    nf = jnp.maximum(n, max_exact).astype(F32)
    large = max_exact + (jnp.log(nf / max_exact) / math.log(MAX_DIST / max_exact)
                         * (N_BUCKETS - max_exact)).astype(I32)
    large = jnp.minimum(large, N_BUCKETS - 1)
    return jnp.where(n < max_exact, n, large)


def _bias_lookup(rel_bias, dist):
    hot = (_rel_bucket(dist)[..., None] == jnp.arange(N_BUCKETS)).astype(F32)
    out = jnp.tensordot(hot, rel_bias.astype(F32), axes=1, precision=lax.Precision.HIGHEST)
    return jnp.moveaxis(out, -1, 0)


def _half_lane_variants(x, group):
    lo = lax.broadcasted_iota(I32, (1, 2 * HEAD_DIM), 1) < HEAD_DIM
    xr = pltpu.roll(x, HEAD_DIM, 1)
    zero = jnp.zeros_like(x)
    if group == 0:
        return jnp.where(lo, x, zero), jnp.where(lo, zero, xr)
    return jnp.where(lo, xr, zero), jnp.where(lo, zero, x)


def _attn_prompt_kernel(q_ref, kc_ref, kp_ref, vc_ref, vp_ref, bias_ref, sink_ref, o_ref, *, nq, layer):
    first = pl.program_id(1) == 0
    col = lax.broadcasted_iota(I32, (1, 2 * WINDOW), 1)
    hide_prev = jnp.logical_and(first, col < WINDOW)
    for j in range(nq):
        rows = slice(j * WINDOW, (j + 1) * WINDOW)
        before = slice((j - 1) * WINDOW, j * WINDOW)
        kk = jnp.concatenate([kp_ref[...] if j == 0 else kc_ref[before, :], kc_ref[rows, :]], axis=0)
        vv = jnp.concatenate([vp_ref[...] if j == 0 else vc_ref[before, :], vc_ref[rows, :]], axis=0)
        for g in range(N_KV):
            k_lo, k_hi = _half_lane_variants(kk, g)
            v_lo, v_hi = _half_lane_variants(vv, g)
            for pair in range(GQA_R // 2):
                p0 = g * (GQA_R // 2) + pair
                qp = (q_ref[rows, p0 * 128:(p0 + 1) * 128] * (HEAD_DIM ** -0.5)).astype(BF16)
                acc = jnp.zeros((WINDOW, 2 * HEAD_DIM), F32)
                for kx, vx, h in ((k_lo, v_lo, 2 * p0), (k_hi, v_hi, 2 * p0 + 1)):
                    s = lax.dot_general(qp, kx.astype(BF16), (((1,), (1,)), ((), ())),
                                        preferred_element_type=F32) + bias_ref[h]
                    if j == 0:
                        s = jnp.where(hide_prev, NEG, s)
                    sink = sink_ref[layer, h]
                    m = jnp.maximum(jnp.max(s, axis=-1, keepdims=True), sink)
                    p = jnp.exp(s - m)
                    denom = jnp.sum(p, axis=-1, keepdims=True) + jnp.exp(sink - m)
                    acc = acc + jnp.dot(p.astype(BF16), vx.astype(BF16), preferred_element_type=F32) / denom
                o_ref[rows, p0 * 128:(p0 + 1) * 128] = acc


def _attn_prompt(z, bsz, t_len, bias_prompt, sinks, layer):
    nb = t_len // WINDOW
    nq = min(8, nb)
    ns = nb // nq
    kcol = (2 * D_RNN + D_ATTN) // 128
    cur = lambda c: pl.BlockSpec((nq * WINDOW, 128), lambda b, i: (b * ns + i, c))
    prev = lambda c: pl.BlockSpec((WINDOW, 128), lambda b, i: (b * nb + jnp.maximum(i * nq - 1, 0), c))
    return pl.pallas_call(
        functools.partial(_attn_prompt_kernel, nq=nq, layer=layer),
        grid=(bsz, ns),
        in_specs=[pl.BlockSpec((nq * WINDOW, D_ATTN), lambda b, i: (b * ns + i, 2 * D_RNN // D_ATTN)),
                  cur(kcol), prev(kcol), cur(kcol + 1), prev(kcol + 1),
                  _const_spec((N_HEADS, WINDOW, 2 * WINDOW)),
                  pl.BlockSpec(memory_space=pltpu.SMEM)],
        out_specs=pl.BlockSpec((nq * WINDOW, D_ATTN), lambda b, i: (b * ns + i, 0)),
        out_shape=jax.ShapeDtypeStruct((bsz * t_len, D_ATTN), F32),
        compiler_params=_cparams("parallel", "arbitrary"), name="attn_prompt",
    )(z, z, z, z, z, bias_prompt, sinks)


def _attn_sample_kernel(q_ref, kn_ref, vn_ref, ck_ref, cv_ref, bias_ref, sink_ref, hmask_ref, *rest):
    o_ref, nk_ref, nv_ref = rest[-3:]
    pos = lax.broadcasted_iota(I32, (1, 1, WINDOW), 2)
    last = pos == WINDOW - 1
    bb = ck_ref.shape[0]
    newk = jnp.where(last, jnp.broadcast_to(kn_ref[...], (bb, 128, WINDOW)), pltpu.roll(ck_ref[...], WINDOW - 1, 2))
    newv = jnp.where(last, jnp.broadcast_to(vn_ref[...], (bb, 128, WINDOW)), pltpu.roll(cv_ref[...], WINDOW - 1, 2))
    nk_ref[...] = newk
    nv_ref[...] = newv
    s = lax.dot_general(q_ref[...].astype(BF16), newk.astype(BF16), (((2,), (1,)), ((0,), (0,))),
                        preferred_element_type=F32)
    s = s + bias_ref[...][None]
    sink = sink_ref[...][None]
    m = jnp.maximum(jnp.max(s, axis=-1, keepdims=True), sink)
    p = jnp.exp(s - m)
    denom = jnp.sum(p, axis=-1, keepdims=True) + jnp.exp(sink - m)
    o = lax.dot_general(p.astype(BF16), newv.astype(BF16), (((2,), (2,)), ((0,), (0,))),
                        preferred_element_type=F32)
    o_ref[...] = o / denom * hmask_ref[...][None]


def _attn_sample(q, k_new, v_new, k_cache, v_cache, new_caches, layer, bias_sample, sinks):
    n = q.shape[0]
    depth = k_cache.shape[0]
    bb = 32
    qh = q.reshape(n, N_KV, GQA_R, HEAD_DIM) * (HEAD_DIM ** -0.5)
    eye = jnp.eye(N_KV, dtype=F32)
    q8 = (qh[:, :, :, None, :] * eye[None, :, None, :, None]).reshape(n, N_HEADS, N_KV * HEAD_DIM)
    hmask = jnp.repeat(jnp.repeat(eye, GQA_R, axis=0), HEAD_DIM, axis=1)
    blk3 = lambda a, b: pl.BlockSpec((bb, a, b), lambda i: (i, 0, 0))
    cache = pl.BlockSpec((None, bb, 128, WINDOW), lambda i: (layer, i, 0, 0))
    in_specs = [blk3(N_HEADS, 128), blk3(128, 1), blk3(128, 1), cache, cache,
                _const_spec((N_HEADS, WINDOW)), _layer_spec(layer, N_HEADS, 1), _const_spec((N_HEADS, 128))]
    args = [q8, k_new.reshape(n, 128, 1), v_new.reshape(n, 128, 1), k_cache, v_cache, bias_sample,
            sinks.reshape(-1, N_HEADS, 1), hmask]
    aliases = {}
    if new_caches is not None:
        in_specs += [pl.BlockSpec(memory_space=pl.ANY)] * 2
        args += list(new_caches)
        aliases = {len(args) - 2: 1, len(args) - 1: 2}
    o8, nk, nv = pl.pallas_call(
        _attn_sample_kernel,
        grid=(n // bb,),
        in_specs=in_specs,
        out_specs=[blk3(N_HEADS, 128), cache, cache],
        out_shape=[jax.ShapeDtypeStruct((n, N_HEADS, 128), F32),
                   jax.ShapeDtypeStruct((depth, n, 128, WINDOW), F32),
                   jax.ShapeDtypeStruct((depth, n, 128, WINDOW), F32)],
        input_output_aliases=aliases,
        compiler_params=_cparams("arbitrary"), name="attn_sample",
    )(*args)
    o = o8.reshape(n, N_KV, GQA_R, N_KV, HEAD_DIM)
    o = jnp.stack([o[:, g, :, g, :] for g in range(N_KV)], axis=1).reshape(n, D_ATTN)
    return o, (nk, nv)


ONE_HOT_BLOCK = 256


def _sorted_rows(tm):
    return TOP_K * tm + N_EXPERTS * SUBLANES


def _out_proj_kernel(x_ref, yr_ref, o_ref, gnr_ref, gna_ref, w_ref, g1_ref, b1_ref, rwt_ref, rb_ref,
                     x1_ref, rrow_ref, rcol_ref, cnt_ref, *, tm, alpha):
    yn = _rms_norm(yr_ref[...], gnr_ref[...]).astype(BF16)
    on = _rms_norm(o_ref[...], gna_ref[...]).astype(BF16)
    mix = jnp.dot(jnp.concatenate([yn, on], axis=1), w_ref[...], preferred_element_type=F32)
    x1 = _layer_norm(alpha * x_ref[...] + mix, g1_ref[...], b1_ref[...])
    x1_ref[...] = x1

    x1h = x1.astype(BF16)
    x1l = (x1 - x1h.astype(F32)).astype(BF16)
    nt_dims = (((1,), (1,)), ((), ()))
    by_hi = lax.dot_general(rwt_ref[...], x1h, nt_dims, preferred_element_type=F32)
    by_lo = lax.dot_general(rwt_ref[0:N_EXPERTS, :], x1l, nt_dims, preferred_element_type=F32)
    logits = by_hi[0:N_EXPERTS] + by_hi[N_EXPERTS:] + by_lo + rb_ref[...]
    eidx = lax.broadcasted_iota(I32, (N_EXPERTS, tm), 0).astype(F32)
    work = logits
    vals, hots = [], []
    for _ in range(TOP_K):
        v = jnp.max(work, axis=0, keepdims=True)
        idx = jnp.min(jnp.where(work == v, eidx, float(N_EXPERTS)), axis=0, keepdims=True)
        hot = eidx == idx
        work = jnp.where(hot, -jnp.inf, work)
        vals.append(v)
        hots.append(hot.astype(F32))
    ex = [jnp.exp(v - vals[0]) for v in vals]
    tot = ex[0] + ex[1] + ex[2] + ex[3]
    tok_hot = hots[0] + hots[1] + hots[2] + hots[3]
    s_i = lax.broadcasted_iota(I32, (tm, tm), 0)
    t_i = lax.broadcasted_iota(I32, (tm, tm), 1)
    earlier = (s_i < t_i).astype(BF16)
    before = jnp.dot(tok_hot.astype(BF16), earlier, preferred_element_type=F32)
    cnt = jnp.sum(tok_hot, axis=1, keepdims=True)
    cnt8 = jnp.floor((cnt + (SUBLANES - 1)) * (1.0 / SUBLANES)) * SUBLANES
    cnt8_b = jnp.broadcast_to(cnt8, (N_EXPERTS, LANES))
    e_r = lax.broadcasted_iota(I32, (N_EXPERTS, N_EXPERTS), 0)
    e_c = lax.broadcasted_iota(I32, (N_EXPERTS, N_EXPERTS), 1)
    start = jnp.dot((e_c < e_r).astype(F32), cnt8_b, preferred_element_type=F32,
                    precision=lax.Precision.HIGHEST)[:, 0:1]
    slot = start + before
    rows = [jnp.sum(h * slot, axis=0, keepdims=True) for h in hots] + [e / tot for e in ex]
    rrow = jnp.concatenate(rows, axis=0)
    rrow_ref[...] = rrow
    rcol_ref[...] = rrow.T
    cnt_ref[...] = cnt8_b.T[0:1, 0:N_EXPERTS].astype(I32)


def _out_proj(x, y_rnn, o, lw):
    n = x.shape[0]
    tm = min(ROW_TILE, n)
    nt = n // tm
    row = lambda w: pl.BlockSpec((tm, w), lambda i: (i, 0))
    layer = lw["layer"]
    vec = lambda w: _layer_spec(layer, 1, w)
    return pl.pallas_call(
        functools.partial(_out_proj_kernel, tm=tm, alpha=lw["alpha"]),
        grid=(nt,),
        in_specs=[row(D_MODEL), row(D_RNN), row(D_ATTN), vec(D_RNN), vec(D_ATTN),
                  _layer_spec(layer, D_MODEL, D_MODEL), vec(D_MODEL), vec(D_MODEL),
                  _layer_spec(layer, 2 * N_EXPERTS, D_MODEL), _layer_spec(layer, N_EXPERTS, 1)],
        out_specs=[row(D_MODEL), pl.BlockSpec((None, 2 * TOP_K, tm), lambda i: (i, 0, 0)), row(2 * TOP_K),
                   pl.BlockSpec((None, 1, N_EXPERTS), lambda i: (i, 0, 0))],
        out_shape=[jax.ShapeDtypeStruct((n, D_MODEL), F32), jax.ShapeDtypeStruct((nt, 2 * TOP_K, tm), F32),
                   jax.ShapeDtypeStruct((n, 2 * TOP_K), F32), jax.ShapeDtypeStruct((nt, 1, N_EXPERTS), I32)],
        compiler_params=_cparams("parallel"), name="out_proj_route",
    )(x, y_rnn, o, lw["gn_rnn"], lw["gn_attn"], lw["w_out"], lw["ln1_g"], lw["ln1_b"],
      lw["router_wt"], lw["router_b"])


RUN_CHUNK = 64
_SMALL_PIECES = (32, 16, 8)
_PAD_PIECES = (128, 64, 32, 16, 8)


def _for_each_piece(n8, fn):
    def chunk(j, c):
        fn(pl.multiple_of(j * RUN_CHUNK, RUN_CHUNK), RUN_CHUNK)
        return c
    lax.fori_loop(0, n8 // RUN_CHUNK, chunk, 0)
    for s in _SMALL_PIECES:
        @pl.when((n8 & s) != 0)
        def _(s=s):
            fn(n8 & (-2 * s), s)


def _wait_tile_rows(n_rows, tm, wait_rows):
    wait_rows(TOP_K * tm)
    pad = n_rows - TOP_K * tm
    for s in _PAD_PIECES:
        @pl.when((pad & s) != 0)
        def _(s=s):
            wait_rows(s)


def _dispatch_kernel(cnt_ref, base_ref, x_ref, rrow_ref, *rest, tm):
    xbuf_ref, xs_ref, carry_ref, nrows_ref, sems = rest[-5:]
    nrow = _sorted_rows(tm)
    i = pl.program_id(0)
    slot = i % 2

    @pl.when(i == 0)
    def _():
        def init(e, c):
            carry_ref[e] = base_ref[e]
            return c
        lax.fori_loop(0, N_EXPERTS, init, 0)

    r = lax.broadcasted_iota(I32, (ONE_HOT_BLOCK, 1), 0).astype(F32).astype(BF16)
    blocks = []
    for c in range(nrow // ONE_HOT_BLOCK):
        hit = None
        for k in range(TOP_K):
            h = r == (rrow_ref[k:k + 1, :] - float(c * ONE_HOT_BLOCK)).astype(BF16)
            hit = h if hit is None else jnp.logical_or(hit, h)
        blocks.append(jnp.where(hit, jnp.ones((), BF16), jnp.zeros((), BF16)))
    perm = jnp.concatenate(blocks, axis=0)
    xs = jnp.dot(perm, x_ref[...].astype(BF16), preferred_element_type=F32)
    lo = lax.bitcast_convert_type(xs[:, :D_PACK], U32)
    hi = lax.bitcast_convert_type(xs[:, D_PACK:], U32)
    xs_ref[slot] = (hi & U32(HI_MASK)) | (lo >> 16)

    def run_copy(buf, src, dst, s):
        return pltpu.make_async_copy(xs_ref.at[buf, pl.ds(pl.multiple_of(src, SUBLANES), s)],
                                     xbuf_ref.at[pl.ds(pl.multiple_of(dst, SUBLANES), s)], sems.at[buf])

    def start_expert(e, src, priority):
        n8 = cnt_ref[0, e]
        dst = carry_ref[e]
        _for_each_piece(n8, lambda off, s: run_copy(slot, src + off, dst + off, s).start(priority=priority))
        carry_ref[e] = dst + n8
        return src + n8

    def start(pair, src):
        return start_expert(2 * pair + 1, start_expert(2 * pair, src, 0), 1)

    def wait_tile(buf):
        _wait_tile_rows(nrows_ref[buf], tm, lambda s: run_copy(buf, 0, 0, s).wait())

    nrows_ref[slot] = lax.fori_loop(0, N_EXPERTS // 2, start, 0)

    @pl.when(i > 0)
    def _():
        wait_tile(1 - slot)

    @pl.when(i == pl.num_programs(0) - 1)
    def _():
        wait_tile(slot)


def _dispatch(x1, rrow, cnt8, base, xbuf, n_rows):
    n = x1.shape[0]
    nt, _, tm = rrow.shape
    in_specs = [pl.BlockSpec((None, 1, N_EXPERTS), lambda i: (i, 0, 0), memory_space=pltpu.SMEM),
                pl.BlockSpec(memory_space=pltpu.SMEM),
                pl.BlockSpec((tm, D_MODEL), lambda i: (i, 0)),
                pl.BlockSpec((None, 2 * TOP_K, tm), lambda i: (i, 0, 0))]
    args = [cnt8, base, x1, rrow]
    aliases = {}
    if xbuf is not None:
        in_specs.append(pl.BlockSpec(memory_space=pl.ANY))
        args.append(xbuf)
        aliases = {len(args) - 1: 0}
    return pl.pallas_call(
        functools.partial(_dispatch_kernel, tm=tm),
        grid=(nt,),
        in_specs=in_specs,
        out_specs=pl.BlockSpec(memory_space=pl.ANY),
        out_shape=jax.ShapeDtypeStruct((n_rows, D_PACK), U32),
        scratch_shapes=[pltpu.VMEM((2, _sorted_rows(tm), D_PACK), U32), pltpu.SMEM((N_EXPERTS,), I32),
                        pltpu.SMEM((2,), I32), pltpu.SemaphoreType.DMA((2,))],
        input_output_aliases=aliases,
        compiler_params=_cparams("arbitrary"), name="moe_dispatch",
    )(*args)


def _moe_kernel(br_ref, be_ref, nv_ref, first_ref, next_ref, x_ref, w1_hbm, b1_ref, w2_hbm, b2_ref, y_ref,
                w1f_ref, w2f_ref, w1s_ref, w2s_ref, sems, *, layer):
    i = pl.program_id(0)

    def load(e):
        return (pltpu.make_async_copy(w1_hbm.at[layer, e], w1f_ref, sems.at[0]),
                pltpu.make_async_copy(w2_hbm.at[layer, e], w2f_ref, sems.at[1]))

    @pl.when(jnp.logical_and(i == 0, first_ref[0] == 1))
    def _():
        for c in load(be_ref[0]):
            c.start()

    nv = nv_ref[i]
    quarter = MOE_TILE // 4
    is_first = first_ref[i] == 1
    fresh_full = jnp.logical_and(is_first, nv > 3 * quarter)

    def start_next_load():
        @pl.when(next_ref[i] >= 0)
        def _():
            for c in load(next_ref[i]):
                c.start()

    def expert_rows(m, fresh):
        if fresh:
            w1 = w1f_ref[...].astype(BF16)
            w2 = w2f_ref[...].astype(BF16)
            w1s_ref[...] = w1
            w2s_ref[...] = w2
        else:
            w1 = w1s_ref[...]
            w2 = w2s_ref[...]
        rows = lax.broadcasted_iota(I32, (m, 1), 0)
        xw = jnp.where(rows < nv, x_ref[0:m, :], U32(0))
        lo = lax.bitcast_convert_type(xw << 16, F32).astype(BF16)
        hi = lax.bitcast_convert_type(xw & U32(HI_MASK), F32).astype(BF16)
        xb = jnp.concatenate([lo, hi], axis=1)
        hdn = jnp.dot(xb, w1, preferred_element_type=F32) + b1_ref[...]
        g = jnp.minimum(hdn[:, :D_FF], SWIGLU_LIMIT)
        lin = jnp.clip(hdn[:, D_FF:], -SWIGLU_LIMIT, SWIGLU_LIMIT)
        act = g * jax.nn.sigmoid(SWIGLU_ALPHA * g) * (lin + 1.0)
        y_ref[0:m, :] = jnp.dot(act.astype(BF16), w2, preferred_element_type=F32) + b2_ref[...]

    @pl.when(is_first)
    def _():
        for c in load(be_ref[i]):
            c.wait()

    @pl.when(fresh_full)
    def _():
        expert_rows(MOE_TILE, True)
        start_next_load()

    @pl.when(jnp.logical_and(is_first, jnp.logical_not(fresh_full)))
    def _():
        w1s_ref[...] = w1f_ref[...].astype(BF16)
        w2s_ref[...] = w2f_ref[...].astype(BF16)
        start_next_load()

    for q in range(1, 5):
        @pl.when(jnp.logical_and(jnp.logical_not(fresh_full),
                                 jnp.logical_and(nv > (q - 1) * quarter, nv <= q * quarter)))
        def _(q=q):
            expert_rows(q * quarter, False)


def _moe_experts(xbuf, geom, w1, b1, w2, b2, layer):
    n_blocks = xbuf.shape[0] // MOE_TILE
    bias = lambda w: pl.BlockSpec((None, None, 1, w), lambda i, br, be, *_: (layer, be[i], 0, 0))
    grid_spec = pltpu.PrefetchScalarGridSpec(
        num_scalar_prefetch=5,
        grid=(n_blocks,),
        in_specs=[pl.BlockSpec((MOE_TILE, D_PACK), lambda i, br, *_: (br[i], 0)),
                  pl.BlockSpec(memory_space=pl.ANY), bias(2 * D_FF),
                  pl.BlockSpec(memory_space=pl.ANY), bias(D_MODEL)],
        out_specs=pl.BlockSpec((MOE_TILE, D_MODEL), lambda i, br, *_: (br[i], 0)),
        scratch_shapes=[pltpu.VMEM((D_MODEL, 2 * D_FF), F32), pltpu.VMEM((D_FF, D_MODEL), F32),
                        pltpu.VMEM((D_MODEL, 2 * D_FF), BF16), pltpu.VMEM((D_FF, D_MODEL), BF16),
                        pltpu.SemaphoreType.DMA((2,))],
    )
    depth = w1.shape[0]
    return pl.pallas_call(
        functools.partial(_moe_kernel, layer=layer), grid_spec=grid_spec,
        out_shape=jax.ShapeDtypeStruct((xbuf.shape[0], D_MODEL), F32),
        compiler_params=_cparams("arbitrary"), name="moe_experts",
    )(*geom, xbuf, w1, b1.reshape(depth, N_EXPERTS, 1, 2 * D_FF), w2, b2.reshape(depth, N_EXPERTS, 1, D_MODEL))


def _final_kernel(cnt_ref, cnt_next_ref, base_ref, x_ref, p_ref, rcol_ref, wg_ref, bg_ref, wp_ref, g2_ref, b2_ref,
                  ybuf_ref, out_ref, ys_ref, carry_ref, nrows_ref, sems, *, tm, alpha):
    nrow = _sorted_rows(tm)
    i = pl.program_id(0)
    slot = i % 2

    def run_copy(buf, src, dst, s):
        return pltpu.make_async_copy(ybuf_ref.at[pl.ds(pl.multiple_of(src, SUBLANES), s)],
                                     ys_ref.at[buf, pl.ds(pl.multiple_of(dst, SUBLANES), s)], sems.at[buf])

    def gather_tile(counts_ref, buf):
        def start_expert(e, dst, priority):
            n8 = counts_ref[0, e]
            src = carry_ref[e]
            _for_each_piece(n8, lambda off, s: run_copy(buf, src + off, dst + off, s).start(priority=priority))
            carry_ref[e] = src + n8
            return dst + n8

        def start(pair, dst):
            return start_expert(2 * pair + 1, start_expert(2 * pair, dst, 0), 1)
        nrows_ref[buf] = lax.fori_loop(0, N_EXPERTS // 2, start, 0)

    @pl.when(i == 0)
    def _():
        def init(e, c):
            carry_ref[e] = base_ref[e]
            return c
        lax.fori_loop(0, N_EXPERTS, init, 0)
        ys_ref[:, TOP_K * tm:, :] = jnp.zeros((2, nrow - TOP_K * tm, D_MODEL), F32)
        gather_tile(cnt_ref, 0)

    @pl.when(i + 1 < pl.num_programs(0))
    def _():
        gather_tile(cnt_next_ref, 1 - slot)

    x = x_ref[...]
    ple = (jax.nn.sigmoid(jnp.dot(x.astype(BF16), wg_ref[...], preferred_element_type=F32) + bg_ref[...])
           * jnp.dot(p_ref[...].astype(BF16), wp_ref[...], preferred_element_type=F32))
    col = lax.broadcasted_iota(I32, (1, ONE_HOT_BLOCK), 1).astype(F32).astype(BF16)
    rc = rcol_ref[...]
    gates = rc[:, TOP_K:].astype(BF16)
    blocks = []
    for c in range(nrow // ONE_HOT_BLOCK):
        blk = jnp.zeros((tm, ONE_HOT_BLOCK), BF16)
        for k in range(TOP_K):
            hit = col == (rc[:, k:k + 1] - float(c * ONE_HOT_BLOCK)).astype(BF16)
            blk = jnp.where(hit, gates[:, k:k + 1], blk)
        blocks.append(blk)
    gmat = jnp.concatenate(blocks, axis=1)
    base = alpha * x + ple

    _wait_tile_rows(nrows_ref[slot], tm, lambda s: run_copy(slot, 0, 0, s).wait())

    ffn = jnp.dot(gmat, ys_ref[slot].astype(BF16), preferred_element_type=F32)
    out_ref[...] = _layer_norm(base + ffn, g2_ref[...], b2_ref[...])


def _final(x1, p_all, layer, rcol, cnt8, base, ybuf, lw):
    n = x1.shape[0]
    nt = cnt8.shape[0]
    tm = n // nt
    row = lambda w: pl.BlockSpec((tm, w), lambda i: (i, 0))
    vec = lambda w: _layer_spec(layer, 1, w)
    return pl.pallas_call(
        functools.partial(_final_kernel, tm=tm, alpha=lw["alpha"]),
        grid=(nt,),
        in_specs=[pl.BlockSpec((None, 1, N_EXPERTS), lambda i: (i, 0, 0), memory_space=pltpu.SMEM),
                  pl.BlockSpec((None, 1, N_EXPERTS), lambda i: (jnp.minimum(i + 1, nt - 1), 0, 0),
                               memory_space=pltpu.SMEM),
                  pl.BlockSpec(memory_space=pltpu.SMEM),
                  row(D_MODEL), pl.BlockSpec((None, tm, D_PLE), lambda i: (layer, i, 0)), row(2 * TOP_K),
                  _layer_spec(layer, D_MODEL, D_MODEL), vec(D_MODEL), _layer_spec(layer, D_PLE, D_MODEL),
                  vec(D_MODEL), vec(D_MODEL), pl.BlockSpec(memory_space=pl.ANY)],
        out_specs=row(D_MODEL),
        out_shape=jax.ShapeDtypeStruct((n, D_MODEL), F32),
        scratch_shapes=[pltpu.VMEM((2, _sorted_rows(tm), D_MODEL), F32), pltpu.SMEM((N_EXPERTS,), I32),
                        pltpu.SMEM((2,), I32), pltpu.SemaphoreType.DMA((2,))],
        compiler_params=_cparams("arbitrary"), name="moe_combine_ln2",
    )(cnt8, cnt8, base, x1, p_all, rcol, lw["ple_w_gate"], lw["ple_b_gate"], lw["ple_w_proj"], lw["ln2_g"], lw["ln2_b"],
      ybuf)


def _block_diag(w):
    depth, nb, bw, _ = w.shape
    eye = jnp.eye(nb, dtype=w.dtype)
    return (w[:, :, :, None, :] * eye[None, :, None, :, None]).reshape(depth, nb * bw, nb * bw)


def _hi_lo_rows(w):
    hi = w.astype(BF16)
    lo = (w - hi.astype(F32)).astype(BF16)
    return jnp.concatenate([hi, lo], axis=-2)


def _block_geometry(tot, n_blocks):
    nblk = (tot + MOE_TILE - 1) // MOE_TILE
    blk_end = jnp.cumsum(nblk)
    blk_start = blk_end - nblk
    ids = jnp.arange(n_blocks, dtype=I32)
    n_valid = blk_end[-1]
    expert_at = lambda j: jnp.minimum(jnp.sum(j[:, None] >= blk_end[None, :], axis=1), N_EXPERTS - 1).astype(I32)
    src = jnp.clip(ids, 0, jnp.maximum(n_valid - 1, 0))
    e_of = expert_at(src)
    hot = e_of[:, None] == jnp.arange(N_EXPERTS)[None, :]
    pick = lambda v: jnp.sum(jnp.where(hot, v[None, :], 0), axis=1)
    start_of, end_of, tot_of = pick(blk_start), pick(blk_end), pick(tot)
    real = ids < n_valid
    nv = jnp.where(real, jnp.clip(tot_of - (ids - start_of) * MOE_TILE, 0, MOE_TILE), 0)
    first = jnp.logical_and(real, ids == start_of)
    nxt = jnp.where(end_of < n_valid, expert_at(end_of), -1)
    geom = tuple(v.astype(I32) for v in (src, e_of, nv, first, nxt))
    return geom, (blk_start * MOE_TILE).astype(I32)


def kernel(x_prompt, x_sample, cache_k, cache_v, state_conv, state_h, p_prompt, p_sample, ln_in_g, ln_in_b, rel_bias, w_in, conv_w, conv_b, rg_wa, rg_ba, rg_wx, rg_bx, rg_lambda, attn_sinks, gn_rnn, gn_attn, w_out, ln1_g, ln1_b, router_w, router_b, moe_w1, moe_b1, moe_w2, moe_b2, ple_w_gate, ple_b_gate, ple_w_proj, ln2_g, ln2_b):
    depth = w_in.shape[0]
    bsz, t_len, _ = x_prompt.shape
    n_p = bsz * t_len
    n_s = x_sample.shape[0]
    wb = cache_k.shape[2]
    alpha = (2 * depth) ** 0.25

    dist = jnp.arange(WINDOW)[:, None] + WINDOW - jnp.arange(2 * WINDOW)[None, :]
    valid = (dist >= 0) & (dist < WINDOW)
    bias_prompt = jnp.where(valid[None], _bias_lookup(rel_bias, dist), NEG)
    bias_sample = _bias_lookup(rel_bias, WINDOW - 1 - jnp.arange(WINDOW))

    tm_p = min(ROW_TILE, n_p)
    n_tiles = n_p // tm_p + 1
    max_rows = (n_p + n_s) * TOP_K + n_tiles * N_EXPERTS * (SUBLANES - 1)
    n_blocks = -(-(max_rows + N_EXPERTS * (MOE_TILE - 1)) // MOE_TILE)

    xp = x_prompt.reshape(n_p, D_MODEL)
    xs = x_sample.reshape(n_s, D_MODEL)
    pp = p_prompt.reshape(depth, n_p, D_PLE)
    ps = p_sample.reshape(depth, n_s, D_PLE)
    to_feature_major = lambda c: jnp.transpose(c, (0, 1, 3, 4, 2)).reshape(depth, n_s, N_KV * HEAD_DIM, wb)
    to_position_major = lambda c: jnp.transpose(c.reshape(depth, n_s, N_KV, HEAD_DIM, wb), (0, 1, 4, 2, 3))
    ck = to_feature_major(cache_k)
    cv = to_feature_major(cache_v)
    new_caches = None
    row2 = lambda v: v.reshape(1, -1)
    kv0 = 2 * D_RNN + D_ATTN
    outs = {k: [] for k in ("kp", "vp", "cp", "hp", "cs", "hs")}

    vecs = lambda v: v.reshape(depth, 1, -1)
    rw_all = {"conv_w": conv_w, "conv_b": vecs(conv_b), "wa": _block_diag(rg_wa).astype(BF16),
              "ba": vecs(rg_ba), "wx": _block_diag(rg_wx).astype(BF16), "bx": vecs(rg_bx),
              "lam": vecs(rg_lambda)}
    lw_all = {"gn_rnn": vecs(gn_rnn), "gn_attn": vecs(gn_attn), "w_out": w_out.astype(BF16),
              "ln1_g": vecs(ln1_g), "ln1_b": vecs(ln1_b),
              "router_wt": _hi_lo_rows(jnp.swapaxes(router_w, 1, 2)),
              "router_b": router_b.reshape(depth, N_EXPERTS, 1), "alpha": alpha,
              "ple_w_gate": ple_w_gate.astype(BF16), "ple_b_gate": vecs(ple_b_gate),
              "ple_w_proj": ple_w_proj.astype(BF16), "ln2_g": vecs(ln2_g), "ln2_b": vecs(ln2_b)}
    w_in_b = w_in.astype(BF16)

    for l in range(depth):
        rw = dict(rw_all, layer=l)
        lw = dict(lw_all, layer=l)
        first = l == 0

        xp, zp = _in_proj(xp, row2(ln_in_g), row2(ln_in_b), w_in_b, l, first)
        zp3 = zp.reshape(bsz, t_len, D_IN)
        yp, hp = _rnn_prompt(zp3, rw)
        op = _attn_prompt(zp, bsz, t_len, bias_prompt, attn_sinks, l)
        wp = min(WINDOW, t_len)
        outs["kp"].append(zp3[:, t_len - wp:, kv0:kv0 + 128].reshape(bsz, wp, N_KV, HEAD_DIM))
        outs["vp"].append(zp3[:, t_len - wp:, kv0 + 128:kv0 + 256].reshape(bsz, wp, N_KV, HEAD_DIM))
        outs["cp"].append(zp3[:, t_len - (CONV_W - 1):, :D_RNN])
        outs["hp"].append(hp.reshape(bsz, D_RNN))

        xs, zs = _in_proj(xs, row2(ln_in_g), row2(ln_in_b), w_in_b, l, first)
        ys, hs = _rnn_sample(zs, state_conv, state_h, rw)
        os_, new_caches = _attn_sample(zs[:, 2 * D_RNN:kv0], zs[:, kv0:kv0 + 128], zs[:, kv0 + 128:], ck, cv,
                                       new_caches, l, bias_sample, attn_sinks)
        outs["cs"].append(jnp.concatenate([state_conv[l][:, 1:], zs[:, None, :D_RNN]], axis=1))
        outs["hs"].append(hs)

        x1p, rrow_p, rcol_p, cnt_p = _out_proj(xp, yp.reshape(n_p, D_RNN), op, lw)
        x1s, rrow_s, rcol_s, cnt_s = _out_proj(xs, ys, os_, lw)

        tot_p = jnp.sum(cnt_p, axis=(0, 1))
        tot = tot_p + jnp.sum(cnt_s, axis=(0, 1))
        geom, base_p = _block_geometry(tot, n_blocks)
        base_s = base_p + tot_p

        xbuf = _dispatch(x1p, rrow_p, cnt_p, base_p, None, n_blocks * MOE_TILE)
        xbuf = _dispatch(x1s, rrow_s, cnt_s, base_s, xbuf, n_blocks * MOE_TILE)
        ybuf = _moe_experts(xbuf, geom, moe_w1, moe_b1, moe_w2, moe_b2, l)
        xp = _final(x1p, pp, l, rcol_p, cnt_p, base_p, ybuf, lw)
        xs = _final(x1s, ps, l, rcol_s, cnt_s, base_s, ybuf, lw)

    st = lambda k: jnp.stack(outs[k])
    return (xp.reshape(bsz, t_len, D_MODEL), xs.reshape(n_s, 1, D_MODEL),
            st("kp"), st("vp"), st("cp"), st("hp"), to_position_major(new_caches[0]),
            to_position_major(new_caches[1]), st("cs"), st("hs"))
```

```python
import functools
import math

import jax
import jax.numpy as jnp
from jax import lax
from jax.experimental import pallas as pl
from jax.experimental.pallas import tpu as pltpu

F32 = jnp.float32
BF16 = jnp.bfloat16
I32 = jnp.int32
U32 = jnp.uint32

D_MODEL = 1024
D_RNN = 512
RNN_BLOCKS = 8
CONV_W = 4
LRU_C = 8.0
N_HEADS = 8
HEAD_DIM = 64
N_KV = 2
GQA_R = N_HEADS // N_KV
D_ATTN = N_HEADS * HEAD_DIM
WINDOW = 128
N_BUCKETS = 32
MAX_DIST = 128
N_EXPERTS = 32
TOP_K = 4
D_FF = 1024
SWIGLU_LIMIT = 7.0
SWIGLU_ALPHA = 1.702
D_PLE = 256
LN_EPS = 1e-5
D_IN = 2 * D_RNN + D_ATTN + 2 * N_KV * HEAD_DIM
NEG = -1e30

SUBLANES = 8
LANES = 128
ROW_TILE = 512
MOE_TILE = 1024
D_PACK = D_MODEL // 2
VMEM_LIMIT = 56 * 1024 * 1024
HI_MASK = 0xFFFF0000


def _cparams(*sem):
    return pltpu.CompilerParams(dimension_semantics=sem, vmem_limit_bytes=VMEM_LIMIT)


def _layer_norm(x, g, b):
    mu = jnp.mean(x, axis=-1, keepdims=True)
    xc = x - mu
    var = jnp.mean(xc * xc, axis=-1, keepdims=True)
    return xc * lax.rsqrt(var + LN_EPS) * g + b


def _rms_norm(x, g):
    return x * lax.rsqrt(jnp.mean(x * x, axis=-1, keepdims=True) + LN_EPS) * g


def _const_spec(shape):
    return pl.BlockSpec(shape, lambda *_: (0,) * len(shape))


def _layer_spec(layer, *tail):
    return pl.BlockSpec((None,) + tail, lambda *_: (layer,) + (0,) * len(tail))


def _in_proj_kernel(x_ref, g_ref, b_ref, w_ref, *out_refs, apply_ln):
    x = x_ref[...]
    if apply_ln:
        x = _layer_norm(x, g_ref[...], b_ref[...])
        out_refs[0][...] = x
    out_refs[-1][...] = jnp.dot(x.astype(BF16), w_ref[...], preferred_element_type=F32)


def _in_proj(x, ln_g, ln_b, w_in_bf16, layer, apply_ln):
    n = x.shape[0]
    tm = min(2 * ROW_TILE, n)
    row = lambda w: pl.BlockSpec((tm, w), lambda i: (i, 0))
    out_shape = [jax.ShapeDtypeStruct((n, D_IN), F32)]
    out_specs = [row(D_IN)]
    if apply_ln:
        out_shape.insert(0, jax.ShapeDtypeStruct((n, D_MODEL), F32))
        out_specs.insert(0, row(D_MODEL))
    outs = pl.pallas_call(
        functools.partial(_in_proj_kernel, apply_ln=apply_ln),
        grid=(n // tm,),
        in_specs=[row(D_MODEL), _const_spec((1, D_MODEL)), _const_spec((1, D_MODEL)),
                  _layer_spec(layer, D_MODEL, D_IN)],
        out_specs=out_specs, out_shape=out_shape,
        compiler_params=_cparams("parallel"), name="in_proj",
    )(x, ln_g, ln_b, w_in_bf16)
    return (outs[0], outs[1]) if apply_ln else (x, outs[0])


def _rglru_coeffs(u, wa_ref, ba_ref, wx_ref, bx_ref, lam_ref):
    ub = u.astype(BF16)
    r = jax.nn.sigmoid(jnp.dot(ub, wa_ref[...], preferred_element_type=F32) + ba_ref[...])
    i = jax.nn.sigmoid(jnp.dot(ub, wx_ref[...], preferred_element_type=F32) + bx_ref[...])
    lam = -lam_ref[...]
    softplus = jnp.maximum(lam, 0.0) + jnp.log1p(jnp.exp(-jnp.abs(lam)))
    log_a = -LRU_C * r * softplus
    a = jnp.exp(log_a)
    b = jnp.sqrt(-jnp.tanh(log_a) * (a * a + 1.0)) * (i * u)
    return a, b


def _rnn_prompt_kernel(xr_ref, gate_ref, cw_ref, cb_ref, wa_ref, ba_ref, wx_ref, bx_ref, lam_ref,
                       y_ref, hlast_ref, xs_ref, a_ref, b_ref, carry_ref, *, tt):
    t = pl.program_id(1)

    @pl.when(t == 0)
    def _():
        xs_ref[0:SUBLANES, :] = jnp.zeros((SUBLANES, D_RNN), F32)
        carry_ref[...] = jnp.zeros((SUBLANES, D_RNN), F32)

    x = xr_ref[...]
    xs_ref[SUBLANES:SUBLANES + tt, :] = x
    cw = cw_ref[...]
    u = cb_ref[...] + x * cw[3:4]
    for j in range(CONV_W - 1):
        off = SUBLANES - (CONV_W - 1) + j
        u = u + xs_ref[off:off + tt, :] * cw[j:j + 1]
    xs_ref[0:SUBLANES, :] = x[tt - SUBLANES:tt, :]

    a, b = _rglru_coeffs(u, wa_ref, ba_ref, wx_ref, bx_ref, lam_ref)
    ng = tt // SUBLANES
    a = a.reshape(ng, SUBLANES, D_RNN)
    b = b.reshape(ng, SUBLANES, D_RNN)
    row = lax.broadcasted_iota(I32, (1, SUBLANES, 1), 1)
    for s in (1, 2, 4):
        a_sh = pltpu.roll(a, s, 1)
        b_sh = pltpu.roll(b, s, 1)
        m = row >= s
        b = jnp.where(m, a * b_sh + b, b)
        a = jnp.where(m, a * a_sh, a)
    a_ref[...] = a
    b_ref[...] = b

    def body(g, carry):
        h = a_ref[g] * carry + b_ref[g]
        b_ref[g] = h
        return jnp.broadcast_to(h[SUBLANES - 1:SUBLANES, :], (SUBLANES, D_RNN))

    carry = lax.fori_loop(0, ng, body, carry_ref[...], unroll=True)
    carry_ref[...] = carry
    h = b_ref[...].reshape(tt, D_RNN)
    y_ref[...] = h * jax.nn.gelu(gate_ref[...])
    hlast_ref[...] = carry[0:1, :]


def _rnn_prompt(z3, rw):
    bsz, t_len, _ = z3.shape
    tt = min(2 * ROW_TILE, t_len)
    vec = _layer_spec(rw["layer"], 1, D_RNN)
    mat = _layer_spec(rw["layer"], D_RNN, D_RNN)
    return pl.pallas_call(
        functools.partial(_rnn_prompt_kernel, tt=tt),
        grid=(bsz, t_len // tt),
        in_specs=[pl.BlockSpec((None, tt, D_RNN), lambda b, t: (b, t, 0)),
                  pl.BlockSpec((None, tt, D_RNN), lambda b, t: (b, t, 1)),
                  _layer_spec(rw["layer"], CONV_W, D_RNN), vec, mat, vec, mat, vec, vec],
        out_specs=[pl.BlockSpec((None, tt, D_RNN), lambda b, t: (b, t, 0)),
                   pl.BlockSpec((None, 1, D_RNN), lambda b, t: (b, 0, 0))],
        out_shape=[jax.ShapeDtypeStruct((bsz, t_len, D_RNN), F32),
                   jax.ShapeDtypeStruct((bsz, 1, D_RNN), F32)],
        scratch_shapes=[pltpu.VMEM((tt + SUBLANES, D_RNN), F32),
                        pltpu.VMEM((tt // SUBLANES, SUBLANES, D_RNN), F32),
                        pltpu.VMEM((tt // SUBLANES, SUBLANES, D_RNN), F32),
                        pltpu.VMEM((SUBLANES, D_RNN), F32)],
        compiler_params=_cparams("parallel", "arbitrary"), name="rnn_prompt",
    )(z3, z3, rw["conv_w"], rw["conv_b"], rw["wa"], rw["ba"], rw["wx"], rw["bx"], rw["lam"])


def _rnn_sample_kernel(xr_ref, gate_ref, c0_ref, c1_ref, c2_ref, h0_ref, cw_ref, cb_ref, wa_ref, ba_ref,
                       wx_ref, bx_ref, lam_ref, y_ref, h_ref):
    x = xr_ref[...]
    cw = cw_ref[...]
    u = (cb_ref[...] + c0_ref[...] * cw[0:1] + c1_ref[...] * cw[1:2] + c2_ref[...] * cw[2:3]
         + x * cw[3:4])
    a, b = _rglru_coeffs(u, wa_ref, ba_ref, wx_ref, bx_ref, lam_ref)
    h = a * h0_ref[...] + b
    h_ref[...] = h
    y_ref[...] = h * jax.nn.gelu(gate_ref[...])


def _rnn_sample(z, conv_state, h0, rw):
    n = z.shape[0]
    layer = rw["layer"]
    vec = _layer_spec(layer, 1, D_RNN)
    mat = _layer_spec(layer, D_RNN, D_RNN)
    full = _const_spec((n, D_RNN))
    depth = conv_state.shape[0]
    conv_flat = conv_state.reshape(depth, n, (CONV_W - 1) * D_RNN)
    tap = lambda j: pl.BlockSpec((None, n, D_RNN), lambda i: (layer, 0, j))
    return pl.pallas_call(
        _rnn_sample_kernel,
        grid=(1,),
        in_specs=[pl.BlockSpec((n, D_RNN), lambda i: (0, 0)), pl.BlockSpec((n, D_RNN), lambda i: (0, 1)),
                  tap(0), tap(1), tap(2), _layer_spec(layer, n, D_RNN),
                  _layer_spec(layer, CONV_W, D_RNN), vec, mat, vec, mat, vec, vec],
        out_specs=[full, full],
        out_shape=[jax.ShapeDtypeStruct((n, D_RNN), F32)] * 2,
        compiler_params=_cparams("arbitrary"), name="rnn_sample",
    )(z, z, conv_flat, conv_flat, conv_flat, h0, rw["conv_w"], rw["conv_b"], rw["wa"], rw["ba"], rw["wx"], rw["bx"], rw["lam"])


def _rel_bucket(dist):
    n = jnp.maximum(dist, 0)
    max_exact = N_BUCKETS // 2
    nf = jnp.maximum(n, max_exact).astype(F32)
    large = max_exact + (jnp.log(nf / max_exact) / math.log(MAX_DIST / max_exact)
                         * (N_BUCKETS - max_exact)).astype(I32)
    large = jnp.minimum(large, N_BUCKETS - 1)
    return jnp.where(n < max_exact, n, large)


def _bias_lookup(rel_bias, dist):
    hot = (_rel_bucket(dist)[..., None] == jnp.arange(N_BUCKETS)).astype(F32)
    out = jnp.tensordot(hot, rel_bias.astype(F32), axes=1, precision=lax.Precision.HIGHEST)
    return jnp.moveaxis(out, -1, 0)


def _half_lane_variants(x, group):
    lo = lax.broadcasted_iota(I32, (1, 2 * HEAD_DIM), 1) < HEAD_DIM
    xr = pltpu.roll(x, HEAD_DIM, 1)
    zero = jnp.zeros_like(x)
    if group == 0:
        return jnp.where(lo, x, zero), jnp.where(lo, zero, xr)
    return jnp.where(lo, xr, zero), jnp.where(lo, zero, x)


def _attn_prompt_kernel(q_ref, kc_ref, kp_ref, vc_ref, vp_ref, bias_ref, sink_ref, o_ref, *, nq, layer):
    first = pl.program_id(1) == 0
    col = lax.broadcasted_iota(I32, (1, 2 * WINDOW), 1)
    hide_prev = jnp.logical_and(first, col < WINDOW)
    for j in range(nq):
        rows = slice(j * WINDOW, (j + 1) * WINDOW)
        before = slice((j - 1) * WINDOW, j * WINDOW)
        kk = jnp.concatenate([kp_ref[...] if j == 0 else kc_ref[before, :], kc_ref[rows, :]], axis=0)
        vv = jnp.concatenate([vp_ref[...] if j == 0 else vc_ref[before, :], vc_ref[rows, :]], axis=0)
        for g in range(N_KV):
            k_lo, k_hi = _half_lane_variants(kk, g)
            v_lo, v_hi = _half_lane_variants(vv, g)
            for pair in range(GQA_R // 2):
                p0 = g * (GQA_R // 2) + pair
                qp = (q_ref[rows, p0 * 128:(p0 + 1) * 128] * (HEAD_DIM ** -0.5)).astype(BF16)
                acc = jnp.zeros((WINDOW, 2 * HEAD_DIM), F32)
                for kx, vx, h in ((k_lo, v_lo, 2 * p0), (k_hi, v_hi, 2 * p0 + 1)):
                    s = lax.dot_general(qp, kx.astype(BF16), (((1,), (1,)), ((), ())),
                                        preferred_element_type=F32) + bias_ref[h]
                    if j == 0:
                        s = jnp.where(hide_prev, NEG, s)
                    sink = sink_ref[layer, h]
                    m = jnp.maximum(jnp.max(s, axis=-1, keepdims=True), sink)
                    p = jnp.exp(s - m)
                    denom = jnp.sum(p, axis=-1, keepdims=True) + jnp.exp(sink - m)
                    acc = acc + jnp.dot(p.astype(BF16), vx.astype(BF16), preferred_element_type=F32) / denom
                o_ref[rows, p0 * 128:(p0 + 1) * 128] = acc


def _attn_prompt(z, bsz, t_len, bias_prompt, sinks, layer):
    nb = t_len // WINDOW
    nq = min(8, nb)
    ns = nb // nq
    kcol = (2 * D_RNN + D_ATTN) // 128
    cur = lambda c: pl.BlockSpec((nq * WINDOW, 128), lambda b, i: (b * ns + i, c))
    prev = lambda c: pl.BlockSpec((WINDOW, 128), lambda b, i: (b * nb + jnp.maximum(i * nq - 1, 0), c))
    return pl.pallas_call(
        functools.partial(_attn_prompt_kernel, nq=nq, layer=layer),
        grid=(bsz, ns),
        in_specs=[pl.BlockSpec((nq * WINDOW, D_ATTN), lambda b, i: (b * ns + i, 2 * D_RNN // D_ATTN)),
                  cur(kcol), prev(kcol), cur(kcol + 1), prev(kcol + 1),
                  _const_spec((N_HEADS, WINDOW, 2 * WINDOW)),
                  pl.BlockSpec(memory_space=pltpu.SMEM)],
        out_specs=pl.BlockSpec((nq * WINDOW, D_ATTN), lambda b, i: (b * ns + i, 0)),
        out_shape=jax.ShapeDtypeStruct((bsz * t_len, D_ATTN), F32),
        compiler_params=_cparams("parallel", "arbitrary"), name="attn_prompt",
    )(z, z, z, z, z, bias_prompt, sinks)


def _attn_sample_kernel(q_ref, kn_ref, vn_ref, ck_ref, cv_ref, bias_ref, sink_ref, hmask_ref, *rest):
    o_ref, nk_ref, nv_ref = rest[-3:]
    pos = lax.broadcasted_iota(I32, (1, 1, WINDOW), 2)
    last = pos == WINDOW - 1
    bb = ck_ref.shape[0]
    newk = jnp.where(last, jnp.broadcast_to(kn_ref[...], (bb, 128, WINDOW)), pltpu.roll(ck_ref[...], WINDOW - 1, 2))
    newv = jnp.where(last, jnp.broadcast_to(vn_ref[...], (bb, 128, WINDOW)), pltpu.roll(cv_ref[...], WINDOW - 1, 2))
    nk_ref[...] = newk
    nv_ref[...] = newv
    s = lax.dot_general(q_ref[...].astype(BF16), newk.astype(BF16), (((2,), (1,)), ((0,), (0,))),
                        preferred_element_type=F32)
    s = s + bias_ref[...][None]
    sink = sink_ref[...][None]
    m = jnp.maximum(jnp.max(s, axis=-1, keepdims=True), sink)
    p = jnp.exp(s - m)
    denom = jnp.sum(p, axis=-1, keepdims=True) + jnp.exp(sink - m)
    o = lax.dot_general(p.astype(BF16), newv.astype(BF16), (((2,), (2,)), ((0,), (0,))),
                        preferred_element_type=F32)
    o_ref[...] = o / denom * hmask_ref[...][None]


def _attn_sample(q, k_new, v_new, k_cache, v_cache, new_caches, layer, bias_sample, sinks):
    n = q.shape[0]
    depth = k_cache.shape[0]
    bb = 32
    qh = q.reshape(n, N_KV, GQA_R, HEAD_DIM) * (HEAD_DIM ** -0.5)
    eye = jnp.eye(N_KV, dtype=F32)
    q8 = (qh[:, :, :, None, :] * eye[None, :, None, :, None]).reshape(n, N_HEADS, N_KV * HEAD_DIM)
    hmask = jnp.repeat(jnp.repeat(eye, GQA_R, axis=0), HEAD_DIM, axis=1)
    blk3 = lambda a, b: pl.BlockSpec((bb, a, b), lambda i: (i, 0, 0))
    cache = pl.BlockSpec((None, bb, 128, WINDOW), lambda i: (layer, i, 0, 0))
    in_specs = [blk3(N_HEADS, 128), blk3(128, 1), blk3(128, 1), cache, cache,
                _const_spec((N_HEADS, WINDOW)), _layer_spec(layer, N_HEADS, 1), _const_spec((N_HEADS, 128))]
    args = [q8, k_new.reshape(n, 128, 1), v_new.reshape(n, 128, 1), k_cache, v_cache, bias_sample,
            sinks.reshape(-1, N_HEADS, 1), hmask]
    aliases = {}
    if new_caches is not None:
        in_specs += [pl.BlockSpec(memory_space=pl.ANY)] * 2
        args += list(new_caches)
        aliases = {len(args) - 2: 1, len(args) - 1: 2}
    o8, nk, nv = pl.pallas_call(
        _attn_sample_kernel,
        grid=(n // bb,),
        in_specs=in_specs,
        out_specs=[blk3(N_HEADS, 128), cache, cache],
        out_shape=[jax.ShapeDtypeStruct((n, N_HEADS, 128), F32),
                   jax.ShapeDtypeStruct((depth, n, 128, WINDOW), F32),
                   jax.ShapeDtypeStruct((depth, n, 128, WINDOW), F32)],
        input_output_aliases=aliases,
        compiler_params=_cparams("arbitrary"), name="attn_sample",
    )(*args)
    o = o8.reshape(n, N_KV, GQA_R, N_KV, HEAD_DIM)
    o = jnp.stack([o[:, g, :, g, :] for g in range(N_KV)], axis=1).reshape(n, D_ATTN)
    return o, (nk, nv)


ONE_HOT_BLOCK = 256


def _sorted_rows(tm):
    return TOP_K * tm + N_EXPERTS * SUBLANES


def _out_proj_kernel(x_ref, yr_ref, o_ref, gnr_ref, gna_ref, w_ref, g1_ref, b1_ref, rwt_ref, rb_ref,
                     x1_ref, rrow_ref, rcol_ref, cnt_ref, *, tm, alpha):
    yn = _rms_norm(yr_ref[...], gnr_ref[...]).astype(BF16)
    on = _rms_norm(o_ref[...], gna_ref[...]).astype(BF16)
    mix = jnp.dot(jnp.concatenate([yn, on], axis=1), w_ref[...], preferred_element_type=F32)
    x1 = _layer_norm(alpha * x_ref[...] + mix, g1_ref[...], b1_ref[...])
    x1_ref[...] = x1

    x1h = x1.astype(BF16)
    x1l = (x1 - x1h.astype(F32)).astype(BF16)
    nt_dims = (((1,), (1,)), ((), ()))
    by_hi = lax.dot_general(rwt_ref[...], x1h, nt_dims, preferred_element_type=F32)
    by_lo = lax.dot_general(rwt_ref[0:N_EXPERTS, :], x1l, nt_dims, preferred_element_type=F32)
    logits = by_hi[0:N_EXPERTS] + by_hi[N_EXPERTS:] + by_lo + rb_ref[...]
    eidx = lax.broadcasted_iota(I32, (N_EXPERTS, tm), 0).astype(F32)
    work = logits
    vals, hots = [], []
    for _ in range(TOP_K):
        v = jnp.max(work, axis=0, keepdims=True)
        idx = jnp.min(jnp.where(work == v, eidx, float(N_EXPERTS)), axis=0, keepdims=True)
        hot = eidx == idx
        work = jnp.where(hot, -jnp.inf, work)
        vals.append(v)
        hots.append(hot.astype(F32))
    ex = [jnp.exp(v - vals[0]) for v in vals]
    tot = ex[0] + ex[1] + ex[2] + ex[3]
    tok_hot = hots[0] + hots[1] + hots[2] + hots[3]
    s_i = lax.broadcasted_iota(I32, (tm, tm), 0)
    t_i = lax.broadcasted_iota(I32, (tm, tm), 1)
    earlier = (s_i < t_i).astype(BF16)
    before = jnp.dot(tok_hot.astype(BF16), earlier, preferred_element_type=F32)
    cnt = jnp.sum(tok_hot, axis=1, keepdims=True)
    cnt8 = jnp.floor((cnt + (SUBLANES - 1)) * (1.0 / SUBLANES)) * SUBLANES
    cnt8_b = jnp.broadcast_to(cnt8, (N_EXPERTS, LANES))
    e_r = lax.broadcasted_iota(I32, (N_EXPERTS, N_EXPERTS), 0)
    e_c = lax.broadcasted_iota(I32, (N_EXPERTS, N_EXPERTS), 1)
    start = jnp.dot((e_c < e_r).astype(F32), cnt8_b, preferred_element_type=F32,
                    precision=lax.Precision.HIGHEST)[:, 0:1]
    slot = start + before
    rows = [jnp.sum(h * slot, axis=0, keepdims=True) for h in hots] + [e / tot for e in ex]
    rrow = jnp.concatenate(rows, axis=0)
    rrow_ref[...] = rrow
    rcol_ref[...] = rrow.T
    cnt_ref[...] = cnt8_b.T[0:1, 0:N_EXPERTS].astype(I32)


def _out_proj(x, y_rnn, o, lw):
    n = x.shape[0]
    tm = min(ROW_TILE, n)
    nt = n // tm
    row = lambda w: pl.BlockSpec((tm, w), lambda i: (i, 0))
    layer = lw["layer"]
    vec = lambda w: _layer_spec(layer, 1, w)
    return pl.pallas_call(
        functools.partial(_out_proj_kernel, tm=tm, alpha=lw["alpha"]),
        grid=(nt,),
        in_specs=[row(D_MODEL), row(D_RNN), row(D_ATTN), vec(D_RNN), vec(D_ATTN),
                  _layer_spec(layer, D_MODEL, D_MODEL), vec(D_MODEL), vec(D_MODEL),
                  _layer_spec(layer, 2 * N_EXPERTS, D_MODEL), _layer_spec(layer, N_EXPERTS, 1)],
        out_specs=[row(D_MODEL), pl.BlockSpec((None, 2 * TOP_K, tm), lambda i: (i, 0, 0)), row(2 * TOP_K),
                   pl.BlockSpec((None, 1, N_EXPERTS), lambda i: (i, 0, 0))],
        out_shape=[jax.ShapeDtypeStruct((n, D_MODEL), F32), jax.ShapeDtypeStruct((nt, 2 * TOP_K, tm), F32),
                   jax.ShapeDtypeStruct((n, 2 * TOP_K), F32), jax.ShapeDtypeStruct((nt, 1, N_EXPERTS), I32)],
        compiler_params=_cparams("parallel"), name="out_proj_route",
    )(x, y_rnn, o, lw["gn_rnn"], lw["gn_attn"], lw["w_out"], lw["ln1_g"], lw["ln1_b"],
      lw["router_wt"], lw["router_b"])


RUN_CHUNK = 64
_SMALL_PIECES = (32, 16, 8)
_PAD_PIECES = (128, 64, 32, 16, 8)


def _for_each_piece(n8, fn):
    def chunk(j, c):
        fn(pl.multiple_of(j * RUN_CHUNK, RUN_CHUNK), RUN_CHUNK)
        return c
    lax.fori_loop(0, n8 // RUN_CHUNK, chunk, 0)
    for s in _SMALL_PIECES:
        @pl.when((n8 & s) != 0)
        def _(s=s):
            fn(n8 & (-2 * s), s)


def _wait_tile_rows(n_rows, tm, wait_rows):
    wait_rows(TOP_K * tm)
    pad = n_rows - TOP_K * tm
    for s in _PAD_PIECES:
        @pl.when((pad & s) != 0)
        def _(s=s):
            wait_rows(s)


def _dispatch_kernel(cnt_ref, base_ref, x_ref, rrow_ref, *rest, tm):
    xbuf_ref, xs_ref, carry_ref, nrows_ref, sems = rest[-5:]
    nrow = _sorted_rows(tm)
    i = pl.program_id(0)
    slot = i % 2

    @pl.when(i == 0)
    def _():
        def init(e, c):
            carry_ref[e] = base_ref[e]
            return c
        lax.fori_loop(0, N_EXPERTS, init, 0)

    r = lax.broadcasted_iota(I32, (ONE_HOT_BLOCK, 1), 0).astype(F32).astype(BF16)
    blocks = []
    for c in range(nrow // ONE_HOT_BLOCK):
        hit = None
        for k in range(TOP_K):
            h = r == (rrow_ref[k:k + 1, :] - float(c * ONE_HOT_BLOCK)).astype(BF16)
            hit = h if hit is None else jnp.logical_or(hit, h)
        blocks.append(jnp.where(hit, jnp.ones((), BF16), jnp.zeros((), BF16)))
    perm = jnp.concatenate(blocks, axis=0)
    xs = jnp.dot(perm, x_ref[...].astype(BF16), preferred_element_type=F32)
    lo = lax.bitcast_convert_type(xs[:, :D_PACK], U32)
    hi = lax.bitcast_convert_type(xs[:, D_PACK:], U32)
    xs_ref[slot] = (hi & U32(HI_MASK)) | (lo >> 16)

    def run_copy(buf, src, dst, s):
        return pltpu.make_async_copy(xs_ref.at[buf, pl.ds(pl.multiple_of(src, SUBLANES), s)],
                                     xbuf_ref.at[pl.ds(pl.multiple_of(dst, SUBLANES), s)], sems.at[buf])

    def start_expert(e, src, priority):
        n8 = cnt_ref[0, e]
        dst = carry_ref[e]
        _for_each_piece(n8, lambda off, s: run_copy(slot, src + off, dst + off, s).start(priority=priority))
        carry_ref[e] = dst + n8
        return src + n8

    def start(pair, src):
        return start_expert(2 * pair + 1, start_expert(2 * pair, src, 0), 1)

    def wait_tile(buf):
        _wait_tile_rows(nrows_ref[buf], tm, lambda s: run_copy(buf, 0, 0, s).wait())

    nrows_ref[slot] = lax.fori_loop(0, N_EXPERTS // 2, start, 0)

    @pl.when(i > 0)
    def _():
        wait_tile(1 - slot)

    @pl.when(i == pl.num_programs(0) - 1)
    def _():
        wait_tile(slot)


def _dispatch(x1, rrow, cnt8, base, xbuf, n_rows):
    n = x1.shape[0]
    nt, _, tm = rrow.shape
    in_specs = [pl.BlockSpec((None, 1, N_EXPERTS), lambda i: (i, 0, 0), memory_space=pltpu.SMEM),
                pl.BlockSpec(memory_space=pltpu.SMEM),
                pl.BlockSpec((tm, D_MODEL), lambda i: (i, 0)),
                pl.BlockSpec((None, 2 * TOP_K, tm), lambda i: (i, 0, 0))]
    args = [cnt8, base, x1, rrow]
    aliases = {}
    if xbuf is not None:
        in_specs.append(pl.BlockSpec(memory_space=pl.ANY))
        args.append(xbuf)
        aliases = {len(args) - 1: 0}
    return pl.pallas_call(
        functools.partial(_dispatch_kernel, tm=tm),
        grid=(nt,),
        in_specs=in_specs,
        out_specs=pl.BlockSpec(memory_space=pl.ANY),
        out_shape=jax.ShapeDtypeStruct((n_rows, D_PACK), U32),
        scratch_shapes=[pltpu.VMEM((2, _sorted_rows(tm), D_PACK), U32), pltpu.SMEM((N_EXPERTS,), I32),
                        pltpu.SMEM((2,), I32), pltpu.SemaphoreType.DMA((2,))],
        input_output_aliases=aliases,
        compiler_params=_cparams("arbitrary"), name="moe_dispatch",
    )(*args)


def _moe_kernel(br_ref, be_ref, nv_ref, first_ref, next_ref, x_ref, w1_hbm, b1_ref, w2_hbm, b2_ref, y_ref,
                w1f_ref, w2f_ref, w1s_ref, w2s_ref, sems, *, layer):
    i = pl.program_id(0)

    def load(e):
        return (pltpu.make_async_copy(w1_hbm.at[layer, e], w1f_ref, sems.at[0]),
                pltpu.make_async_copy(w2_hbm.at[layer, e], w2f_ref, sems.at[1]))

    @pl.when(jnp.logical_and(i == 0, first_ref[0] == 1))
    def _():
        for c in load(be_ref[0]):
            c.start()

    nv = nv_ref[i]
    quarter = MOE_TILE // 4
    is_first = first_ref[i] == 1
    fresh_full = jnp.logical_and(is_first, nv > 3 * quarter)

    def start_next_load():
        @pl.when(next_ref[i] >= 0)
        def _():
            for c in load(next_ref[i]):
                c.start(priority=1)

    def expert_rows(m, fresh):
        if fresh:
            w1 = w1f_ref[...].astype(BF16)
            w2 = w2f_ref[...].astype(BF16)
            w1s_ref[...] = w1
            w2s_ref[...] = w2
        else:
            w1 = w1s_ref[...]
            w2 = w2s_ref[...]
        rows = lax.broadcasted_iota(I32, (m, 1), 0)
        xw = jnp.where(rows < nv, x_ref[0:m, :], U32(0))
        lo = lax.bitcast_convert_type(xw << 16, F32).astype(BF16)
        hi = lax.bitcast_convert_type(xw & U32(HI_MASK), F32).astype(BF16)
        xb = jnp.concatenate([lo, hi], axis=1)
        hdn = jnp.dot(xb, w1, preferred_element_type=F32) + b1_ref[...]
        g = jnp.minimum(hdn[:, :D_FF], SWIGLU_LIMIT)
        lin = jnp.clip(hdn[:, D_FF:], -SWIGLU_LIMIT, SWIGLU_LIMIT)
        act = g * jax.nn.sigmoid(SWIGLU_ALPHA * g) * (lin + 1.0)
        y_ref[0:m, :] = jnp.dot(act.astype(BF16), w2, preferred_element_type=F32) + b2_ref[...]

    @pl.when(is_first)
    def _():
        for c in load(be_ref[i]):
            c.wait()

    @pl.when(fresh_full)
    def _():
        expert_rows(MOE_TILE, True)
        start_next_load()

    @pl.when(jnp.logical_and(is_first, jnp.logical_not(fresh_full)))
    def _():
        w1s_ref[...] = w1f_ref[...].astype(BF16)
        w2s_ref[...] = w2f_ref[...].astype(BF16)
        start_next_load()

    for q in range(1, 5):
        @pl.when(jnp.logical_and(jnp.logical_not(fresh_full),
                                 jnp.logical_and(nv > (q - 1) * quarter, nv <= q * quarter)))
        def _(q=q):
            expert_rows(q * quarter, False)


def _moe_experts(xbuf, geom, w1, b1, w2, b2, layer):
    n_blocks = xbuf.shape[0] // MOE_TILE
    bias = lambda w: pl.BlockSpec((None, None, 1, w), lambda i, br, be, *_: (layer, be[i], 0, 0))
    grid_spec = pltpu.PrefetchScalarGridSpec(
        num_scalar_prefetch=5,
        grid=(n_blocks,),
        in_specs=[pl.BlockSpec((MOE_TILE, D_PACK), lambda i, br, *_: (br[i], 0)),
                  pl.BlockSpec(memory_space=pl.ANY), bias(2 * D_FF),
                  pl.BlockSpec(memory_space=pl.ANY), bias(D_MODEL)],
        out_specs=pl.BlockSpec((MOE_TILE, D_MODEL), lambda i, br, *_: (br[i], 0)),
        scratch_shapes=[pltpu.VMEM((D_MODEL, 2 * D_FF), F32), pltpu.VMEM((D_FF, D_MODEL), F32),
                        pltpu.VMEM((D_MODEL, 2 * D_FF), BF16), pltpu.VMEM((D_FF, D_MODEL), BF16),
                        pltpu.SemaphoreType.DMA((2,))],
    )
    depth = w1.shape[0]
    return pl.pallas_call(
        functools.partial(_moe_kernel, layer=layer), grid_spec=grid_spec,
        out_shape=jax.ShapeDtypeStruct((xbuf.shape[0], D_MODEL), F32),
        compiler_params=_cparams("arbitrary"), name="moe_experts",
    )(*geom, xbuf, w1, b1.reshape(depth, N_EXPERTS, 1, 2 * D_FF), w2, b2.reshape(depth, N_EXPERTS, 1, D_MODEL))


def _final_kernel(cnt_ref, cnt_next_ref, base_ref, x_ref, p_ref, rcol_ref, wg_ref, bg_ref, wp_ref, g2_ref, b2_ref,
                  ybuf_ref, out_ref, ys_ref, carry_ref, nrows_ref, sems, *, tm, alpha):
    nrow = _sorted_rows(tm)
    i = pl.program_id(0)
    slot = i % 2

    def run_copy(buf, src, dst, s):
        return pltpu.make_async_copy(ybuf_ref.at[pl.ds(pl.multiple_of(src, SUBLANES), s)],
                                     ys_ref.at[buf, pl.ds(pl.multiple_of(dst, SUBLANES), s)], sems.at[buf])

    def gather_tile(counts_ref, buf):
        def start_expert(e, dst, priority):
            n8 = counts_ref[0, e]
            src = carry_ref[e]
            _for_each_piece(n8, lambda off, s: run_copy(buf, src + off, dst + off, s).start(priority=priority))
            carry_ref[e] = src + n8
            return dst + n8

        def start(pair, dst):
            return start_expert(2 * pair + 1, start_expert(2 * pair, dst, 0), 1)
        nrows_ref[buf] = lax.fori_loop(0, N_EXPERTS // 2, start, 0)

    @pl.when(i == 0)
    def _():
        def init(e, c):
            carry_ref[e] = base_ref[e]
            return c
        lax.fori_loop(0, N_EXPERTS, init, 0)
        ys_ref[:, TOP_K * tm:, :] = jnp.zeros((2, nrow - TOP_K * tm, D_MODEL), F32)
        gather_tile(cnt_ref, 0)

    @pl.when(i + 1 < pl.num_programs(0))
    def _():
        gather_tile(cnt_next_ref, 1 - slot)

    x = x_ref[...]
    ple = (jax.nn.sigmoid(jnp.dot(x.astype(BF16), wg_ref[...], preferred_element_type=F32) + bg_ref[...])
           * jnp.dot(p_ref[...].astype(BF16), wp_ref[...], preferred_element_type=F32))
    col = lax.broadcasted_iota(I32, (1, ONE_HOT_BLOCK), 1).astype(F32).astype(BF16)
    rc = rcol_ref[...]
    gates = rc[:, TOP_K:].astype(BF16)
    blocks = []
    for c in range(nrow // ONE_HOT_BLOCK):
        blk = jnp.zeros((tm, ONE_HOT_BLOCK), BF16)
        for k in range(TOP_K):
            hit = col == (rc[:, k:k + 1] - float(c * ONE_HOT_BLOCK)).astype(BF16)
            blk = jnp.where(hit, gates[:, k:k + 1], blk)
        blocks.append(blk)
    gmat = jnp.concatenate(blocks, axis=1)
    base = alpha * x + ple

    _wait_tile_rows(nrows_ref[slot], tm, lambda s: run_copy(slot, 0, 0, s).wait())

    ffn = jnp.dot(gmat, ys_ref[slot].astype(BF16), preferred_element_type=F32)
    out_ref[...] = _layer_norm(base + ffn, g2_ref[...], b2_ref[...])


def _final(x1, p_all, layer, rcol, cnt8, base, ybuf, lw):
    n = x1.shape[0]
    nt = cnt8.shape[0]
    tm = n // nt
    row = lambda w: pl.BlockSpec((tm, w), lambda i: (i, 0))
    vec = lambda w: _layer_spec(layer, 1, w)
    return pl.pallas_call(
        functools.partial(_final_kernel, tm=tm, alpha=lw["alpha"]),
        grid=(nt,),
        in_specs=[pl.BlockSpec((None, 1, N_EXPERTS), lambda i: (i, 0, 0), memory_space=pltpu.SMEM),
                  pl.BlockSpec((None, 1, N_EXPERTS), lambda i: (jnp.minimum(i + 1, nt - 1), 0, 0),
                               memory_space=pltpu.SMEM),
                  pl.BlockSpec(memory_space=pltpu.SMEM),
                  row(D_MODEL), pl.BlockSpec((None, tm, D_PLE), lambda i: (layer, i, 0)), row(2 * TOP_K),
                  _layer_spec(layer, D_MODEL, D_MODEL), vec(D_MODEL), _layer_spec(layer, D_PLE, D_MODEL),
                  vec(D_MODEL), vec(D_MODEL), pl.BlockSpec(memory_space=pl.ANY)],
        out_specs=row(D_MODEL),
        out_shape=jax.ShapeDtypeStruct((n, D_MODEL), F32),
        scratch_shapes=[pltpu.VMEM((2, _sorted_rows(tm), D_MODEL), F32), pltpu.SMEM((N_EXPERTS,), I32),
                        pltpu.SMEM((2,), I32), pltpu.SemaphoreType.DMA((2,))],
        compiler_params=_cparams("arbitrary"), name="moe_combine_ln2",
    )(cnt8, cnt8, base, x1, p_all, rcol, lw["ple_w_gate"], lw["ple_b_gate"], lw["ple_w_proj"], lw["ln2_g"], lw["ln2_b"],
      ybuf)


def _block_diag(w):
    depth, nb, bw, _ = w.shape
    eye = jnp.eye(nb, dtype=w.dtype)
    return (w[:, :, :, None, :] * eye[None, :, None, :, None]).reshape(depth, nb * bw, nb * bw)


def _hi_lo_rows(w):
    hi = w.astype(BF16)
    lo = (w - hi.astype(F32)).astype(BF16)
    return jnp.concatenate([hi, lo], axis=-2)


def _block_geometry(tot, n_blocks):
    nblk = (tot + MOE_TILE - 1) // MOE_TILE
    blk_end = jnp.cumsum(nblk)
    blk_start = blk_end - nblk
    ids = jnp.arange(n_blocks, dtype=I32)
    n_valid = blk_end[-1]
    expert_at = lambda j: jnp.minimum(jnp.sum(j[:, None] >= blk_end[None, :], axis=1), N_EXPERTS - 1).astype(I32)
    src = jnp.clip(ids, 0, jnp.maximum(n_valid - 1, 0))
    e_of = expert_at(src)
    hot = e_of[:, None] == jnp.arange(N_EXPERTS)[None, :]
    pick = lambda v: jnp.sum(jnp.where(hot, v[None, :], 0), axis=1)
    start_of, end_of, tot_of = pick(blk_start), pick(blk_end), pick(tot)
    real = ids < n_valid
    nv = jnp.where(real, jnp.clip(tot_of - (ids - start_of) * MOE_TILE, 0, MOE_TILE), 0)
    first = jnp.logical_and(real, ids == start_of)
    nxt = jnp.where(end_of < n_valid, expert_at(end_of), -1)
    geom = tuple(v.astype(I32) for v in (src, e_of, nv, first, nxt))
    return geom, (blk_start * MOE_TILE).astype(I32)


def kernel(x_prompt, x_sample, cache_k, cache_v, state_conv, state_h, p_prompt, p_sample, ln_in_g, ln_in_b, rel_bias, w_in, conv_w, conv_b, rg_wa, rg_ba, rg_wx, rg_bx, rg_lambda, attn_sinks, gn_rnn, gn_attn, w_out, ln1_g, ln1_b, router_w, router_b, moe_w1, moe_b1, moe_w2, moe_b2, ple_w_gate, ple_b_gate, ple_w_proj, ln2_g, ln2_b):
    depth = w_in.shape[0]
    bsz, t_len, _ = x_prompt.shape
    n_p = bsz * t_len
    n_s = x_sample.shape[0]
    wb = cache_k.shape[2]
    alpha = (2 * depth) ** 0.25

    dist = jnp.arange(WINDOW)[:, None] + WINDOW - jnp.arange(2 * WINDOW)[None, :]
    valid = (dist >= 0) & (dist < WINDOW)
    bias_prompt = jnp.where(valid[None], _bias_lookup(rel_bias, dist), NEG)
    bias_sample = _bias_lookup(rel_bias, WINDOW - 1 - jnp.arange(WINDOW))

    tm_p = min(ROW_TILE, n_p)
    n_tiles = n_p // tm_p + 1
    max_rows = (n_p + n_s) * TOP_K + n_tiles * N_EXPERTS * (SUBLANES - 1)
    n_blocks = -(-(max_rows + N_EXPERTS * (MOE_TILE - 1)) // MOE_TILE)

    xp = x_prompt.reshape(n_p, D_MODEL)
    xs = x_sample.reshape(n_s, D_MODEL)
    pp = p_prompt.reshape(depth, n_p, D_PLE)
    ps = p_sample.reshape(depth, n_s, D_PLE)
    to_feature_major = lambda c: jnp.transpose(c, (0, 1, 3, 4, 2)).reshape(depth, n_s, N_KV * HEAD_DIM, wb)
    to_position_major = lambda c: jnp.transpose(c.reshape(depth, n_s, N_KV, HEAD_DIM, wb), (0, 1, 4, 2, 3))
    ck = to_feature_major(cache_k)
    cv = to_feature_major(cache_v)
    new_caches = None
    row2 = lambda v: v.reshape(1, -1)
    kv0 = 2 * D_RNN + D_ATTN
    outs = {k: [] for k in ("kp", "vp", "cp", "hp", "cs", "hs")}

    vecs = lambda v: v.reshape(depth, 1, -1)
    rw_all = {"conv_w": conv_w, "conv_b": vecs(conv_b), "wa": _block_diag(rg_wa).astype(BF16),
              "ba": vecs(rg_ba), "wx": _block_diag(rg_wx).astype(BF16), "bx": vecs(rg_bx),
              "lam": vecs(rg_lambda)}
    lw_all = {"gn_rnn": vecs(gn_rnn), "gn_attn": vecs(gn_attn), "w_out": w_out.astype(BF16),
              "ln1_g": vecs(ln1_g), "ln1_b": vecs(ln1_b),
              "router_wt": _hi_lo_rows(jnp.swapaxes(router_w, 1, 2)),
              "router_b": router_b.reshape(depth, N_EXPERTS, 1), "alpha": alpha,
              "ple_w_gate": ple_w_gate.astype(BF16), "ple_b_gate": vecs(ple_b_gate),
              "ple_w_proj": ple_w_proj.astype(BF16), "ln2_g": vecs(ln2_g), "ln2_b": vecs(ln2_b)}
    w_in_b = w_in.astype(BF16)

    for l in range(depth):
        rw = dict(rw_all, layer=l)
        lw = dict(lw_all, layer=l)
        first = l == 0

        xp, zp = _in_proj(xp, row2(ln_in_g), row2(ln_in_b), w_in_b, l, first)
        zp3 = zp.reshape(bsz, t_len, D_IN)
        yp, hp = _rnn_prompt(zp3, rw)
        op = _attn_prompt(zp, bsz, t_len, bias_prompt, attn_sinks, l)
        wp = min(WINDOW, t_len)
        outs["kp"].append(zp3[:, t_len - wp:, kv0:kv0 + 128].reshape(bsz, wp, N_KV, HEAD_DIM))
        outs["vp"].append(zp3[:, t_len - wp:, kv0 + 128:kv0 + 256].reshape(bsz, wp, N_KV, HEAD_DIM))
        outs["cp"].append(zp3[:, t_len - (CONV_W - 1):, :D_RNN])
        outs["hp"].append(hp.reshape(bsz, D_RNN))

        xs, zs = _in_proj(xs, row2(ln_in_g), row2(ln_in_b), w_in_b, l, first)
        ys, hs = _rnn_sample(zs, state_conv, state_h, rw)
        os_, new_caches = _attn_sample(zs[:, 2 * D_RNN:kv0], zs[:, kv0:kv0 + 128], zs[:, kv0 + 128:], ck, cv,
                                       new_caches, l, bias_sample, attn_sinks)
        outs["cs"].append(jnp.concatenate([state_conv[l][:, 1:], zs[:, None, :D_RNN]], axis=1))
        outs["hs"].append(hs)

        x1p, rrow_p, rcol_p, cnt_p = _out_proj(xp, yp.reshape(n_p, D_RNN), op, lw)
        x1s, rrow_s, rcol_s, cnt_s = _out_proj(xs, ys, os_, lw)

        tot_p = jnp.sum(cnt_p, axis=(0, 1))
        tot = tot_p + jnp.sum(cnt_s, axis=(0, 1))
        geom, base_p = _block_geometry(tot, n_blocks)
        base_s = base_p + tot_p

        xbuf = _dispatch(x1p, rrow_p, cnt_p, base_p, None, n_blocks * MOE_TILE)
        xbuf = _dispatch(x1s, rrow_s, cnt_s, base_s, xbuf, n_blocks * MOE_TILE)
        ybuf = _moe_experts(xbuf, geom, moe_w1, moe_b1, moe_w2, moe_b2, l)
        xp = _final(x1p, pp, l, rcol_p, cnt_p, base_p, ybuf, lw)
        xs = _final(x1s, ps, l, rcol_s, cnt_s, base_s, ybuf, lw)

    st = lambda k: jnp.stack(outs[k])
    return (xp.reshape(bsz, t_len, D_MODEL), xs.reshape(n_s, 1, D_MODEL),
            st("kp"), st("vp"), st("cp"), st("hp"), to_position_major(new_caches[0]),
            to_position_major(new_caches[1]), st("cs"), st("hs"))
```
